```python
import math
import jax, jax.numpy as jnp
from jax import lax
import numpy as np

D_MODEL = 1024
BATCH = 16
SEQ = 2048
DEPTH = 2

N_MIXERS = 2
HEAD_DIM = 64
N_HEADS = D_MODEL // HEAD_DIM
RWKV_DECAY_LORA = 64
RWKV_AAA_LORA = 64
RWKV_GATE_LORA = 128
RWKV_GN_EPS = 64e-5
MOBA_BLOCK = 256
MOBA_TOPK = 3
Q_CHUNK = 128
ROPE_THETA = 10000.0
MEM_TOKENS = 256
MEM_HEADS = 4
MEM_HEAD_DIM = D_MODEL // MEM_HEADS
MOE_GROUPS = 4
MOE_EXPERTS_PER_GROUP = 8
MOE_EXPERTS = MOE_GROUPS * MOE_EXPERTS_PER_GROUP
MOE_TOPK = 2
MOE_FF = D_MODEL // 2
MOE_ROW_BLOCK = 128
RMS_EPS = 1e-6
NEG_INF = -1e30
N_RWKV_LAYERS = (DEPTH + N_MIXERS - 1) // N_MIXERS
N_MOBA_LAYERS = DEPTH // N_MIXERS

kernel_name = "hybrid_rwkv7_moba_memxattn_hmoe"

F32 = jnp.float32


def rms_norm(x, g):
    xf = x.astype(F32)
    y = xf * lax.rsqrt(jnp.mean(xf * xf, axis=-1, keepdims=True) + RMS_EPS)
    return (y * g.astype(F32)).astype(x.dtype)


def rotary(t, positions):
    half = t.shape[-1] // 2
    inv = ROPE_THETA ** (-jnp.arange(half, dtype=F32) * 2.0 / t.shape[-1])
    ang = positions.astype(F32)[..., None] * inv
    cos = jnp.cos(ang)[:, :, None, :]
    sin = jnp.sin(ang)[:, :, None, :]
    tf = t.astype(F32)
    t1, t2 = tf[..., :half], tf[..., half:]
    return jnp.concatenate([t1 * cos - t2 * sin, t2 * cos + t1 * sin], axis=-1).astype(t.dtype)


def rwkv7_time_mix(h, mu, w0, w1, w2, a0, a1, a2, g1, g2, k_k, k_a, r_k, w_rkv, lnx_g, lnx_b, w_o):
    B, T, C = h.shape
    dx = jnp.pad(h, ((0, 0), (1, 0), (0, 0)))[:, :T] - h
    xr, xw, xk, xv, xa, xg = [h + dx * mu[s] for s in range(6)]
    r = xr @ w_rkv[0]
    k = xk @ w_rkv[1]
    v = xv @ w_rkv[2]
    w = -jax.nn.softplus(-(w0 + jnp.tanh(xw @ w1) @ w2)) - 0.5
    a = jax.nn.sigmoid(a0 + (xa @ a1) @ a2)
    g = jax.nn.sigmoid(xg @ g1) @ g2

    def heads(t):
        return t.reshape(B, T, N_HEADS, HEAD_DIM).astype(F32)

    kk = heads(k * k_k)
    kk = kk / jnp.maximum(jnp.sqrt(jnp.sum(kk * kk, axis=-1, keepdims=True)), 1e-12)
    k = k * (1.0 + (a - 1.0) * k_a)
    r_h, k_h, v_h, a_h = heads(r), heads(k), heads(v), heads(a)
    decay = jnp.exp(-jnp.exp(heads(w)))

    def step(S, inp):
        r_t, d_t, k_t, v_t, kk_t, akk_t = inp
        s_kk = jnp.einsum('bhvk,bhk->bhv', S, kk_t)
        S = S * d_t[:, :, None, :] - s_kk[..., None] * akk_t[:, :, None, :] + v_t[..., None] * k_t[:, :, None, :]
        return S, jnp.einsum('bhvk,bhk->bhv', S, r_t)

    def tm(t):
        return jnp.swapaxes(t, 0, 1)

    S0 = jnp.zeros((B, N_HEADS, HEAD_DIM, HEAD_DIM), F32)
    _, o = lax.scan(step, S0, (tm(r_h), tm(decay), tm(k_h), tm(v_h), tm(kk), tm(kk * a_h)))
    o = tm(o)
    mean = jnp.mean(o, axis=-1, keepdims=True)
    var = jnp.mean((o - mean) ** 2, axis=-1, keepdims=True)
    o = ((o - mean) * lax.rsqrt(var + RWKV_GN_EPS)).reshape(B, T, C) * lnx_g.astype(F32) + lnx_b.astype(F32)
    bonus = (jnp.sum(r_h * k_h * r_k.astype(F32), axis=-1, keepdims=True) * v_h).reshape(B, T, C)
    return ((o + bonus).astype(h.dtype) * g) @ w_o


def moba_attention(h, positions, w_qkv, w_o):
    B, T, C = h.shape
    qkv = (h @ w_qkv).reshape(B, T, 3, N_HEADS, HEAD_DIM)
    q = jnp.transpose(rotary(qkv[:, :, 0], positions), (0, 2, 1, 3))
    k = jnp.transpose(rotary(qkv[:, :, 1], positions), (0, 2, 1, 3))
    v = jnp.transpose(qkv[:, :, 2], (0, 2, 1, 3))
    n_blk = -(-T // MOBA_BLOCK)
    pad = n_blk * MOBA_BLOCK - T
    kb = jnp.pad(k, ((0, 0), (0, 0), (0, pad), (0, 0))).reshape(B, N_HEADS, n_blk, MOBA_BLOCK, HEAD_DIM)
    vb = jnp.pad(v, ((0, 0), (0, 0), (0, pad), (0, 0))).reshape(B, N_HEADS, n_blk, MOBA_BLOCK, HEAD_DIM)
    scale = 1.0 / math.sqrt(HEAD_DIM)
    topk_eff = min(MOBA_TOPK, n_blk - 1)
    q_blk = jnp.arange(T) // MOBA_BLOCK
    if topk_eff > 0:
        kmean = jnp.mean(kb.astype(F32), axis=3)
        gate = jnp.einsum('bhtd,bhnd->bhtn', q.astype(F32), kmean)
        past = jnp.arange(n_blk)[None, :] < q_blk[:, None]
        gate = jnp.where(past, gate, NEG_INF)
        _, sel = lax.top_k(gate, topk_eff)
        sel_valid = jnp.arange(topk_eff)[None, :] < q_blk[:, None]
    n_qc = T // Q_CHUNK
    h_ar = jnp.arange(N_HEADS)[:, None, None]
    q_off = jnp.arange(Q_CHUNK)
    k_off = jnp.arange(MOBA_BLOCK)

    def chunk(i):
        b = i // n_qc
        q0 = (i % n_qc) * Q_CHUNK
        qc = lax.dynamic_slice_in_dim(q[b], q0, Q_CHUNK, axis=1)
        kb_b = kb[b]
        vb_b = vb[b]
        own = q0 // MOBA_BLOCK
        k_own = lax.dynamic_index_in_dim(kb_b, own, axis=1, keepdims=False)
        v_own = lax.dynamic_index_in_dim(vb_b, own, axis=1, keepdims=False)
        s_own = jnp.einsum('hqd,hkd->hqk', qc, k_own, preferred_element_type=F32) * scale
        causal = (own * MOBA_BLOCK + k_off)[None, :] <= (q0 + q_off)[:, None]
        s_own = jnp.where(causal[None], s_own, NEG_INF)
        if topk_eff > 0:
            sel_c = lax.dynamic_slice_in_dim(sel[b], q0, Q_CHUNK, axis=1)
            valid_c = lax.dynamic_slice_in_dim(sel_valid, q0, Q_CHUNK, axis=0)
            k_sel = kb_b[h_ar, sel_c]
            v_sel = vb_b[h_ar, sel_c]
            s_sel = jnp.einsum('hqd,hqnkd->hqnk', qc, k_sel, preferred_element_type=F32) * scale
            s_sel = jnp.where(valid_c[None, :, :, None], s_sel, NEG_INF)
            s_sel = s_sel.reshape(N_HEADS, Q_CHUNK, topk_eff * MOBA_BLOCK)
            p = jax.nn.softmax(jnp.concatenate([s_sel, s_own], axis=-1), axis=-1)
            p_sel = p[..., :topk_eff * MOBA_BLOCK].reshape(N_HEADS, Q_CHUNK, topk_eff, MOBA_BLOCK).astype(v.dtype)
            p_own = p[..., topk_eff * MOBA_BLOCK:].astype(v.dtype)
            return (jnp.einsum('hqnk,hqnkd->hqd', p_sel, v_sel)
                    + jnp.einsum('hqk,hkd->hqd', p_own, v_own))
        p_own = jax.nn.softmax(s_own, axis=-1).astype(v.dtype)
        return jnp.einsum('hqk,hkd->hqd', p_own, v_own)

    outs = lax.map(chunk, jnp.arange(B * n_qc))
    o = outs.reshape(B, n_qc, N_HEADS, Q_CHUNK, HEAD_DIM).transpose(0, 1, 3, 2, 4).reshape(B, T, C)
    return o @ w_o


def mem_cross_attention(h, mem_n, w_q, w_kv, w_o):
    B, T, C = h.shape
    M = mem_n.shape[1]
    q = (h @ w_q).reshape(B, T, MEM_HEADS, MEM_HEAD_DIM)
    kv = (mem_n @ w_kv).reshape(B, M, 2, MEM_HEADS, MEM_HEAD_DIM)
    s = jnp.einsum('bthd,bmhd->bhtm', q, kv[:, :, 0], preferred_element_type=F32) / math.sqrt(MEM_HEAD_DIM)
    p = jax.nn.softmax(s, axis=-1).astype(h.dtype)
    o = jnp.einsum('bhtm,bmhd->bthd', p, kv[:, :, 1]).reshape(B, T, C)
    return o @ w_o


def hierarchical_moe(h, w_grp, b_grp, w_exp, b_exp, w1, w3, w2):
    N, C = h.shape
    R = MOE_ROW_BLOCK
    g_logits = jnp.einsum('nc,cg->ng', h, w_grp, preferred_element_type=F32) + b_grp.astype(F32)
    p_g, g_idx = lax.top_k(jax.nn.softmax(g_logits, axis=-1), 1)
    e_logits = (jnp.einsum('nc,ce->ne', h, w_exp, preferred_element_type=F32) + b_exp.astype(F32))
    e_logits = e_logits.reshape(N, MOE_GROUPS, MOE_EXPERTS_PER_GROUP)
    e_logits = jnp.take_along_axis(e_logits, g_idx[:, :, None], axis=1)[:, 0]
    p_e, e_loc = lax.top_k(jax.nn.softmax(e_logits, axis=-1), MOE_TOPK)
    gate = p_g * p_e / jnp.sum(p_e, axis=-1, keepdims=True)
    expert = g_idx * MOE_EXPERTS_PER_GROUP + e_loc

    P = N * MOE_TOPK
    flat_e = expert.reshape(-1)
    flat_tok = jnp.repeat(jnp.arange(N, dtype=jnp.int32), MOE_TOPK)
    flat_gate = gate.reshape(-1)
    order = jnp.argsort(flat_e)
    e_sorted = flat_e[order]
    counts = jnp.bincount(flat_e, length=MOE_EXPERTS)
    padded = (counts + R - 1) // R * R
    start = jnp.cumsum(counts) - counts
    pad_end = jnp.cumsum(padded)
    pad_start = pad_end - padded
    dest = pad_start[e_sorted] + (jnp.arange(P) - start[e_sorted])
    n_blocks = (P + MOE_EXPERTS * (R - 1) + R - 1) // R
    n_rows = n_blocks * R
    row_tok = jnp.zeros((n_rows,), jnp.int32).at[dest].set(flat_tok[order])
    row_gate = jnp.zeros((n_rows,), F32).at[dest].set(flat_gate[order])
    block_expert = jnp.minimum(jnp.searchsorted(pad_end, jnp.arange(n_blocks) * R, side='right'),
                               MOE_EXPERTS - 1)
    x_rows = h[row_tok].reshape(n_blocks, R, C)

    def expert_block(args):
        xb, e = args
        return (jax.nn.silu(xb @ w1[e]) * (xb @ w3[e])) @ w2[e]

    y = lax.map(expert_block, (x_rows, block_expert)).reshape(n_rows, C)
    out = jnp.zeros((N, C), F32).at[row_tok].add(y.astype(F32) * row_gate[:, None])
    return out.astype(h.dtype)


def setup_inputs(seed: int = 0) -> dict:
    key = jax.random.key(seed)
    ks = iter(jax.random.split(key, 40))
    D = D_MODEL

    def nrm(shape, scale):
        return jax.random.normal(next(ks), shape, F32) * scale

    def gain(shape):
        return 1.0 + nrm(shape, 0.05)

    NR, NM = N_RWKV_LAYERS, N_MOBA_LAYERS
    return {
        "x": nrm((BATCH, SEQ, D), 1.0),
        "mem": nrm((BATCH, MEM_TOKENS, D), 1.0),
        "positions": jnp.tile(jnp.arange(SEQ, dtype=jnp.int32)[None, :], (BATCH, 1)),
        "ln_mix": gain((DEPTH, D)),
        "ln_mem": gain((DEPTH, D)),
        "ln_memkv": gain((DEPTH, D)),
        "ln_ffn": gain((DEPTH, D)),
        "rw_mu": jax.random.uniform(next(ks), (NR, 6, D), F32),
        "rw_w0": nrm((NR, D), 0.5),
        "rw_w1": nrm((NR, D, RWKV_DECAY_LORA), D ** -0.5),
        "rw_w2": nrm((NR, RWKV_DECAY_LORA, D), 0.1 * RWKV_DECAY_LORA ** -0.5),
        "rw_a0": nrm((NR, D), 0.3),
        "rw_a1": nrm((NR, D, RWKV_AAA_LORA), D ** -0.5),
        "rw_a2": nrm((NR, RWKV_AAA_LORA, D), 0.1 * RWKV_AAA_LORA ** -0.5),
        "rw_g1": nrm((NR, D, RWKV_GATE_LORA), D ** -0.5),
        "rw_g2": nrm((NR, RWKV_GATE_LORA, D), RWKV_GATE_LORA ** -0.5),
        "rw_kk": 0.85 + nrm((NR, D), 0.05),
        "rw_ka": 1.0 + nrm((NR, D), 0.05),
        "rw_rk": nrm((NR, N_HEADS, HEAD_DIM), 0.3),
        "rw_wrkv": nrm((NR, 3, D, D), D ** -0.5),
        "rw_lnx_g": gain((NR, D)),
        "rw_lnx_b": nrm((NR, D), 0.01),
        "rw_wo": nrm((NR, D, D), D ** -0.5),
        "mb_wqkv": nrm((NM, D, 3 * D), D ** -0.5),
        "mb_wo": nrm((NM, D, D), D ** -0.5),
        "mx_wq": nrm((DEPTH, D, D), D ** -0.5),
        "mx_wkv": nrm((DEPTH, D, 2 * D), D ** -0.5),
        "mx_wo": nrm((DEPTH, D, D), D ** -0.5),
        "moe_wg": nrm((DEPTH, D, MOE_GROUPS), D ** -0.5),
        "moe_bg": nrm((DEPTH, MOE_GROUPS), 0.01),
        "moe_we": nrm((DEPTH, D, MOE_EXPERTS), D ** -0.5),
        "moe_be": nrm((DEPTH, MOE_EXPERTS), 0.01),
        "moe_w1": nrm((DEPTH, MOE_EXPERTS, D, MOE_FF), D ** -0.5),
        "moe_w3": nrm((DEPTH, MOE_EXPERTS, D, MOE_FF), D ** -0.5),
        "moe_w2": nrm((DEPTH, MOE_EXPERTS, MOE_FF, D), MOE_FF ** -0.5),
        "ln_f": gain((D,)),
    }


def reference(x, mem, positions, ln_mix, ln_mem, ln_memkv, ln_ffn,
              rw_mu, rw_w0, rw_w1, rw_w2, rw_a0, rw_a1, rw_a2, rw_g1, rw_g2, rw_kk, rw_ka, rw_rk,
              rw_wrkv, rw_lnx_g, rw_lnx_b, rw_wo,
              mb_wqkv, mb_wo,
              mx_wq, mx_wkv, mx_wo,
              moe_wg, moe_bg, moe_we, moe_be, moe_w1, moe_w3, moe_w2,
              ln_f):
    B, T, C = x.shape
    for i in range(DEPTH):
        j = i // N_MIXERS
        h = rms_norm(x, ln_mix[i])
        if i % N_MIXERS == 0:
            mixed = rwkv7_time_mix(h, rw_mu[j], rw_w0[j], rw_w1[j], rw_w2[j], rw_a0[j], rw_a1[j], rw_a2[j],
                                   rw_g1[j], rw_g2[j], rw_kk[j], rw_ka[j], rw_rk[j], rw_wrkv[j],
                                   rw_lnx_g[j], rw_lnx_b[j], rw_wo[j])
        else:
            mixed = moba_attention(h, positions, mb_wqkv[j], mb_wo[j])
        x = x + mixed
        x = x + mem_cross_attention(rms_norm(x, ln_mem[i]), rms_norm(mem, ln_memkv[i]),
                                    mx_wq[i], mx_wkv[i], mx_wo[i])
        h2 = rms_norm(x, ln_ffn[i]).reshape(B * T, C)
        x = x + hierarchical_moe(h2, moe_wg[i], moe_bg[i], moe_we[i], moe_be[i],
                                 moe_w1[i], moe_w3[i], moe_w2[i]).reshape(B, T, C)
    return rms_norm(x, ln_f)
```

```python
import functools
import math

import jax
import jax.numpy as jnp
from jax import lax
from jax.experimental import pallas as pl
from jax.experimental.pallas import tpu as pltpu

F32 = jnp.float32
BF16 = jnp.bfloat16
I32 = jnp.int32

D_MODEL = 1024
HEAD_DIM = 64
PAIR = 2 * HEAD_DIM
N_PAIRS = D_MODEL // PAIR
RWKV_GN_EPS = 64e-5
RWKV_CHUNK = 64
MOBA_BLOCK = 256
MOBA_TOPK = 3
ROPE_THETA = 10000.0
MEM_HEADS = 4
MEM_HEAD_DIM = D_MODEL // MEM_HEADS
MOE_GROUPS = 4
MOE_EPG = 8
MOE_EXPERTS = MOE_GROUPS * MOE_EPG
MOE_FF = D_MODEL // 2
RMS_EPS = 1e-6
NEG_INF = -1e30

LANES = 128
ROW_ALIGN = 8
MOE_TILE = 512
MOE_ROWS = 256
MOE_SORTED = 2 * MOE_TILE + MOE_EXPERTS * (ROW_ALIGN - 1) + (-(2 * MOE_TILE + MOE_EXPERTS * (ROW_ALIGN - 1))) % 16
RUN_CHUNKS = tuple(ROW_ALIGN << s for s in range((MOE_TILE // ROW_ALIGN).bit_length() - 1, -1, -1))
VMEM_LIMIT = 56 * 2 ** 20


def _params(*sem):
    return pltpu.CompilerParams(dimension_semantics=sem, vmem_limit_bytes=VMEM_LIMIT)


def _iota(shape, dim):
    return lax.broadcasted_iota(I32, shape, dim)


def _dot(a, b):
    return jnp.dot(a, b, preferred_element_type=F32)


def _dot_nt(a, b):
    return lax.dot_general(a, b, (((1,), (1,)), ((), ())), preferred_element_type=F32)


def _dot_tn(a, b):
    return lax.dot_general(a, b, (((0,), (0,)), ((), ())), preferred_element_type=F32)


def _split(x):
    hi = x.astype(BF16)
    lo = (x - hi.astype(F32)).astype(BF16)
    return hi, lo


def _rms(x, g):
    return x * lax.rsqrt(jnp.mean(x * x, axis=-1, keepdims=True) + RMS_EPS) * g


def _norm_linear_kernel(x_ref, g_ref, w_ref, o_ref):
    xn = _rms(x_ref[...], g_ref[...]).astype(BF16)
    o_ref[...] = _dot(xn, w_ref[...]).astype(o_ref.dtype)


def norm_linear(x, g, w, tm=256):
    n, k = x.shape
    dout = w.shape[1]
    return pl.pallas_call(
        _norm_linear_kernel,
        grid=(n // tm,),
        in_specs=[pl.BlockSpec((tm, k), lambda i: (i, 0)),
                  pl.BlockSpec((1, k), lambda i: (0, 0)),
                  pl.BlockSpec((k, dout), lambda i: (0, 0))],
        out_specs=pl.BlockSpec((tm, dout), lambda i: (i, 0)),
        out_shape=jax.ShapeDtypeStruct((n, dout), BF16),
        compiler_params=_params("parallel"),
        name="norm_linear",
    )(x, g.reshape(1, k), w)


def _linear_residual_kernel(y_ref, w_ref, r_ref, o_ref):
    o_ref[...] = r_ref[...] + _dot(y_ref[...], w_ref[...])


def linear_residual(y, w, res, tm=512):
    n, k = y.shape
    dout = w.shape[1]
    return pl.pallas_call(
        _linear_residual_kernel,
        grid=(n // tm,),
        in_specs=[pl.BlockSpec((tm, k), lambda i: (i, 0)),
                  pl.BlockSpec((k, dout), lambda i: (0, 0)),
                  pl.BlockSpec((tm, dout), lambda i: (i, 0))],
        out_specs=pl.BlockSpec((tm, dout), lambda i: (i, 0)),
        out_shape=jax.ShapeDtypeStruct((n, dout), F32),
        compiler_params=_params("parallel"),
        name="linear_residual",
    )(y, w, res)


def _head_sum(x_sq):
    r = _iota((PAIR, PAIR), 0) // HEAD_DIM
    c = _iota((PAIR, PAIR), 1) // HEAD_DIM
    bd = jnp.where(r == c, 1.0, 0.0).astype(BF16)
    hi, lo = _split(x_sq)
    return _dot(hi, bd) + _dot(lo, bd)


def _rwkv_mix_kernel(seq_len, x_ref, xp_ref, g_ref, mu_ref, vec_ref, wr_ref, wk_ref, wv_ref,
                     l1_ref, w2_ref, a2_ref, g2_ref,
                     r_out, k_out, v_out, kk_out, akk_out, lw_out, g_out):
    i = pl.program_id(0)
    tm = x_ref.shape[0]
    gain = g_ref[...]
    h = _rms(x_ref[...], gain)
    hp = _rms(xp_ref[...], gain)[ROW_ALIGN - 1:ROW_ALIGN, :]
    hp = jnp.where((i * tm) % seq_len == 0, 0.0, hp)
    hs = pltpu.roll(h, 1, 0)
    hs = jnp.where(_iota((tm, 1), 0) == 0, hp, hs)
    dx = hs - h
    mu = mu_ref[...]

    def mix(s):
        return (h + dx * mu[s:s + 1, :]).astype(BF16)

    vec = vec_ref[...]
    w0, a0, k_k, k_a = vec[0:1], vec[1:2], vec[2:3], vec[3:4]
    l1 = l1_ref[...]
    r = _dot(mix(0), wr_ref[...])
    k = _dot(mix(2), wk_ref[...])
    v = _dot(mix(3), wv_ref[...])
    tw = jnp.tanh(_dot(mix(1), l1[:, 0:64])).astype(BF16)
    w = -jax.nn.softplus(-(w0 + _dot(tw, w2_ref[...]))) - 0.5
    ta = _dot(mix(4), l1[:, 64:128]).astype(BF16)
    a = jax.nn.sigmoid(a0 + _dot(ta, a2_ref[...]))
    tg = jax.nn.sigmoid(_dot(mix(5), l1[:, 128:256])).astype(BF16)
    g = _dot(tg, g2_ref[...])

    kk = k * k_k
    for p in range(N_PAIRS):
        sl = slice(p * PAIR, (p + 1) * PAIR)
        kkp = kk[:, sl]
        nrm = jnp.maximum(jnp.sqrt(_head_sum(kkp * kkp)), 1e-12)
        kkn = kkp / nrm
        kk_out[:, sl] = kkn.astype(kk_out.dtype)
        akk_out[:, sl] = (kkn * a[:, sl]).astype(akk_out.dtype)
    r_out[...] = r.astype(r_out.dtype)
    k_out[...] = (k * (1.0 + (a - 1.0) * k_a)).astype(k_out.dtype)
    v_out[...] = v.astype(v_out.dtype)
    lw_out[...] = -jnp.exp(w)
    g_out[...] = g.astype(g_out.dtype)


def rwkv_mix(x, seq_len, gain, mu, w0, w1, w2, a0, a1, a2, g1, g2, k_k, k_a, w_rkv, tm=256):
    n, d = x.shape
    mu8 = jnp.zeros((8, d), F32).at[:6].set(mu)
    vec = jnp.zeros((8, d), F32).at[0].set(w0).at[1].set(a0).at[2].set(k_k).at[3].set(k_a)
    l1 = jnp.concatenate([w1, a1, g1], axis=1).astype(BF16)
    wb = w_rkv.astype(BF16)
    row = pl.BlockSpec((tm, d), lambda i: (i, 0))
    full = lambda a: pl.BlockSpec(a.shape, lambda i: (0,) * a.ndim)
    args = (x, x, gain.reshape(1, d), mu8, vec, wb[0], wb[1], wb[2], l1,
            w2.astype(BF16), a2.astype(BF16), g2.astype(BF16))
    in_specs = [row, pl.BlockSpec((ROW_ALIGN, d), lambda i: (jnp.maximum(i * (tm // ROW_ALIGN) - 1, 0), 0))]
    in_specs += [full(a) for a in args[2:]]
    outs = [jax.ShapeDtypeStruct((n, d), BF16)] * 5 + [jax.ShapeDtypeStruct((n, d), F32),
                                                       jax.ShapeDtypeStruct((n, d), BF16)]
    return pl.pallas_call(
        functools.partial(_rwkv_mix_kernel, seq_len),
        grid=(n // tm,),
        in_specs=in_specs,
        out_specs=[row] * 7,
        out_shape=outs,
        compiler_params=_params("parallel"),
        name="rwkv_mix",
    )(*args)


def _rwkv_rec_kernel(r_ref, k_ref, v_ref, kk_ref, akk_ref, lw_ref, g_ref, vec_ref, y_ref, s_ref):
    c = pl.program_id(1)
    L = RWKV_CHUNK

    @pl.when(c == 0)
    def _():
        s_ref[...] = jnp.zeros_like(s_ref)

    lane = _iota((1, PAIR), 1)
    m0 = lane < HEAD_DIM
    ri = _iota((2 * L, 2 * L), 0)
    ci = _iota((2 * L, 2 * L), 1)
    same = (ri // L) == (ci // L)
    strict = same & (ci < ri)
    incl = same & (ci <= ri)
    eye = jnp.where(ri == ci, 1.0, 0.0)
    tri = jnp.where(_iota((L, L), 1) <= _iota((L, L), 0), 1.0, 0.0).astype(BF16)
    rb = _iota((PAIR, PAIR), 0) // HEAD_DIM
    cb = _iota((PAIR, PAIR), 1) // HEAD_DIM
    bd = rb == cb

    def stack(x):
        return jnp.concatenate([jnp.where(m0, x, 0.0), jnp.where(m0, 0.0, x)], axis=0)

    def fold(x):
        return x[:L] + x[L:]

    def head_mean(x):
        s0 = jnp.sum(jnp.where(m0, x, 0.0), axis=-1, keepdims=True)
        s1 = jnp.sum(jnp.where(m0, 0.0, x), axis=-1, keepdims=True)
        return jnp.where(m0, s0, s1) * (1.0 / HEAD_DIM)

    vec = vec_ref[...]
    for p in range(N_PAIRS):
        sl = slice(p * PAIR, (p + 1) * PAIR)
        r = r_ref[:, sl].astype(F32)
        k = k_ref[:, sl].astype(F32)
        v = v_ref[:, sl].astype(F32)
        kk = kk_ref[:, sl].astype(F32)
        akk = akk_ref[:, sl].astype(F32)
        lw = lw_ref[:, sl]
        lw_hi, lw_lo = _split(lw)
        cum = _dot(tri, lw_hi) + _dot(tri, lw_lo)
        dec = jnp.exp(cum)
        inv = jnp.exp(-cum)
        dec_prev = jnp.exp(cum - lw)
        dec_all = dec[L - 1:L, :]

        x_kap = stack(kk * dec_prev)
        x_r = stack(r * dec)
        a_t = akk * inv
        k_t = k * inv
        y_a = jnp.concatenate([a_t, a_t], axis=0)
        y_k = jnp.concatenate([k_t, k_t], axis=0)
        xs = jnp.concatenate([x_kap, x_r], axis=0).astype(BF16)
        ys = jnp.concatenate([y_a, y_k], axis=0).astype(BF16)
        sc = _dot_nt(xs, ys)
        m_ka = jnp.where(strict, sc[:2 * L, :2 * L], 0.0)
        m_kk = jnp.where(strict, sc[:2 * L, 2 * L:], 0.0)
        s_ra = jnp.where(incl, sc[2 * L:, :2 * L], 0.0)
        s_rk = jnp.where(incl, sc[2 * L:, 2 * L:], 0.0)

        mp = -m_ka
        t_inv = eye + mp
        for _ in range(int(math.log2(L)) - 1):
            mpb = mp.astype(BF16)
            mp = _dot(mpb, mpb)
            t_inv = t_inv + _dot(t_inv.astype(BF16), mp.astype(BF16))

        v_st = stack(v).astype(BF16)
        b1 = _dot(m_kk.astype(BF16), v_st)
        rhs = jnp.concatenate([x_kap, b1], axis=1).astype(BF16)
        wu = _dot(t_inv.astype(BF16), rhs)
        corr = _dot(s_ra.astype(BF16), wu.astype(BF16))
        r_hat = fold(x_r - corr[:, :PAIR])
        o0 = fold(_dot(s_rk.astype(BF16), v_st) - corr[:, PAIR:])
        w_f = fold(wu[:, :PAIR]).astype(BF16)
        u0_f = fold(wu[:, PAIR:]).astype(BF16)
        a_b = a_t.astype(BF16)
        g_m = jnp.where(bd, _dot_tn(a_b, w_f), 0.0)
        h0t = jnp.where(bd, _dot_tn(v.astype(BF16), k_t.astype(BF16)) - _dot_tn(u0_f, a_b), 0.0)

        s = s_ref[p]
        sb = s.astype(BF16)
        o = _dot_nt(r_hat.astype(BF16), sb) + o0
        s_ref[p] = (s - _dot_nt(sb, g_m.astype(BF16)) + h0t) * dec_all

        mean = head_mean(o)
        cen = o - mean
        var = head_mean(cen * cen)
        gn = cen * lax.rsqrt(var + RWKV_GN_EPS) * vec[0:1, sl] + vec[1:2, sl]
        bonus = head_mean(r * k * vec[2:3, sl]) * HEAD_DIM * v
        y_ref[:, sl] = ((gn + bonus) * g_ref[:, sl].astype(F32)).astype(y_ref.dtype)


def rwkv_recurrence(r, k, v, kk, akk, lw, g, lnx_g, lnx_b, r_k, batch, seq_len):
    n, d = r.shape
    L = RWKV_CHUNK
    nc = seq_len // L
    vec = jnp.zeros((8, d), F32).at[0].set(lnx_g).at[1].set(lnx_b).at[2].set(r_k.reshape(d))
    blk = pl.BlockSpec((L, d), lambda b, c: (b * nc + c, 0))
    return pl.pallas_call(
        _rwkv_rec_kernel,
        grid=(batch, nc),
        in_specs=[blk] * 7 + [pl.BlockSpec((8, d), lambda b, c: (0, 0))],
        out_specs=blk,
        out_shape=jax.ShapeDtypeStruct((n, d), BF16),
        scratch_shapes=[pltpu.VMEM((N_PAIRS, PAIR, PAIR), F32)],
        compiler_params=_params("parallel", "arbitrary"),
        name="rwkv_recurrence",
    )(r, k, v, kk, akk, lw, g, vec)


def _qkv_rope_kernel(x_ref, g_ref, pos_ref, inv_ref, w_ref, o_ref, xn_ref, cs_ref):
    j = pl.program_id(1)
    tm = x_ref.shape[0]

    @pl.when(j == 0)
    def _():
        xn_ref[...] = _rms(x_ref[...], g_ref[...]).astype(BF16)
        ang = pos_ref[...].astype(F32) * inv_ref[...]
        first = (_iota((1, PAIR), 1) % HEAD_DIM) < HEAD_DIM // 2
        sn = jnp.sin(ang)
        cs_ref[0] = jnp.cos(ang)
        cs_ref[1] = jnp.where(first, -sn, 0.0)
        cs_ref[2] = jnp.where(first, 0.0, sn)

    y = _dot(xn_ref[...], w_ref[...])

    @pl.when(j == 2)
    def _():
        o_ref[...] = y.astype(o_ref.dtype)

    @pl.when(j < 2)
    def _():
        scale = jnp.where(j == 0, 1.0 / math.sqrt(HEAD_DIM), 1.0)
        for p in range(N_PAIRS):
            sl = slice(p * PAIR, (p + 1) * PAIR)
            t = y[:, sl]
            rot = (t * cs_ref[0] + pltpu.roll(t, PAIR - HEAD_DIM // 2, 1) * cs_ref[1]
                   + pltpu.roll(t, HEAD_DIM // 2, 1) * cs_ref[2])
            o_ref[:, sl] = (rot * scale).astype(o_ref.dtype)


def qkv_rope(x, gain, positions, w_qkv, tm=512):
    n, d = x.shape
    half = HEAD_DIM // 2
    inv = ROPE_THETA ** (-jnp.arange(half, dtype=F32) * 2.0 / HEAD_DIM)
    inv128 = jnp.tile(inv, PAIR // half).reshape(1, PAIR)
    return pl.pallas_call(
        _qkv_rope_kernel,
        grid=(n // tm, 3),
        in_specs=[pl.BlockSpec((tm, d), lambda i, j: (i, 0)),
                  pl.BlockSpec((1, d), lambda i, j: (0, 0)),
                  pl.BlockSpec((tm, 1), lambda i, j: (i, 0)),
                  pl.BlockSpec((1, PAIR), lambda i, j: (0, 0)),
                  pl.BlockSpec((d, d), lambda i, j: (0, j))],
        out_specs=pl.BlockSpec((tm, d), lambda i, j: (i, j)),
        out_shape=jax.ShapeDtypeStruct((n, 3 * d), BF16),
        scratch_shapes=[pltpu.VMEM((tm, d), BF16), pltpu.VMEM((3, tm, PAIR), F32)],
        compiler_params=_params("parallel", "arbitrary"),
        name="qkv_rope",
    )(x, gain.reshape(1, d), positions.reshape(n, 1), inv128, w_qkv.astype(BF16))


def _moba_kernel(q_ref, k_ref, v_ref, o_ref, km_ref):
    qi = pl.program_id(2)
    nblk = k_ref.shape[0] // MOBA_BLOCK
    bq = MOBA_BLOCK

    @pl.when(qi == 0)
    def _():
        km_ref[...] = jnp.zeros_like(km_ref)
        for n in range(nblk):
            kb = k_ref[n * bq:(n + 1) * bq, :].astype(F32)
            km_ref[n:n + 1, :] = jnp.mean(kb, axis=0, keepdims=True)

    lane = _iota((1, PAIR), 1)
    m0 = lane < HEAD_DIM
    q = q_ref[...]
    zero = jnp.zeros_like(q)
    qh = (jnp.where(m0, q, zero), jnp.where(m0, zero, q))
    km_hi, km_lo = _split(km_ref[...])
    past = lane < qi

    def select_bias(qhead):
        gate = _dot_nt(qhead, km_hi) + _dot_nt(qhead, km_lo)
        gate = jnp.where(past, gate, NEG_INF)
        rank = jnp.zeros(gate.shape, F32)
        for m in range(nblk):
            col = gate[:, m:m + 1]
            beats = (col > gate) | ((col == gate) & (m < lane))
            rank = rank + jnp.where(beats, 1.0, 0.0)
        return jnp.where(past & (rank < MOBA_TOPK), 0.0, NEG_INF)

    bias = (select_bias(qh[0]), select_bias(qh[1]))

    def attend(h, kb, vb, mask, carry):
        m_old, l_old, acc = carry
        s = _dot_nt(qh[h], kb)
        s = jnp.where(mask, s, NEG_INF)
        m_new = jnp.maximum(m_old, jnp.max(s, axis=-1, keepdims=True))
        alpha = jnp.exp(m_old - m_new)
        pr = jnp.exp(s - m_new)
        l_new = alpha * l_old + jnp.sum(pr, axis=-1, keepdims=True)
        acc = alpha * acc + _dot(pr.astype(BF16), vb)
        return m_new, l_new, acc

    start = pl.multiple_of(qi * bq, bq)
    k_own = k_ref[pl.ds(start, bq), :]
    v_own = v_ref[pl.ds(start, bq), :]
    causal = _iota((bq, bq), 1) <= _iota((bq, bq), 0)
    init = (jnp.full((bq, 1), NEG_INF, F32), jnp.zeros((bq, 1), F32), jnp.zeros((bq, PAIR), F32))
    carry = tuple(attend(h, k_own, v_own, causal, init) for h in range(2))

    def body(n, carry):
        off = pl.multiple_of(n * bq, bq)
        kb = k_ref[pl.ds(off, bq), :]
        vb = v_ref[pl.ds(off, bq), :]
        out = []
        for h in range(2):
            bcol = jnp.max(jnp.where(lane == n, bias[h], NEG_INF), axis=-1, keepdims=True)
            out.append(attend(h, kb, vb, bcol == 0.0, carry[h]))
        return tuple(out)

    carry = lax.fori_loop(0, qi, body, carry)
    o0 = carry[0][2] / carry[0][1]
    o1 = carry[1][2] / carry[1][1]
    o_ref[...] = jnp.where(m0, o0, o1).astype(o_ref.dtype)


def moba_attention(qkv, batch, seq_len):
    n = qkv.shape[0]
    nb = seq_len // MOBA_BLOCK
    return pl.pallas_call(
        _moba_kernel,
        grid=(batch, N_PAIRS, nb),
        in_specs=[pl.BlockSpec((MOBA_BLOCK, PAIR), lambda b, p, q: (b * nb + q, p)),
                  pl.BlockSpec((seq_len, PAIR), lambda b, p, q: (b, N_PAIRS + p)),
                  pl.BlockSpec((seq_len, PAIR), lambda b, p, q: (b, 2 * N_PAIRS + p))],
        out_specs=pl.BlockSpec((MOBA_BLOCK, PAIR), lambda b, p, q: (b * nb + q, p)),
        out_shape=jax.ShapeDtypeStruct((n, D_MODEL), BF16),
        scratch_shapes=[pltpu.VMEM((LANES, PAIR), F32)],
        compiler_params=_params("parallel", "parallel", "arbitrary"),
        name="moba_attention",
    )(qkv, qkv, qkv)


def _mem_xattn_kernel(x_ref, g_ref, wq_ref, kv_ref, wo_ref, o_ref):
    x = x_ref[...]
    xn = _rms(x, g_ref[...]).astype(BF16)
    q = (_dot(xn, wq_ref[...]) * (1.0 / math.sqrt(MEM_HEAD_DIM))).astype(BF16)
    outs = []
    for h in range(MEM_HEADS):
        sl = slice(h * MEM_HEAD_DIM, (h + 1) * MEM_HEAD_DIM)
        s = _dot_nt(q[:, sl], kv_ref[:, sl])
        s = s - jnp.max(s, axis=-1, keepdims=True)
        e = jnp.exp(s)
        pr = e / jnp.sum(e, axis=-1, keepdims=True)
        vh = kv_ref[:, D_MODEL + h * MEM_HEAD_DIM:D_MODEL + (h + 1) * MEM_HEAD_DIM]
        outs.append(_dot(pr.astype(BF16), vh).astype(BF16))
    o = jnp.concatenate(outs, axis=1)
    o_ref[...] = x + _dot(o, wo_ref[...])


def mem_cross_attention(x, gain, wq, kv, wo, seq_len, tm=256):
    n, d = x.shape
    m = kv.shape[0] // (n // seq_len)
    per_seq = seq_len // tm
    return pl.pallas_call(
        _mem_xattn_kernel,
        grid=(n // tm,),
        in_specs=[pl.BlockSpec((tm, d), lambda i: (i, 0)),
                  pl.BlockSpec((1, d), lambda i: (0, 0)),
                  pl.BlockSpec((d, d), lambda i: (0, 0)),
                  pl.BlockSpec((m, 2 * d), lambda i: (i // per_seq, 0)),
                  pl.BlockSpec((d, d), lambda i: (0, 0))],
        out_specs=pl.BlockSpec((tm, d), lambda i: (i, 0)),
        out_shape=jax.ShapeDtypeStruct((n, d), F32),
        compiler_params=_params("parallel"),
        name="mem_cross_attention",
    )(x, gain.reshape(1, d), wq.astype(BF16), kv, wo.astype(BF16))


def _router_kernel(x_ref, g_ref, w_ref, b_ref, h_out, meta_out, cnt_out):
    tm = x_ref.shape[0]
    h2 = _rms(x_ref[...], g_ref[...])
    hi = h2.astype(BF16)
    h_out[...] = hi
    lo = (h2 - hi.astype(F32)).astype(BF16)
    w_hi, w_lo = _split(w_ref[...])
    lg = _dot_nt(w_hi, hi) + _dot_nt(w_hi, lo) + _dot_nt(w_lo, hi) + b_ref[...]
    row = _iota((8, tm), 0).astype(F32)

    def first_argmax(val, vmax):
        return jnp.min(jnp.where(val == vmax, row, 8.0), axis=0, keepdims=True)

    gl = jnp.where(row < MOE_GROUPS, lg[0:8], -jnp.inf)
    gmax = jnp.max(gl, axis=0, keepdims=True)
    p_g = 1.0 / jnp.sum(jnp.exp(gl - gmax), axis=0, keepdims=True)
    gidx = first_argmax(gl, gmax)
    el = jnp.zeros((8, tm), F32)
    for g in range(MOE_GROUPS):
        el = el + jnp.where(gidx == g, lg[8 + 8 * g:16 + 8 * g], 0.0)
    ee = jnp.exp(el - jnp.max(el, axis=0, keepdims=True))
    pe = ee / jnp.sum(ee, axis=0, keepdims=True)
    p1 = jnp.max(pe, axis=0, keepdims=True)
    i1 = first_argmax(pe, p1)
    pe2 = jnp.where(row == i1, -1.0, pe)
    p2 = jnp.max(pe2, axis=0, keepdims=True)
    i2 = first_argmax(pe2, p2)
    e1 = gidx * MOE_EPG + i1
    e2 = gidx * MOE_EPG + i2
    gate1 = p_g * p1 / (p1 + p2)
    gate2 = p_g * p2 / (p1 + p2)

    erow = _iota((MOE_EXPERTS, tm), 0).astype(F32)
    oh1 = erow == e1
    oh2 = erow == e2
    oh = jnp.where(oh1 | oh2, 1.0, 0.0)
    before = jnp.where(_iota((tm, tm), 0) < _iota((tm, tm), 1), 1.0, 0.0).astype(BF16)
    cnt_before = _dot(oh.astype(BF16), before)
    lr1 = jnp.sum(jnp.where(oh1, cnt_before, 0.0), axis=0, keepdims=True)
    lr2 = jnp.sum(jnp.where(oh2, cnt_before, 0.0), axis=0, keepdims=True)
    zero = jnp.zeros((1, tm), F32)
    meta_out[...] = jnp.concatenate([e1, e2, gate1, gate2, lr1, lr2, zero, zero], axis=0)
    cnt_out[...] = jnp.broadcast_to(jnp.sum(oh, axis=1, keepdims=True), (MOE_EXPERTS, LANES))


def moe_router(x, gain, w_grp, b_grp, w_exp, b_exp):
    n, d = x.shape
    tm = MOE_TILE
    wt = jnp.zeros((LANES, d), F32).at[0:MOE_GROUPS].set(w_grp.T).at[8:8 + MOE_EXPERTS].set(w_exp.T)
    bt = jnp.zeros((LANES, 1), F32).at[0:MOE_GROUPS, 0].set(b_grp).at[8:8 + MOE_EXPERTS, 0].set(b_exp)
    return pl.pallas_call(
        _router_kernel,
        grid=(n // tm,),
        in_specs=[pl.BlockSpec((tm, d), lambda i: (i, 0)),
                  pl.BlockSpec((1, d), lambda i: (0, 0)),
                  pl.BlockSpec((LANES, d), lambda i: (0, 0)),
                  pl.BlockSpec((LANES, 1), lambda i: (0, 0))],
        out_specs=[pl.BlockSpec((tm, d), lambda i: (i, 0)),
                   pl.BlockSpec((8, tm), lambda i: (0, i)),
                   pl.BlockSpec((MOE_EXPERTS, LANES), lambda i: (i, 0))],
        out_shape=[jax.ShapeDtypeStruct((n, d), BF16),
                   jax.ShapeDtypeStruct((8, n), F32),
                   jax.ShapeDtypeStruct((n // tm * MOE_EXPERTS, LANES), F32)],
        compiler_params=_params("parallel"),
        name="moe_router",
    )(x, gain.reshape(1, d), wt, bt)


def _run_copies(tab_ref, t, make_copy, act):
    def body(e, _):
        base = (t * MOE_EXPERTS + e) * 3
        length = tab_ref[base]
        src = tab_ref[base + 1]
        dst = tab_ref[base + 2]
        for ck in RUN_CHUNKS:
            @pl.when((length & ck) != 0)
            def _():
                off = length & (-2 * ck)
                act(make_copy(pl.multiple_of(src + off, ROW_ALIGN), pl.multiple_of(dst + off, ROW_ALIGN), ck))
        return 0
    lax.fori_loop(0, MOE_EXPERTS, body, 0)


def _perm_matrix(pos_ref):
    r = _iota((MOE_SORTED, MOE_TILE), 0)
    return (r == pos_ref[0:1, :]) | (r == pos_ref[1:2, :])


def _dispatch_kernel(tab_ref, h_ref, pos_ref, rows_in, rows_hbm, sorted_ref, sem):
    del rows_in
    t = pl.program_id(0)
    perm = jnp.where(_perm_matrix(pos_ref), 1.0, 0.0).astype(BF16)
    sorted_ref[...] = _dot(perm, h_ref[...])

    def make_copy(src, dst, ck):
        return pltpu.make_async_copy(sorted_ref.at[pl.ds(src, ck)], rows_hbm.at[pl.ds(dst, ck)], sem)

    _run_copies(tab_ref, t, make_copy, lambda cp: cp.start())
    _run_copies(tab_ref, t, make_copy, lambda cp: cp.wait())


def moe_dispatch(tab, h2, pos, n_rows):
    n, d = h2.shape
    tm = MOE_TILE
    zeros = jnp.zeros((n_rows, d), F32)
    return pl.pallas_call(
        _dispatch_kernel,
        grid_spec=pltpu.PrefetchScalarGridSpec(
            num_scalar_prefetch=1,
            grid=(n // tm,),
            in_specs=[pl.BlockSpec((tm, d), lambda i, tab: (i, 0)),
                      pl.BlockSpec((8, tm), lambda i, tab: (0, i)),
                      pl.BlockSpec(memory_space=pl.ANY)],
            out_specs=pl.BlockSpec(memory_space=pl.ANY),
            scratch_shapes=[pltpu.VMEM((MOE_SORTED, d), F32), pltpu.SemaphoreType.DMA],
        ),
        out_shape=jax.ShapeDtypeStruct((n_rows, d), F32),
        input_output_aliases={3: 0},
        compiler_params=_params("arbitrary"),
        name="moe_dispatch",
    )(tab, h2, pos, zeros)


def _expert_kernel(be_ref, nu_ref, x_ref, w1_ref, w3_ref, w2_ref, y_ref):
    b = pl.program_id(0)

    @pl.when(b < nu_ref[0])
    def _():
        xb = x_ref[...].astype(BF16)
        a = _dot(xb, w1_ref[0])
        c = _dot(xb, w3_ref[0])
        hid = (a * jax.nn.sigmoid(a) * c).astype(BF16)
        y_ref[...] = _dot(hid, w2_ref[0])

    @pl.when(b >= nu_ref[0])
    def _():
        y_ref[...] = jnp.zeros_like(y_ref)


def moe_experts(block_expert, n_used, rows, w1, w3, w2):
    n_rows, d = rows.shape
    nb = n_rows // MOE_ROWS
    ff = w1.shape[2]

    def xmap(b, be, nu):
        return (jnp.minimum(b, nu[0] - 1), 0)

    def wmap(b, be, nu):
        return (be[jnp.minimum(b, nu[0] - 1)], 0, 0)

    return pl.pallas_call(
        _expert_kernel,
        grid_spec=pltpu.PrefetchScalarGridSpec(
            num_scalar_prefetch=2,
            grid=(nb,),
            in_specs=[pl.BlockSpec((MOE_ROWS, d), xmap),
                      pl.BlockSpec((1, d, ff), wmap),
                      pl.BlockSpec((1, d, ff), wmap),
                      pl.BlockSpec((1, ff, d), wmap)],
            out_specs=pl.BlockSpec((MOE_ROWS, d), lambda b, be, nu: (b, 0)),
        ),
        out_shape=jax.ShapeDtypeStruct((n_rows, d), F32),
        compiler_params=_params("arbitrary"),
        name="moe_experts",
    )(block_expert, n_used, rows, w1, w3, w2)


def _combine_kernel(final_norm, tab_ref, y_hbm, pos_ref, gate_ref, x_ref, g_ref, o_ref, ys_ref, sem):
    t = pl.program_id(0)

    @pl.when(t == 0)
    def _():
        ys_ref[...] = jnp.zeros_like(ys_ref)

    def make_copy(src, dst, ck):
        return pltpu.make_async_copy(y_hbm.at[pl.ds(dst, ck)], ys_ref.at[pl.ds(src, ck)], sem)

    _run_copies(tab_ref, t, make_copy, lambda cp: cp.start())
    r = _iota((MOE_SORTED, MOE_TILE), 0)
    hit1 = r == pos_ref[0:1, :]
    hit2 = r == pos_ref[1:2, :]
    wgt = jnp.where(hit1, gate_ref[2:3, :], 0.0) + jnp.where(hit2, gate_ref[3:4, :], 0.0)
    row_gate = jnp.sum(wgt, axis=1, keepdims=True)
    row_used = jnp.sum(jnp.where(hit1 | hit2, 1.0, 0.0), axis=1, keepdims=True) > 0.0
    perm = jnp.where(hit1 | hit2, 1.0, 0.0).astype(BF16)
    _run_copies(tab_ref, t, make_copy, lambda cp: cp.wait())
    ys = jnp.where(row_used, ys_ref[...] * row_gate, 0.0).astype(BF16)
    out = x_ref[...] + _dot_tn(perm, ys)
    if final_norm:
        out = _rms(out, g_ref[...])
    o_ref[...] = out


def moe_combine(tab, y, pos, meta, x, final_gain):
    n, d = x.shape
    tm = MOE_TILE
    final_norm = final_gain is not None
    gain = (final_gain if final_norm else jnp.ones((d,), F32)).reshape(1, d)
    return pl.pallas_call(
        functools.partial(_combine_kernel, final_norm),
        grid_spec=pltpu.PrefetchScalarGridSpec(
            num_scalar_prefetch=1,
            grid=(n // tm,),
            in_specs=[pl.BlockSpec(memory_space=pl.ANY),
                      pl.BlockSpec((8, tm), lambda i, tab: (0, i)),
                      pl.BlockSpec((8, tm), lambda i, tab: (0, i)),
                      pl.BlockSpec((tm, d), lambda i, tab: (i, 0)),
                      pl.BlockSpec((1, d), lambda i, tab: (0, 0))],
            out_specs=pl.BlockSpec((tm, d), lambda i, tab: (i, 0)),
            scratch_shapes=[pltpu.VMEM((MOE_SORTED, d), F32), pltpu.SemaphoreType.DMA],
        ),
        out_shape=jax.ShapeDtypeStruct((n, d), F32),
        compiler_params=_params("arbitrary"),
        name="moe_combine",
    )(tab, y, pos, meta, x, gain)


def hierarchical_moe(x, gain, w_grp, b_grp, w_exp, b_exp, w1, w3, w2, final_gain):
    n, d = x.shape
    tm = MOE_TILE
    nt = n // tm
    h2, meta, cnt = moe_router(x, gain, w_grp, b_grp, w_exp, b_exp)

    cnt = cnt.reshape(nt, MOE_EXPERTS, LANES)[:, :, 0].astype(I32)
    run = (cnt + ROW_ALIGN - 1) // ROW_ALIGN * ROW_ALIGN
    src = jnp.cumsum(run, axis=1) - run
    before = jnp.cumsum(run, axis=0) - run
    total = jnp.sum(run, axis=0)
    padded = (total + MOE_ROWS - 1) // MOE_ROWS * MOE_ROWS
    pad_end = jnp.cumsum(padded)
    dst = (pad_end - padded)[None, :] + before
    tab = jnp.stack([run, src, dst], axis=-1).reshape(-1).astype(I32)
    max_rows = 2 * n + nt * MOE_EXPERTS * (ROW_ALIGN - 1) + MOE_EXPERTS * (MOE_ROWS - 1)
    nb = -(-max_rows // MOE_ROWS)
    n_used = (pad_end[-1] // MOE_ROWS).astype(I32).reshape(1)
    block_expert = jnp.minimum(
        jnp.searchsorted(pad_end, jnp.arange(nb, dtype=I32) * MOE_ROWS, side='right'), MOE_EXPERTS - 1).astype(I32)
    tile_id = jnp.arange(n, dtype=I32) // tm
    e = meta[0:2].astype(I32)
    pos = src[tile_id[None, :], e] + meta[4:6].astype(I32)
    pos8 = jnp.zeros((8, n), I32).at[0:2].set(pos)

    rows = moe_dispatch(tab, h2, pos8, nb * MOE_ROWS)
    y = moe_experts(block_expert, n_used, rows, w1.astype(BF16), w3.astype(BF16), w2.astype(BF16))
    return moe_combine(tab, y, pos8, meta, x, final_gain)


def kernel(x, mem, positions, ln_mix, ln_mem, ln_memkv, ln_ffn, rw_mu, rw_w0, rw_w1, rw_w2, rw_a0, rw_a1, rw_a2, rw_g1, rw_g2, rw_kk, rw_ka, rw_rk, rw_wrkv, rw_lnx_g, rw_lnx_b, rw_wo, mb_wqkv, mb_wo, mx_wq, mx_wkv, mx_wo, moe_wg, moe_bg, moe_we, moe_be, moe_w1, moe_w3, moe_w2, ln_f):
    B, T, C = x.shape
    n = B * T
    depth = ln_mix.shape[0]
    xf = x.reshape(n, C)
    memf = mem.reshape(-1, C)
    for i in range(depth):
        j = i // 2
        if i % 2 == 0:
            r, k, v, kk, akk, lw, g = rwkv_mix(xf, T, ln_mix[i], rw_mu[j], rw_w0[j], rw_w1[j], rw_w2[j],
                                               rw_a0[j], rw_a1[j], rw_a2[j], rw_g1[j], rw_g2[j],
                                               rw_kk[j], rw_ka[j], rw_wrkv[j])
            y = rwkv_recurrence(r, k, v, kk, akk, lw, g, rw_lnx_g[j], rw_lnx_b[j], rw_rk[j], B, T)
            xf = linear_residual(y, rw_wo[j].astype(BF16), xf)
        else:
            qkv = qkv_rope(xf, ln_mix[i], positions, mb_wqkv[j])
            o = moba_attention(qkv, B, T)
            xf = linear_residual(o, mb_wo[j].astype(BF16), xf)
        kv = norm_linear(memf, ln_memkv[i], mx_wkv[i].astype(BF16))
        xf = mem_cross_attention(xf, ln_mem[i], mx_wq[i], kv, mx_wo[i], T)
        xf = hierarchical_moe(xf, ln_ffn[i], moe_wg[i], moe_bg[i], moe_we[i], moe_be[i],
                              moe_w1[i], moe_w3[i], moe_w2[i], ln_f if i == depth - 1 else None)
    return xf.reshape(B, T, C)
```

```python
import functools
import math

import jax
import jax.numpy as jnp
from jax import lax
from jax.experimental import pallas as pl
from jax.experimental.pallas import tpu as pltpu

F32 = jnp.float32
BF16 = jnp.bfloat16
I32 = jnp.int32

D_MODEL = 1024
HEAD_DIM = 64
PAIR = 2 * HEAD_DIM
N_PAIRS = D_MODEL // PAIR
RWKV_GN_EPS = 64e-5
RWKV_CHUNK = 64
MOBA_BLOCK = 256
MOBA_TOPK = 3
ROPE_THETA = 10000.0
MEM_HEADS = 4
MEM_HEAD_DIM = D_MODEL // MEM_HEADS
MOE_GROUPS = 4
MOE_EPG = 8
MOE_EXPERTS = MOE_GROUPS * MOE_EPG
MOE_FF = D_MODEL // 2
RMS_EPS = 1e-6
NEG_INF = -1e30

LANES = 128
ROW_ALIGN = 8
MOE_TILE = 512
MOE_ROWS = 256
MOE_SORTED = 2 * MOE_TILE + MOE_EXPERTS * (ROW_ALIGN - 1) + (-(2 * MOE_TILE + MOE_EXPERTS * (ROW_ALIGN - 1))) % 16
RUN_CHUNKS = tuple(ROW_ALIGN << s for s in range((MOE_TILE // ROW_ALIGN).bit_length() - 1, -1, -1))
VMEM_LIMIT = 56 * 2 ** 20


def _params(*sem):
    return pltpu.CompilerParams(dimension_semantics=sem, vmem_limit_bytes=VMEM_LIMIT)


def _iota(shape, dim):
    return lax.broadcasted_iota(I32, shape, dim)


def _dot(a, b):
    return jnp.dot(a, b, preferred_element_type=F32)


def _dot_nt(a, b):
    return lax.dot_general(a, b, (((1,), (1,)), ((), ())), preferred_element_type=F32)


def _dot_tn(a, b):
    return lax.dot_general(a, b, (((0,), (0,)), ((), ())), preferred_element_type=F32)


def _split(x):
    hi = x.astype(BF16)
    lo = (x - hi.astype(F32)).astype(BF16)
    return hi, lo


def _rms(x, g):
    return x * lax.rsqrt(jnp.mean(x * x, axis=-1, keepdims=True) + RMS_EPS) * g


def _norm_linear_kernel(x_ref, g_ref, w_ref, o_ref):
    xn = _rms(x_ref[...], g_ref[...]).astype(BF16)
    o_ref[...] = _dot(xn, w_ref[...]).astype(o_ref.dtype)


def norm_linear(x, g, w, tm=256):
    n, k = x.shape
    dout = w.shape[1]
    return pl.pallas_call(
        _norm_linear_kernel,
        grid=(n // tm,),
        in_specs=[pl.BlockSpec((tm, k), lambda i: (i, 0)),
                  pl.BlockSpec((1, k), lambda i: (0, 0)),
                  pl.BlockSpec((k, dout), lambda i: (0, 0))],
        out_specs=pl.BlockSpec((tm, dout), lambda i: (i, 0)),
        out_shape=jax.ShapeDtypeStruct((n, dout), BF16),
        compiler_params=_params("parallel"),
        name="norm_linear",
    )(x, g.reshape(1, k), w)


def _linear_residual_kernel(y_ref, w_ref, r_ref, o_ref):
    o_ref[...] = r_ref[...] + _dot(y_ref[...], w_ref[...])


def linear_residual(y, w, res, tm=512):
    n, k = y.shape
    dout = w.shape[1]
    return pl.pallas_call(
        _linear_residual_kernel,
        grid=(n // tm,),
        in_specs=[pl.BlockSpec((tm, k), lambda i: (i, 0)),
                  pl.BlockSpec((k, dout), lambda i: (0, 0)),
                  pl.BlockSpec((tm, dout), lambda i: (i, 0))],
        out_specs=pl.BlockSpec((tm, dout), lambda i: (i, 0)),
        out_shape=jax.ShapeDtypeStruct((n, dout), F32),
        compiler_params=_params("parallel"),
        name="linear_residual",
    )(y, w, res)


def _head_sum(x_sq):
    r = _iota((PAIR, PAIR), 0) // HEAD_DIM
    c = _iota((PAIR, PAIR), 1) // HEAD_DIM
    bd = jnp.where(r == c, 1.0, 0.0).astype(BF16)
    hi, lo = _split(x_sq)
    return _dot(hi, bd) + _dot(lo, bd)


def _rwkv_mix_kernel(seq_len, x_ref, xp_ref, g_ref, mu_ref, vec_ref, wr_ref, wk_ref, wv_ref,
                     l1_ref, w2_ref, a2_ref, g2_ref,
                     r_out, k_out, v_out, kk_out, akk_out, lw_out, g_out):
    i = pl.program_id(0)
    tm = x_ref.shape[0]
    gain = g_ref[...]
    h = _rms(x_ref[...], gain)
    hp = _rms(xp_ref[...], gain)[ROW_ALIGN - 1:ROW_ALIGN, :]
    hp = jnp.where((i * tm) % seq_len == 0, 0.0, hp)
    hs = pltpu.roll(h, 1, 0)
    hs = jnp.where(_iota((tm, 1), 0) == 0, hp, hs)
    dx = hs - h
    mu = mu_ref[...]

    def mix(s):
        return (h + dx * mu[s:s + 1, :]).astype(BF16)

    vec = vec_ref[...]
    w0, a0, k_k, k_a = vec[0:1], vec[1:2], vec[2:3], vec[3:4]
    l1 = l1_ref[...]
    r = _dot(mix(0), wr_ref[...])
    k = _dot(mix(2), wk_ref[...])
    v = _dot(mix(3), wv_ref[...])
    tw = jnp.tanh(_dot(mix(1), l1[:, 0:64])).astype(BF16)
    w = -jax.nn.softplus(-(w0 + _dot(tw, w2_ref[...]))) - 0.5
    ta = _dot(mix(4), l1[:, 64:128]).astype(BF16)
    a = jax.nn.sigmoid(a0 + _dot(ta, a2_ref[...]))
    tg = jax.nn.sigmoid(_dot(mix(5), l1[:, 128:256])).astype(BF16)
    g = _dot(tg, g2_ref[...])

    kk = k * k_k
    for p in range(N_PAIRS):
        sl = slice(p * PAIR, (p + 1) * PAIR)
        kkp = kk[:, sl]
        nrm = jnp.maximum(jnp.sqrt(_head_sum(kkp * kkp)), 1e-12)
        kkn = kkp / nrm
        kk_out[:, sl] = kkn.astype(kk_out.dtype)
        akk_out[:, sl] = (kkn * a[:, sl]).astype(akk_out.dtype)
    r_out[...] = r.astype(r_out.dtype)
    k_out[...] = (k * (1.0 + (a - 1.0) * k_a)).astype(k_out.dtype)
    v_out[...] = v.astype(v_out.dtype)
    lw_out[...] = -jnp.exp(w)
    g_out[...] = g.astype(g_out.dtype)


def rwkv_mix(x, seq_len, gain, mu, w0, w1, w2, a0, a1, a2, g1, g2, k_k, k_a, w_rkv, tm=256):
    n, d = x.shape
    mu8 = jnp.zeros((8, d), F32).at[:6].set(mu)
    vec = jnp.zeros((8, d), F32).at[0].set(w0).at[1].set(a0).at[2].set(k_k).at[3].set(k_a)
    l1 = jnp.concatenate([w1, a1, g1], axis=1).astype(BF16)
    wb = w_rkv.astype(BF16)
    row = pl.BlockSpec((tm, d), lambda i: (i, 0))
    full = lambda a: pl.BlockSpec(a.shape, lambda i: (0,) * a.ndim)
    args = (x, x, gain.reshape(1, d), mu8, vec, wb[0], wb[1], wb[2], l1,
            w2.astype(BF16), a2.astype(BF16), g2.astype(BF16))
    in_specs = [row, pl.BlockSpec((ROW_ALIGN, d), lambda i: (jnp.maximum(i * (tm // ROW_ALIGN) - 1, 0), 0))]
    in_specs += [full(a) for a in args[2:]]
    outs = [jax.ShapeDtypeStruct((n, d), BF16)] * 5 + [jax.ShapeDtypeStruct((n, d), F32),
                                                       jax.ShapeDtypeStruct((n, d), BF16)]
    return pl.pallas_call(
        functools.partial(_rwkv_mix_kernel, seq_len),
        grid=(n // tm,),
        in_specs=in_specs,
        out_specs=[row] * 7,
        out_shape=outs,
        compiler_params=_params("parallel"),
        name="rwkv_mix",
    )(*args)


def _rwkv_rec_kernel(r_ref, k_ref, v_ref, kk_ref, akk_ref, lw_ref, g_ref, vec_ref, y_ref, s_ref):
    c = pl.program_id(1)
    L = RWKV_CHUNK

    @pl.when(c == 0)
    def _():
        s_ref[...] = jnp.zeros_like(s_ref)

    lane = _iota((1, PAIR), 1)
    m0 = lane < HEAD_DIM
    ri = _iota((2 * L, 2 * L), 0)
    ci = _iota((2 * L, 2 * L), 1)
    same = (ri // L) == (ci // L)
    strict = same & (ci < ri)
    incl = same & (ci <= ri)
    eye = jnp.where(ri == ci, 1.0, 0.0)
    tri = jnp.where(_iota((L, L), 1) <= _iota((L, L), 0), 1.0, 0.0).astype(BF16)
    rb = _iota((PAIR, PAIR), 0) // HEAD_DIM
    cb = _iota((PAIR, PAIR), 1) // HEAD_DIM
    bd = rb == cb

    def stack(x):
        return jnp.concatenate([jnp.where(m0, x, 0.0), jnp.where(m0, 0.0, x)], axis=0)

    def fold(x):
        return x[:L] + x[L:]

    def head_mean(x):
        s0 = jnp.sum(jnp.where(m0, x, 0.0), axis=-1, keepdims=True)
        s1 = jnp.sum(jnp.where(m0, 0.0, x), axis=-1, keepdims=True)
        return jnp.where(m0, s0, s1) * (1.0 / HEAD_DIM)

    vec = vec_ref[...]
    pairs = range(N_PAIRS)
    sls = [slice(p * PAIR, (p + 1) * PAIR) for p in pairs]
    lw = [lw_ref[:, sl] for sl in sls]
    cum = []
    for p in pairs:
        lw_hi, lw_lo = _split(lw[p])
        cum.append(_dot(tri, lw_hi) + _dot(tri, lw_lo))
    dec_all, x_kap, x_r, a_t, k_t, sc = [], [], [], [], [], []
    for p in pairs:
        sl = sls[p]
        dec = jnp.exp(cum[p])
        inv = jnp.exp(-cum[p])
        dec_prev = jnp.exp(cum[p] - lw[p])
        dec_all.append(dec[L - 1:L, :])
        x_kap.append(stack(kk_ref[:, sl].astype(F32) * dec_prev).astype(BF16))
        x_r.append(stack(r_ref[:, sl].astype(F32) * dec))
        a_t.append((akk_ref[:, sl].astype(F32) * inv).astype(BF16))
        k_t.append((k_ref[:, sl].astype(F32) * inv).astype(BF16))
        xs = jnp.concatenate([x_kap[p], x_r[p].astype(BF16)], axis=0)
        ys = jnp.concatenate([a_t[p], a_t[p], k_t[p], k_t[p]], axis=0)
        sc.append(_dot_nt(xs, ys))
    mp = [jnp.where(strict, -sc[p][:2 * L, :2 * L], 0.0) for p in pairs]
    m_kk = [jnp.where(strict, sc[p][:2 * L, 2 * L:], 0.0).astype(BF16) for p in pairs]
    s_ra = [jnp.where(incl, sc[p][2 * L:, :2 * L], 0.0).astype(BF16) for p in pairs]
    s_rk = [jnp.where(incl, sc[p][2 * L:, 2 * L:], 0.0).astype(BF16) for p in pairs]
    v_st = [stack(v_ref[:, sl].astype(F32)).astype(BF16) for sl in sls]
    b1 = [_dot(m_kk[p], v_st[p]) for p in pairs]
    o1 = [_dot(s_rk[p], v_st[p]) for p in pairs]

    t_inv = [eye + mp[p] for p in pairs]
    for _ in range(int(math.log2(L)) - 1):
        mpb = [mp[p].astype(BF16) for p in pairs]
        mp = [_dot(mpb[p], mpb[p]) for p in pairs]
        t_inv = [t_inv[p] + _dot(t_inv[p].astype(BF16), mp[p].astype(BF16)) for p in pairs]

    wu = [_dot(t_inv[p].astype(BF16), jnp.concatenate([x_kap[p], b1[p].astype(BF16)], axis=1))
          for p in pairs]
    corr = [_dot(s_ra[p], wu[p].astype(BF16)) for p in pairs]
    g_m, h0t, r_hat, o0 = [], [], [], []
    for p in pairs:
        r_hat.append(fold(x_r[p] - corr[p][:, :PAIR]).astype(BF16))
        o0.append(fold(o1[p] - corr[p][:, PAIR:]))
        w_f = fold(wu[p][:, :PAIR]).astype(BF16)
        u0_f = fold(wu[p][:, PAIR:]).astype(BF16)
        g_m.append(jnp.where(bd, _dot_tn(a_t[p], w_f), 0.0).astype(BF16))
        h0t.append(jnp.where(bd, _dot_tn(v_ref[:, sls[p]], k_t[p]) - _dot_tn(u0_f, a_t[p]), 0.0))
    o = []
    for p in pairs:
        s = s_ref[p]
        sb = s.astype(BF16)
        o.append(_dot_nt(r_hat[p], sb) + o0[p])
        s_ref[p] = (s - _dot_nt(sb, g_m[p]) + h0t[p]) * dec_all[p]
    for p in pairs:
        sl = sls[p]
        mean = head_mean(o[p])
        cen = o[p] - mean
        var = head_mean(cen * cen)
        gn = cen * lax.rsqrt(var + RWKV_GN_EPS) * vec[0:1, sl] + vec[1:2, sl]
        r = r_ref[:, sl].astype(F32)
        k = k_ref[:, sl].astype(F32)
        bonus = head_mean(r * k * vec[2:3, sl]) * HEAD_DIM * v_ref[:, sl].astype(F32)
        y_ref[:, sl] = ((gn + bonus) * g_ref[:, sl].astype(F32)).astype(y_ref.dtype)


def rwkv_recurrence(r, k, v, kk, akk, lw, g, lnx_g, lnx_b, r_k, batch, seq_len):
    n, d = r.shape
    L = RWKV_CHUNK
    nc = seq_len // L
    vec = jnp.zeros((8, d), F32).at[0].set(lnx_g).at[1].set(lnx_b).at[2].set(r_k.reshape(d))
    blk = pl.BlockSpec((L, d), lambda b, c: (b * nc + c, 0))
    return pl.pallas_call(
        _rwkv_rec_kernel,
        grid=(batch, nc),
        in_specs=[blk] * 7 + [pl.BlockSpec((8, d), lambda b, c: (0, 0))],
        out_specs=blk,
        out_shape=jax.ShapeDtypeStruct((n, d), BF16),
        scratch_shapes=[pltpu.VMEM((N_PAIRS, PAIR, PAIR), F32)],
        compiler_params=_params("parallel", "arbitrary"),
        name="rwkv_recurrence",
    )(r, k, v, kk, akk, lw, g, vec)


def _qkv_rope_kernel(x_ref, g_ref, pos_ref, inv_ref, w_ref, o_ref, xn_ref, cs_ref):
    j = pl.program_id(1)
    tm = x_ref.shape[0]

    @pl.when(j == 0)
    def _():
        xn_ref[...] = _rms(x_ref[...], g_ref[...]).astype(BF16)
        ang = pos_ref[...].astype(F32) * inv_ref[...]
        first = (_iota((1, PAIR), 1) % HEAD_DIM) < HEAD_DIM // 2
        sn = jnp.sin(ang)
        cs_ref[0] = jnp.cos(ang)
        cs_ref[1] = jnp.where(first, -sn, 0.0)
        cs_ref[2] = jnp.where(first, 0.0, sn)

    y = _dot(xn_ref[...], w_ref[...])

    @pl.when(j == 2)
    def _():
        o_ref[...] = y.astype(o_ref.dtype)

    @pl.when(j < 2)
    def _():
        scale = jnp.where(j == 0, 1.0 / math.sqrt(HEAD_DIM), 1.0)
        for p in range(N_PAIRS):
            sl = slice(p * PAIR, (p + 1) * PAIR)
            t = y[:, sl]
            rot = (t * cs_ref[0] + pltpu.roll(t, PAIR - HEAD_DIM // 2, 1) * cs_ref[1]
                   + pltpu.roll(t, HEAD_DIM // 2, 1) * cs_ref[2])
            o_ref[:, sl] = (rot * scale).astype(o_ref.dtype)


def qkv_rope(x, gain, positions, w_qkv, tm=512):
    n, d = x.shape
    half = HEAD_DIM // 2
    inv = ROPE_THETA ** (-jnp.arange(half, dtype=F32) * 2.0 / HEAD_DIM)
    inv128 = jnp.tile(inv, PAIR // half).reshape(1, PAIR)
    return pl.pallas_call(
        _qkv_rope_kernel,
        grid=(n // tm, 3),
        in_specs=[pl.BlockSpec((tm, d), lambda i, j: (i, 0)),
                  pl.BlockSpec((1, d), lambda i, j: (0, 0)),
                  pl.BlockSpec((tm, 1), lambda i, j: (i, 0)),
                  pl.BlockSpec((1, PAIR), lambda i, j: (0, 0)),
                  pl.BlockSpec((d, d), lambda i, j: (0, j))],
        out_specs=pl.BlockSpec((tm, d), lambda i, j: (i, j)),
        out_shape=jax.ShapeDtypeStruct((n, 3 * d), BF16),
        scratch_shapes=[pltpu.VMEM((tm, d), BF16), pltpu.VMEM((3, tm, PAIR), F32)],
        compiler_params=_params("parallel", "arbitrary"),
        name="qkv_rope",
    )(x, gain.reshape(1, d), positions.reshape(n, 1), inv128, w_qkv.astype(BF16))


def _moba_kernel(q_ref, k_ref, v_ref, o_ref, km_ref, vt_ref, bias_ref):
    qi = pl.program_id(2)
    nblk = k_ref.shape[0] // MOBA_BLOCK
    bq = MOBA_BLOCK

    @pl.when(qi == 0)
    def _():
        km_ref[...] = jnp.zeros_like(km_ref)
        for n in range(nblk):
            kb = k_ref[n * bq:(n + 1) * bq, :].astype(F32)
            km_ref[n:n + 1, :] = jnp.mean(kb, axis=0, keepdims=True)
            vt_ref[:, n * bq:(n + 1) * bq] = v_ref[n * bq:(n + 1) * bq, :].astype(F32).T.astype(BF16)

    lane = _iota((1, PAIR), 1)
    m0 = lane < HEAD_DIM
    q = q_ref[...]
    zero = jnp.zeros_like(q)
    qh = (jnp.where(m0, q, zero), jnp.where(m0, zero, q))
    km_hi, km_lo = _split(km_ref[...])
    blk = _iota((km_ref.shape[0], bq), 0)
    past = blk < qi

    for h in range(2):
        gate = _dot_nt(km_hi, qh[h]) + _dot_nt(km_lo, qh[h])
        gate = jnp.where(past, gate, NEG_INF)
        rank = jnp.zeros(gate.shape, F32)
        for m in range(nblk):
            row = gate[m:m + 1, :]
            tie = jnp.where(m < blk, 1.0, 0.0)
            rank = rank + jnp.where(row > gate, 1.0, jnp.where(row == gate, tie, 0.0))
        bias_ref[h] = jnp.where(past & (rank < MOBA_TOPK), 0.0, NEG_INF)

    causal = jnp.where(_iota((bq, bq), 0) <= _iota((bq, bq), 1), 0.0, NEG_INF)

    def attend_blocks(c):
        nk = (c + 1) * bq
        k_all = k_ref[0:nk, :]
        scores = [_dot_nt(k_all, qh[h]) for h in range(2)]
        outs = []
        for h in range(2):
            parts = [scores[h][n * bq:(n + 1) * bq] + bias_ref[h, n:n + 1, :] for n in range(c)]
            parts.append(scores[h][c * bq:] + causal)
            top = parts[0].max(axis=0, keepdims=True)
            for part in parts[1:]:
                top = jnp.maximum(top, part.max(axis=0, keepdims=True))
            probs = [jnp.exp(part - top) for part in parts]
            denom = sum(pr.sum(axis=0, keepdims=True) for pr in probs)
            p_all = jnp.concatenate([pr.astype(BF16) for pr in probs], axis=0)
            acc = _dot(vt_ref[h * HEAD_DIM:(h + 1) * HEAD_DIM, 0:nk], p_all)
            outs.append(acc / denom)
        o_ref[...] = jnp.concatenate(outs, axis=0).T.astype(o_ref.dtype)

    for c in range(nblk):
        pl.when(qi == c)(functools.partial(attend_blocks, c))


def moba_attention(qkv, batch, seq_len):
    n = qkv.shape[0]
    nb = seq_len // MOBA_BLOCK
    return pl.pallas_call(
        _moba_kernel,
        grid=(batch, N_PAIRS, nb),
        in_specs=[pl.BlockSpec((MOBA_BLOCK, PAIR), lambda b, p, q: (b * nb + q, p)),
                  pl.BlockSpec((seq_len, PAIR), lambda b, p, q: (b, N_PAIRS + p)),
                  pl.BlockSpec((seq_len, PAIR), lambda b, p, q: (b, 2 * N_PAIRS + p))],
        out_specs=pl.BlockSpec((MOBA_BLOCK, PAIR), lambda b, p, q: (b * nb + q, p)),
        out_shape=jax.ShapeDtypeStruct((n, D_MODEL), BF16),
        scratch_shapes=[pltpu.VMEM((16, PAIR), F32),
                        pltpu.VMEM((PAIR, seq_len), BF16),
                        pltpu.VMEM((2, 16, MOBA_BLOCK), F32)],
        compiler_params=_params("parallel", "parallel", "arbitrary"),
        name="moba_attention",
    )(qkv, qkv, qkv)


def _mem_xattn_kernel(x_ref, g_ref, wq_ref, kv_ref, wo_ref, o_ref):
    x = x_ref[...]
    xn = _rms(x, g_ref[...]).astype(BF16)
    q = (_dot(xn, wq_ref[...]) * (1.0 / math.sqrt(MEM_HEAD_DIM))).astype(BF16)
    outs = []
    for h in range(MEM_HEADS):
        sl = slice(h * MEM_HEAD_DIM, (h + 1) * MEM_HEAD_DIM)
        s = _dot_nt(q[:, sl], kv_ref[:, sl])
        s = s - jnp.max(s, axis=-1, keepdims=True)
        e = jnp.exp(s)
        pr = e / jnp.sum(e, axis=-1, keepdims=True)
        vh = kv_ref[:, D_MODEL + h * MEM_HEAD_DIM:D_MODEL + (h + 1) * MEM_HEAD_DIM]
        outs.append(_dot(pr.astype(BF16), vh).astype(BF16))
    o = jnp.concatenate(outs, axis=1)
    o_ref[...] = x + _dot(o, wo_ref[...])


def mem_cross_attention(x, gain, wq, kv, wo, seq_len, tm=256):
    n, d = x.shape
    m = kv.shape[0] // (n // seq_len)
    per_seq = seq_len // tm
    return pl.pallas_call(
        _mem_xattn_kernel,
        grid=(n // tm,),
        in_specs=[pl.BlockSpec((tm, d), lambda i: (i, 0)),
                  pl.BlockSpec((1, d), lambda i: (0, 0)),
                  pl.BlockSpec((d, d), lambda i: (0, 0)),
                  pl.BlockSpec((m, 2 * d), lambda i: (i // per_seq, 0)),
                  pl.BlockSpec((d, d), lambda i: (0, 0))],
        out_specs=pl.BlockSpec((tm, d), lambda i: (i, 0)),
        out_shape=jax.ShapeDtypeStruct((n, d), F32),
        compiler_params=_params("parallel"),
        name="mem_cross_attention",
    )(x, gain.reshape(1, d), wq.astype(BF16), kv, wo.astype(BF16))


def _router_kernel(x_ref, g_ref, w_ref, b_ref, h_out, meta_out, cnt_out):
    tm = x_ref.shape[0]
    h2 = _rms(x_ref[...], g_ref[...])
    hi = h2.astype(BF16)
    h_out[...] = hi
    lo = (h2 - hi.astype(F32)).astype(BF16)
    w_hi, w_lo = _split(w_ref[...])
    lg = _dot_nt(w_hi, hi) + _dot_nt(w_hi, lo) + _dot_nt(w_lo, hi) + b_ref[...]
    row = _iota((8, tm), 0).astype(F32)

    def first_argmax(val, vmax):
        return jnp.min(jnp.where(val == vmax, row, 8.0), axis=0, keepdims=True)

    gl = jnp.where(row < MOE_GROUPS, lg[0:8], -jnp.inf)
    gmax = jnp.max(gl, axis=0, keepdims=True)
    p_g = 1.0 / jnp.sum(jnp.exp(gl - gmax), axis=0, keepdims=True)
    gidx = first_argmax(gl, gmax)
    el = jnp.zeros((8, tm), F32)
    for g in range(MOE_GROUPS):
        el = el + jnp.where(gidx == g, lg[8 + 8 * g:16 + 8 * g], 0.0)
    ee = jnp.exp(el - jnp.max(el, axis=0, keepdims=True))
    pe = ee / jnp.sum(ee, axis=0, keepdims=True)
    p1 = jnp.max(pe, axis=0, keepdims=True)
    i1 = first_argmax(pe, p1)
    pe2 = jnp.where(row == i1, -1.0, pe)
    p2 = jnp.max(pe2, axis=0, keepdims=True)
    i2 = first_argmax(pe2, p2)
    e1 = gidx * MOE_EPG + i1
    e2 = gidx * MOE_EPG + i2
    gate1 = p_g * p1 / (p1 + p2)
    gate2 = p_g * p2 / (p1 + p2)

    erow = _iota((MOE_EXPERTS, tm), 0).astype(F32)
    oh1 = erow == e1
    oh2 = erow == e2
    oh = jnp.where(oh1 | oh2, 1.0, 0.0)
    before = jnp.where(_iota((tm, tm), 0) < _iota((tm, tm), 1), 1.0, 0.0).astype(BF16)
    cnt_before = _dot(oh.astype(BF16), before)
    lr1 = jnp.sum(jnp.where(oh1, cnt_before, 0.0), axis=0, keepdims=True)
    lr2 = jnp.sum(jnp.where(oh2, cnt_before, 0.0), axis=0, keepdims=True)
    zero = jnp.zeros((1, tm), F32)
    meta_out[...] = jnp.concatenate([e1, e2, gate1, gate2, lr1, lr2, zero, zero], axis=0)
    cnt_out[...] = jnp.broadcast_to(jnp.sum(oh, axis=1, keepdims=True), (MOE_EXPERTS, LANES))


def moe_router(x, gain, w_grp, b_grp, w_exp, b_exp):
    n, d = x.shape
    tm = MOE_TILE
    wt = jnp.zeros((LANES, d), F32).at[0:MOE_GROUPS].set(w_grp.T).at[8:8 + MOE_EXPERTS].set(w_exp.T)
    bt = jnp.zeros((LANES, 1), F32).at[0:MOE_GROUPS, 0].set(b_grp).at[8:8 + MOE_EXPERTS, 0].set(b_exp)
    return pl.pallas_call(
        _router_kernel,
        grid=(n // tm,),
        in_specs=[pl.BlockSpec((tm, d), lambda i: (i, 0)),
                  pl.BlockSpec((1, d), lambda i: (0, 0)),
                  pl.BlockSpec((LANES, d), lambda i: (0, 0)),
                  pl.BlockSpec((LANES, 1), lambda i: (0, 0))],
        out_specs=[pl.BlockSpec((tm, d), lambda i: (i, 0)),
                   pl.BlockSpec((8, tm), lambda i: (0, i)),
                   pl.BlockSpec((MOE_EXPERTS, LANES), lambda i: (i, 0))],
        out_shape=[jax.ShapeDtypeStruct((n, d), BF16),
                   jax.ShapeDtypeStruct((8, n), F32),
                   jax.ShapeDtypeStruct((n // tm * MOE_EXPERTS, LANES), F32)],
        compiler_params=_params("parallel"),
        name="moe_router",
    )(x, gain.reshape(1, d), wt, bt)


def _run_copies(tab_ref, t, make_copy, act):
    def body(e, _):
        base = (t * MOE_EXPERTS + e) * 3
        length = tab_ref[base]
        src = tab_ref[base + 1]
        dst = tab_ref[base + 2]
        for ck in RUN_CHUNKS:
            @pl.when((length & ck) != 0)
            def _():
                off = length & (-2 * ck)
                act(make_copy(pl.multiple_of(src + off, ROW_ALIGN), pl.multiple_of(dst + off, ROW_ALIGN), ck))
        return 0
    lax.fori_loop(0, MOE_EXPERTS, body, 0)


def _perm_matrix(pos_ref):
    r = _iota((MOE_SORTED, MOE_TILE), 0)
    return (r == pos_ref[0:1, :]) | (r == pos_ref[1:2, :])


def _dispatch_kernel(tab_ref, h_ref, pos_ref, rows_in, rows_hbm, sorted_ref, sem):
    del rows_in
    t = pl.program_id(0)
    perm = jnp.where(_perm_matrix(pos_ref), 1.0, 0.0).astype(BF16)
    sorted_ref[...] = _dot(perm, h_ref[...])

    def make_copy(src, dst, ck):
        return pltpu.make_async_copy(sorted_ref.at[pl.ds(src, ck)], rows_hbm.at[pl.ds(dst, ck)], sem)

    _run_copies(tab_ref, t, make_copy, lambda cp: cp.start())
    _run_copies(tab_ref, t, make_copy, lambda cp: cp.wait())


def moe_dispatch(tab, h2, pos, n_rows):
    n, d = h2.shape
    tm = MOE_TILE
    zeros = jnp.zeros((n_rows, d), F32)
    return pl.pallas_call(
        _dispatch_kernel,
        grid_spec=pltpu.PrefetchScalarGridSpec(
            num_scalar_prefetch=1,
            grid=(n // tm,),
            in_specs=[pl.BlockSpec((tm, d), lambda i, tab: (i, 0)),
                      pl.BlockSpec((8, tm), lambda i, tab: (0, i)),
                      pl.BlockSpec(memory_space=pl.ANY)],
            out_specs=pl.BlockSpec(memory_space=pl.ANY),
            scratch_shapes=[pltpu.VMEM((MOE_SORTED, d), F32), pltpu.SemaphoreType.DMA],
        ),
        out_shape=jax.ShapeDtypeStruct((n_rows, d), F32),
        input_output_aliases={3: 0},
        compiler_params=_params("arbitrary"),
        name="moe_dispatch",
    )(tab, h2, pos, zeros)


def _expert_kernel(be_ref, nu_ref, x_ref, w1_ref, w3_ref, w2_ref, y_ref):
    b = pl.program_id(0)

    @pl.when(b < nu_ref[0])
    def _():
        xb = x_ref[...].astype(BF16)
        a = _dot(xb, w1_ref[0])
        c = _dot(xb, w3_ref[0])
        hid = (a * jax.nn.sigmoid(a) * c).astype(BF16)
        y_ref[...] = _dot(hid, w2_ref[0])

    @pl.when(b >= nu_ref[0])
    def _():
        y_ref[...] = jnp.zeros_like(y_ref)


def moe_experts(block_expert, n_used, rows, w1, w3, w2):
    n_rows, d = rows.shape
    nb = n_rows // MOE_ROWS
    ff = w1.shape[2]

    def xmap(b, be, nu):
        return (jnp.minimum(b, nu[0] - 1), 0)

    def wmap(b, be, nu):
        return (be[jnp.minimum(b, nu[0] - 1)], 0, 0)

    return pl.pallas_call(
        _expert_kernel,
        grid_spec=pltpu.PrefetchScalarGridSpec(
            num_scalar_prefetch=2,
            grid=(nb,),
            in_specs=[pl.BlockSpec((MOE_ROWS, d), xmap),
                      pl.BlockSpec((1, d, ff), wmap),
                      pl.BlockSpec((1, d, ff), wmap),
                      pl.BlockSpec((1, ff, d), wmap)],
            out_specs=pl.BlockSpec((MOE_ROWS, d), lambda b, be, nu: (b, 0)),
        ),
        out_shape=jax.ShapeDtypeStruct((n_rows, d), F32),
        compiler_params=_params("arbitrary"),
        name="moe_experts",
    )(block_expert, n_used, rows, w1, w3, w2)


def _combine_kernel(final_norm, tab_ref, y_hbm, pos_ref, gate_ref, x_ref, g_ref, o_ref, ys_ref, sem):
    t = pl.program_id(0)

    @pl.when(t == 0)
    def _():
        ys_ref[...] = jnp.zeros_like(ys_ref)

    def make_copy(src, dst, ck):
        return pltpu.make_async_copy(y_hbm.at[pl.ds(dst, ck)], ys_ref.at[pl.ds(src, ck)], sem)

    _run_copies(tab_ref, t, make_copy, lambda cp: cp.start())
    r = _iota((MOE_SORTED, MOE_TILE), 0)
    hit1 = r == pos_ref[0:1, :]
    hit2 = r == pos_ref[1:2, :]
    wgt = jnp.where(hit1, gate_ref[2:3, :], 0.0) + jnp.where(hit2, gate_ref[3:4, :], 0.0)
    row_gate = jnp.sum(wgt, axis=1, keepdims=True)
    row_used = jnp.sum(jnp.where(hit1 | hit2, 1.0, 0.0), axis=1, keepdims=True) > 0.0
    perm = jnp.where(hit1 | hit2, 1.0, 0.0).astype(BF16)
    _run_copies(tab_ref, t, make_copy, lambda cp: cp.wait())
    ys = jnp.where(row_used, ys_ref[...] * row_gate, 0.0).astype(BF16)
    out = x_ref[...] + _dot_tn(perm, ys)
    if final_norm:
        out = _rms(out, g_ref[...])
    o_ref[...] = out


def moe_combine(tab, y, pos, meta, x, final_gain):
    n, d = x.shape
    tm = MOE_TILE
    final_norm = final_gain is not None
    gain = (final_gain if final_norm else jnp.ones((d,), F32)).reshape(1, d)
    return pl.pallas_call(
        functools.partial(_combine_kernel, final_norm),
        grid_spec=pltpu.PrefetchScalarGridSpec(
            num_scalar_prefetch=1,
            grid=(n // tm,),
            in_specs=[pl.BlockSpec(memory_space=pl.ANY),
                      pl.BlockSpec((8, tm), lambda i, tab: (0, i)),
                      pl.BlockSpec((8, tm), lambda i, tab: (0, i)),
                      pl.BlockSpec((tm, d), lambda i, tab: (i, 0)),
                      pl.BlockSpec((1, d), lambda i, tab: (0, 0))],
            out_specs=pl.BlockSpec((tm, d), lambda i, tab: (i, 0)),
            scratch_shapes=[pltpu.VMEM((MOE_SORTED, d), F32), pltpu.SemaphoreType.DMA],
        ),
        out_shape=jax.ShapeDtypeStruct((n, d), F32),
        compiler_params=_params("arbitrary"),
        name="moe_combine",
    )(tab, y, pos, meta, x, gain)


def hierarchical_moe(x, gain, w_grp, b_grp, w_exp, b_exp, w1, w3, w2, final_gain):
    n, d = x.shape
    tm = MOE_TILE
    nt = n // tm
    h2, meta, cnt = moe_router(x, gain, w_grp, b_grp, w_exp, b_exp)

    cnt = cnt.reshape(nt, MOE_EXPERTS, LANES)[:, :, 0].astype(I32)
    run = (cnt + ROW_ALIGN - 1) // ROW_ALIGN * ROW_ALIGN
    src = jnp.cumsum(run, axis=1) - run
    before = jnp.cumsum(run, axis=0) - run
    total = jnp.sum(run, axis=0)
    padded = (total + MOE_ROWS - 1) // MOE_ROWS * MOE_ROWS
    pad_end = jnp.cumsum(padded)
    dst = (pad_end - padded)[None, :] + before
    tab = jnp.stack([run, src, dst], axis=-1).reshape(-1).astype(I32)
    max_rows = 2 * n + nt * MOE_EXPERTS * (ROW_ALIGN - 1) + MOE_EXPERTS * (MOE_ROWS - 1)
    nb = -(-max_rows // MOE_ROWS)
    n_used = (pad_end[-1] // MOE_ROWS).astype(I32).reshape(1)
    block_start = jnp.arange(nb, dtype=I32) * MOE_ROWS
    block_expert = jnp.minimum(
        jnp.sum((pad_end[None, :] <= block_start[:, None]).astype(I32), axis=1), MOE_EXPERTS - 1).astype(I32)
    e = meta[0:2].astype(I32).reshape(2, nt, tm, 1)
    hit = e == jnp.arange(MOE_EXPERTS, dtype=I32)
    pos = jnp.sum(jnp.where(hit, src[None, :, None, :], 0), axis=-1).reshape(2, n) + meta[4:6].astype(I32)
    pos8 = jnp.zeros((8, n), I32).at[0:2].set(pos)

    rows = moe_dispatch(tab, h2, pos8, nb * MOE_ROWS)
    y = moe_experts(block_expert, n_used, rows, w1.astype(BF16), w3.astype(BF16), w2.astype(BF16))
    return moe_combine(tab, y, pos8, meta, x, final_gain)


def kernel(x, mem, positions, ln_mix, ln_mem, ln_memkv, ln_ffn, rw_mu, rw_w0, rw_w1, rw_w2, rw_a0, rw_a1, rw_a2, rw_g1, rw_g2, rw_kk, rw_ka, rw_rk, rw_wrkv, rw_lnx_g, rw_lnx_b, rw_wo, mb_wqkv, mb_wo, mx_wq, mx_wkv, mx_wo, moe_wg, moe_bg, moe_we, moe_be, moe_w1, moe_w3, moe_w2, ln_f):
    B, T, C = x.shape
    n = B * T
    depth = ln_mix.shape[0]
    xf = x.reshape(n, C)
    memf = mem.reshape(-1, C)
    for i in range(depth):
        j = i // 2
        if i % 2 == 0:
            r, k, v, kk, akk, lw, g = rwkv_mix(xf, T, ln_mix[i], rw_mu[j], rw_w0[j], rw_w1[j], rw_w2[j],
                                               rw_a0[j], rw_a1[j], rw_a2[j], rw_g1[j], rw_g2[j],
                                               rw_kk[j], rw_ka[j], rw_wrkv[j])
            y = rwkv_recurrence(r, k, v, kk, akk, lw, g, rw_lnx_g[j], rw_lnx_b[j], rw_rk[j], B, T)
            xf = linear_residual(y, rw_wo[j].astype(BF16), xf)
        else:
            qkv = qkv_rope(xf, ln_mix[i], positions, mb_wqkv[j])
            o = moba_attention(qkv, B, T)
            xf = linear_residual(o, mb_wo[j].astype(BF16), xf)
        kv = norm_linear(memf, ln_memkv[i], mx_wkv[i].astype(BF16))
        xf = mem_cross_attention(xf, ln_mem[i], mx_wq[i], kv, mx_wo[i], T)
        xf = hierarchical_moe(xf, ln_ffn[i], moe_wg[i], moe_bg[i], moe_we[i], moe_be[i],
                              moe_w1[i], moe_w3[i], moe_w2[i], ln_f if i == depth - 1 else None)
    return xf.reshape(B, T, C)
```

```python
import functools
import math

import jax
import jax.numpy as jnp
from jax import lax
from jax.experimental import pallas as pl
from jax.experimental.pallas import tpu as pltpu

F32 = jnp.float32
BF16 = jnp.bfloat16
I32 = jnp.int32
U32 = jnp.uint32

D_MODEL = 1024
HEAD_DIM = 64
PAIR = 2 * HEAD_DIM
N_PAIRS = D_MODEL // PAIR
RWKV_GN_EPS = 64e-5
RWKV_CHUNK = 64
MOBA_BLOCK = 256
MOBA_TOPK = 3
VT_ROWS = HEAD_DIM + 16
ROPE_THETA = 10000.0
MEM_HEADS = 4
MEM_HEAD_DIM = D_MODEL // MEM_HEADS
MOE_GROUPS = 4
MOE_EPG = 8
MOE_EXPERTS = MOE_GROUPS * MOE_EPG
MOE_FF = D_MODEL // 2
RMS_EPS = 1e-6
NEG_INF = -1e30

LANES = 128
ROW_ALIGN = 8
MOE_TILE = 512
MOE_ROWS = 512
MOE_ALIGN = 16
MOE_SORTED = 2 * MOE_TILE + MOE_EXPERTS * (MOE_ALIGN - 1) + (-(2 * MOE_TILE + MOE_EXPERTS * (MOE_ALIGN - 1))) % MOE_ALIGN
RUN_CHUNKS = tuple(MOE_ALIGN << s for s in range((MOE_TILE // MOE_ALIGN).bit_length() - 1, -1, -1))
PAD_ROWS = MOE_ROWS // 2
PAD_CHUNKS = tuple(MOE_ALIGN << s for s in range((PAD_ROWS // MOE_ALIGN).bit_length() - 1, -1, -1))
VMEM_LIMIT = 56 * 2 ** 20


def _params(*sem):
    return pltpu.CompilerParams(dimension_semantics=sem, vmem_limit_bytes=VMEM_LIMIT)


def _iota(shape, dim):
    return lax.broadcasted_iota(I32, shape, dim)


def _dot(a, b):
    return jnp.dot(a, b, preferred_element_type=F32)


def _dot_nt(a, b):
    return lax.dot_general(a, b, (((1,), (1,)), ((), ())), preferred_element_type=F32)


def _dot_tn(a, b):
    return lax.dot_general(a, b, (((0,), (0,)), ((), ())), preferred_element_type=F32)


def _split(x):
    hi = x.astype(BF16)
    lo = (x - hi.astype(F32)).astype(BF16)
    return hi, lo


def _rms(x, g):
    return x * lax.rsqrt(jnp.mean(x * x, axis=-1, keepdims=True) + RMS_EPS) * g


def _norm_linear_kernel(x_ref, g_ref, w_ref, o_ref):
    xn = _rms(x_ref[...], g_ref[...]).astype(BF16)
    o_ref[...] = _dot(xn, w_ref[...]).astype(o_ref.dtype)


def norm_linear(x, g, w, tm=256):
    n, k = x.shape
    dout = w.shape[1]
    return pl.pallas_call(
        _norm_linear_kernel,
        grid=(n // tm,),
        in_specs=[pl.BlockSpec((tm, k), lambda i: (i, 0)),
                  pl.BlockSpec((1, k), lambda i: (0, 0)),
                  pl.BlockSpec((k, dout), lambda i: (0, 0))],
        out_specs=pl.BlockSpec((tm, dout), lambda i: (i, 0)),
        out_shape=jax.ShapeDtypeStruct((n, dout), BF16),
        compiler_params=_params("parallel"),
        name="norm_linear",
    )(x, g.reshape(1, k), w)


def _linear_residual_kernel(y_ref, w_ref, r_ref, o_ref):
    o_ref[...] = r_ref[...] + _dot(y_ref[...], w_ref[...])


def linear_residual(y, w, res, tm=512):
    n, k = y.shape
    dout = w.shape[1]
    return pl.pallas_call(
        _linear_residual_kernel,
        grid=(n // tm,),
        in_specs=[pl.BlockSpec((tm, k), lambda i: (i, 0)),
                  pl.BlockSpec((k, dout), lambda i: (0, 0)),
                  pl.BlockSpec((tm, dout), lambda i: (i, 0))],
        out_specs=pl.BlockSpec((tm, dout), lambda i: (i, 0)),
        out_shape=jax.ShapeDtypeStruct((n, dout), F32),
        compiler_params=_params("parallel"),
        name="linear_residual",
    )(y, w, res)


def _head_sum(x_sq):
    r = _iota((PAIR, PAIR), 0) // HEAD_DIM
    c = _iota((PAIR, PAIR), 1) // HEAD_DIM
    bd = jnp.where(r == c, 1.0, 0.0).astype(BF16)
    hi, lo = _split(x_sq)
    return _dot(hi, bd) + _dot(lo, bd)


def _rwkv_mix_kernel(seq_len, x_ref, xp_ref, g_ref, mu_ref, vec_ref, wr_ref, wk_ref, wv_ref,
                     l1_ref, w2_ref, a2_ref, g2_ref,
                     r_out, k_out, v_out, kk_out, akk_out, lw_out, g_out):
    i = pl.program_id(0)
    tm = x_ref.shape[0]
    gain = g_ref[...]
    h = _rms(x_ref[...], gain)
    hp = _rms(xp_ref[...], gain)[ROW_ALIGN - 1:ROW_ALIGN, :]
    hp = jnp.where((i * tm) % seq_len == 0, 0.0, hp)
    hs = pltpu.roll(h, 1, 0)
    hs = jnp.where(_iota((tm, 1), 0) == 0, hp, hs)
    hb = h.astype(BF16)
    dxb = (hs - h).astype(BF16)
    mub = mu_ref[...].astype(BF16)

    def mix(s):
        return hb + dxb * mub[s:s + 1, :]

    vec = vec_ref[...]
    w0, a0, k_k, k_a = vec[0:1], vec[1:2], vec[2:3], vec[3:4]
    l1 = l1_ref[...]
    r = _dot(mix(0), wr_ref[...])
    k = _dot(mix(2), wk_ref[...])
    v = _dot(mix(3), wv_ref[...])
    tw = jnp.tanh(_dot(mix(1), l1[:, 0:64])).astype(BF16)
    lw = -math.exp(-0.5) * jax.nn.sigmoid(w0 + _dot(tw, w2_ref[...]))
    ta = _dot(mix(4), l1[:, 64:128]).astype(BF16)
    a = jax.nn.sigmoid(a0 + _dot(ta, a2_ref[...]))
    tg = jax.nn.sigmoid(_dot(mix(5), l1[:, 128:256])).astype(BF16)
    g = _dot(tg, g2_ref[...])

    kk = k * k_k
    for p in range(N_PAIRS):
        sl = slice(p * PAIR, (p + 1) * PAIR)
        kkp = kk[:, sl]
        kkn = kkp * lax.rsqrt(jnp.maximum(_head_sum(kkp * kkp), 1e-24))
        kk_out[:, sl] = kkn.astype(kk_out.dtype)
        akk_out[:, sl] = (kkn * a[:, sl]).astype(akk_out.dtype)
    r_out[...] = r.astype(r_out.dtype)
    k_out[...] = (k * (1.0 + (a - 1.0) * k_a)).astype(k_out.dtype)
    v_out[...] = v.astype(v_out.dtype)
    lw_out[...] = lw
    g_out[...] = g.astype(g_out.dtype)


def rwkv_mix(x, seq_len, gain, mu, w0, w1, w2, a0, a1, a2, g1, g2, k_k, k_a, w_rkv, tm=256):
    n, d = x.shape
    mu8 = jnp.zeros((8, d), F32).at[:6].set(mu)
    vec = jnp.zeros((8, d), F32).at[0].set(w0).at[1].set(a0).at[2].set(k_k).at[3].set(k_a)
    l1 = jnp.concatenate([w1, a1, g1], axis=1).astype(BF16)
    wb = w_rkv.astype(BF16)
    row = pl.BlockSpec((tm, d), lambda i: (i, 0))
    full = lambda a: pl.BlockSpec(a.shape, lambda i: (0,) * a.ndim)
    args = (x, x, gain.reshape(1, d), mu8, vec, wb[0], wb[1], wb[2], l1,
            w2.astype(BF16), a2.astype(BF16), g2.astype(BF16))
    in_specs = [row, pl.BlockSpec((ROW_ALIGN, d), lambda i: (jnp.maximum(i * (tm // ROW_ALIGN) - 1, 0), 0))]
    in_specs += [full(a) for a in args[2:]]
    outs = [jax.ShapeDtypeStruct((n, d), BF16)] * 5 + [jax.ShapeDtypeStruct((n, d), F32),
                                                       jax.ShapeDtypeStruct((n, d), BF16)]
    return pl.pallas_call(
        functools.partial(_rwkv_mix_kernel, seq_len),
        grid=(n // tm,),
        in_specs=in_specs,
        out_specs=[row] * 7,
        out_shape=outs,
        compiler_params=_params("parallel"),
        name="rwkv_mix",
    )(*args)


def _rwkv_rec_kernel(r_ref, k_ref, v_ref, kk_ref, akk_ref, lw_ref, g_ref, vec_ref, y_ref, s_ref):
    c = pl.program_id(1)
    L = RWKV_CHUNK

    @pl.when(c == 0)
    def _():
        s_ref[...] = jnp.zeros_like(s_ref)

    lane = _iota((1, PAIR), 1)
    m0 = lane < HEAD_DIM
    ri = _iota((2 * L, 2 * L), 0)
    ci = _iota((2 * L, 2 * L), 1)
    same = (ri // L) == (ci // L)
    strict = same & (ci < ri)
    incl = same & (ci <= ri)
    eye = jnp.where(ri == ci, 1.0, 0.0)
    tri = jnp.where(_iota((L, L), 1) <= _iota((L, L), 0), 1.0, 0.0).astype(BF16)
    rb = _iota((PAIR, PAIR), 0) // HEAD_DIM
    cb = _iota((PAIR, PAIR), 1) // HEAD_DIM
    bd = rb == cb

    def stack(x):
        return jnp.concatenate([jnp.where(m0, x, 0.0), jnp.where(m0, 0.0, x)], axis=0)

    def fold(x):
        return x[:L] + x[L:]

    def head_mean(x):
        s0 = jnp.sum(jnp.where(m0, x, 0.0), axis=-1, keepdims=True)
        s1 = jnp.sum(jnp.where(m0, 0.0, x), axis=-1, keepdims=True)
        return jnp.where(m0, s0, s1) * (1.0 / HEAD_DIM)

    vec = vec_ref[...]
    pairs = range(N_PAIRS)
    sls = [slice(p * PAIR, (p + 1) * PAIR) for p in pairs]
    lw = [lw_ref[:, sl] for sl in sls]
    cum = []
    for p in pairs:
        lw_hi, lw_lo = _split(lw[p])
        cum.append(_dot(tri, lw_hi) + _dot(tri, lw_lo))
    dec_all, x_kap, x_r, a_t, k_t, sc = [], [], [], [], [], []
    for p in pairs:
        sl = sls[p]
        dec = jnp.exp(cum[p])
        inv = jnp.exp(-cum[p])
        dec_prev = jnp.exp(cum[p] - lw[p])
        dec_all.append(dec[L - 1:L, :])
        x_kap.append(stack(kk_ref[:, sl].astype(F32) * dec_prev).astype(BF16))
        x_r.append(stack(r_ref[:, sl].astype(F32) * dec))
        a_t.append((akk_ref[:, sl].astype(F32) * inv).astype(BF16))
        k_t.append((k_ref[:, sl].astype(F32) * inv).astype(BF16))
        xs = jnp.concatenate([x_kap[p], x_r[p].astype(BF16)], axis=0)
        ys = jnp.concatenate([a_t[p], a_t[p], k_t[p], k_t[p]], axis=0)
        sc.append(_dot_nt(xs, ys))
    mp = [jnp.where(strict, -sc[p][:2 * L, :2 * L], 0.0) for p in pairs]
    m_kk = [jnp.where(strict, sc[p][:2 * L, 2 * L:], 0.0).astype(BF16) for p in pairs]
    s_ra = [jnp.where(incl, sc[p][2 * L:, :2 * L], 0.0).astype(BF16) for p in pairs]
    s_rk = [jnp.where(incl, sc[p][2 * L:, 2 * L:], 0.0).astype(BF16) for p in pairs]
    v_st = [stack(v_ref[:, sl].astype(F32)).astype(BF16) for sl in sls]
    b1 = [_dot(m_kk[p], v_st[p]) for p in pairs]
    o1 = [_dot(s_rk[p], v_st[p]) for p in pairs]

    t_inv = [eye + mp[p] for p in pairs]
    for _ in range(int(math.log2(L)) - 1):
        mpb = [mp[p].astype(BF16) for p in pairs]
        mp = [_dot(mpb[p], mpb[p]) for p in pairs]
        t_inv = [t_inv[p] + _dot(t_inv[p].astype(BF16), mp[p].astype(BF16)) for p in pairs]

    wu = [_dot(t_inv[p].astype(BF16), jnp.concatenate([x_kap[p], b1[p].astype(BF16)], axis=1))
          for p in pairs]
    corr = [_dot(s_ra[p], wu[p].astype(BF16)) for p in pairs]
    g_m, h0t, r_hat, o0 = [], [], [], []
    for p in pairs:
        r_hat.append(fold(x_r[p] - corr[p][:, :PAIR]).astype(BF16))
        o0.append(fold(o1[p] - corr[p][:, PAIR:]))
        w_f = fold(wu[p][:, :PAIR]).astype(BF16)
        u0_f = fold(wu[p][:, PAIR:]).astype(BF16)
        g_m.append(jnp.where(bd, _dot_tn(a_t[p], w_f), 0.0).astype(BF16))
        h0t.append(jnp.where(bd, _dot_tn(v_ref[:, sls[p]], k_t[p]) - _dot_tn(u0_f, a_t[p]), 0.0))
    o = []
    for p in pairs:
        s = s_ref[p]
        sb = s.astype(BF16)
        o.append(_dot_nt(r_hat[p], sb) + o0[p])
        s_ref[p] = (s - _dot_nt(sb, g_m[p]) + h0t[p]) * dec_all[p]
    for p in pairs:
        sl = sls[p]
        mean = head_mean(o[p])
        cen = o[p] - mean
        var = head_mean(cen * cen)
        gn = cen * lax.rsqrt(var + RWKV_GN_EPS) * vec[0:1, sl] + vec[1:2, sl]
        r = r_ref[:, sl].astype(F32)
        k = k_ref[:, sl].astype(F32)
        bonus = head_mean(r * k * vec[2:3, sl]) * HEAD_DIM * v_ref[:, sl].astype(F32)
        y_ref[:, sl] = ((gn + bonus) * g_ref[:, sl].astype(F32)).astype(y_ref.dtype)


def rwkv_recurrence(r, k, v, kk, akk, lw, g, lnx_g, lnx_b, r_k, batch, seq_len):
    n, d = r.shape
    L = RWKV_CHUNK
    nc = seq_len // L
    vec = jnp.zeros((8, d), F32).at[0].set(lnx_g).at[1].set(lnx_b).at[2].set(r_k.reshape(d))
    blk = pl.BlockSpec((L, d), lambda b, c: (b * nc + c, 0))
    return pl.pallas_call(
        _rwkv_rec_kernel,
        grid=(batch, nc),
        in_specs=[blk] * 7 + [pl.BlockSpec((8, d), lambda b, c: (0, 0))],
        out_specs=blk,
        out_shape=jax.ShapeDtypeStruct((n, d), BF16),
        scratch_shapes=[pltpu.VMEM((N_PAIRS, PAIR, PAIR), F32)],
        compiler_params=_params("parallel", "arbitrary"),
        name="rwkv_recurrence",
    )(r, k, v, kk, akk, lw, g, vec)


def _qkv_rope_kernel(x_ref, g_ref, pos_ref, inv_ref, w_ref, o_ref, xn_ref, cs_ref):
    j = pl.program_id(1)
    tm = x_ref.shape[0]

    @pl.when(j == 0)
    def _():
        xn_ref[...] = _rms(x_ref[...], g_ref[...]).astype(BF16)
        ang = pos_ref[...].astype(F32) * inv_ref[...]
        first = (_iota((1, PAIR), 1) % HEAD_DIM) < HEAD_DIM // 2
        sn = jnp.sin(ang)
        cs_ref[0] = jnp.cos(ang)
        cs_ref[1] = jnp.where(first, -sn, 0.0)
        cs_ref[2] = jnp.where(first, 0.0, sn)

    y = _dot(xn_ref[...], w_ref[...])

    @pl.when(j == 2)
    def _():
        o_ref[...] = y.astype(o_ref.dtype)

    @pl.when(j < 2)
    def _():
        for p in range(N_PAIRS):
            sl = slice(p * PAIR, (p + 1) * PAIR)
            t = y[:, sl]
            rot = (t * cs_ref[0] + pltpu.roll(t, PAIR - HEAD_DIM // 2, 1) * cs_ref[1]
                   + pltpu.roll(t, HEAD_DIM // 2, 1) * cs_ref[2])
            o_ref[:, sl] = rot.astype(o_ref.dtype)


def qkv_rope(x, gain, positions, w_qkv, tm=512):
    n, d = x.shape
    half = HEAD_DIM // 2
    inv = ROPE_THETA ** (-jnp.arange(half, dtype=F32) * 2.0 / HEAD_DIM)
    inv128 = jnp.tile(inv, PAIR // half).reshape(1, PAIR)
    col_scale = jnp.where(jnp.arange(3 * d) < d, math.log2(math.e) / math.sqrt(HEAD_DIM), 1.0).astype(F32)
    return pl.pallas_call(
        _qkv_rope_kernel,
        grid=(n // tm, 3),
        in_specs=[pl.BlockSpec((tm, d), lambda i, j: (i, 0)),
                  pl.BlockSpec((1, d), lambda i, j: (0, 0)),
                  pl.BlockSpec((tm, 1), lambda i, j: (i, 0)),
                  pl.BlockSpec((1, PAIR), lambda i, j: (0, 0)),
                  pl.BlockSpec((d, d), lambda i, j: (0, j))],
        out_specs=pl.BlockSpec((tm, d), lambda i, j: (i, j)),
        out_shape=jax.ShapeDtypeStruct((n, 3 * d), BF16),
        scratch_shapes=[pltpu.VMEM((tm, d), BF16), pltpu.VMEM((3, tm, PAIR), F32)],
        compiler_params=_params("parallel", "arbitrary"),
        name="qkv_rope",
    )(x, gain.reshape(1, d), positions.reshape(n, 1), inv128, (w_qkv * col_scale).astype(BF16))


def _moba_kernel(q_ref, k_ref, v_ref, o_ref, km_ref, vt_ref):
    qi = pl.program_id(2)
    nblk = k_ref.shape[0] // MOBA_BLOCK
    bq = MOBA_BLOCK

    @pl.when(qi == 0)
    def _():
        km_ref[...] = jnp.zeros_like(km_ref)
        for n in range(nblk):
            kb = k_ref[n * bq:(n + 1) * bq, :].astype(F32)
            km_ref[n:n + 1, :] = jnp.mean(kb, axis=0, keepdims=True)
            vt = v_ref[n * bq:(n + 1) * bq, :].astype(F32).T.astype(BF16)
            ones_row = jnp.where(_iota((VT_ROWS - HEAD_DIM, bq), 0) == 0, 1.0, 0.0).astype(BF16)
            for h in range(2):
                vt_ref[h * VT_ROWS:h * VT_ROWS + HEAD_DIM, n * bq:(n + 1) * bq] = vt[h * HEAD_DIM:(h + 1) * HEAD_DIM]
                vt_ref[h * VT_ROWS + HEAD_DIM:(h + 1) * VT_ROWS, n * bq:(n + 1) * bq] = ones_row

    lane = _iota((1, PAIR), 1)
    own = (lane < HEAD_DIM, lane >= HEAD_DIM)
    spare = (HEAD_DIM, 0)
    q = q_ref[...]
    zero = jnp.zeros_like(q)
    km_hi, km_lo = _split(km_ref[...])
    blk = _iota((km_ref.shape[0], bq), 0)
    past = blk < qi
    place_r = _iota((km_ref.shape[0], PAIR), 0)
    place_c = _iota((km_ref.shape[0], PAIR), 1)

    q_aug = []
    for h in range(2):
        qm = jnp.where(own[h], q, zero)
        gate = _dot_nt(km_hi, qm) + _dot_nt(km_lo, qm)
        gate = jnp.where(past, gate, NEG_INF)
        rank = jnp.zeros(gate.shape, F32)
        for m in range(nblk):
            row = gate[m:m + 1, :]
            tie = jnp.where(m < blk, 1.0, 0.0)
            rank = rank + jnp.where(row > gate, 1.0, jnp.where(row == gate, tie, 0.0))
        drop = jnp.where((blk < nblk) & jnp.logical_not(past & (rank < MOBA_TOPK)), 1.0, 0.0).astype(BF16)
        place = jnp.where(place_c == place_r + spare[h], 1.0, 0.0).astype(BF16)
        dropped = _dot_tn(drop, place)
        q_aug.append(jnp.where(own[h], q, jnp.where(dropped > 0.5, NEG_INF, 0.0).astype(BF16)))

    hq = bq // 2

    def attend_blocks(c):
        chains = [(h, half) for h in range(2) for half in range(2)]
        q_part = {(h, half): q_aug[h][half * hq:(half + 1) * hq] for h, half in chains}
        items = [(c, ch) for ch in chains] + [(n, ch) for n in range(c) for ch in chains]

        def keys_of(n, half):
            return (n * bq, n * bq + (half + 1) * hq) if n == c else (n * bq, (n + 1) * bq)

        def score(item):
            n, (h, half) = item
            lo, hi = keys_of(n, half)
            kb = k_ref[lo:hi, :]
            if n == c:
                s = _dot_nt(jnp.where(own[h], kb, jnp.zeros_like(kb)), q_part[(h, half)])
                key = _iota(s.shape, 0)
                qry = _iota(s.shape, 1) + half * hq
                return jnp.where(key <= qry, s, NEG_INF)
            marker = jnp.where(lane == spare[h] + n, 1.0, 0.0).astype(BF16)
            return _dot_nt(jnp.where(own[h], kb, marker), q_part[(h, half)])

        state = {}
        ahead = 2 * len(chains)
        pending = [score(item) for item in items[:ahead]]
        for i, item in enumerate(items):
            s_cur = pending.pop(0)
            if i + ahead < len(items):
                pending.append(score(items[i + ahead]))
            n, (h, half) = item
            lo, hi = keys_of(n, half)
            vt = vt_ref[h * VT_ROWS:(h + 1) * VT_ROWS, lo:hi]
            top = s_cur.max(axis=0, keepdims=True)
            if n == c:
                state[(h, half)] = (top, _dot(vt, jnp.exp2(s_cur - top).astype(BF16)))
            else:
                m_old, acc = state[(h, half)]
                m_new = jnp.maximum(m_old, top)
                pr = jnp.exp2(s_cur - m_new).astype(BF16)
                state[(h, half)] = (m_new, jnp.exp2(m_old - m_new) * acc + _dot(vt, pr))
        rows = []
        for h in range(2):
            accs = [state[(h, half)][1] for half in range(2)]
            rows.append(jnp.concatenate([a[:HEAD_DIM] / a[HEAD_DIM:HEAD_DIM + 1] for a in accs], axis=1))
        o_ref[...] = jnp.concatenate(rows, axis=0).T.astype(o_ref.dtype)

    for c in range(nblk):
        pl.when(qi == c)(functools.partial(attend_blocks, c))


def moba_attention(qkv, batch, seq_len):
    n = qkv.shape[0]
    nb = seq_len // MOBA_BLOCK
    return pl.pallas_call(
        _moba_kernel,
        grid=(batch, N_PAIRS, nb),
        in_specs=[pl.BlockSpec((MOBA_BLOCK, PAIR), lambda b, p, q: (b * nb + q, p)),
                  pl.BlockSpec((seq_len, PAIR), lambda b, p, q: (b, N_PAIRS + p)),
                  pl.BlockSpec((seq_len, PAIR), lambda b, p, q: (b, 2 * N_PAIRS + p))],
        out_specs=pl.BlockSpec((MOBA_BLOCK, PAIR), lambda b, p, q: (b * nb + q, p)),
        out_shape=jax.ShapeDtypeStruct((n, D_MODEL), BF16),
        scratch_shapes=[pltpu.VMEM((16, PAIR), F32),
                        pltpu.VMEM((2 * VT_ROWS, seq_len), BF16)],
        compiler_params=_params("parallel", "parallel", "arbitrary"),
        name="moba_attention",
    )(qkv, qkv, qkv)


def _mem_xattn_kernel(x_ref, g_ref, wq_ref, kv_ref, wo_ref, o_ref):
    x = x_ref[...]
    xn = _rms(x, g_ref[...]).astype(BF16)
    q = (_dot(xn, wq_ref[...]) * (1.0 / math.sqrt(MEM_HEAD_DIM))).astype(BF16)
    outs = []
    for h in range(MEM_HEADS):
        sl = slice(h * MEM_HEAD_DIM, (h + 1) * MEM_HEAD_DIM)
        s = _dot_nt(q[:, sl], kv_ref[:, sl])
        s = s - jnp.max(s, axis=-1, keepdims=True)
        e = jnp.exp(s)
        pr = e / jnp.sum(e, axis=-1, keepdims=True)
        vh = kv_ref[:, D_MODEL + h * MEM_HEAD_DIM:D_MODEL + (h + 1) * MEM_HEAD_DIM]
        outs.append(_dot(pr.astype(BF16), vh).astype(BF16))
    o = jnp.concatenate(outs, axis=1)
    o_ref[...] = x + _dot(o, wo_ref[...])


def mem_cross_attention(x, gain, wq, kv, wo, seq_len, tm=256):
    n, d = x.shape
    m = kv.shape[0] // (n // seq_len)
    per_seq = seq_len // tm
    return pl.pallas_call(
        _mem_xattn_kernel,
        grid=(n // tm,),
        in_specs=[pl.BlockSpec((tm, d), lambda i: (i, 0)),
                  pl.BlockSpec((1, d), lambda i: (0, 0)),
                  pl.BlockSpec((d, d), lambda i: (0, 0)),
                  pl.BlockSpec((m, 2 * d), lambda i: (i // per_seq, 0)),
                  pl.BlockSpec((d, d), lambda i: (0, 0))],
        out_specs=pl.BlockSpec((tm, d), lambda i: (i, 0)),
        out_shape=jax.ShapeDtypeStruct((n, d), F32),
        compiler_params=_params("parallel"),
        name="mem_cross_attention",
    )(x, gain.reshape(1, d), wq.astype(BF16), kv, wo.astype(BF16))


def _router_kernel(x_ref, g_ref, w_ref, b_ref, h_out, meta_out, cnt_out):
    tm = x_ref.shape[0]
    h2 = _rms(x_ref[...], g_ref[...])
    hi = h2.astype(BF16)
    h_out[...] = hi
    lo = (h2 - hi.astype(F32)).astype(BF16)
    w_hi, w_lo = _split(w_ref[...])
    lg = _dot_nt(w_hi, hi) + _dot_nt(w_hi, lo) + _dot_nt(w_lo, hi) + b_ref[...]
    row = _iota((8, tm), 0).astype(F32)

    def first_argmax(val, vmax):
        return jnp.min(jnp.where(val == vmax, row, 8.0), axis=0, keepdims=True)

    gl = jnp.where(row < MOE_GROUPS, lg[0:8], -jnp.inf)
    gmax = jnp.max(gl, axis=0, keepdims=True)
    p_g = 1.0 / jnp.sum(jnp.exp(gl - gmax), axis=0, keepdims=True)
    gidx = first_argmax(gl, gmax)
    el = jnp.zeros((8, tm), F32)
    for g in range(MOE_GROUPS):
        el = el + jnp.where(gidx == g, lg[8 + 8 * g:16 + 8 * g], 0.0)
    ee = jnp.exp(el - jnp.max(el, axis=0, keepdims=True))
    pe = ee / jnp.sum(ee, axis=0, keepdims=True)
    p1 = jnp.max(pe, axis=0, keepdims=True)
    i1 = first_argmax(pe, p1)
    pe2 = jnp.where(row == i1, -1.0, pe)
    p2 = jnp.max(pe2, axis=0, keepdims=True)
    i2 = first_argmax(pe2, p2)
    e1 = gidx * MOE_EPG + i1
    e2 = gidx * MOE_EPG + i2
    gate1 = p_g * p1 / (p1 + p2)
    gate2 = p_g * p2 / (p1 + p2)

    erow = _iota((MOE_EXPERTS, tm), 0).astype(F32)
    oh1 = erow == e1
    oh2 = erow == e2
    oh = jnp.where(oh1 | oh2, 1.0, 0.0)
    before = jnp.where(_iota((tm, tm), 0) < _iota((tm, tm), 1), 1.0, 0.0).astype(BF16)
    cnt_before = _dot(oh.astype(BF16), before)
    lr1 = jnp.sum(jnp.where(oh1, cnt_before, 0.0), axis=0, keepdims=True)
    lr2 = jnp.sum(jnp.where(oh2, cnt_before, 0.0), axis=0, keepdims=True)
    zero = jnp.zeros((1, tm), F32)
    meta_out[...] = jnp.concatenate([e1, e2, gate1, gate2, lr1, lr2, zero, zero], axis=0)
    cnt_out[...] = jnp.broadcast_to(jnp.sum(oh, axis=1, keepdims=True), (MOE_EXPERTS, LANES))


def moe_router(x, gain, w_grp, b_grp, w_exp, b_exp):
    n, d = x.shape
    tm = MOE_TILE
    wt = jnp.zeros((LANES, d), F32).at[0:MOE_GROUPS].set(w_grp.T).at[8:8 + MOE_EXPERTS].set(w_exp.T)
    bt = jnp.zeros((LANES, 1), F32).at[0:MOE_GROUPS, 0].set(b_grp).at[8:8 + MOE_EXPERTS, 0].set(b_exp)
    return pl.pallas_call(
        _router_kernel,
        grid=(n // tm,),
        in_specs=[pl.BlockSpec((tm, d), lambda i: (i, 0)),
                  pl.BlockSpec((1, d), lambda i: (0, 0)),
                  pl.BlockSpec((LANES, d), lambda i: (0, 0)),
                  pl.BlockSpec((LANES, 1), lambda i: (0, 0))],
        out_specs=[pl.BlockSpec((tm, d), lambda i: (i, 0)),
                   pl.BlockSpec((8, tm), lambda i: (0, i)),
                   pl.BlockSpec((MOE_EXPERTS, LANES), lambda i: (i, 0))],
        out_shape=[jax.ShapeDtypeStruct((n, d), BF16),
                   jax.ShapeDtypeStruct((8, n), F32),
                   jax.ShapeDtypeStruct((n // tm * MOE_EXPERTS, LANES), F32)],
        compiler_params=_params("parallel"),
        name="moe_router",
    )(x, gain.reshape(1, d), wt, bt)


def _run_copies(tab_ref, first, count, chunks, make_copy, act):
    def body(e, _):
        base = (first + e) * 3
        length = tab_ref[base]
        src = tab_ref[base + 1]
        dst = tab_ref[base + 2]
        for ck in chunks:
            @pl.when((length & ck) != 0)
            def _():
                off = length & (-2 * ck)
                act(make_copy(pl.multiple_of(src + off, MOE_ALIGN), pl.multiple_of(dst + off, MOE_ALIGN), ck))
        return 0
    lax.fori_loop(0, count, body, 0)


def _start(cp):
    cp.start()


def _wait(cp):
    cp.wait()


def _dispatch_kernel(tab_ref, h_ref, pos_ref, rows_hbm, sorted_ref, zero_ref, sem, zsem):
    t = pl.program_id(0)
    nt = pl.num_programs(0)
    slot = t % 2
    r = _iota((MOE_SORTED, MOE_TILE), 0)
    perm = jnp.where((r == pos_ref[0:1, :]) | (r == pos_ref[1:2, :]), 1.0, 0.0).astype(BF16)
    sorted_ref[slot] = _dot(perm, h_ref[...]).astype(BF16)

    def run_copy(s):
        def make(src, dst, ck):
            return pltpu.make_async_copy(sorted_ref.at[s, pl.ds(src, ck)], rows_hbm.at[pl.ds(dst, ck)], sem.at[s])
        return make

    def zero_copy(src, dst, ck):
        del src
        return pltpu.make_async_copy(zero_ref.at[pl.ds(0, ck)], rows_hbm.at[pl.ds(dst, ck)], zsem)

    _run_copies(tab_ref, t * MOE_EXPERTS, MOE_EXPERTS, RUN_CHUNKS, run_copy(slot), _start)

    @pl.when(t > 0)
    def _():
        _run_copies(tab_ref, (t - 1) * MOE_EXPERTS, MOE_EXPERTS, RUN_CHUNKS, run_copy(1 - slot), _wait)

    @pl.when(t == nt - 1)
    def _():
        zero_ref[...] = jnp.zeros_like(zero_ref)
        pad = nt * MOE_EXPERTS
        _run_copies(tab_ref, pad, MOE_EXPERTS, PAD_CHUNKS, zero_copy, _start)
        spare = (pad + MOE_EXPERTS) * 3
        n_spare = tab_ref[spare]

        def spare_copy(i):
            dst = pl.multiple_of(tab_ref[spare + 2] + i * PAD_ROWS, PAD_ROWS)
            return zero_copy(0, dst, PAD_ROWS)

        lax.fori_loop(0, n_spare, lambda i, c: (_start(spare_copy(i)), c)[1], 0)
        _run_copies(tab_ref, t * MOE_EXPERTS, MOE_EXPERTS, RUN_CHUNKS, run_copy(slot), _wait)
        _run_copies(tab_ref, pad, MOE_EXPERTS, PAD_CHUNKS, zero_copy, _wait)
        lax.fori_loop(0, n_spare, lambda i, c: (_wait(spare_copy(i)), c)[1], 0)


def moe_dispatch(tab, h2, pos, n_rows):
    n, d = h2.shape
    tm = MOE_TILE
    return pl.pallas_call(
        _dispatch_kernel,
        grid_spec=pltpu.PrefetchScalarGridSpec(
            num_scalar_prefetch=1,
            grid=(n // tm,),
            in_specs=[pl.BlockSpec((tm, d), lambda i, tab: (i, 0)),
                      pl.BlockSpec((8, tm), lambda i, tab: (0, i))],
            out_specs=pl.BlockSpec(memory_space=pl.ANY),
            scratch_shapes=[pltpu.VMEM((2, MOE_SORTED, d), BF16), pltpu.VMEM((PAD_ROWS, d), BF16),
                            pltpu.SemaphoreType.DMA((2,)), pltpu.SemaphoreType.DMA],
        ),
        out_shape=jax.ShapeDtypeStruct((n_rows, d), BF16),
        compiler_params=_params("arbitrary"),
        name="moe_dispatch",
    )(tab, h2, pos)


def _expert_kernel(be_ref, nu_ref, x_ref, w1_ref, w3_ref, w2_ref, y_ref, w1b, w3b, w2b):
    b = pl.program_id(0)
    used = b < nu_ref[0]
    new_expert = (b == 0) | (be_ref[b] != be_ref[jnp.maximum(b - 1, 0)])

    @pl.when(used & new_expert)
    def _():
        w1b[...] = w1_ref[0, 0].astype(BF16)
        w3b[...] = w3_ref[0, 0].astype(BF16)
        w2b[...] = w2_ref[0, 0].astype(BF16)

    @pl.when(used)
    def _():
        xb = x_ref[...]
        a = _dot(xb, w1b[...])
        c = _dot(xb, w3b[...])
        hid = (a * jax.nn.sigmoid(a) * c).astype(BF16)
        y_ref[...] = _dot(hid, w2b[...]).astype(y_ref.dtype)

    @pl.when(jnp.logical_not(used))
    def _():
        y_ref[...] = jnp.zeros_like(y_ref)


def moe_experts(block_expert, n_used, rows, layer, w1, w3, w2):
    n_rows = rows.shape[0]
    nb = n_rows // MOE_ROWS
    d, ff = w1.shape[2], w1.shape[3]

    def xmap(b, be, nu):
        return (jnp.minimum(b, nu[0] - 1), 0)

    def wmap(b, be, nu):
        return (layer, be[jnp.minimum(b, nu[0] - 1)], 0, 0)

    return pl.pallas_call(
        _expert_kernel,
        grid_spec=pltpu.PrefetchScalarGridSpec(
            num_scalar_prefetch=2,
            grid=(nb,),
            in_specs=[pl.BlockSpec((MOE_ROWS, d), xmap),
                      pl.BlockSpec((1, 1, d, ff), wmap),
                      pl.BlockSpec((1, 1, d, ff), wmap),
                      pl.BlockSpec((1, 1, ff, d), wmap)],
            out_specs=pl.BlockSpec((MOE_ROWS, d), lambda b, be, nu: (b, 0)),
            scratch_shapes=[pltpu.VMEM((d, ff), BF16), pltpu.VMEM((d, ff), BF16), pltpu.VMEM((ff, d), BF16)],
        ),
        out_shape=jax.ShapeDtypeStruct((n_rows, d), BF16),
        compiler_params=_params("arbitrary"),
        name="moe_experts",
    )(block_expert, n_used, rows, w1, w3, w2)


def _combine_kernel(final_norm, tab_ref, y_hbm, pos_ref, gate_ref, x_ref, g_ref, o_ref, ys_ref, sem):
    t = pl.program_id(0)
    nt = pl.num_programs(0)
    slot = t % 2

    def run_copy(s):
        def make(src, dst, ck):
            return pltpu.make_async_copy(y_hbm.at[pl.ds(dst, ck)], ys_ref.at[s, pl.ds(src, ck)], sem.at[s])
        return make

    @pl.when(t == 0)
    def _():
        ys_ref[...] = jnp.zeros_like(ys_ref)
        _run_copies(tab_ref, 0, MOE_EXPERTS, RUN_CHUNKS, run_copy(0), _start)

    @pl.when(t + 1 < nt)
    def _():
        _run_copies(tab_ref, (t + 1) * MOE_EXPERTS, MOE_EXPERTS, RUN_CHUNKS, run_copy(1 - slot), _start)

    r = _iota((MOE_SORTED, MOE_TILE), 0)
    hit1 = r == pos_ref[0:1, :]
    hit2 = r == pos_ref[1:2, :]
    wgt = jnp.where(hit1, gate_ref[2:3, :], 0.0) + jnp.where(hit2, gate_ref[3:4, :], 0.0)
    row_gate = jnp.sum(wgt, axis=1, keepdims=True)
    row_used = jnp.sum(jnp.where(hit1 | hit2, 1.0, 0.0), axis=1, keepdims=True) > 0.0
    perm = jnp.where(hit1 | hit2, 1.0, 0.0).astype(BF16)
    _run_copies(tab_ref, t * MOE_EXPERTS, MOE_EXPERTS, RUN_CHUNKS, run_copy(slot), _wait)
    ys = jnp.where(row_used, ys_ref[slot].astype(F32) * row_gate, 0.0).astype(BF16)
    out = x_ref[...] + _dot_tn(perm, ys)
    if final_norm:
        out = _rms(out, g_ref[...])
    o_ref[...] = out


def moe_combine(tab, y, pos, meta, x, final_gain):
    n, d = x.shape
    tm = MOE_TILE
    final_norm = final_gain is not None
    gain = (final_gain if final_norm else jnp.ones((d,), F32)).reshape(1, d)
    return pl.pallas_call(
        functools.partial(_combine_kernel, final_norm),
        grid_spec=pltpu.PrefetchScalarGridSpec(
            num_scalar_prefetch=1,
            grid=(n // tm,),
            in_specs=[pl.BlockSpec(memory_space=pl.ANY),
                      pl.BlockSpec((8, tm), lambda i, tab: (0, i)),
                      pl.BlockSpec((8, tm), lambda i, tab: (0, i)),
                      pl.BlockSpec((tm, d), lambda i, tab: (i, 0)),
                      pl.BlockSpec((1, d), lambda i, tab: (0, 0))],
            out_specs=pl.BlockSpec((tm, d), lambda i, tab: (i, 0)),
            scratch_shapes=[pltpu.VMEM((2, MOE_SORTED, d), BF16), pltpu.SemaphoreType.DMA((2,))],
        ),
        out_shape=jax.ShapeDtypeStruct((n, d), F32),
        compiler_params=_params("arbitrary"),
        name="moe_combine",
    )(tab, y, pos, meta, x, gain)


def hierarchical_moe(x, gain, w_grp, b_grp, w_exp, b_exp, layer, w1, w3, w2, final_gain):
    n, d = x.shape
    tm = MOE_TILE
    nt = n // tm
    h2, meta, cnt = moe_router(x, gain, w_grp, b_grp, w_exp, b_exp)

    cnt = cnt.reshape(nt, MOE_EXPERTS, LANES)[:, :, 0].astype(I32)
    run = (cnt + MOE_ALIGN - 1) // MOE_ALIGN * MOE_ALIGN
    src = jnp.cumsum(run, axis=1) - run
    before = jnp.cumsum(run, axis=0) - run
    total = jnp.sum(run, axis=0)
    padded = (total + MOE_ROWS - 1) // MOE_ROWS * MOE_ROWS
    pad_end = jnp.cumsum(padded)
    dst = (pad_end - padded)[None, :] + before
    max_rows = 2 * n + nt * MOE_EXPERTS * (MOE_ALIGN - 1) + MOE_EXPERTS * (MOE_ROWS - 1)
    nb = -(-max_rows // MOE_ROWS)
    zero = jnp.zeros((MOE_EXPERTS,), I32)
    runs = jnp.stack([run, src, dst], axis=-1).reshape(-1, 3)
    pads = jnp.stack([padded - total, zero, pad_end - padded + total], axis=-1)
    spare = jnp.stack([(nb * MOE_ROWS - pad_end[-1]) // PAD_ROWS, zero[0], pad_end[-1]]).reshape(1, 3)
    tab = jnp.concatenate([runs, pads, spare], axis=0).reshape(-1).astype(I32)
    n_used = (pad_end[-1] // MOE_ROWS).astype(I32).reshape(1)
    block_start = jnp.arange(nb, dtype=I32) * MOE_ROWS
    block_expert = jnp.minimum(
        jnp.sum((pad_end[None, :] <= block_start[:, None]).astype(I32), axis=1), MOE_EXPERTS - 1).astype(I32)
    e = meta[0:2].astype(I32).reshape(2, nt, tm, 1)
    hit = e == jnp.arange(MOE_EXPERTS, dtype=I32)
    pos = jnp.sum(jnp.where(hit, src[None, :, None, :], 0), axis=-1).reshape(2, n) + meta[4:6].astype(I32)
    pos8 = jnp.zeros((8, n), I32).at[0:2].set(pos)

    rows = moe_dispatch(tab, h2, pos8, nb * MOE_ROWS)
    y = moe_experts(block_expert, n_used, rows, layer, w1, w3, w2)
    return moe_combine(tab, y, pos8, meta, x, final_gain)


def kernel(x, mem, positions, ln_mix, ln_mem, ln_memkv, ln_ffn, rw_mu, rw_w0, rw_w1, rw_w2, rw_a0, rw_a1, rw_a2, rw_g1, rw_g2, rw_kk, rw_ka, rw_rk, rw_wrkv, rw_lnx_g, rw_lnx_b, rw_wo, mb_wqkv, mb_wo, mx_wq, mx_wkv, mx_wo, moe_wg, moe_bg, moe_we, moe_be, moe_w1, moe_w3, moe_w2, ln_f):
    B, T, C = x.shape
    n = B * T
    depth = ln_mix.shape[0]
    xf = x.reshape(n, C)
    memf = mem.reshape(-1, C)
    for i in range(depth):
        j = i // 2
        if i % 2 == 0:
            r, k, v, kk, akk, lw, g = rwkv_mix(xf, T, ln_mix[i], rw_mu[j], rw_w0[j], rw_w1[j], rw_w2[j],
                                               rw_a0[j], rw_a1[j], rw_a2[j], rw_g1[j], rw_g2[j],
                                               rw_kk[j], rw_ka[j], rw_wrkv[j])
            y = rwkv_recurrence(r, k, v, kk, akk, lw, g, rw_lnx_g[j], rw_lnx_b[j], rw_rk[j], B, T)
            xf = linear_residual(y, rw_wo[j].astype(BF16), xf)
        else:
            qkv = qkv_rope(xf, ln_mix[i], positions, mb_wqkv[j])
            o = moba_attention(qkv, B, T)
            xf = linear_residual(o, mb_wo[j].astype(BF16), xf)
        kv = norm_linear(memf, ln_memkv[i], mx_wkv[i].astype(BF16))
        xf = mem_cross_attention(xf, ln_mem[i], mx_wq[i], kv, mx_wo[i], T)
        xf = hierarchical_moe(xf, ln_ffn[i], moe_wg[i], moe_bg[i], moe_we[i], moe_be[i],
                              i, moe_w1, moe_w3, moe_w2, ln_f if i == depth - 1 else None)
    return xf.reshape(B, T, C)
```

```python
import functools
import math

import jax
import jax.numpy as jnp
from jax import lax
from jax.experimental import pallas as pl
from jax.experimental.pallas import tpu as pltpu

F32 = jnp.float32
BF16 = jnp.bfloat16
I32 = jnp.int32
U32 = jnp.uint32

D_MODEL = 1024
HEAD_DIM = 64
PAIR = 2 * HEAD_DIM
N_PAIRS = D_MODEL // PAIR
RWKV_GN_EPS = 64e-5
RWKV_CHUNK = 64
RWKV_CHUNKS_PER_STEP = 4
MOBA_BLOCK = 256
MOBA_TOPK = 3
VT_ROWS = HEAD_DIM + 16
ROPE_THETA = 10000.0
MEM_HEADS = 4
MEM_HEAD_DIM = D_MODEL // MEM_HEADS
MOE_GROUPS = 4
MOE_EPG = 8
MOE_EXPERTS = MOE_GROUPS * MOE_EPG
MOE_FF = D_MODEL // 2
RMS_EPS = 1e-6
NEG_INF = -1e30

LANES = 128
ROW_ALIGN = 8
MOE_TILE = 512
MOE_ROWS = 512
MOE_ALIGN = 16
MOE_SORTED = 2 * MOE_TILE + MOE_EXPERTS * (MOE_ALIGN - 1) + (-(2 * MOE_TILE + MOE_EXPERTS * (MOE_ALIGN - 1))) % MOE_ALIGN
RUN_CHUNKS = tuple(MOE_ALIGN << s for s in range((MOE_TILE // MOE_ALIGN).bit_length() - 1, -1, -1))
PAD_ROWS = MOE_ROWS // 2
PAD_CHUNKS = tuple(MOE_ALIGN << s for s in range((PAD_ROWS // MOE_ALIGN).bit_length() - 1, -1, -1))
VMEM_LIMIT = 56 * 2 ** 20


def _params(*sem):
    return pltpu.CompilerParams(dimension_semantics=sem, vmem_limit_bytes=VMEM_LIMIT)


def _iota(shape, dim):
    return lax.broadcasted_iota(I32, shape, dim)


def _dot(a, b):
    return jnp.dot(a, b, preferred_element_type=F32)


def _dot_nt(a, b):
    return lax.dot_general(a, b, (((1,), (1,)), ((), ())), preferred_element_type=F32)


def _dot_tn(a, b):
    return lax.dot_general(a, b, (((0,), (0,)), ((), ())), preferred_element_type=F32)


def _split(x):
    hi = x.astype(BF16)
    lo = (x - hi.astype(F32)).astype(BF16)
    return hi, lo


def _rms(x, g):
    return x * lax.rsqrt(jnp.mean(x * x, axis=-1, keepdims=True) + RMS_EPS) * g


def _norm_linear_kernel(x_ref, g_ref, w_ref, o_ref):
    xn = _rms(x_ref[...], g_ref[...]).astype(BF16)
    o_ref[...] = _dot(xn, w_ref[...]).astype(o_ref.dtype)


def norm_linear(x, g, w, tm=256):
    n, k = x.shape
    dout = w.shape[1]
    return pl.pallas_call(
        _norm_linear_kernel,
        grid=(n // tm,),
        in_specs=[pl.BlockSpec((tm, k), lambda i: (i, 0)),
                  pl.BlockSpec((1, k), lambda i: (0, 0)),
                  pl.BlockSpec((k, dout), lambda i: (0, 0))],
        out_specs=pl.BlockSpec((tm, dout), lambda i: (i, 0)),
        out_shape=jax.ShapeDtypeStruct((n, dout), BF16),
        compiler_params=_params("parallel"),
        name="norm_linear",
    )(x, g.reshape(1, k), w)


def _linear_residual_kernel(y_ref, w_ref, r_ref, o_ref):
    o_ref[...] = r_ref[...] + _dot(y_ref[...], w_ref[...])


def linear_residual(y, w, res, tm=512):
    n, k = y.shape
    dout = w.shape[1]
    return pl.pallas_call(
        _linear_residual_kernel,
        grid=(n // tm,),
        in_specs=[pl.BlockSpec((tm, k), lambda i: (i, 0)),
                  pl.BlockSpec((k, dout), lambda i: (0, 0)),
                  pl.BlockSpec((tm, dout), lambda i: (i, 0))],
        out_specs=pl.BlockSpec((tm, dout), lambda i: (i, 0)),
        out_shape=jax.ShapeDtypeStruct((n, dout), F32),
        compiler_params=_params("parallel"),
        name="linear_residual",
    )(y, w, res)


def _head_sum(x_sq):
    r = _iota((PAIR, PAIR), 0) // HEAD_DIM
    c = _iota((PAIR, PAIR), 1) // HEAD_DIM
    bd = jnp.where(r == c, 1.0, 0.0).astype(BF16)
    hi, lo = _split(x_sq)
    return _dot(hi, bd) + _dot(lo, bd)


def _rwkv_mix_kernel(seq_len, x_ref, xp_ref, g_ref, mu_ref, vec_ref, wr_ref, wk_ref, wv_ref,
                     l1_ref, w2_ref, a2_ref, g2_ref,
                     r_out, k_out, v_out, kk_out, akk_out, lw_out, g_out):
    i = pl.program_id(0)
    tm = x_ref.shape[0]
    gain = g_ref[...]
    h = _rms(x_ref[...], gain)
    hp = _rms(xp_ref[...], gain)[ROW_ALIGN - 1:ROW_ALIGN, :]
    hp = jnp.where((i * tm) % seq_len == 0, 0.0, hp)
    hs = pltpu.roll(h, 1, 0)
    hs = jnp.where(_iota((tm, 1), 0) == 0, hp, hs)
    hb = h.astype(BF16)
    dxb = (hs - h).astype(BF16)
    mub = mu_ref[...].astype(BF16)

    def mix(s):
        return hb + dxb * mub[s:s + 1, :]

    vec = vec_ref[...]
    w0, a0, k_k, k_a = vec[0:1], vec[1:2], vec[2:3], vec[3:4]
    l1 = l1_ref[...]
    r = _dot(mix(0), wr_ref[...])
    k = _dot(mix(2), wk_ref[...])
    v = _dot(mix(3), wv_ref[...])
    tw = jnp.tanh(_dot(mix(1), l1[:, 0:64])).astype(BF16)
    lw = -math.exp(-0.5) * jax.nn.sigmoid(w0 + _dot(tw, w2_ref[...]))
    ta = _dot(mix(4), l1[:, 64:128]).astype(BF16)
    a = jax.nn.sigmoid(a0 + _dot(ta, a2_ref[...]))
    tg = jax.nn.sigmoid(_dot(mix(5), l1[:, 128:256])).astype(BF16)
    g = _dot(tg, g2_ref[...])

    kk = k * k_k
    for p in range(N_PAIRS):
        sl = slice(p * PAIR, (p + 1) * PAIR)
        kkp = kk[:, sl]
        kkn = kkp * lax.rsqrt(jnp.maximum(_head_sum(kkp * kkp), 1e-24))
        kk_out[:, sl] = kkn.astype(kk_out.dtype)
        akk_out[:, sl] = (kkn * a[:, sl]).astype(akk_out.dtype)
    r_out[...] = r.astype(r_out.dtype)
    k_out[...] = (k * (1.0 + (a - 1.0) * k_a)).astype(k_out.dtype)
    v_out[...] = v.astype(v_out.dtype)
    lw_out[...] = lw
    g_out[...] = g.astype(g_out.dtype)


def rwkv_mix(x, seq_len, gain, mu, w0, w1, w2, a0, a1, a2, g1, g2, k_k, k_a, w_rkv, tm=256):
    n, d = x.shape
    mu8 = jnp.zeros((8, d), F32).at[:6].set(mu)
    vec = jnp.zeros((8, d), F32).at[0].set(w0).at[1].set(a0).at[2].set(k_k).at[3].set(k_a)
    l1 = jnp.concatenate([w1, a1, g1], axis=1).astype(BF16)
    wb = w_rkv.astype(BF16)
    row = pl.BlockSpec((tm, d), lambda i: (i, 0))
    full = lambda a: pl.BlockSpec(a.shape, lambda i: (0,) * a.ndim)
    args = (x, x, gain.reshape(1, d), mu8, vec, wb[0], wb[1], wb[2], l1,
            w2.astype(BF16), a2.astype(BF16), g2.astype(BF16))
    in_specs = [row, pl.BlockSpec((ROW_ALIGN, d), lambda i: (jnp.maximum(i * (tm // ROW_ALIGN) - 1, 0), 0))]
    in_specs += [full(a) for a in args[2:]]
    outs = [jax.ShapeDtypeStruct((n, d), BF16)] * 5 + [jax.ShapeDtypeStruct((n, d), F32),
                                                       jax.ShapeDtypeStruct((n, d), BF16)]
    return pl.pallas_call(
        functools.partial(_rwkv_mix_kernel, seq_len),
        grid=(n // tm,),
        in_specs=in_specs,
        out_specs=[row] * 7,
        out_shape=outs,
        compiler_params=_params("parallel"),
        name="rwkv_mix",
    )(*args)


def _rwkv_rec_kernel(r_ref, k_ref, v_ref, kk_ref, akk_ref, lw_ref, g_ref, vec_ref, y_ref, s_ref):
    c = pl.program_id(1)
    L = RWKV_CHUNK

    @pl.when(c == 0)
    def _():
        s_ref[...] = jnp.zeros_like(s_ref)

    lane = _iota((1, PAIR), 1)
    m0 = lane < HEAD_DIM
    ri = _iota((2 * L, 2 * L), 0)
    ci = _iota((2 * L, 2 * L), 1)
    same = (ri // L) == (ci // L)
    strict = same & (ci < ri)
    incl = same & (ci <= ri)
    eye = jnp.where(ri == ci, 1.0, 0.0)
    tri = jnp.where(_iota((L, L), 1) <= _iota((L, L), 0), 1.0, 0.0).astype(BF16)
    rb = _iota((PAIR, PAIR), 0) // HEAD_DIM
    cb = _iota((PAIR, PAIR), 1) // HEAD_DIM
    bd = rb == cb

    def stack(x):
        return jnp.concatenate([jnp.where(m0, x, 0.0), jnp.where(m0, 0.0, x)], axis=0)

    def fold(x):
        return x[:L] + x[L:]

    def head_mean(x):
        s0 = jnp.sum(jnp.where(m0, x, 0.0), axis=-1, keepdims=True)
        s1 = jnp.sum(jnp.where(m0, 0.0, x), axis=-1, keepdims=True)
        return jnp.where(m0, s0, s1) * (1.0 / HEAD_DIM)

    vec = vec_ref[...]
    n_sub = r_ref.shape[0] // L
    pairs = range(n_sub * N_PAIRS)
    rws = [slice((q // N_PAIRS) * L, (q // N_PAIRS + 1) * L) for q in pairs]
    sls = [slice((q % N_PAIRS) * PAIR, (q % N_PAIRS + 1) * PAIR) for q in pairs]
    lw = [lw_ref[rws[p], sls[p]] for p in pairs]
    cum = []
    for p in pairs:
        lw_hi, lw_lo = _split(lw[p])
        cum.append(_dot(tri, lw_hi) + _dot(tri, lw_lo))
    dec_all, x_kap, x_r, a_t, k_t, sc = [], [], [], [], [], []
    for p in pairs:
        rw, sl = rws[p], sls[p]
        dec = jnp.exp(cum[p])
        inv = jnp.exp(-cum[p])
        dec_prev = jnp.exp(cum[p] - lw[p])
        dec_all.append(dec[L - 1:L, :])
        x_kap.append(stack(kk_ref[rw, sl].astype(F32) * dec_prev).astype(BF16))
        x_r.append(stack(r_ref[rw, sl].astype(F32) * dec))
        a_t.append((akk_ref[rw, sl].astype(F32) * inv).astype(BF16))
        k_t.append((k_ref[rw, sl].astype(F32) * inv).astype(BF16))
        xs = jnp.concatenate([x_kap[p], x_r[p].astype(BF16)], axis=0)
        ys = jnp.concatenate([a_t[p], a_t[p], k_t[p], k_t[p]], axis=0)
        sc.append(_dot_nt(xs, ys))
    mp = [jnp.where(strict, -sc[p][:2 * L, :2 * L], 0.0) for p in pairs]
    m_kk = [jnp.where(strict, sc[p][:2 * L, 2 * L:], 0.0).astype(BF16) for p in pairs]
    s_ra = [jnp.where(incl, sc[p][2 * L:, :2 * L], 0.0).astype(BF16) for p in pairs]
    s_rk = [jnp.where(incl, sc[p][2 * L:, 2 * L:], 0.0).astype(BF16) for p in pairs]
    v_st = [stack(v_ref[rws[p], sls[p]].astype(F32)).astype(BF16) for p in pairs]
    b1 = [_dot(m_kk[p], v_st[p]) for p in pairs]
    o1 = [_dot(s_rk[p], v_st[p]) for p in pairs]

    t_inv = [eye + mp[p] for p in pairs]
    for _ in range(int(math.log2(L)) - 1):
        mpb = [mp[p].astype(BF16) for p in pairs]
        mp = [_dot(mpb[p], mpb[p]) for p in pairs]
        t_inv = [t_inv[p] + _dot(t_inv[p].astype(BF16), mp[p].astype(BF16)) for p in pairs]

    wu = [_dot(t_inv[p].astype(BF16), jnp.concatenate([x_kap[p], b1[p].astype(BF16)], axis=1))
          for p in pairs]
    corr = [_dot(s_ra[p], wu[p].astype(BF16)) for p in pairs]
    g_m, h0t, r_hat, o0 = [], [], [], []
    for p in pairs:
        r_hat.append(fold(x_r[p] - corr[p][:, :PAIR]).astype(BF16))
        o0.append(fold(o1[p] - corr[p][:, PAIR:]))
        w_f = fold(wu[p][:, :PAIR]).astype(BF16)
        u0_f = fold(wu[p][:, PAIR:]).astype(BF16)
        g_m.append(jnp.where(bd, _dot_tn(a_t[p], w_f), 0.0).astype(BF16))
        h0t.append(jnp.where(bd, _dot_tn(v_ref[rws[p], sls[p]], k_t[p]) - _dot_tn(u0_f, a_t[p]), 0.0))
    o = []
    for p in pairs:
        hp = p % N_PAIRS
        s = s_ref[hp]
        sb = s.astype(BF16)
        o.append(_dot_nt(r_hat[p], sb) + o0[p])
        s_ref[hp] = (s - _dot_nt(sb, g_m[p]) + h0t[p]) * dec_all[p]
    for p in pairs:
        rw, sl = rws[p], sls[p]
        mean = head_mean(o[p])
        cen = o[p] - mean
        var = head_mean(cen * cen)
        gn = cen * lax.rsqrt(var + RWKV_GN_EPS) * vec[0:1, sl] + vec[1:2, sl]
        r = r_ref[rw, sl].astype(F32)
        k = k_ref[rw, sl].astype(F32)
        bonus = head_mean(r * k * vec[2:3, sl]) * HEAD_DIM * v_ref[rw, sl].astype(F32)
        y_ref[rw, sl] = ((gn + bonus) * g_ref[rw, sl].astype(F32)).astype(y_ref.dtype)


def rwkv_recurrence(r, k, v, kk, akk, lw, g, lnx_g, lnx_b, r_k, batch, seq_len):
    n, d = r.shape
    rows = RWKV_CHUNK * RWKV_CHUNKS_PER_STEP
    nc = seq_len // rows
    vec = jnp.zeros((8, d), F32).at[0].set(lnx_g).at[1].set(lnx_b).at[2].set(r_k.reshape(d))
    blk = pl.BlockSpec((rows, d), lambda b, c: (b * nc + c, 0))
    return pl.pallas_call(
        _rwkv_rec_kernel,
        grid=(batch, nc),
        in_specs=[blk] * 7 + [pl.BlockSpec((8, d), lambda b, c: (0, 0))],
        out_specs=blk,
        out_shape=jax.ShapeDtypeStruct((n, d), BF16),
        scratch_shapes=[pltpu.VMEM((N_PAIRS, PAIR, PAIR), F32)],
        compiler_params=_params("parallel", "arbitrary"),
        name="rwkv_recurrence",
    )(r, k, v, kk, akk, lw, g, vec)


def _qkv_rope_kernel(x_ref, g_ref, pos_ref, inv_ref, w_ref, o_ref, xn_ref, cs_ref):
    j = pl.program_id(1)
    tm = x_ref.shape[0]

    @pl.when(j == 0)
    def _():
        xn_ref[...] = _rms(x_ref[...], g_ref[...]).astype(BF16)
        ang = pos_ref[...].astype(F32) * inv_ref[...]
        first = (_iota((1, PAIR), 1) % HEAD_DIM) < HEAD_DIM // 2
        sn = jnp.sin(ang)
        cs_ref[0] = jnp.cos(ang)
        cs_ref[1] = jnp.where(first, -sn, 0.0)
        cs_ref[2] = jnp.where(first, 0.0, sn)

    y = _dot(xn_ref[...], w_ref[...])

    @pl.when(j == 2)
    def _():
        o_ref[...] = y.astype(o_ref.dtype)

    @pl.when(j < 2)
    def _():
        for p in range(N_PAIRS):
            sl = slice(p * PAIR, (p + 1) * PAIR)
            t = y[:, sl]
            rot = (t * cs_ref[0] + pltpu.roll(t, PAIR - HEAD_DIM // 2, 1) * cs_ref[1]
                   + pltpu.roll(t, HEAD_DIM // 2, 1) * cs_ref[2])
            o_ref[:, sl] = rot.astype(o_ref.dtype)


def qkv_rope(x, gain, positions, w_qkv, tm=512):
    n, d = x.shape
    half = HEAD_DIM // 2
    inv = ROPE_THETA ** (-jnp.arange(half, dtype=F32) * 2.0 / HEAD_DIM)
    inv128 = jnp.tile(inv, PAIR // half).reshape(1, PAIR)
    col_scale = jnp.where(jnp.arange(3 * d) < d, math.log2(math.e) / math.sqrt(HEAD_DIM), 1.0).astype(F32)
    return pl.pallas_call(
        _qkv_rope_kernel,
        grid=(n // tm, 3),
        in_specs=[pl.BlockSpec((tm, d), lambda i, j: (i, 0)),
                  pl.BlockSpec((1, d), lambda i, j: (0, 0)),
                  pl.BlockSpec((tm, 1), lambda i, j: (i, 0)),
                  pl.BlockSpec((1, PAIR), lambda i, j: (0, 0)),
                  pl.BlockSpec((d, d), lambda i, j: (0, j))],
        out_specs=pl.BlockSpec((tm, d), lambda i, j: (i, j)),
        out_shape=jax.ShapeDtypeStruct((n, 3 * d), BF16),
        scratch_shapes=[pltpu.VMEM((tm, d), BF16), pltpu.VMEM((3, tm, PAIR), F32)],
        compiler_params=_params("parallel", "arbitrary"),
        name="qkv_rope",
    )(x, gain.reshape(1, d), positions.reshape(n, 1), inv128, (w_qkv * col_scale).astype(BF16))


def _moba_kernel(q_ref, k_ref, v_ref, o_ref, vt_ref):
    nblk = k_ref.shape[0] // MOBA_BLOCK
    bq = MOBA_BLOCK
    km_rows = 16

    means = []
    ones_row = jnp.where(_iota((VT_ROWS - HEAD_DIM, bq), 0) == 0, 1.0, 0.0).astype(BF16)
    for n in range(nblk):
        cols = slice(n * bq, (n + 1) * bq)
        means.append(jnp.mean(k_ref[cols, :].astype(F32), axis=0, keepdims=True))
        vt = v_ref[cols, :].astype(F32).T.astype(BF16)
        for h in range(2):
            vt_ref[h * VT_ROWS:h * VT_ROWS + HEAD_DIM, cols] = vt[h * HEAD_DIM:(h + 1) * HEAD_DIM]
            vt_ref[h * VT_ROWS + HEAD_DIM:(h + 1) * VT_ROWS, cols] = ones_row
    km_hi, km_lo = _split(jnp.concatenate(means + [jnp.zeros((km_rows - nblk, PAIR), F32)], axis=0))

    lane = _iota((1, PAIR), 1)
    own = (lane < HEAD_DIM, lane >= HEAD_DIM)
    spare = (HEAD_DIM, 0)
    blk = _iota((km_rows, bq), 0)
    causal = _iota((bq, bq), 0) <= _iota((bq, bq), 1)
    eye = jnp.where(_iota((bq, bq), 0) == _iota((bq, bq), 1), 1.0, 0.0).astype(BF16)

    def augmented_queries(c, h):
        q = q_ref[c * bq:(c + 1) * bq, :]
        qm = jnp.where(own[h], q, jnp.zeros_like(q))
        past = blk < c
        gate = _dot_nt(km_hi, qm) + _dot_nt(km_lo, qm)
        gate = jnp.where(past, gate, NEG_INF)
        rank = jnp.zeros(gate.shape, F32)
        for m in range(c):
            row = gate[m:m + 1, :]
            tie = jnp.where(m < blk, 1.0, 0.0)
            rank = rank + jnp.where(row > gate, 1.0, jnp.where(row == gate, tie, 0.0))
        drop = jnp.where((blk < nblk) & jnp.logical_not(past & (rank < MOBA_TOPK)), 1.0, 0.0)
        pad = [jnp.zeros((HEAD_DIM - km_rows, bq), F32)]
        rows = ([jnp.zeros((HEAD_DIM, bq), F32), drop] + pad) if spare[h] else ([drop] + pad + [jnp.zeros((HEAD_DIM, bq), F32)])
        dropped = _dot_nt(eye, jnp.concatenate(rows, axis=0).astype(BF16))
        return jnp.where(own[h], q, jnp.where(dropped > 0.5, NEG_INF, 0.0).astype(BF16))

    def score(c, h):
        k_aug = []
        for n in range(c + 1):
            kb = k_ref[n * bq:(n + 1) * bq, :]
            marker = jnp.where(lane == spare[h] + n, 1.0, 0.0) if n < c else jnp.zeros(lane.shape, F32)
            k_aug.append(jnp.where(own[h], kb, marker.astype(BF16)))
        return _dot_nt(jnp.concatenate(k_aug, axis=0), augmented_queries(c, h))

    def attend(c, h, s):
        parts = [s[n * bq:(n + 1) * bq] for n in range(c)]
        parts.append(jnp.where(causal, s[c * bq:], NEG_INF))
        top = parts[0].max(axis=0, keepdims=True)
        for part in parts[1:]:
            top = jnp.maximum(top, part.max(axis=0, keepdims=True))
        probs = jnp.concatenate([jnp.exp2((part - top).astype(BF16)) for part in parts], axis=0)
        acc = _dot(vt_ref[h * VT_ROWS:(h + 1) * VT_ROWS, 0:(c + 1) * bq], probs)
        return acc[:HEAD_DIM] / acc[HEAD_DIM:HEAD_DIM + 1]

    units = [(c, h) for c in range(nblk) for h in range(2)]
    ahead = 2
    pending = [score(*u) for u in units[:ahead]]
    outs = {}
    for i, (c, h) in enumerate(units):
        s = pending.pop(0)
        if i + ahead < len(units):
            pending.append(score(*units[i + ahead]))
        outs[h] = attend(c, h, s)
        if h == 1:
            o_ref[c * bq:(c + 1) * bq, :] = jnp.concatenate([outs[0], outs[1]], axis=0).T.astype(o_ref.dtype)


def moba_attention(qkv, batch, seq_len):
    n = qkv.shape[0]
    return pl.pallas_call(
        _moba_kernel,
        grid=(batch, N_PAIRS),
        in_specs=[pl.BlockSpec((seq_len, PAIR), lambda b, p: (b, p)),
                  pl.BlockSpec((seq_len, PAIR), lambda b, p: (b, N_PAIRS + p)),
                  pl.BlockSpec((seq_len, PAIR), lambda b, p: (b, 2 * N_PAIRS + p))],
        out_specs=pl.BlockSpec((seq_len, PAIR), lambda b, p: (b, p)),
        out_shape=jax.ShapeDtypeStruct((n, D_MODEL), BF16),
        scratch_shapes=[pltpu.VMEM((2 * VT_ROWS, seq_len), BF16)],
        compiler_params=_params("parallel", "parallel"),
        name="moba_attention",
    )(qkv, qkv, qkv)


def _mem_xattn_kernel(x_ref, g_ref, wq_ref, kv_ref, wo_ref, o_ref):
    x = x_ref[...]
    xn = _rms(x, g_ref[...]).astype(BF16)
    q = (_dot(xn, wq_ref[...]) * (1.0 / math.sqrt(MEM_HEAD_DIM))).astype(BF16)
    outs = []
    for h in range(MEM_HEADS):
        sl = slice(h * MEM_HEAD_DIM, (h + 1) * MEM_HEAD_DIM)
        s = _dot_nt(q[:, sl], kv_ref[:, sl])
        s = s - jnp.max(s, axis=-1, keepdims=True)
        e = jnp.exp(s)
        pr = e / jnp.sum(e, axis=-1, keepdims=True)
        vh = kv_ref[:, D_MODEL + h * MEM_HEAD_DIM:D_MODEL + (h + 1) * MEM_HEAD_DIM]
        outs.append(_dot(pr.astype(BF16), vh).astype(BF16))
    o = jnp.concatenate(outs, axis=1)
    o_ref[...] = x + _dot(o, wo_ref[...])


def mem_cross_attention(x, gain, wq, kv, wo, seq_len, tm=256):
    n, d = x.shape
    m = kv.shape[0] // (n // seq_len)
    per_seq = seq_len // tm
    return pl.pallas_call(
        _mem_xattn_kernel,
        grid=(n // tm,),
        in_specs=[pl.BlockSpec((tm, d), lambda i: (i, 0)),
                  pl.BlockSpec((1, d), lambda i: (0, 0)),
                  pl.BlockSpec((d, d), lambda i: (0, 0)),
                  pl.BlockSpec((m, 2 * d), lambda i: (i // per_seq, 0)),
                  pl.BlockSpec((d, d), lambda i: (0, 0))],
        out_specs=pl.BlockSpec((tm, d), lambda i: (i, 0)),
        out_shape=jax.ShapeDtypeStruct((n, d), F32),
        compiler_params=_params("parallel"),
        name="mem_cross_attention",
    )(x, gain.reshape(1, d), wq.astype(BF16), kv, wo.astype(BF16))


def _router_kernel(x_ref, g_ref, w_ref, b_ref, h_out, meta_out, cnt_out):
    tm = x_ref.shape[0]
    h2 = _rms(x_ref[...], g_ref[...])
    hi = h2.astype(BF16)
    h_out[...] = hi
    lo = (h2 - hi.astype(F32)).astype(BF16)
    w_hi, w_lo = _split(w_ref[...])
    lg = _dot_nt(w_hi, hi) + _dot_nt(w_hi, lo) + _dot_nt(w_lo, hi) + b_ref[...]
    row = _iota((8, tm), 0).astype(F32)

    def first_argmax(val, vmax):
        return jnp.min(jnp.where(val == vmax, row, 8.0), axis=0, keepdims=True)

    gl = jnp.where(row < MOE_GROUPS, lg[0:8], -jnp.inf)
    gmax = jnp.max(gl, axis=0, keepdims=True)
    p_g = 1.0 / jnp.sum(jnp.exp(gl - gmax), axis=0, keepdims=True)
    gidx = first_argmax(gl, gmax)
    el = jnp.zeros((8, tm), F32)
    for g in range(MOE_GROUPS):
        el = el + jnp.where(gidx == g, lg[8 + 8 * g:16 + 8 * g], 0.0)
    ee = jnp.exp(el - jnp.max(el, axis=0, keepdims=True))
    pe = ee / jnp.sum(ee, axis=0, keepdims=True)
    p1 = jnp.max(pe, axis=0, keepdims=True)
    i1 = first_argmax(pe, p1)
    pe2 = jnp.where(row == i1, -1.0, pe)
    p2 = jnp.max(pe2, axis=0, keepdims=True)
    i2 = first_argmax(pe2, p2)
    e1 = gidx * MOE_EPG + i1
    e2 = gidx * MOE_EPG + i2
    gate1 = p_g * p1 / (p1 + p2)
    gate2 = p_g * p2 / (p1 + p2)

    erow = _iota((MOE_EXPERTS, tm), 0).astype(F32)
    oh1 = erow == e1
    oh2 = erow == e2
    oh = jnp.where(oh1 | oh2, 1.0, 0.0)
    before = jnp.where(_iota((tm, tm), 0) < _iota((tm, tm), 1), 1.0, 0.0).astype(BF16)
    cnt_before = _dot(oh.astype(BF16), before)
    lr1 = jnp.sum(jnp.where(oh1, cnt_before, 0.0), axis=0, keepdims=True)
    lr2 = jnp.sum(jnp.where(oh2, cnt_before, 0.0), axis=0, keepdims=True)
    zero = jnp.zeros((1, tm), F32)
    meta_out[...] = jnp.concatenate([e1, e2, gate1, gate2, lr1, lr2, zero, zero], axis=0)
    cnt_out[...] = jnp.broadcast_to(jnp.sum(oh, axis=1, keepdims=True), (MOE_EXPERTS, LANES))


def moe_router(x, gain, w_grp, b_grp, w_exp, b_exp):
    n, d = x.shape
    tm = MOE_TILE
    wt = jnp.zeros((LANES, d), F32).at[0:MOE_GROUPS].set(w_grp.T).at[8:8 + MOE_EXPERTS].set(w_exp.T)
    bt = jnp.zeros((LANES, 1), F32).at[0:MOE_GROUPS, 0].set(b_grp).at[8:8 + MOE_EXPERTS, 0].set(b_exp)
    return pl.pallas_call(
        _router_kernel,
        grid=(n // tm,),
        in_specs=[pl.BlockSpec((tm, d), lambda i: (i, 0)),
                  pl.BlockSpec((1, d), lambda i: (0, 0)),
                  pl.BlockSpec((LANES, d), lambda i: (0, 0)),
                  pl.BlockSpec((LANES, 1), lambda i: (0, 0))],
        out_specs=[pl.BlockSpec((tm, d), lambda i: (i, 0)),
                   pl.BlockSpec((8, tm), lambda i: (0, i)),
                   pl.BlockSpec((MOE_EXPERTS, LANES), lambda i: (i, 0))],
        out_shape=[jax.ShapeDtypeStruct((n, d), BF16),
                   jax.ShapeDtypeStruct((8, n), F32),
                   jax.ShapeDtypeStruct((n // tm * MOE_EXPERTS, LANES), F32)],
        compiler_params=_params("parallel"),
        name="moe_router",
    )(x, gain.reshape(1, d), wt, bt)


def _run_copies(tab_ref, first, count, chunks, make_copy, act):
    def body(e, _):
        base = (first + e) * 3
        length = tab_ref[base]
        src = tab_ref[base + 1]
        dst = tab_ref[base + 2]
        for ck in chunks:
            @pl.when((length & ck) != 0)
            def _():
                off = length & (-2 * ck)
                act(make_copy(pl.multiple_of(src + off, MOE_ALIGN), pl.multiple_of(dst + off, MOE_ALIGN), ck))
        return 0
    lax.fori_loop(0, count, body, 0)


def _start(cp):
    cp.start()


def _wait(cp):
    cp.wait()


def _dispatch_kernel(tab_ref, h_ref, pos_ref, rows_hbm, sorted_ref, zero_ref, sem, zsem):
    t = pl.program_id(0)
    nt = pl.num_programs(0)
    slot = t % 2
    r = _iota((MOE_SORTED, MOE_TILE), 0)
    perm = jnp.where((r == pos_ref[0:1, :]) | (r == pos_ref[1:2, :]), 1.0, 0.0).astype(BF16)
    sorted_ref[slot] = _dot(perm, h_ref[...]).astype(BF16)

    def run_copy(s):
        def make(src, dst, ck):
            return pltpu.make_async_copy(sorted_ref.at[s, pl.ds(src, ck)], rows_hbm.at[pl.ds(dst, ck)], sem.at[s])
        return make

    def zero_copy(src, dst, ck):
        del src
        return pltpu.make_async_copy(zero_ref.at[pl.ds(0, ck)], rows_hbm.at[pl.ds(dst, ck)], zsem)

    _run_copies(tab_ref, t * MOE_EXPERTS, MOE_EXPERTS, RUN_CHUNKS, run_copy(slot), _start)

    @pl.when(t > 0)
    def _():
        _run_copies(tab_ref, (t - 1) * MOE_EXPERTS, MOE_EXPERTS, RUN_CHUNKS, run_copy(1 - slot), _wait)

    @pl.when(t == nt - 1)
    def _():
        zero_ref[...] = jnp.zeros_like(zero_ref)
        pad = nt * MOE_EXPERTS
        _run_copies(tab_ref, pad, MOE_EXPERTS, PAD_CHUNKS, zero_copy, _start)
        spare = (pad + MOE_EXPERTS) * 3
        n_spare = tab_ref[spare]

        def spare_copy(i):
            dst = pl.multiple_of(tab_ref[spare + 2] + i * PAD_ROWS, PAD_ROWS)
            return zero_copy(0, dst, PAD_ROWS)

        lax.fori_loop(0, n_spare, lambda i, c: (_start(spare_copy(i)), c)[1], 0)
        _run_copies(tab_ref, t * MOE_EXPERTS, MOE_EXPERTS, RUN_CHUNKS, run_copy(slot), _wait)
        _run_copies(tab_ref, pad, MOE_EXPERTS, PAD_CHUNKS, zero_copy, _wait)
        lax.fori_loop(0, n_spare, lambda i, c: (_wait(spare_copy(i)), c)[1], 0)


def moe_dispatch(tab, h2, pos, n_rows):
    n, d = h2.shape
    tm = MOE_TILE
    return pl.pallas_call(
        _dispatch_kernel,
        grid_spec=pltpu.PrefetchScalarGridSpec(
            num_scalar_prefetch=1,
            grid=(n // tm,),
            in_specs=[pl.BlockSpec((tm, d), lambda i, tab: (i, 0)),
                      pl.BlockSpec((8, tm), lambda i, tab: (0, i))],
            out_specs=pl.BlockSpec(memory_space=pl.ANY),
            scratch_shapes=[pltpu.VMEM((2, MOE_SORTED, d), BF16), pltpu.VMEM((PAD_ROWS, d), BF16),
                            pltpu.SemaphoreType.DMA((2,)), pltpu.SemaphoreType.DMA],
        ),
        out_shape=jax.ShapeDtypeStruct((n_rows, d), BF16),
        compiler_params=_params("arbitrary"),
        name="moe_dispatch",
    )(tab, h2, pos)


def _expert_kernel(be_ref, nu_ref, x_ref, w1_ref, w3_ref, w2_ref, y_ref, w1b, w3b, w2b):
    b = pl.program_id(0)
    used = b < nu_ref[0]
    new_expert = (b == 0) | (be_ref[b] != be_ref[jnp.maximum(b - 1, 0)])

    @pl.when(used & new_expert)
    def _():
        w1b[...] = w1_ref[0, 0].astype(BF16)
        w3b[...] = w3_ref[0, 0].astype(BF16)
        w2b[...] = w2_ref[0, 0].astype(BF16)

    @pl.when(used)
    def _():
        xb = x_ref[...]
        a = _dot(xb, w1b[...])
        c = _dot(xb, w3b[...])
        hid = (a * jax.nn.sigmoid(a) * c).astype(BF16)
        y_ref[...] = _dot(hid, w2b[...]).astype(y_ref.dtype)

    @pl.when(jnp.logical_not(used))
    def _():
        y_ref[...] = jnp.zeros_like(y_ref)


def moe_experts(block_expert, n_used, rows, layer, w1, w3, w2):
    n_rows = rows.shape[0]
    nb = n_rows // MOE_ROWS
    d, ff = w1.shape[2], w1.shape[3]

    def xmap(b, be, nu):
        return (jnp.minimum(b, nu[0] - 1), 0)

    def wmap(b, be, nu):
        return (layer, be[jnp.minimum(b, nu[0] - 1)], 0, 0)

    return pl.pallas_call(
        _expert_kernel,
        grid_spec=pltpu.PrefetchScalarGridSpec(
            num_scalar_prefetch=2,
            grid=(nb,),
            in_specs=[pl.BlockSpec((MOE_ROWS, d), xmap),
                      pl.BlockSpec((1, 1, d, ff), wmap),
                      pl.BlockSpec((1, 1, d, ff), wmap),
                      pl.BlockSpec((1, 1, ff, d), wmap)],
            out_specs=pl.BlockSpec((MOE_ROWS, d), lambda b, be, nu: (b, 0)),
            scratch_shapes=[pltpu.VMEM((d, ff), BF16), pltpu.VMEM((d, ff), BF16), pltpu.VMEM((ff, d), BF16)],
        ),
        out_shape=jax.ShapeDtypeStruct((n_rows, d), BF16),
        compiler_params=_params("arbitrary"),
        name="moe_experts",
    )(block_expert, n_used, rows, w1, w3, w2)


def _combine_kernel(final_norm, tab_ref, y_hbm, pos_ref, gate_ref, x_ref, g_ref, o_ref, ys_ref, sem):
    t = pl.program_id(0)
    nt = pl.num_programs(0)
    slot = t % 2

    def run_copy(s):
        def make(src, dst, ck):
            return pltpu.make_async_copy(y_hbm.at[pl.ds(dst, ck)], ys_ref.at[s, pl.ds(src, ck)], sem.at[s])
        return make

    @pl.when(t == 0)
    def _():
        ys_ref[...] = jnp.zeros_like(ys_ref)
        _run_copies(tab_ref, 0, MOE_EXPERTS, RUN_CHUNKS, run_copy(0), _start)

    @pl.when(t + 1 < nt)
    def _():
        _run_copies(tab_ref, (t + 1) * MOE_EXPERTS, MOE_EXPERTS, RUN_CHUNKS, run_copy(1 - slot), _start)

    r = _iota((MOE_SORTED, MOE_TILE), 0)
    hit1 = r == pos_ref[0:1, :]
    hit2 = r == pos_ref[1:2, :]
    wgt = jnp.where(hit1, gate_ref[2:3, :], 0.0) + jnp.where(hit2, gate_ref[3:4, :], 0.0)
    row_gate = jnp.sum(wgt, axis=1, keepdims=True)
    row_used = jnp.sum(jnp.where(hit1 | hit2, 1.0, 0.0), axis=1, keepdims=True) > 0.0
    perm = jnp.where(hit1 | hit2, 1.0, 0.0).astype(BF16)
    _run_copies(tab_ref, t * MOE_EXPERTS, MOE_EXPERTS, RUN_CHUNKS, run_copy(slot), _wait)
    ys = jnp.where(row_used, ys_ref[slot].astype(F32) * row_gate, 0.0).astype(BF16)
    out = x_ref[...] + _dot_tn(perm, ys)
    if final_norm:
        out = _rms(out, g_ref[...])
    o_ref[...] = out


def moe_combine(tab, y, pos, meta, x, final_gain):
    n, d = x.shape
    tm = MOE_TILE
    final_norm = final_gain is not None
    gain = (final_gain if final_norm else jnp.ones((d,), F32)).reshape(1, d)
    return pl.pallas_call(
        functools.partial(_combine_kernel, final_norm),
        grid_spec=pltpu.PrefetchScalarGridSpec(
            num_scalar_prefetch=1,
            grid=(n // tm,),
            in_specs=[pl.BlockSpec(memory_space=pl.ANY),
                      pl.BlockSpec((8, tm), lambda i, tab: (0, i)),
                      pl.BlockSpec((8, tm), lambda i, tab: (0, i)),
                      pl.BlockSpec((tm, d), lambda i, tab: (i, 0)),
                      pl.BlockSpec((1, d), lambda i, tab: (0, 0))],
            out_specs=pl.BlockSpec((tm, d), lambda i, tab: (i, 0)),
            scratch_shapes=[pltpu.VMEM((2, MOE_SORTED, d), BF16), pltpu.SemaphoreType.DMA((2,))],
        ),
        out_shape=jax.ShapeDtypeStruct((n, d), F32),
        compiler_params=_params("arbitrary"),
        name="moe_combine",
    )(tab, y, pos, meta, x, gain)


def hierarchical_moe(x, gain, w_grp, b_grp, w_exp, b_exp, layer, w1, w3, w2, final_gain):
    n, d = x.shape
    tm = MOE_TILE
    nt = n // tm
    h2, meta, cnt = moe_router(x, gain, w_grp, b_grp, w_exp, b_exp)

    cnt = cnt.reshape(nt, MOE_EXPERTS, LANES)[:, :, 0].astype(I32)
    run = (cnt + MOE_ALIGN - 1) // MOE_ALIGN * MOE_ALIGN
    src = jnp.cumsum(run, axis=1) - run
    before = jnp.cumsum(run, axis=0) - run
    total = jnp.sum(run, axis=0)
    padded = (total + MOE_ROWS - 1) // MOE_ROWS * MOE_ROWS
    pad_end = jnp.cumsum(padded)
    dst = (pad_end - padded)[None, :] + before
    max_rows = 2 * n + nt * MOE_EXPERTS * (MOE_ALIGN - 1) + MOE_EXPERTS * (MOE_ROWS - 1)
    nb = -(-max_rows // MOE_ROWS)
    zero = jnp.zeros((MOE_EXPERTS,), I32)
    runs = jnp.stack([run, src, dst], axis=-1).reshape(-1, 3)
    pads = jnp.stack([padded - total, zero, pad_end - padded + total], axis=-1)
    spare = jnp.stack([(nb * MOE_ROWS - pad_end[-1]) // PAD_ROWS, zero[0], pad_end[-1]]).reshape(1, 3)
    tab = jnp.concatenate([runs, pads, spare], axis=0).reshape(-1).astype(I32)
    n_used = (pad_end[-1] // MOE_ROWS).astype(I32).reshape(1)
    block_start = jnp.arange(nb, dtype=I32) * MOE_ROWS
    block_expert = jnp.minimum(
        jnp.sum((pad_end[None, :] <= block_start[:, None]).astype(I32), axis=1), MOE_EXPERTS - 1).astype(I32)
    e = meta[0:2].astype(I32).reshape(2, nt, tm, 1)
    hit = e == jnp.arange(MOE_EXPERTS, dtype=I32)
    pos = jnp.sum(jnp.where(hit, src[None, :, None, :], 0), axis=-1).reshape(2, n) + meta[4:6].astype(I32)
    pos8 = jnp.zeros((8, n), I32).at[0:2].set(pos)

    rows = moe_dispatch(tab, h2, pos8, nb * MOE_ROWS)
    y = moe_experts(block_expert, n_used, rows, layer, w1, w3, w2)
    return moe_combine(tab, y, pos8, meta, x, final_gain)


def kernel(x, mem, positions, ln_mix, ln_mem, ln_memkv, ln_ffn, rw_mu, rw_w0, rw_w1, rw_w2, rw_a0, rw_a1, rw_a2, rw_g1, rw_g2, rw_kk, rw_ka, rw_rk, rw_wrkv, rw_lnx_g, rw_lnx_b, rw_wo, mb_wqkv, mb_wo, mx_wq, mx_wkv, mx_wo, moe_wg, moe_bg, moe_we, moe_be, moe_w1, moe_w3, moe_w2, ln_f):
    B, T, C = x.shape
    n = B * T
    depth = ln_mix.shape[0]
    xf = x.reshape(n, C)
    memf = mem.reshape(-1, C)
    for i in range(depth):
        j = i // 2
        if i % 2 == 0:
            r, k, v, kk, akk, lw, g = rwkv_mix(xf, T, ln_mix[i], rw_mu[j], rw_w0[j], rw_w1[j], rw_w2[j],
                                               rw_a0[j], rw_a1[j], rw_a2[j], rw_g1[j], rw_g2[j],
                                               rw_kk[j], rw_ka[j], rw_wrkv[j])
            y = rwkv_recurrence(r, k, v, kk, akk, lw, g, rw_lnx_g[j], rw_lnx_b[j], rw_rk[j], B, T)
            xf = linear_residual(y, rw_wo[j].astype(BF16), xf)
        else:
            qkv = qkv_rope(xf, ln_mix[i], positions, mb_wqkv[j])
            o = moba_attention(qkv, B, T)
            xf = linear_residual(o, mb_wo[j].astype(BF16), xf)
        kv = norm_linear(memf, ln_memkv[i], mx_wkv[i].astype(BF16))
        xf = mem_cross_attention(xf, ln_mem[i], mx_wq[i], kv, mx_wo[i], T)
        xf = hierarchical_moe(xf, ln_ffn[i], moe_wg[i], moe_bg[i], moe_we[i], moe_be[i],
                              i, moe_w1, moe_w3, moe_w2, ln_f if i == depth - 1 else None)
    return xf.reshape(B, T, C)
```

```python
import functools
import math

import jax
import jax.numpy as jnp
from jax import lax
from jax.experimental import pallas as pl
from jax.experimental.pallas import tpu as pltpu

F32 = jnp.float32
BF16 = jnp.bfloat16
I32 = jnp.int32
U32 = jnp.uint32

D_MODEL = 1024
HEAD_DIM = 64
PAIR = 2 * HEAD_DIM
N_PAIRS = D_MODEL // PAIR
RWKV_GN_EPS = 64e-5
RWKV_CHUNK = 64
RWKV_CHUNKS_PER_STEP = 4
MOBA_BLOCK = 256
MOBA_TOPK = 3
VT_ROWS = HEAD_DIM + 16
ROPE_THETA = 10000.0
MEM_HEADS = 4
MEM_HEAD_DIM = D_MODEL // MEM_HEADS
MOE_GROUPS = 4
MOE_EPG = 8
MOE_EXPERTS = MOE_GROUPS * MOE_EPG
MOE_FF = D_MODEL // 2
RMS_EPS = 1e-6
NEG_INF = -1e30

LANES = 128
ROW_ALIGN = 8
MOE_TILE = 512
MOE_ROWS = 512
MOE_ALIGN = 16
MOE_SORTED = 2 * MOE_TILE + MOE_EXPERTS * (MOE_ALIGN - 1) + (-(2 * MOE_TILE + MOE_EXPERTS * (MOE_ALIGN - 1))) % MOE_ALIGN
RUN_CHUNKS = tuple(MOE_ALIGN << s for s in range((MOE_TILE // MOE_ALIGN).bit_length() - 1, -1, -1))
RUN_TAB = len(RUN_CHUNKS) * (1 + 2 * MOE_EXPERTS)
PAD_ROWS = MOE_ROWS // 2
PAD_CHUNKS = tuple(MOE_ALIGN << s for s in range((PAD_ROWS // MOE_ALIGN).bit_length() - 1, -1, -1))
VMEM_LIMIT = 56 * 2 ** 20


def _params(*sem):
    return pltpu.CompilerParams(dimension_semantics=sem, vmem_limit_bytes=VMEM_LIMIT)


def _iota(shape, dim):
    return lax.broadcasted_iota(I32, shape, dim)


def _dot(a, b):
    return jnp.dot(a, b, preferred_element_type=F32)


def _dot_nt(a, b):
    return lax.dot_general(a, b, (((1,), (1,)), ((), ())), preferred_element_type=F32)


def _dot_tn(a, b):
    return lax.dot_general(a, b, (((0,), (0,)), ((), ())), preferred_element_type=F32)


def _split(x):
    hi = x.astype(BF16)
    lo = (x - hi.astype(F32)).astype(BF16)
    return hi, lo


def _rms(x, g):
    return x * lax.rsqrt(jnp.mean(x * x, axis=-1, keepdims=True) + RMS_EPS) * g


def _norm_linear_kernel(x_ref, g_ref, w_ref, o_ref):
    xn = _rms(x_ref[...], g_ref[...]).astype(BF16)
    o_ref[...] = _dot(xn, w_ref[...]).astype(o_ref.dtype)


def norm_linear(x, g, w, tm=256):
    n, k = x.shape
    dout = w.shape[1]
    return pl.pallas_call(
        _norm_linear_kernel,
        grid=(n // tm,),
        in_specs=[pl.BlockSpec((tm, k), lambda i: (i, 0)),
                  pl.BlockSpec((1, k), lambda i: (0, 0)),
                  pl.BlockSpec((k, dout), lambda i: (0, 0))],
        out_specs=pl.BlockSpec((tm, dout), lambda i: (i, 0)),
        out_shape=jax.ShapeDtypeStruct((n, dout), BF16),
        compiler_params=_params("parallel"),
        name="norm_linear",
    )(x, g.reshape(1, k), w)


def _head_sum(x_sq):
    r = _iota((PAIR, PAIR), 0) // HEAD_DIM
    c = _iota((PAIR, PAIR), 1) // HEAD_DIM
    bd = jnp.where(r == c, 1.0, 0.0).astype(BF16)
    hi, lo = _split(x_sq)
    return _dot(hi, bd) + _dot(lo, bd)


def _rwkv_mix_kernel(seq_len, x_ref, xp_ref, g_ref, mu_ref, vec_ref, wr_ref, wk_ref, wv_ref,
                     l1_ref, w2_ref, a2_ref, g2_ref,
                     r_out, k_out, v_out, kk_out, akk_out, lw_out, g_out):
    i = pl.program_id(0)
    tm = x_ref.shape[0]
    gain = g_ref[...]
    h = _rms(x_ref[...], gain)
    hp = _rms(xp_ref[...], gain)[ROW_ALIGN - 1:ROW_ALIGN, :]
    hp = jnp.where((i * tm) % seq_len == 0, 0.0, hp)
    hs = pltpu.roll(h, 1, 0)
    hs = jnp.where(_iota((tm, 1), 0) == 0, hp, hs)
    hb = h.astype(BF16)
    dxb = (hs - h).astype(BF16)
    mub = mu_ref[...].astype(BF16)

    def mix(s):
        return hb + dxb * mub[s:s + 1, :]

    vec = vec_ref[...]
    w0, a0, k_k, k_a = vec[0:1], vec[1:2], vec[2:3], vec[3:4]
    l1 = l1_ref[...]
    r = _dot(mix(0), wr_ref[...])
    k = _dot(mix(2), wk_ref[...])
    v = _dot(mix(3), wv_ref[...])
    tw = jnp.tanh(_dot(mix(1), l1[:, 0:64])).astype(BF16)
    lw = -math.exp(-0.5) * jax.nn.sigmoid(w0 + _dot(tw, w2_ref[...]))
    ta = _dot(mix(4), l1[:, 64:128]).astype(BF16)
    a = jax.nn.sigmoid(a0 + _dot(ta, a2_ref[...]))
    tg = jax.nn.sigmoid(_dot(mix(5), l1[:, 128:256])).astype(BF16)
    g = _dot(tg, g2_ref[...])

    kk = k * k_k
    for p in range(N_PAIRS):
        sl = slice(p * PAIR, (p + 1) * PAIR)
        kkp = kk[:, sl]
        kkn = kkp * lax.rsqrt(jnp.maximum(_head_sum(kkp * kkp), 1e-24))
        kk_out[:, sl] = kkn.astype(kk_out.dtype)
        akk_out[:, sl] = (kkn * a[:, sl]).astype(akk_out.dtype)
    r_out[...] = r.astype(r_out.dtype)
    k_out[...] = (k * (1.0 + (a - 1.0) * k_a)).astype(k_out.dtype)
    v_out[...] = v.astype(v_out.dtype)
    lw_out[...] = lw
    g_out[...] = g.astype(g_out.dtype)


def rwkv_mix(x, seq_len, gain, mu, w0, w1, w2, a0, a1, a2, g1, g2, k_k, k_a, w_rkv, tm=256):
    n, d = x.shape
    mu8 = jnp.zeros((8, d), F32).at[:6].set(mu)
    vec = jnp.zeros((8, d), F32).at[0].set(w0).at[1].set(a0).at[2].set(k_k).at[3].set(k_a)
    l1 = jnp.concatenate([w1, a1, g1], axis=1).astype(BF16)
    wb = w_rkv.astype(BF16)
    row = pl.BlockSpec((tm, d), lambda i: (i, 0))
    full = lambda a: pl.BlockSpec(a.shape, lambda i: (0,) * a.ndim)
    args = (x, x, gain.reshape(1, d), mu8, vec, wb[0], wb[1], wb[2], l1,
            w2.astype(BF16), a2.astype(BF16), g2.astype(BF16))
    in_specs = [row, pl.BlockSpec((ROW_ALIGN, d), lambda i: (jnp.maximum(i * (tm // ROW_ALIGN) - 1, 0), 0))]
    in_specs += [full(a) for a in args[2:]]
    outs = [jax.ShapeDtypeStruct((n, d), BF16)] * 5 + [jax.ShapeDtypeStruct((n, d), F32),
                                                       jax.ShapeDtypeStruct((n, d), BF16)]
    return pl.pallas_call(
        functools.partial(_rwkv_mix_kernel, seq_len),
        grid=(n // tm,),
        in_specs=in_specs,
        out_specs=[row] * 7,
        out_shape=outs,
        compiler_params=_params("parallel"),
        name="rwkv_mix",
    )(*args)


def _rwkv_rec_kernel(r_ref, k_ref, v_ref, kk_ref, akk_ref, lw_ref, g_ref, vec_ref, y_ref, s_ref):
    c = pl.program_id(1)
    L = RWKV_CHUNK

    @pl.when(c == 0)
    def _():
        s_ref[...] = jnp.zeros_like(s_ref)

    lane = _iota((1, PAIR), 1)
    m0 = lane < HEAD_DIM
    ri = _iota((2 * L, 2 * L), 0)
    ci = _iota((2 * L, 2 * L), 1)
    same = (ri // L) == (ci // L)
    strict = same & (ci < ri)
    incl = same & (ci <= ri)
    eye = jnp.where(ri == ci, 1.0, 0.0)
    tri = jnp.where(_iota((L, L), 1) <= _iota((L, L), 0), 1.0, 0.0).astype(BF16)
    rb = _iota((PAIR, PAIR), 0) // HEAD_DIM
    cb = _iota((PAIR, PAIR), 1) // HEAD_DIM
    bd = rb == cb

    def stack(x):
        return jnp.concatenate([jnp.where(m0, x, 0.0), jnp.where(m0, 0.0, x)], axis=0)

    def fold(x):
        return x[:L] + x[L:]

    def head_mean(x):
        s0 = jnp.sum(jnp.where(m0, x, 0.0), axis=-1, keepdims=True)
        s1 = jnp.sum(jnp.where(m0, 0.0, x), axis=-1, keepdims=True)
        return jnp.where(m0, s0, s1) * (1.0 / HEAD_DIM)

    vec = vec_ref[...]
    n_sub = r_ref.shape[0] // L
    pairs = range(n_sub * N_PAIRS)
    rws = [slice((q // N_PAIRS) * L, (q // N_PAIRS + 1) * L) for q in pairs]
    sls = [slice((q % N_PAIRS) * PAIR, (q % N_PAIRS + 1) * PAIR) for q in pairs]
    lw = [lw_ref[rws[p], sls[p]] for p in pairs]
    cum = []
    for p in pairs:
        lw_hi, lw_lo = _split(lw[p])
        cum.append(_dot(tri, lw_hi) + _dot(tri, lw_lo))
    dec_all, x_kap, x_r, a_t, k_t, sc = [], [], [], [], [], []
    for p in pairs:
        rw, sl = rws[p], sls[p]
        dec = jnp.exp(cum[p])
        inv = jnp.exp(-cum[p])
        dec_prev = jnp.exp(cum[p] - lw[p])
        dec_all.append(dec[L - 1:L, :])
        x_kap.append(stack(kk_ref[rw, sl].astype(F32) * dec_prev).astype(BF16))
        x_r.append(stack(r_ref[rw, sl].astype(F32) * dec))
        a_t.append((akk_ref[rw, sl].astype(F32) * inv).astype(BF16))
        k_t.append((k_ref[rw, sl].astype(F32) * inv).astype(BF16))
        xs = jnp.concatenate([x_kap[p], x_r[p].astype(BF16)], axis=0)
        ys = jnp.concatenate([a_t[p], a_t[p], k_t[p], k_t[p]], axis=0)
        sc.append(_dot_nt(xs, ys))
    mp = [jnp.where(strict, -sc[p][:2 * L, :2 * L], 0.0) for p in pairs]
    m_kk = [jnp.where(strict, sc[p][:2 * L, 2 * L:], 0.0).astype(BF16) for p in pairs]
    s_ra = [jnp.where(incl, sc[p][2 * L:, :2 * L], 0.0).astype(BF16) for p in pairs]
    s_rk = [jnp.where(incl, sc[p][2 * L:, 2 * L:], 0.0).astype(BF16) for p in pairs]
    v_st = [stack(v_ref[rws[p], sls[p]].astype(F32)).astype(BF16) for p in pairs]
    b1 = [_dot(m_kk[p], v_st[p]) for p in pairs]
    o1 = [_dot(s_rk[p], v_st[p]) for p in pairs]

    t_inv = [eye + mp[p] for p in pairs]
    for _ in range(int(math.log2(L)) - 1):
        mpb = [mp[p].astype(BF16) for p in pairs]
        mp = [_dot(mpb[p], mpb[p]) for p in pairs]
        t_inv = [t_inv[p] + _dot(t_inv[p].astype(BF16), mp[p].astype(BF16)) for p in pairs]

    wu = [_dot(t_inv[p].astype(BF16), jnp.concatenate([x_kap[p], b1[p].astype(BF16)], axis=1))
          for p in pairs]
    corr = [_dot(s_ra[p], wu[p].astype(BF16)) for p in pairs]
    g_m, h0t, r_hat, o0 = [], [], [], []
    for p in pairs:
        r_hat.append(fold(x_r[p] - corr[p][:, :PAIR]).astype(BF16))
        o0.append(fold(o1[p] - corr[p][:, PAIR:]))
        w_f = fold(wu[p][:, :PAIR]).astype(BF16)
        u0_f = fold(wu[p][:, PAIR:]).astype(BF16)
        g_m.append(jnp.where(bd, _dot_tn(a_t[p], w_f), 0.0).astype(BF16))
        h0t.append(jnp.where(bd, _dot_tn(v_ref[rws[p], sls[p]], k_t[p]) - _dot_tn(u0_f, a_t[p]), 0.0))
    o = []
    for p in pairs:
        hp = p % N_PAIRS
        s = s_ref[hp]
        sb = s.astype(BF16)
        o.append(_dot_nt(r_hat[p], sb) + o0[p])
        s_ref[hp] = (s - _dot_nt(sb, g_m[p]) + h0t[p]) * dec_all[p]
    for p in pairs:
        rw, sl = rws[p], sls[p]
        mean = head_mean(o[p])
        cen = o[p] - mean
        var = head_mean(cen * cen)
        gn = cen * lax.rsqrt(var + RWKV_GN_EPS) * vec[0:1, sl] + vec[1:2, sl]
        r = r_ref[rw, sl].astype(F32)
        k = k_ref[rw, sl].astype(F32)
        bonus = head_mean(r * k * vec[2:3, sl]) * HEAD_DIM * v_ref[rw, sl].astype(F32)
        y_ref[rw, sl] = ((gn + bonus) * g_ref[rw, sl].astype(F32)).astype(y_ref.dtype)


def rwkv_recurrence(r, k, v, kk, akk, lw, g, lnx_g, lnx_b, r_k, batch, seq_len):
    n, d = r.shape
    rows = RWKV_CHUNK * RWKV_CHUNKS_PER_STEP
    nc = seq_len // rows
    vec = jnp.zeros((8, d), F32).at[0].set(lnx_g).at[1].set(lnx_b).at[2].set(r_k.reshape(d))
    blk = pl.BlockSpec((rows, d), lambda b, c: (b * nc + c, 0))
    return pl.pallas_call(
        _rwkv_rec_kernel,
        grid=(batch, nc),
        in_specs=[blk] * 7 + [pl.BlockSpec((8, d), lambda b, c: (0, 0))],
        out_specs=blk,
        out_shape=jax.ShapeDtypeStruct((n, d), BF16),
        scratch_shapes=[pltpu.VMEM((N_PAIRS, PAIR, PAIR), F32)],
        compiler_params=_params("parallel", "arbitrary"),
        name="rwkv_recurrence",
    )(r, k, v, kk, akk, lw, g, vec)


def _qkv_rope_kernel(x_ref, g_ref, pos_ref, inv_ref, w_ref, o_ref, xn_ref, cs_ref):
    j = pl.program_id(1)
    tm = x_ref.shape[0]

    @pl.when(j == 0)
    def _():
        xn_ref[...] = _rms(x_ref[...], g_ref[...]).astype(BF16)
        ang = pos_ref[...].astype(F32) * inv_ref[...]
        first = (_iota((1, PAIR), 1) % HEAD_DIM) < HEAD_DIM // 2
        sn = jnp.sin(ang)
        cs_ref[0] = jnp.cos(ang)
        cs_ref[1] = jnp.where(first, -sn, 0.0)
        cs_ref[2] = jnp.where(first, 0.0, sn)

    y = _dot(xn_ref[...], w_ref[...])

    @pl.when(j == 2)
    def _():
        o_ref[...] = y.astype(o_ref.dtype)

    @pl.when(j < 2)
    def _():
        for p in range(N_PAIRS):
            sl = slice(p * PAIR, (p + 1) * PAIR)
            t = y[:, sl]
            rot = (t * cs_ref[0] + pltpu.roll(t, PAIR - HEAD_DIM // 2, 1) * cs_ref[1]
                   + pltpu.roll(t, HEAD_DIM // 2, 1) * cs_ref[2])
            o_ref[:, sl] = rot.astype(o_ref.dtype)


def qkv_rope(x, gain, positions, w_qkv, tm=512):
    n, d = x.shape
    half = HEAD_DIM // 2
    inv = ROPE_THETA ** (-jnp.arange(half, dtype=F32) * 2.0 / HEAD_DIM)
    inv128 = jnp.tile(inv, PAIR // half).reshape(1, PAIR)
    col_scale = jnp.where(jnp.arange(3 * d) < d, math.log2(math.e) / math.sqrt(HEAD_DIM), 1.0).astype(F32)
    return pl.pallas_call(
        _qkv_rope_kernel,
        grid=(n // tm, 3),
        in_specs=[pl.BlockSpec((tm, d), lambda i, j: (i, 0)),
                  pl.BlockSpec((1, d), lambda i, j: (0, 0)),
                  pl.BlockSpec((tm, 1), lambda i, j: (i, 0)),
                  pl.BlockSpec((1, PAIR), lambda i, j: (0, 0)),
                  pl.BlockSpec((d, d), lambda i, j: (0, j))],
        out_specs=pl.BlockSpec((tm, d), lambda i, j: (i, j)),
        out_shape=jax.ShapeDtypeStruct((n, 3 * d), BF16),
        scratch_shapes=[pltpu.VMEM((tm, d), BF16), pltpu.VMEM((3, tm, PAIR), F32)],
        compiler_params=_params("parallel", "arbitrary"),
        name="qkv_rope",
    )(x, gain.reshape(1, d), positions.reshape(n, 1), inv128, (w_qkv * col_scale).astype(BF16))


def _moba_kernel(q_ref, k_ref, v_ref, o_ref, vt_ref):
    nblk = k_ref.shape[0] // MOBA_BLOCK
    bq = MOBA_BLOCK
    km_rows = 16

    means = []
    ones_row = jnp.where(_iota((VT_ROWS - HEAD_DIM, bq), 0) == 0, 1.0, 0.0).astype(BF16)
    for n in range(nblk):
        cols = slice(n * bq, (n + 1) * bq)
        means.append(jnp.mean(k_ref[cols, :].astype(F32), axis=0, keepdims=True))
        vt = v_ref[cols, :].astype(F32).T.astype(BF16)
        for h in range(2):
            vt_ref[h * VT_ROWS:h * VT_ROWS + HEAD_DIM, cols] = vt[h * HEAD_DIM:(h + 1) * HEAD_DIM]
            vt_ref[h * VT_ROWS + HEAD_DIM:(h + 1) * VT_ROWS, cols] = ones_row
    km_hi, km_lo = _split(jnp.concatenate(means + [jnp.zeros((km_rows - nblk, PAIR), F32)], axis=0))

    lane = _iota((1, PAIR), 1)
    own = (lane < HEAD_DIM, lane >= HEAD_DIM)
    spare = (HEAD_DIM, 0)
    blk = _iota((km_rows, bq), 0)
    causal = _iota((bq, bq), 0) <= _iota((bq, bq), 1)
    eye = jnp.where(_iota((bq, bq), 0) == _iota((bq, bq), 1), 1.0, 0.0).astype(BF16)

    def augmented_queries(c, h):
        q = q_ref[c * bq:(c + 1) * bq, :]
        qm = jnp.where(own[h], q, jnp.zeros_like(q))
        past = blk < c
        gate = _dot_nt(km_hi, qm) + _dot_nt(km_lo, qm)
        gate = jnp.where(past, gate, NEG_INF)
        rank = jnp.zeros(gate.shape, F32)
        for m in range(c):
            row = gate[m:m + 1, :]
            tie = jnp.where(m < blk, 1.0, 0.0)
            rank = rank + jnp.where(row > gate, 1.0, jnp.where(row == gate, tie, 0.0))
        drop = jnp.where((blk < nblk) & jnp.logical_not(past & (rank < MOBA_TOPK)), 1.0, 0.0)
        pad = [jnp.zeros((HEAD_DIM - km_rows, bq), F32)]
        rows = ([jnp.zeros((HEAD_DIM, bq), F32), drop] + pad) if spare[h] else ([drop] + pad + [jnp.zeros((HEAD_DIM, bq), F32)])
        dropped = _dot_nt(eye, jnp.concatenate(rows, axis=0).astype(BF16))
        return jnp.where(own[h], q, jnp.where(dropped > 0.5, NEG_INF, 0.0).astype(BF16))

    def score(c, h):
        k_aug = []
        for n in range(c + 1):
            kb = k_ref[n * bq:(n + 1) * bq, :]
            marker = jnp.where(lane == spare[h] + n, 1.0, 0.0) if n < c else jnp.zeros(lane.shape, F32)
            k_aug.append(jnp.where(own[h], kb, marker.astype(BF16)))
        return _dot_nt(jnp.concatenate(k_aug, axis=0), augmented_queries(c, h))

    def attend(c, h, s):
        parts = [s[n * bq:(n + 1) * bq] for n in range(c)]
        parts.append(jnp.where(causal, s[c * bq:], NEG_INF))
        top = parts[0].max(axis=0, keepdims=True)
        for part in parts[1:]:
            top = jnp.maximum(top, part.max(axis=0, keepdims=True))
        probs = jnp.concatenate([jnp.exp2((part - top).astype(BF16)) for part in parts], axis=0)
        acc = _dot(vt_ref[h * VT_ROWS:(h + 1) * VT_ROWS, 0:(c + 1) * bq], probs)
        return acc[:HEAD_DIM] / acc[HEAD_DIM:HEAD_DIM + 1]

    units = [(c, h) for c in range(nblk) for h in range(2)]
    ahead = 2
    pending = [score(*u) for u in units[:ahead]]
    outs = {}
    for i, (c, h) in enumerate(units):
        s = pending.pop(0)
        if i + ahead < len(units):
            pending.append(score(*units[i + ahead]))
        outs[h] = attend(c, h, s)
        if h == 1:
            o_ref[c * bq:(c + 1) * bq, :] = jnp.concatenate([outs[0], outs[1]], axis=0).T.astype(o_ref.dtype)


def moba_attention(qkv, batch, seq_len):
    n = qkv.shape[0]
    return pl.pallas_call(
        _moba_kernel,
        grid=(batch, N_PAIRS),
        in_specs=[pl.BlockSpec((seq_len, PAIR), lambda b, p: (b, p)),
                  pl.BlockSpec((seq_len, PAIR), lambda b, p: (b, N_PAIRS + p)),
                  pl.BlockSpec((seq_len, PAIR), lambda b, p: (b, 2 * N_PAIRS + p))],
        out_specs=pl.BlockSpec((seq_len, PAIR), lambda b, p: (b, p)),
        out_shape=jax.ShapeDtypeStruct((n, D_MODEL), BF16),
        scratch_shapes=[pltpu.VMEM((2 * VT_ROWS, seq_len), BF16)],
        compiler_params=_params("parallel", "parallel"),
        name="moba_attention",
    )(qkv, qkv, qkv)


def _mem_xattn_kernel(y_ref, wy_ref, x_ref, g_ref, wq_ref, kv_ref, wo_ref, o_ref):
    x = x_ref[...] + _dot(y_ref[...], wy_ref[...])
    xn = _rms(x, g_ref[...]).astype(BF16)
    q = (_dot(xn, wq_ref[...]) * (1.0 / math.sqrt(MEM_HEAD_DIM))).astype(BF16)
    outs = []
    for h in range(MEM_HEADS):
        sl = slice(h * MEM_HEAD_DIM, (h + 1) * MEM_HEAD_DIM)
        s = _dot_nt(q[:, sl], kv_ref[:, sl])
        s = s - jnp.max(s, axis=-1, keepdims=True)
        e = jnp.exp(s)
        pr = e / jnp.sum(e, axis=-1, keepdims=True)
        vh = kv_ref[:, D_MODEL + h * MEM_HEAD_DIM:D_MODEL + (h + 1) * MEM_HEAD_DIM]
        outs.append(_dot(pr.astype(BF16), vh).astype(BF16))
    o = jnp.concatenate(outs, axis=1)
    o_ref[...] = x + _dot(o, wo_ref[...])


def mem_cross_attention(y, wy, x, gain, wq, kv, wo, seq_len, tm=256):
    n, d = x.shape
    m = kv.shape[0] // (n // seq_len)
    per_seq = seq_len // tm
    return pl.pallas_call(
        _mem_xattn_kernel,
        grid=(n // tm,),
        in_specs=[pl.BlockSpec((tm, d), lambda i: (i, 0)),
                  pl.BlockSpec((d, d), lambda i: (0, 0)),
                  pl.BlockSpec((tm, d), lambda i: (i, 0)),
                  pl.BlockSpec((1, d), lambda i: (0, 0)),
                  pl.BlockSpec((d, d), lambda i: (0, 0)),
                  pl.BlockSpec((m, 2 * d), lambda i: (i // per_seq, 0)),
                  pl.BlockSpec((d, d), lambda i: (0, 0))],
        out_specs=pl.BlockSpec((tm, d), lambda i: (i, 0)),
        out_shape=jax.ShapeDtypeStruct((n, d), F32),
        compiler_params=_params("parallel"),
        name="mem_cross_attention",
    )(y, wy.astype(BF16), x, gain.reshape(1, d), wq.astype(BF16), kv, wo.astype(BF16))


def _router_kernel(x_ref, g_ref, w_ref, b_ref, h_out, meta_out, cnt_out):
    tm = x_ref.shape[0]
    h2 = _rms(x_ref[...], g_ref[...])
    hi = h2.astype(BF16)
    h_out[...] = hi
    lo = (h2 - hi.astype(F32)).astype(BF16)
    w_hi, w_lo = _split(w_ref[...])
    lg = _dot_nt(w_hi, hi) + _dot_nt(w_hi, lo) + _dot_nt(w_lo, hi) + b_ref[...]
    row = _iota((8, tm), 0).astype(F32)

    def first_argmax(val, vmax):
        return jnp.min(jnp.where(val == vmax, row, 8.0), axis=0, keepdims=True)

    gl = jnp.where(row < MOE_GROUPS, lg[0:8], -jnp.inf)
    gmax = jnp.max(gl, axis=0, keepdims=True)
    p_g = 1.0 / jnp.sum(jnp.exp(gl - gmax), axis=0, keepdims=True)
    gidx = first_argmax(gl, gmax)
    el = jnp.zeros((8, tm), F32)
    for g in range(MOE_GROUPS):
        el = el + jnp.where(gidx == g, lg[8 + 8 * g:16 + 8 * g], 0.0)
    ee = jnp.exp(el - jnp.max(el, axis=0, keepdims=True))
    pe = ee / jnp.sum(ee, axis=0, keepdims=True)
    p1 = jnp.max(pe, axis=0, keepdims=True)
    i1 = first_argmax(pe, p1)
    pe2 = jnp.where(row == i1, -1.0, pe)
    p2 = jnp.max(pe2, axis=0, keepdims=True)
    i2 = first_argmax(pe2, p2)
    e1 = gidx * MOE_EPG + i1
    e2 = gidx * MOE_EPG + i2
    gate1 = p_g * p1 / (p1 + p2)
    gate2 = p_g * p2 / (p1 + p2)

    erow = _iota((MOE_EXPERTS, tm), 0).astype(F32)
    oh1 = erow == e1
    oh2 = erow == e2
    oh = jnp.where(oh1 | oh2, 1.0, 0.0)
    before = jnp.where(_iota((tm, tm), 0) < _iota((tm, tm), 1), 1.0, 0.0).astype(BF16)
    cnt_before = _dot(oh.astype(BF16), before)
    lr1 = jnp.sum(jnp.where(oh1, cnt_before, 0.0), axis=0, keepdims=True)
    lr2 = jnp.sum(jnp.where(oh2, cnt_before, 0.0), axis=0, keepdims=True)
    zero = jnp.zeros((1, tm), F32)
    meta_out[...] = jnp.concatenate([e1, e2, gate1, gate2, lr1, lr2, zero, zero], axis=0)
    cnt_out[...] = jnp.broadcast_to(jnp.sum(oh, axis=1, keepdims=True), (MOE_EXPERTS, LANES))


def moe_router(x, gain, w_grp, b_grp, w_exp, b_exp):
    n, d = x.shape
    tm = MOE_TILE
    wt = jnp.zeros((LANES, d), F32).at[0:MOE_GROUPS].set(w_grp.T).at[8:8 + MOE_EXPERTS].set(w_exp.T)
    bt = jnp.zeros((LANES, 1), F32).at[0:MOE_GROUPS, 0].set(b_grp).at[8:8 + MOE_EXPERTS, 0].set(b_exp)
    return pl.pallas_call(
        _router_kernel,
        grid=(n // tm,),
        in_specs=[pl.BlockSpec((tm, d), lambda i: (i, 0)),
                  pl.BlockSpec((1, d), lambda i: (0, 0)),
                  pl.BlockSpec((LANES, d), lambda i: (0, 0)),
                  pl.BlockSpec((LANES, 1), lambda i: (0, 0))],
        out_specs=[pl.BlockSpec((tm, d), lambda i: (i, 0)),
                   pl.BlockSpec((8, tm), lambda i: (0, i)),
                   pl.BlockSpec((MOE_EXPERTS, LANES), lambda i: (i, 0))],
        out_shape=[jax.ShapeDtypeStruct((n, d), BF16),
                   jax.ShapeDtypeStruct((8, n), F32),
                   jax.ShapeDtypeStruct((n // tm * MOE_EXPERTS, LANES), F32)],
        compiler_params=_params("parallel"),
        name="moe_router",
    )(x, gain.reshape(1, d), wt, bt)


def _chunk_copies(tab_ref, t, make_copy, act):
    base = t * RUN_TAB
    for ci, ck in enumerate(RUN_CHUNKS):
        first = base + len(RUN_CHUNKS) + ci * 2 * MOE_EXPERTS

        def body(i, _, first=first, ck=ck):
            src = tab_ref[first + 2 * i]
            dst = tab_ref[first + 2 * i + 1]
            act(make_copy(pl.multiple_of(src, MOE_ALIGN), pl.multiple_of(dst, MOE_ALIGN), ck))
            return 0

        lax.fori_loop(0, tab_ref[base + ci], body, 0)


def _run_copies(tab_ref, first, count, chunks, make_copy, act):
    def body(e, _):
        base = first + e * 3
        length = tab_ref[base]
        src = tab_ref[base + 1]
        dst = tab_ref[base + 2]
        for ck in chunks:
            @pl.when((length & ck) != 0)
            def _():
                off = length & (-2 * ck)
                act(make_copy(pl.multiple_of(src + off, MOE_ALIGN), pl.multiple_of(dst + off, MOE_ALIGN), ck))
        return 0
    lax.fori_loop(0, count, body, 0)


def _start(cp):
    cp.start()


def _wait(cp):
    cp.wait()


def _dispatch_kernel(tab_ref, h_ref, pos_ref, rows_hbm, sorted_ref, zero_ref, sem, zsem):
    t = pl.program_id(0)
    nt = pl.num_programs(0)
    slot = t % 2
    r = _iota((MOE_SORTED, MOE_TILE), 0)
    perm = jnp.where((r == pos_ref[0:1, :]) | (r == pos_ref[1:2, :]), 1.0, 0.0).astype(BF16)
    sorted_ref[slot] = _dot(perm, h_ref[...]).astype(BF16)

    def run_copy(s):
        def make(src, dst, ck):
            return pltpu.make_async_copy(sorted_ref.at[s, pl.ds(src, ck)], rows_hbm.at[pl.ds(dst, ck)], sem.at[s])
        return make

    def zero_copy(src, dst, ck):
        del src
        return pltpu.make_async_copy(zero_ref.at[pl.ds(0, ck)], rows_hbm.at[pl.ds(dst, ck)], zsem)

    _chunk_copies(tab_ref, t, run_copy(slot), _start)

    @pl.when(t > 0)
    def _():
        _chunk_copies(tab_ref, t - 1, run_copy(1 - slot), _wait)

    @pl.when(t == nt - 1)
    def _():
        zero_ref[...] = jnp.zeros_like(zero_ref)
        pad = nt * RUN_TAB
        _run_copies(tab_ref, pad, MOE_EXPERTS, PAD_CHUNKS, zero_copy, _start)
        spare = pad + MOE_EXPERTS * 3
        n_spare = tab_ref[spare]

        def spare_copy(i):
            dst = pl.multiple_of(tab_ref[spare + 2] + i * PAD_ROWS, PAD_ROWS)
            return zero_copy(0, dst, PAD_ROWS)

        lax.fori_loop(0, n_spare, lambda i, c: (_start(spare_copy(i)), c)[1], 0)
        _chunk_copies(tab_ref, t, run_copy(slot), _wait)
        _run_copies(tab_ref, pad, MOE_EXPERTS, PAD_CHUNKS, zero_copy, _wait)
        lax.fori_loop(0, n_spare, lambda i, c: (_wait(spare_copy(i)), c)[1], 0)


def moe_dispatch(tab, h2, pos, n_rows):
    n, d = h2.shape
    tm = MOE_TILE
    return pl.pallas_call(
        _dispatch_kernel,
        grid_spec=pltpu.PrefetchScalarGridSpec(
            num_scalar_prefetch=1,
            grid=(n // tm,),
            in_specs=[pl.BlockSpec((tm, d), lambda i, tab: (i, 0)),
                      pl.BlockSpec((8, tm), lambda i, tab: (0, i))],
            out_specs=pl.BlockSpec(memory_space=pl.ANY),
            scratch_shapes=[pltpu.VMEM((2, MOE_SORTED, d), BF16), pltpu.VMEM((PAD_ROWS, d), BF16),
                            pltpu.SemaphoreType.DMA((2,)), pltpu.SemaphoreType.DMA],
        ),
        out_shape=jax.ShapeDtypeStruct((n_rows, d), BF16),
        compiler_params=_params("arbitrary"),
        name="moe_dispatch",
    )(tab, h2, pos)


def _expert_kernel(be_ref, nu_ref, x_ref, w1_ref, w3_ref, w2_ref, y_ref, w1b, w3b, w2b):
    b = pl.program_id(0)
    used = b < nu_ref[0]
    new_expert = (b == 0) | (be_ref[b] != be_ref[jnp.maximum(b - 1, 0)])

    @pl.when(used & new_expert)
    def _():
        w1b[...] = w1_ref[0, 0].astype(BF16)
        w3b[...] = w3_ref[0, 0].astype(BF16)
        w2b[...] = w2_ref[0, 0].astype(BF16)

    @pl.when(used)
    def _():
        xb = x_ref[...]
        a = _dot(xb, w1b[...])
        c = _dot(xb, w3b[...])
        hid = (a * jax.nn.sigmoid(a) * c).astype(BF16)
        y_ref[...] = _dot(hid, w2b[...]).astype(y_ref.dtype)

    @pl.when(jnp.logical_not(used))
    def _():
        y_ref[...] = jnp.zeros_like(y_ref)


def moe_experts(block_expert, n_used, rows, layer, w1, w3, w2):
    n_rows = rows.shape[0]
    nb = n_rows // MOE_ROWS
    d, ff = w1.shape[2], w1.shape[3]

    def xmap(b, be, nu):
        return (jnp.minimum(b, nu[0] - 1), 0)

    def wmap(b, be, nu):
        return (layer, be[jnp.minimum(b, nu[0] - 1)], 0, 0)

    return pl.pallas_call(
        _expert_kernel,
        grid_spec=pltpu.PrefetchScalarGridSpec(
            num_scalar_prefetch=2,
            grid=(nb,),
            in_specs=[pl.BlockSpec((MOE_ROWS, d), xmap),
                      pl.BlockSpec((1, 1, d, ff), wmap),
                      pl.BlockSpec((1, 1, d, ff), wmap),
                      pl.BlockSpec((1, 1, ff, d), wmap)],
            out_specs=pl.BlockSpec((MOE_ROWS, d), lambda b, be, nu: (b, 0)),
            scratch_shapes=[pltpu.VMEM((d, ff), BF16), pltpu.VMEM((d, ff), BF16), pltpu.VMEM((ff, d), BF16)],
        ),
        out_shape=jax.ShapeDtypeStruct((n_rows, d), BF16),
        compiler_params=_params("arbitrary"),
        name="moe_experts",
    )(block_expert, n_used, rows, w1, w3, w2)


def _combine_kernel(final_norm, tab_ref, y_hbm, pos_ref, gate_ref, x_ref, g_ref, o_ref, ys_ref, sem):
    t = pl.program_id(0)
    nt = pl.num_programs(0)
    slot = t % 2

    def run_copy(s):
        def make(src, dst, ck):
            return pltpu.make_async_copy(y_hbm.at[pl.ds(dst, ck)], ys_ref.at[s, pl.ds(src, ck)], sem.at[s])
        return make

    @pl.when(t == 0)
    def _():
        ys_ref[...] = jnp.zeros_like(ys_ref)
        _chunk_copies(tab_ref, 0, run_copy(0), _start)

    @pl.when(t + 1 < nt)
    def _():
        _chunk_copies(tab_ref, t + 1, run_copy(1 - slot), _start)

    r = _iota((MOE_SORTED, MOE_TILE), 0)
    hit1 = r == pos_ref[0:1, :]
    hit2 = r == pos_ref[1:2, :]
    wgt = jnp.where(hit1, gate_ref[2:3, :], jnp.where(hit2, gate_ref[3:4, :], 0.0))
    row_gate = jnp.sum(wgt, axis=1, keepdims=True)
    row_used = row_gate > 0.0
    perm = jnp.where(hit1 | hit2, 1.0, 0.0).astype(BF16)
    _chunk_copies(tab_ref, t, run_copy(slot), _wait)
    ys = jnp.where(row_used, ys_ref[slot].astype(F32) * row_gate, 0.0).astype(BF16)
    out = x_ref[...] + _dot_tn(perm, ys)
    if final_norm:
        out = _rms(out, g_ref[...])
    o_ref[...] = out


def moe_combine(tab, y, pos, meta, x, final_gain):
    n, d = x.shape
    tm = MOE_TILE
    final_norm = final_gain is not None
    gain = (final_gain if final_norm else jnp.ones((d,), F32)).reshape(1, d)
    return pl.pallas_call(
        functools.partial(_combine_kernel, final_norm),
        grid_spec=pltpu.PrefetchScalarGridSpec(
            num_scalar_prefetch=1,
            grid=(n // tm,),
            in_specs=[pl.BlockSpec(memory_space=pl.ANY),
                      pl.BlockSpec((8, tm), lambda i, tab: (0, i)),
                      pl.BlockSpec((8, tm), lambda i, tab: (0, i)),
                      pl.BlockSpec((tm, d), lambda i, tab: (i, 0)),
                      pl.BlockSpec((1, d), lambda i, tab: (0, 0))],
            out_specs=pl.BlockSpec((tm, d), lambda i, tab: (i, 0)),
            scratch_shapes=[pltpu.VMEM((2, MOE_SORTED, d), BF16), pltpu.SemaphoreType.DMA((2,))],
        ),
        out_shape=jax.ShapeDtypeStruct((n, d), F32),
        compiler_params=_params("arbitrary"),
        name="moe_combine",
    )(tab, y, pos, meta, x, gain)


def hierarchical_moe(x, gain, w_grp, b_grp, w_exp, b_exp, layer, w1, w3, w2, final_gain):
    n, d = x.shape
    tm = MOE_TILE
    nt = n // tm
    h2, meta, cnt = moe_router(x, gain, w_grp, b_grp, w_exp, b_exp)

    cnt = cnt.reshape(nt, MOE_EXPERTS, LANES)[:, :, 0].astype(I32)
    run = (cnt + MOE_ALIGN - 1) // MOE_ALIGN * MOE_ALIGN
    src = jnp.cumsum(run, axis=1) - run
    before = jnp.cumsum(run, axis=0) - run
    total = jnp.sum(run, axis=0)
    padded = (total + MOE_ROWS - 1) // MOE_ROWS * MOE_ROWS
    pad_end = jnp.cumsum(padded)
    dst = (pad_end - padded)[None, :] + before
    max_rows = 2 * n + nt * MOE_EXPERTS * (MOE_ALIGN - 1) + MOE_EXPERTS * (MOE_ROWS - 1)
    nb = -(-max_rows // MOE_ROWS)
    sizes = jnp.array(RUN_CHUNKS, I32)[None, :, None]
    has = (run[:, None, :] & sizes) != 0
    above = run[:, None, :] & (-2 * sizes)
    place = jnp.where(has, jnp.cumsum(has.astype(I32), axis=2) - 1, -1)
    front = place[..., None] == jnp.arange(MOE_EXPERTS, dtype=I32)
    c_src = jnp.sum(jnp.where(front, (src[:, None, :] + above)[..., None], 0), axis=2)
    c_dst = jnp.sum(jnp.where(front, (dst[:, None, :] + above)[..., None], 0), axis=2)
    chunks = jnp.stack([c_src, c_dst], axis=-1).reshape(nt, -1)
    runs = jnp.concatenate([jnp.sum(has.astype(I32), axis=2), chunks], axis=1).reshape(-1)
    zero = jnp.zeros((MOE_EXPERTS,), I32)
    pads = jnp.stack([padded - total, zero, pad_end - padded + total], axis=-1)
    spare = jnp.stack([(nb * MOE_ROWS - pad_end[-1]) // PAD_ROWS, zero[0], pad_end[-1]])
    tab = jnp.concatenate([runs, pads.reshape(-1), spare]).astype(I32)
    n_used = (pad_end[-1] // MOE_ROWS).astype(I32).reshape(1)
    block_start = jnp.arange(nb, dtype=I32) * MOE_ROWS
    block_expert = jnp.minimum(
        jnp.sum((pad_end[None, :] <= block_start[:, None]).astype(I32), axis=1), MOE_EXPERTS - 1).astype(I32)
    e = meta[0:2].astype(I32).reshape(2, nt, tm, 1)
    hit = e == jnp.arange(MOE_EXPERTS, dtype=I32)
    pos = jnp.sum(jnp.where(hit, src[None, :, None, :], 0), axis=-1).reshape(2, n) + meta[4:6].astype(I32)
    pos8 = jnp.zeros((8, n), I32).at[0:2].set(pos)

    rows = moe_dispatch(tab, h2, pos8, nb * MOE_ROWS)
    y = moe_experts(block_expert, n_used, rows, layer, w1, w3, w2)
    return moe_combine(tab, y, pos8, meta, x, final_gain)


def kernel(x, mem, positions, ln_mix, ln_mem, ln_memkv, ln_ffn, rw_mu, rw_w0, rw_w1, rw_w2, rw_a0, rw_a1, rw_a2, rw_g1, rw_g2, rw_kk, rw_ka, rw_rk, rw_wrkv, rw_lnx_g, rw_lnx_b, rw_wo, mb_wqkv, mb_wo, mx_wq, mx_wkv, mx_wo, moe_wg, moe_bg, moe_we, moe_be, moe_w1, moe_w3, moe_w2, ln_f):
    B, T, C = x.shape
    n = B * T
    depth = ln_mix.shape[0]
    xf = x.reshape(n, C)
    memf = mem.reshape(-1, C)
    for i in range(depth):
        j = i // 2
        if i % 2 == 0:
            r, k, v, kk, akk, lw, g = rwkv_mix(xf, T, ln_mix[i], rw_mu[j], rw_w0[j], rw_w1[j], rw_w2[j],
                                               rw_a0[j], rw_a1[j], rw_a2[j], rw_g1[j], rw_g2[j],
                                               rw_kk[j], rw_ka[j], rw_wrkv[j])
            y = rwkv_recurrence(r, k, v, kk, akk, lw, g, rw_lnx_g[j], rw_lnx_b[j], rw_rk[j], B, T)
            w_mix_o = rw_wo[j]
        else:
            qkv = qkv_rope(xf, ln_mix[i], positions, mb_wqkv[j])
            y = moba_attention(qkv, B, T)
            w_mix_o = mb_wo[j]
        kv = norm_linear(memf, ln_memkv[i], mx_wkv[i].astype(BF16))
        xf = mem_cross_attention(y, w_mix_o, xf, ln_mem[i], mx_wq[i], kv, mx_wo[i], T)
        xf = hierarchical_moe(xf, ln_ffn[i], moe_wg[i], moe_bg[i], moe_we[i], moe_be[i],
                              i, moe_w1, moe_w3, moe_w2, ln_f if i == depth - 1 else None)
    return xf.reshape(B, T, C)
```

```python
import functools
import math

import jax
import jax.numpy as jnp
from jax import lax
from jax.experimental import pallas as pl
from jax.experimental.pallas import tpu as pltpu

F32 = jnp.float32
BF16 = jnp.bfloat16
I32 = jnp.int32
U32 = jnp.uint32

D_MODEL = 1024
HEAD_DIM = 64
PAIR = 2 * HEAD_DIM
N_PAIRS = D_MODEL // PAIR
RWKV_GN_EPS = 64e-5
RWKV_CHUNK = 64
RWKV_CHUNKS_PER_STEP = 4
MOBA_BLOCK = 256
MOBA_TOPK = 3
VT_ROWS = HEAD_DIM + 16
ROPE_THETA = 10000.0
MEM_HEADS = 4
MEM_HEAD_DIM = D_MODEL // MEM_HEADS
MOE_GROUPS = 4
MOE_EPG = 8
MOE_EXPERTS = MOE_GROUPS * MOE_EPG
MOE_FF = D_MODEL // 2
RMS_EPS = 1e-6
NEG_INF = -1e30

LANES = 128
ROW_ALIGN = 8
MOE_TILE = 512
MOE_ROWS = 512
MOE_ALIGN = 16
MOE_SORTED = 2 * MOE_TILE + MOE_EXPERTS * (MOE_ALIGN - 1) + (-(2 * MOE_TILE + MOE_EXPERTS * (MOE_ALIGN - 1))) % MOE_ALIGN
RUN_CHUNKS = tuple(MOE_ALIGN << s for s in range((MOE_TILE // MOE_ALIGN).bit_length() - 1, -1, -1))
RUN_TAB = len(RUN_CHUNKS) * (1 + 2 * MOE_EXPERTS)
PAD_ROWS = MOE_ROWS // 2
PAD_CHUNKS = tuple(MOE_ALIGN << s for s in range((PAD_ROWS // MOE_ALIGN).bit_length() - 1, -1, -1))
VMEM_LIMIT = 56 * 2 ** 20


def _params(*sem):
    return pltpu.CompilerParams(dimension_semantics=sem, vmem_limit_bytes=VMEM_LIMIT)


def _iota(shape, dim):
    return lax.broadcasted_iota(I32, shape, dim)


def _dot(a, b):
    return jnp.dot(a, b, preferred_element_type=F32)


def _dot_nt(a, b):
    return lax.dot_general(a, b, (((1,), (1,)), ((), ())), preferred_element_type=F32)


def _dot_tn(a, b):
    return lax.dot_general(a, b, (((0,), (0,)), ((), ())), preferred_element_type=F32)


def _split(x):
    hi = x.astype(BF16)
    lo = (x - hi.astype(F32)).astype(BF16)
    return hi, lo


def _rms(x, g):
    return x * lax.rsqrt(jnp.mean(x * x, axis=-1, keepdims=True) + RMS_EPS) * g


def _norm_linear_kernel(x_ref, g_ref, w_ref, o_ref):
    xn = _rms(x_ref[...], g_ref[...]).astype(BF16)
    o_ref[...] = _dot(xn, w_ref[...]).astype(o_ref.dtype)


def norm_linear(x, g, w, tm=256):
    n, k = x.shape
    dout = w.shape[1]
    return pl.pallas_call(
        _norm_linear_kernel,
        grid=(n // tm,),
        in_specs=[pl.BlockSpec((tm, k), lambda i: (i, 0)),
                  pl.BlockSpec((1, k), lambda i: (0, 0)),
                  pl.BlockSpec((k, dout), lambda i: (0, 0))],
        out_specs=pl.BlockSpec((tm, dout), lambda i: (i, 0)),
        out_shape=jax.ShapeDtypeStruct((n, dout), BF16),
        compiler_params=_params("parallel"),
        name="norm_linear",
    )(x, g.reshape(1, k), w)


def _head_sum(x_sq):
    r = _iota((PAIR, PAIR), 0) // HEAD_DIM
    c = _iota((PAIR, PAIR), 1) // HEAD_DIM
    bd = jnp.where(r == c, 1.0, 0.0).astype(BF16)
    hi, lo = _split(x_sq)
    return _dot(hi, bd) + _dot(lo, bd)


def _rwkv_mix_kernel(seq_len, x_ref, xp_ref, g_ref, mu_ref, vec_ref, wr_ref, wk_ref, wv_ref,
                     l1_ref, w2_ref, a2_ref, g2_ref,
                     r_out, k_out, v_out, kk_out, akk_out, lw_out, g_out):
    i = pl.program_id(0)
    tm = x_ref.shape[0]
    gain = g_ref[...]
    h = _rms(x_ref[...], gain)
    hp = _rms(xp_ref[...], gain)[ROW_ALIGN - 1:ROW_ALIGN, :]
    hp = jnp.where((i * tm) % seq_len == 0, 0.0, hp)
    hs = pltpu.roll(h, 1, 0)
    hs = jnp.where(_iota((tm, 1), 0) == 0, hp, hs)
    hb = h.astype(BF16)
    dxb = (hs - h).astype(BF16)
    mub = mu_ref[...].astype(BF16)

    def mix(s):
        return hb + dxb * mub[s:s + 1, :]

    vec = vec_ref[...]
    w0, a0, k_k, k_a = vec[0:1], vec[1:2], vec[2:3], vec[3:4]
    l1 = l1_ref[...]
    r = _dot(mix(0), wr_ref[...])
    k = _dot(mix(2), wk_ref[...])
    v = _dot(mix(3), wv_ref[...])
    tw = jnp.tanh(_dot(mix(1), l1[:, 0:64])).astype(BF16)
    lw = -math.exp(-0.5) * jax.nn.sigmoid(w0 + _dot(tw, w2_ref[...]))
    ta = _dot(mix(4), l1[:, 64:128]).astype(BF16)
    a = jax.nn.sigmoid(a0 + _dot(ta, a2_ref[...]))
    tg = jax.nn.sigmoid(_dot(mix(5), l1[:, 128:256])).astype(BF16)
    g = _dot(tg, g2_ref[...])

    kk = k * k_k
    for p in range(N_PAIRS):
        sl = slice(p * PAIR, (p + 1) * PAIR)
        kkp = kk[:, sl]
        kkn = kkp * lax.rsqrt(jnp.maximum(_head_sum(kkp * kkp), 1e-24))
        kk_out[:, sl] = kkn.astype(kk_out.dtype)
        akk_out[:, sl] = (kkn * a[:, sl]).astype(akk_out.dtype)
    r_out[...] = r.astype(r_out.dtype)
    k_out[...] = (k * (1.0 + (a - 1.0) * k_a)).astype(k_out.dtype)
    v_out[...] = v.astype(v_out.dtype)
    lw_out[...] = lw
    g_out[...] = g.astype(g_out.dtype)


def rwkv_mix(x, seq_len, gain, mu, w0, w1, w2, a0, a1, a2, g1, g2, k_k, k_a, w_rkv, tm=256):
    n, d = x.shape
    mu8 = jnp.zeros((8, d), F32).at[:6].set(mu)
    vec = jnp.zeros((8, d), F32).at[0].set(w0).at[1].set(a0).at[2].set(k_k).at[3].set(k_a)
    l1 = jnp.concatenate([w1, a1, g1], axis=1).astype(BF16)
    wb = w_rkv.astype(BF16)
    row = pl.BlockSpec((tm, d), lambda i: (i, 0))
    full = lambda a: pl.BlockSpec(a.shape, lambda i: (0,) * a.ndim)
    args = (x, x, gain.reshape(1, d), mu8, vec, wb[0], wb[1], wb[2], l1,
            w2.astype(BF16), a2.astype(BF16), g2.astype(BF16))
    in_specs = [row, pl.BlockSpec((ROW_ALIGN, d), lambda i: (jnp.maximum(i * (tm // ROW_ALIGN) - 1, 0), 0))]
    in_specs += [full(a) for a in args[2:]]
    outs = [jax.ShapeDtypeStruct((n, d), BF16)] * 5 + [jax.ShapeDtypeStruct((n, d), F32),
                                                       jax.ShapeDtypeStruct((n, d), BF16)]
    return pl.pallas_call(
        functools.partial(_rwkv_mix_kernel, seq_len),
        grid=(n // tm,),
        in_specs=in_specs,
        out_specs=[row] * 7,
        out_shape=outs,
        compiler_params=_params("parallel"),
        name="rwkv_mix",
    )(*args)


def _rwkv_rec_kernel(r_ref, k_ref, v_ref, kk_ref, akk_ref, lw_ref, g_ref, vec_ref, y_ref, s_ref):
    c = pl.program_id(1)
    L = RWKV_CHUNK

    @pl.when(c == 0)
    def _():
        s_ref[...] = jnp.zeros_like(s_ref)

    lane = _iota((1, PAIR), 1)
    m0 = lane < HEAD_DIM
    ri = _iota((2 * L, 2 * L), 0)
    ci = _iota((2 * L, 2 * L), 1)
    same = (ri // L) == (ci // L)
    strict = same & (ci < ri)
    incl = same & (ci <= ri)
    eye = jnp.where(ri == ci, 1.0, 0.0)
    tri = jnp.where(_iota((L, L), 1) <= _iota((L, L), 0), 1.0, 0.0).astype(BF16)
    rb = _iota((PAIR, PAIR), 0) // HEAD_DIM
    cb = _iota((PAIR, PAIR), 1) // HEAD_DIM
    bd = rb == cb

    def stack(x):
        return jnp.concatenate([jnp.where(m0, x, 0.0), jnp.where(m0, 0.0, x)], axis=0)

    def fold(x):
        return x[:L] + x[L:]

    def head_mean(x):
        s0 = jnp.sum(jnp.where(m0, x, 0.0), axis=-1, keepdims=True)
        s1 = jnp.sum(jnp.where(m0, 0.0, x), axis=-1, keepdims=True)
        return jnp.where(m0, s0, s1) * (1.0 / HEAD_DIM)

    vec = vec_ref[...]
    n_sub = r_ref.shape[0] // L
    pairs = range(n_sub * N_PAIRS)
    rws = [slice((q // N_PAIRS) * L, (q // N_PAIRS + 1) * L) for q in pairs]
    sls = [slice((q % N_PAIRS) * PAIR, (q % N_PAIRS + 1) * PAIR) for q in pairs]
    lw = [lw_ref[rws[p], sls[p]] for p in pairs]
    cum = []
    for p in pairs:
        lw_hi, lw_lo = _split(lw[p])
        cum.append(_dot(tri, lw_hi) + _dot(tri, lw_lo))
    dec_all, x_kap, x_r, a_t, k_t, sc = [], [], [], [], [], []
    for p in pairs:
        rw, sl = rws[p], sls[p]
        dec = jnp.exp(cum[p])
        inv = jnp.exp(-cum[p])
        dec_prev = jnp.exp(cum[p] - lw[p])
        dec_all.append(dec[L - 1:L, :])
        x_kap.append(stack(kk_ref[rw, sl].astype(F32) * dec_prev).astype(BF16))
        x_r.append(stack(r_ref[rw, sl].astype(F32) * dec))
        a_t.append((akk_ref[rw, sl].astype(F32) * inv).astype(BF16))
        k_t.append((k_ref[rw, sl].astype(F32) * inv).astype(BF16))
        xs = jnp.concatenate([x_kap[p], x_r[p].astype(BF16)], axis=0)
        ys = jnp.concatenate([a_t[p], a_t[p], k_t[p], k_t[p]], axis=0)
        sc.append(_dot_nt(xs, ys))
    mp = [jnp.where(strict, -sc[p][:2 * L, :2 * L], 0.0) for p in pairs]
    m_kk = [jnp.where(strict, sc[p][:2 * L, 2 * L:], 0.0).astype(BF16) for p in pairs]
    s_ra = [jnp.where(incl, sc[p][2 * L:, :2 * L], 0.0).astype(BF16) for p in pairs]
    s_rk = [jnp.where(incl, sc[p][2 * L:, 2 * L:], 0.0).astype(BF16) for p in pairs]
    v_st = [stack(v_ref[rws[p], sls[p]].astype(F32)).astype(BF16) for p in pairs]
    b1 = [_dot(m_kk[p], v_st[p]) for p in pairs]
    o1 = [_dot(s_rk[p], v_st[p]) for p in pairs]

    t_inv = [eye + mp[p] for p in pairs]
    for _ in range(int(math.log2(L)) - 1):
        mpb = [mp[p].astype(BF16) for p in pairs]
        mp = [_dot(mpb[p], mpb[p]) for p in pairs]
        t_inv = [t_inv[p] + _dot(t_inv[p].astype(BF16), mp[p].astype(BF16)) for p in pairs]

    wu = [_dot(t_inv[p].astype(BF16), jnp.concatenate([x_kap[p], b1[p].astype(BF16)], axis=1))
          for p in pairs]
    corr = [_dot(s_ra[p], wu[p].astype(BF16)) for p in pairs]
    g_m, h0t, r_hat, o0 = [], [], [], []
    for p in pairs:
        r_hat.append(fold(x_r[p] - corr[p][:, :PAIR]).astype(BF16))
        o0.append(fold(o1[p] - corr[p][:, PAIR:]))
        w_f = fold(wu[p][:, :PAIR]).astype(BF16)
        u0_f = fold(wu[p][:, PAIR:]).astype(BF16)
        g_m.append(jnp.where(bd, _dot_tn(a_t[p], w_f), 0.0).astype(BF16))
        h0t.append(jnp.where(bd, _dot_tn(v_ref[rws[p], sls[p]], k_t[p]) - _dot_tn(u0_f, a_t[p]), 0.0))
    o = []
    for p in pairs:
        hp = p % N_PAIRS
        s = s_ref[hp]
        sb = s.astype(BF16)
        o.append(_dot_nt(r_hat[p], sb) + o0[p])
        s_ref[hp] = (s - _dot_nt(sb, g_m[p]) + h0t[p]) * dec_all[p]
    for p in pairs:
        rw, sl = rws[p], sls[p]
        mean = head_mean(o[p])
        cen = o[p] - mean
        var = head_mean(cen * cen)
        gn = cen * lax.rsqrt(var + RWKV_GN_EPS) * vec[0:1, sl] + vec[1:2, sl]
        r = r_ref[rw, sl].astype(F32)
        k = k_ref[rw, sl].astype(F32)
        bonus = head_mean(r * k * vec[2:3, sl]) * HEAD_DIM * v_ref[rw, sl].astype(F32)
        y_ref[rw, sl] = ((gn + bonus) * g_ref[rw, sl].astype(F32)).astype(y_ref.dtype)


def rwkv_recurrence(r, k, v, kk, akk, lw, g, lnx_g, lnx_b, r_k, batch, seq_len):
    n, d = r.shape
    rows = RWKV_CHUNK * RWKV_CHUNKS_PER_STEP
    nc = seq_len // rows
    vec = jnp.zeros((8, d), F32).at[0].set(lnx_g).at[1].set(lnx_b).at[2].set(r_k.reshape(d))
    blk = pl.BlockSpec((rows, d), lambda b, c: (b * nc + c, 0))
    return pl.pallas_call(
        _rwkv_rec_kernel,
        grid=(batch, nc),
        in_specs=[blk] * 7 + [pl.BlockSpec((8, d), lambda b, c: (0, 0))],
        out_specs=blk,
        out_shape=jax.ShapeDtypeStruct((n, d), BF16),
        scratch_shapes=[pltpu.VMEM((N_PAIRS, PAIR, PAIR), F32)],
        compiler_params=_params("parallel", "arbitrary"),
        name="rwkv_recurrence",
    )(r, k, v, kk, akk, lw, g, vec)


def _qkv_rope_kernel(x_ref, g_ref, pos_ref, inv_ref, w_ref, o_ref):
    d = x_ref.shape[1]
    xn = _rms(x_ref[...], g_ref[...]).astype(BF16)
    y = [_dot(xn, w_ref[:, part * d:(part + 1) * d]) for part in range(3)]
    ang = pos_ref[...].astype(F32) * inv_ref[...]
    first = (_iota((1, PAIR), 1) % HEAD_DIM) < HEAD_DIM // 2
    sn = jnp.sin(ang)
    cos = jnp.cos(ang)
    sin_lo = jnp.where(first, -sn, 0.0)
    sin_hi = jnp.where(first, 0.0, sn)
    for part in range(2):
        for p in range(N_PAIRS):
            t = y[part][:, p * PAIR:(p + 1) * PAIR]
            rot = (t * cos + pltpu.roll(t, PAIR - HEAD_DIM // 2, 1) * sin_lo
                   + pltpu.roll(t, HEAD_DIM // 2, 1) * sin_hi)
            o_ref[:, part * d + p * PAIR:part * d + (p + 1) * PAIR] = rot.astype(o_ref.dtype)
    o_ref[:, 2 * d:] = y[2].astype(o_ref.dtype)


def qkv_rope(x, gain, positions, w_qkv, tm=256):
    n, d = x.shape
    half = HEAD_DIM // 2
    inv = ROPE_THETA ** (-jnp.arange(half, dtype=F32) * 2.0 / HEAD_DIM)
    inv128 = jnp.tile(inv, PAIR // half).reshape(1, PAIR)
    col_scale = jnp.where(jnp.arange(3 * d) < d, math.log2(math.e) / math.sqrt(HEAD_DIM), 1.0).astype(F32)
    return pl.pallas_call(
        _qkv_rope_kernel,
        grid=(n // tm,),
        in_specs=[pl.BlockSpec((tm, d), lambda i: (i, 0)),
                  pl.BlockSpec((1, d), lambda i: (0, 0)),
                  pl.BlockSpec((tm, 1), lambda i: (i, 0)),
                  pl.BlockSpec((1, PAIR), lambda i: (0, 0)),
                  pl.BlockSpec((d, 3 * d), lambda i: (0, 0))],
        out_specs=pl.BlockSpec((tm, 3 * d), lambda i: (i, 0)),
        out_shape=jax.ShapeDtypeStruct((n, 3 * d), BF16),
        compiler_params=_params("parallel"),
        name="qkv_rope",
    )(x, gain.reshape(1, d), positions.reshape(n, 1), inv128, (w_qkv * col_scale).astype(BF16))


def _moba_kernel(q_ref, k_ref, v_ref, o_ref, vt_ref):
    nblk = k_ref.shape[0] // MOBA_BLOCK
    bq = MOBA_BLOCK
    km_rows = 16

    means = []
    ones_row = jnp.where(_iota((VT_ROWS - HEAD_DIM, bq), 0) == 0, 1.0, 0.0).astype(BF16)
    for n in range(nblk):
        cols = slice(n * bq, (n + 1) * bq)
        means.append(jnp.mean(k_ref[cols, :].astype(F32), axis=0, keepdims=True))
        vt = v_ref[cols, :].astype(F32).T.astype(BF16)
        for h in range(2):
            vt_ref[h * VT_ROWS:h * VT_ROWS + HEAD_DIM, cols] = vt[h * HEAD_DIM:(h + 1) * HEAD_DIM]
            vt_ref[h * VT_ROWS + HEAD_DIM:(h + 1) * VT_ROWS, cols] = ones_row
    km_hi, km_lo = _split(jnp.concatenate(means + [jnp.zeros((km_rows - nblk, PAIR), F32)], axis=0))

    lane = _iota((1, PAIR), 1)
    own = (lane < HEAD_DIM, lane >= HEAD_DIM)
    spare = (HEAD_DIM, 0)
    blk = _iota((km_rows, bq), 0)
    causal = _iota((bq, bq), 0) <= _iota((bq, bq), 1)
    eye = jnp.where(_iota((bq, bq), 0) == _iota((bq, bq), 1), 1.0, 0.0).astype(BF16)

    def augmented_queries(c, h):
        q = q_ref[c * bq:(c + 1) * bq, :]
        qm = jnp.where(own[h], q, jnp.zeros_like(q))
        past = blk < c
        gate = _dot_nt(km_hi, qm) + _dot_nt(km_lo, qm)
        gate = jnp.where(past, gate, NEG_INF)
        rank = jnp.zeros(gate.shape, F32)
        for m in range(c):
            row = gate[m:m + 1, :]
            tie = jnp.where(m < blk, 1.0, 0.0)
            rank = rank + jnp.where(row > gate, 1.0, jnp.where(row == gate, tie, 0.0))
        drop = jnp.where((blk < nblk) & jnp.logical_not(past & (rank < MOBA_TOPK)), 1.0, 0.0)
        pad = [jnp.zeros((HEAD_DIM - km_rows, bq), F32)]
        rows = ([jnp.zeros((HEAD_DIM, bq), F32), drop] + pad) if spare[h] else ([drop] + pad + [jnp.zeros((HEAD_DIM, bq), F32)])
        dropped = _dot_nt(eye, jnp.concatenate(rows, axis=0).astype(BF16))
        return jnp.where(own[h], q, jnp.where(dropped > 0.5, NEG_INF, 0.0).astype(BF16))

    def score(c, h):
        k_aug = []
        for n in range(c + 1):
            kb = k_ref[n * bq:(n + 1) * bq, :]
            marker = jnp.where(lane == spare[h] + n, 1.0, 0.0) if n < c else jnp.zeros(lane.shape, F32)
            k_aug.append(jnp.where(own[h], kb, marker.astype(BF16)))
        return _dot_nt(jnp.concatenate(k_aug, axis=0), augmented_queries(c, h))

    def attend(c, h, s):
        parts = [s[n * bq:(n + 1) * bq] for n in range(c)]
        parts.append(jnp.where(causal, s[c * bq:], NEG_INF))
        top = parts[0].max(axis=0, keepdims=True)
        for part in parts[1:]:
            top = jnp.maximum(top, part.max(axis=0, keepdims=True))
        probs = jnp.concatenate([jnp.exp2((part - top).astype(BF16)) for part in parts], axis=0)
        acc = _dot(vt_ref[h * VT_ROWS:(h + 1) * VT_ROWS, 0:(c + 1) * bq], probs)
        return acc[:HEAD_DIM] / acc[HEAD_DIM:HEAD_DIM + 1]

    units = [(c, h) for c in range(nblk) for h in range(2)]
    ahead = 2
    pending = [score(*u) for u in units[:ahead]]
    outs = {}
    for i, (c, h) in enumerate(units):
        s = pending.pop(0)
        if i + ahead < len(units):
            pending.append(score(*units[i + ahead]))
        outs[h] = attend(c, h, s)
        if h == 1:
            o_ref[c * bq:(c + 1) * bq, :] = jnp.concatenate([outs[0], outs[1]], axis=0).T.astype(o_ref.dtype)


def moba_attention(qkv, batch, seq_len):
    n = qkv.shape[0]
    return pl.pallas_call(
        _moba_kernel,
        grid=(batch, N_PAIRS),
        in_specs=[pl.BlockSpec((seq_len, PAIR), lambda b, p: (b, p)),
                  pl.BlockSpec((seq_len, PAIR), lambda b, p: (b, N_PAIRS + p)),
                  pl.BlockSpec((seq_len, PAIR), lambda b, p: (b, 2 * N_PAIRS + p))],
        out_specs=pl.BlockSpec((seq_len, PAIR), lambda b, p: (b, p)),
        out_shape=jax.ShapeDtypeStruct((n, D_MODEL), BF16),
        scratch_shapes=[pltpu.VMEM((2 * VT_ROWS, seq_len), BF16)],
        compiler_params=_params("parallel", "parallel"),
        name="moba_attention",
    )(qkv, qkv, qkv)


def _mem_xattn_kernel(y_ref, wy_ref, x_ref, g_ref, wq_ref, kv_ref, wo_ref, o_ref):
    x = x_ref[...] + _dot(y_ref[...], wy_ref[...])
    xn = _rms(x, g_ref[...]).astype(BF16)
    q = (_dot(xn, wq_ref[...]) * (1.0 / math.sqrt(MEM_HEAD_DIM))).astype(BF16)
    outs = []
    for h in range(MEM_HEADS):
        sl = slice(h * MEM_HEAD_DIM, (h + 1) * MEM_HEAD_DIM)
        s = _dot_nt(q[:, sl], kv_ref[:, sl])
        s = s - jnp.max(s, axis=-1, keepdims=True)
        e = jnp.exp(s)
        pr = e / jnp.sum(e, axis=-1, keepdims=True)
        vh = kv_ref[:, D_MODEL + h * MEM_HEAD_DIM:D_MODEL + (h + 1) * MEM_HEAD_DIM]
        outs.append(_dot(pr.astype(BF16), vh).astype(BF16))
    o = jnp.concatenate(outs, axis=1)
    o_ref[...] = x + _dot(o, wo_ref[...])


def mem_cross_attention(y, wy, x, gain, wq, kv, wo, seq_len, tm=256):
    n, d = x.shape
    m = kv.shape[0] // (n // seq_len)
    per_seq = seq_len // tm
    return pl.pallas_call(
        _mem_xattn_kernel,
        grid=(n // tm,),
        in_specs=[pl.BlockSpec((tm, d), lambda i: (i, 0)),
                  pl.BlockSpec((d, d), lambda i: (0, 0)),
                  pl.BlockSpec((tm, d), lambda i: (i, 0)),
                  pl.BlockSpec((1, d), lambda i: (0, 0)),
                  pl.BlockSpec((d, d), lambda i: (0, 0)),
                  pl.BlockSpec((m, 2 * d), lambda i: (i // per_seq, 0)),
                  pl.BlockSpec((d, d), lambda i: (0, 0))],
        out_specs=pl.BlockSpec((tm, d), lambda i: (i, 0)),
        out_shape=jax.ShapeDtypeStruct((n, d), F32),
        compiler_params=_params("parallel"),
        name="mem_cross_attention",
    )(y, wy.astype(BF16), x, gain.reshape(1, d), wq.astype(BF16), kv, wo.astype(BF16))


def _router_kernel(x_ref, g_ref, w_ref, b_ref, h_out, meta_out, cnt_out):
    tm = x_ref.shape[0]
    h2 = _rms(x_ref[...], g_ref[...])
    hi = h2.astype(BF16)
    h_out[...] = hi
    lo = (h2 - hi.astype(F32)).astype(BF16)
    w_hi, w_lo = _split(w_ref[...])
    lg = _dot_nt(w_hi, hi) + _dot_nt(w_hi, lo) + _dot_nt(w_lo, hi) + b_ref[...]
    row = _iota((8, tm), 0).astype(F32)

    def first_argmax(val, vmax):
        return jnp.min(jnp.where(val == vmax, row, 8.0), axis=0, keepdims=True)

    gl = jnp.where(row < MOE_GROUPS, lg[0:8], -jnp.inf)
    gmax = jnp.max(gl, axis=0, keepdims=True)
    p_g = 1.0 / jnp.sum(jnp.exp(gl - gmax), axis=0, keepdims=True)
    gidx = first_argmax(gl, gmax)
    el = jnp.zeros((8, tm), F32)
    for g in range(MOE_GROUPS):
        el = el + jnp.where(gidx == g, lg[8 + 8 * g:16 + 8 * g], 0.0)
    ee = jnp.exp(el - jnp.max(el, axis=0, keepdims=True))
    pe = ee / jnp.sum(ee, axis=0, keepdims=True)
    p1 = jnp.max(pe, axis=0, keepdims=True)
    i1 = first_argmax(pe, p1)
    pe2 = jnp.where(row == i1, -1.0, pe)
    p2 = jnp.max(pe2, axis=0, keepdims=True)
    i2 = first_argmax(pe2, p2)
    e1 = gidx * MOE_EPG + i1
    e2 = gidx * MOE_EPG + i2
    gate1 = p_g * p1 / (p1 + p2)
    gate2 = p_g * p2 / (p1 + p2)

    erow = _iota((MOE_EXPERTS, tm), 0).astype(F32)
    oh1 = erow == e1
    oh2 = erow == e2
    oh = jnp.where(oh1 | oh2, 1.0, 0.0)
    before = jnp.where(_iota((tm, tm), 0) < _iota((tm, tm), 1), 1.0, 0.0).astype(BF16)
    cnt_before = _dot(oh.astype(BF16), before)
    lr1 = jnp.sum(jnp.where(oh1, cnt_before, 0.0), axis=0, keepdims=True)
    lr2 = jnp.sum(jnp.where(oh2, cnt_before, 0.0), axis=0, keepdims=True)
    zero = jnp.zeros((1, tm), F32)
    meta_out[...] = jnp.concatenate([e1, e2, gate1, gate2, lr1, lr2, zero, zero], axis=0)
    cnt_out[...] = jnp.broadcast_to(jnp.sum(oh, axis=1, keepdims=True), (MOE_EXPERTS, LANES))


def moe_router(x, gain, w_grp, b_grp, w_exp, b_exp):
    n, d = x.shape
    tm = MOE_TILE
    wt = jnp.zeros((LANES, d), F32).at[0:MOE_GROUPS].set(w_grp.T).at[8:8 + MOE_EXPERTS].set(w_exp.T)
    bt = jnp.zeros((LANES, 1), F32).at[0:MOE_GROUPS, 0].set(b_grp).at[8:8 + MOE_EXPERTS, 0].set(b_exp)
    return pl.pallas_call(
        _router_kernel,
        grid=(n // tm,),
        in_specs=[pl.BlockSpec((tm, d), lambda i: (i, 0)),
                  pl.BlockSpec((1, d), lambda i: (0, 0)),
                  pl.BlockSpec((LANES, d), lambda i: (0, 0)),
                  pl.BlockSpec((LANES, 1), lambda i: (0, 0))],
        out_specs=[pl.BlockSpec((tm, d), lambda i: (i, 0)),
                   pl.BlockSpec((8, tm), lambda i: (0, i)),
                   pl.BlockSpec((MOE_EXPERTS, LANES), lambda i: (i, 0))],
        out_shape=[jax.ShapeDtypeStruct((n, d), BF16),
                   jax.ShapeDtypeStruct((8, n), F32),
                   jax.ShapeDtypeStruct((n // tm * MOE_EXPERTS, LANES), F32)],
        compiler_params=_params("parallel"),
        name="moe_router",
    )(x, gain.reshape(1, d), wt, bt)


def _chunk_copies(tab_ref, t, make_copy, act):
    base = t * RUN_TAB
    for ci, ck in enumerate(RUN_CHUNKS):
        first = base + len(RUN_CHUNKS) + ci * 2 * MOE_EXPERTS

        def body(i, _, first=first, ck=ck):
            src = tab_ref[first + 2 * i]
            dst = tab_ref[first + 2 * i + 1]
            act(make_copy(pl.multiple_of(src, MOE_ALIGN), pl.multiple_of(dst, MOE_ALIGN), ck))
            return 0

        lax.fori_loop(0, tab_ref[base + ci], body, 0)


def _run_copies(tab_ref, first, count, chunks, make_copy, act):
    def body(e, _):
        base = first + e * 3
        length = tab_ref[base]
        src = tab_ref[base + 1]
        dst = tab_ref[base + 2]
        for ck in chunks:
            @pl.when((length & ck) != 0)
            def _():
                off = length & (-2 * ck)
                act(make_copy(pl.multiple_of(src + off, MOE_ALIGN), pl.multiple_of(dst + off, MOE_ALIGN), ck))
        return 0
    lax.fori_loop(0, count, body, 0)


def _start(cp):
    cp.start()


def _wait(cp):
    cp.wait()


def _dispatch_kernel(tab_ref, h_ref, pos_ref, rows_hbm, sorted_ref, zero_ref, sem, zsem):
    t = pl.program_id(0)
    nt = pl.num_programs(0)
    slot = t % 2
    r = _iota((MOE_SORTED, MOE_TILE), 0)
    perm = jnp.where((r == pos_ref[0:1, :]) | (r == pos_ref[1:2, :]), 1.0, 0.0).astype(BF16)
    sorted_ref[slot] = _dot(perm, h_ref[...]).astype(BF16)

    def run_copy(s):
        def make(src, dst, ck):
            return pltpu.make_async_copy(sorted_ref.at[s, pl.ds(src, ck)], rows_hbm.at[pl.ds(dst, ck)], sem.at[s])
        return make

    def zero_copy(src, dst, ck):
        del src
        return pltpu.make_async_copy(zero_ref.at[pl.ds(0, ck)], rows_hbm.at[pl.ds(dst, ck)], zsem)

    _chunk_copies(tab_ref, t, run_copy(slot), _start)

    @pl.when(t > 0)
    def _():
        _chunk_copies(tab_ref, t - 1, run_copy(1 - slot), _wait)

    @pl.when(t == nt - 1)
    def _():
        zero_ref[...] = jnp.zeros_like(zero_ref)
        pad = nt * RUN_TAB
        _run_copies(tab_ref, pad, MOE_EXPERTS, PAD_CHUNKS, zero_copy, _start)
        spare = pad + MOE_EXPERTS * 3
        n_spare = tab_ref[spare]

        def spare_copy(i):
            dst = pl.multiple_of(tab_ref[spare + 2] + i * PAD_ROWS, PAD_ROWS)
            return zero_copy(0, dst, PAD_ROWS)

        lax.fori_loop(0, n_spare, lambda i, c: (_start(spare_copy(i)), c)[1], 0)
        _chunk_copies(tab_ref, t, run_copy(slot), _wait)
        _run_copies(tab_ref, pad, MOE_EXPERTS, PAD_CHUNKS, zero_copy, _wait)
        lax.fori_loop(0, n_spare, lambda i, c: (_wait(spare_copy(i)), c)[1], 0)


def moe_dispatch(tab, h2, pos, n_rows):
    n, d = h2.shape
    tm = MOE_TILE
    return pl.pallas_call(
        _dispatch_kernel,
        grid_spec=pltpu.PrefetchScalarGridSpec(
            num_scalar_prefetch=1,
            grid=(n // tm,),
            in_specs=[pl.BlockSpec((tm, d), lambda i, tab: (i, 0)),
                      pl.BlockSpec((8, tm), lambda i, tab: (0, i))],
            out_specs=pl.BlockSpec(memory_space=pl.ANY),
            scratch_shapes=[pltpu.VMEM((2, MOE_SORTED, d), BF16), pltpu.VMEM((PAD_ROWS, d), BF16),
                            pltpu.SemaphoreType.DMA((2,)), pltpu.SemaphoreType.DMA],
        ),
        out_shape=jax.ShapeDtypeStruct((n_rows, d), BF16),
        compiler_params=_params("arbitrary"),
        name="moe_dispatch",
    )(tab, h2, pos)


def _expert_kernel(be_ref, nu_ref, x_ref, w1_ref, w3_ref, w2_ref, y_ref, w1b, w3b, w2b):
    b = pl.program_id(0)
    used = b < nu_ref[0]
    new_expert = (b == 0) | (be_ref[b] != be_ref[jnp.maximum(b - 1, 0)])

    @pl.when(used & new_expert)
    def _():
        w1b[...] = w1_ref[0, 0].astype(BF16)
        w3b[...] = w3_ref[0, 0].astype(BF16)
        w2b[...] = w2_ref[0, 0].astype(BF16)

    @pl.when(used)
    def _():
        xb = x_ref[...]
        a = _dot(xb, w1b[...])
        c = _dot(xb, w3b[...])
        hid = (a * jax.nn.sigmoid(a) * c).astype(BF16)
        y_ref[...] = _dot(hid, w2b[...]).astype(y_ref.dtype)

    @pl.when(jnp.logical_not(used))
    def _():
        y_ref[...] = jnp.zeros_like(y_ref)


def moe_experts(block_expert, n_used, rows, layer, w1, w3, w2):
    n_rows = rows.shape[0]
    nb = n_rows // MOE_ROWS
    d, ff = w1.shape[2], w1.shape[3]

    def xmap(b, be, nu):
        return (jnp.minimum(b, nu[0] - 1), 0)

    def wmap(b, be, nu):
        return (layer, be[jnp.minimum(b, nu[0] - 1)], 0, 0)

    return pl.pallas_call(
        _expert_kernel,
        grid_spec=pltpu.PrefetchScalarGridSpec(
            num_scalar_prefetch=2,
            grid=(nb,),
            in_specs=[pl.BlockSpec((MOE_ROWS, d), xmap),
                      pl.BlockSpec((1, 1, d, ff), wmap),
                      pl.BlockSpec((1, 1, d, ff), wmap),
                      pl.BlockSpec((1, 1, ff, d), wmap)],
            out_specs=pl.BlockSpec((MOE_ROWS, d), lambda b, be, nu: (b, 0)),
            scratch_shapes=[pltpu.VMEM((d, ff), BF16), pltpu.VMEM((d, ff), BF16), pltpu.VMEM((ff, d), BF16)],
        ),
        out_shape=jax.ShapeDtypeStruct((n_rows, d), BF16),
        compiler_params=_params("arbitrary"),
        name="moe_experts",
    )(block_expert, n_used, rows, w1, w3, w2)


def _combine_kernel(final_norm, tab_ref, y_hbm, pos_ref, gate_ref, x_ref, g_ref, o_ref, ys_ref, sem):
    t = pl.program_id(0)
    nt = pl.num_programs(0)
    slot = t % 2

    def run_copy(s):
        def make(src, dst, ck):
            return pltpu.make_async_copy(y_hbm.at[pl.ds(dst, ck)], ys_ref.at[s, pl.ds(src, ck)], sem.at[s])
        return make

    @pl.when(t == 0)
    def _():
        ys_ref[...] = jnp.zeros_like(ys_ref)
        _chunk_copies(tab_ref, 0, run_copy(0), _start)

    @pl.when(t + 1 < nt)
    def _():
        _chunk_copies(tab_ref, t + 1, run_copy(1 - slot), _start)

    r = _iota((MOE_SORTED, MOE_TILE), 0)
    hit1 = r == pos_ref[0:1, :]
    hit2 = r == pos_ref[1:2, :]
    wgt = jnp.where(hit1, gate_ref[2:3, :], jnp.where(hit2, gate_ref[3:4, :], 0.0))
    row_gate = jnp.sum(wgt, axis=1, keepdims=True)
    row_used = row_gate > 0.0
    perm = jnp.where(hit1 | hit2, 1.0, 0.0).astype(BF16)
    _chunk_copies(tab_ref, t, run_copy(slot), _wait)
    ys = jnp.where(row_used, ys_ref[slot].astype(F32) * row_gate, 0.0).astype(BF16)
    out = x_ref[...] + _dot_tn(perm, ys)
    if final_norm:
        out = _rms(out, g_ref[...])
    o_ref[...] = out


def moe_combine(tab, y, pos, meta, x, final_gain):
    n, d = x.shape
    tm = MOE_TILE
    final_norm = final_gain is not None
    gain = (final_gain if final_norm else jnp.ones((d,), F32)).reshape(1, d)
    return pl.pallas_call(
        functools.partial(_combine_kernel, final_norm),
        grid_spec=pltpu.PrefetchScalarGridSpec(
            num_scalar_prefetch=1,
            grid=(n // tm,),
            in_specs=[pl.BlockSpec(memory_space=pl.ANY),
                      pl.BlockSpec((8, tm), lambda i, tab: (0, i)),
                      pl.BlockSpec((8, tm), lambda i, tab: (0, i)),
                      pl.BlockSpec((tm, d), lambda i, tab: (i, 0)),
                      pl.BlockSpec((1, d), lambda i, tab: (0, 0))],
            out_specs=pl.BlockSpec((tm, d), lambda i, tab: (i, 0)),
            scratch_shapes=[pltpu.VMEM((2, MOE_SORTED, d), BF16), pltpu.SemaphoreType.DMA((2,))],
        ),
        out_shape=jax.ShapeDtypeStruct((n, d), F32),
        compiler_params=_params("arbitrary"),
        name="moe_combine",
    )(tab, y, pos, meta, x, gain)


def hierarchical_moe(x, gain, w_grp, b_grp, w_exp, b_exp, layer, w1, w3, w2, final_gain):
    n, d = x.shape
    tm = MOE_TILE
    nt = n // tm
    h2, meta, cnt = moe_router(x, gain, w_grp, b_grp, w_exp, b_exp)

    cnt = cnt.reshape(nt, MOE_EXPERTS, LANES)[:, :, 0].astype(I32)
    run = (cnt + MOE_ALIGN - 1) // MOE_ALIGN * MOE_ALIGN
    src = jnp.cumsum(run, axis=1) - run
    before = jnp.cumsum(run, axis=0) - run
    total = jnp.sum(run, axis=0)
    padded = (total + MOE_ROWS - 1) // MOE_ROWS * MOE_ROWS
    pad_end = jnp.cumsum(padded)
    dst = (pad_end - padded)[None, :] + before
    max_rows = 2 * n + nt * MOE_EXPERTS * (MOE_ALIGN - 1) + MOE_EXPERTS * (MOE_ROWS - 1)
    nb = -(-max_rows // MOE_ROWS)
    sizes = jnp.array(RUN_CHUNKS, I32)[None, :, None]
    has = (run[:, None, :] & sizes) != 0
    above = run[:, None, :] & (-2 * sizes)
    place = jnp.where(has, jnp.cumsum(has.astype(I32), axis=2) - 1, -1)
    front = place[..., None] == jnp.arange(MOE_EXPERTS, dtype=I32)
    c_src = jnp.sum(jnp.where(front, (src[:, None, :] + above)[..., None], 0), axis=2)
    c_dst = jnp.sum(jnp.where(front, (dst[:, None, :] + above)[..., None], 0), axis=2)
    chunks = jnp.stack([c_src, c_dst], axis=-1).reshape(nt, -1)
    runs = jnp.concatenate([jnp.sum(has.astype(I32), axis=2), chunks], axis=1).reshape(-1)
    zero = jnp.zeros((MOE_EXPERTS,), I32)
    pads = jnp.stack([padded - total, zero, pad_end - padded + total], axis=-1)
    spare = jnp.stack([(nb * MOE_ROWS - pad_end[-1]) // PAD_ROWS, zero[0], pad_end[-1]])
    tab = jnp.concatenate([runs, pads.reshape(-1), spare]).astype(I32)
    n_used = (pad_end[-1] // MOE_ROWS).astype(I32).reshape(1)
    block_start = jnp.arange(nb, dtype=I32) * MOE_ROWS
    block_expert = jnp.minimum(
        jnp.sum((pad_end[None, :] <= block_start[:, None]).astype(I32), axis=1), MOE_EXPERTS - 1).astype(I32)
    e = meta[0:2].astype(I32).reshape(2, nt, tm, 1)
    hit = e == jnp.arange(MOE_EXPERTS, dtype=I32)
    pos = jnp.sum(jnp.where(hit, src[None, :, None, :], 0), axis=-1).reshape(2, n) + meta[4:6].astype(I32)
    pos8 = jnp.zeros((8, n), I32).at[0:2].set(pos)

    rows = moe_dispatch(tab, h2, pos8, nb * MOE_ROWS)
    y = moe_experts(block_expert, n_used, rows, layer, w1, w3, w2)
    return moe_combine(tab, y, pos8, meta, x, final_gain)


def kernel(x, mem, positions, ln_mix, ln_mem, ln_memkv, ln_ffn, rw_mu, rw_w0, rw_w1, rw_w2, rw_a0, rw_a1, rw_a2, rw_g1, rw_g2, rw_kk, rw_ka, rw_rk, rw_wrkv, rw_lnx_g, rw_lnx_b, rw_wo, mb_wqkv, mb_wo, mx_wq, mx_wkv, mx_wo, moe_wg, moe_bg, moe_we, moe_be, moe_w1, moe_w3, moe_w2, ln_f):
    B, T, C = x.shape
    n = B * T
    depth = ln_mix.shape[0]
    xf = x.reshape(n, C)
    memf = mem.reshape(-1, C)
    for i in range(depth):
        j = i // 2
        if i % 2 == 0:
            r, k, v, kk, akk, lw, g = rwkv_mix(xf, T, ln_mix[i], rw_mu[j], rw_w0[j], rw_w1[j], rw_w2[j],
                                               rw_a0[j], rw_a1[j], rw_a2[j], rw_g1[j], rw_g2[j],
                                               rw_kk[j], rw_ka[j], rw_wrkv[j])
            y = rwkv_recurrence(r, k, v, kk, akk, lw, g, rw_lnx_g[j], rw_lnx_b[j], rw_rk[j], B, T)
            w_mix_o = rw_wo[j]
        else:
            qkv = qkv_rope(xf, ln_mix[i], positions, mb_wqkv[j])
            y = moba_attention(qkv, B, T)
            w_mix_o = mb_wo[j]
        kv = norm_linear(memf, ln_memkv[i], mx_wkv[i].astype(BF16))
        xf = mem_cross_attention(y, w_mix_o, xf, ln_mem[i], mx_wq[i], kv, mx_wo[i], T)
        xf = hierarchical_moe(xf, ln_ffn[i], moe_wg[i], moe_bg[i], moe_we[i], moe_be[i],
                              i, moe_w1, moe_w3, moe_w2, ln_f if i == depth - 1 else None)
    return xf.reshape(B, T, C)
```

```python
import functools
import math

import jax
import jax.numpy as jnp
from jax import lax
from jax.experimental import pallas as pl
from jax.experimental.pallas import tpu as pltpu

F32 = jnp.float32
BF16 = jnp.bfloat16
I32 = jnp.int32
U32 = jnp.uint32

D_MODEL = 1024
HEAD_DIM = 64
PAIR = 2 * HEAD_DIM
N_PAIRS = D_MODEL // PAIR
RWKV_GN_EPS = 64e-5
RWKV_CHUNK = 64
RWKV_CHUNKS_PER_STEP = 4
MOBA_BLOCK = 256
MOBA_TOPK = 3
VT_ROWS = HEAD_DIM + 16
ROPE_THETA = 10000.0
MEM_HEADS = 4
MEM_HEAD_DIM = D_MODEL // MEM_HEADS
MOE_GROUPS = 4
MOE_EPG = 8
MOE_EXPERTS = MOE_GROUPS * MOE_EPG
MOE_FF = D_MODEL // 2
RMS_EPS = 1e-6
NEG_INF = -1e30

LANES = 128
ROW_ALIGN = 8
MOE_TILE = 512
MOE_ROWS = 512
MOE_ALIGN = 16
MOE_SORTED = 2 * MOE_TILE + MOE_EXPERTS * (MOE_ALIGN - 1) + (-(2 * MOE_TILE + MOE_EXPERTS * (MOE_ALIGN - 1))) % MOE_ALIGN
RUN_CHUNKS = tuple(MOE_ALIGN << s for s in range((MOE_TILE // MOE_ALIGN).bit_length() - 1, -1, -1))
RUN_TAB = len(RUN_CHUNKS) * (1 + 2 * MOE_EXPERTS)
PAD_ROWS = MOE_ROWS // 2
PAD_CHUNKS = tuple(MOE_ALIGN << s for s in range((PAD_ROWS // MOE_ALIGN).bit_length() - 1, -1, -1))
VMEM_LIMIT = 56 * 2 ** 20


def _params(*sem):
    return pltpu.CompilerParams(dimension_semantics=sem, vmem_limit_bytes=VMEM_LIMIT)


def _iota(shape, dim):
    return lax.broadcasted_iota(I32, shape, dim)


def _dot(a, b):
    return jnp.dot(a, b, preferred_element_type=F32)


def _dot_nt(a, b):
    return lax.dot_general(a, b, (((1,), (1,)), ((), ())), preferred_element_type=F32)


def _dot_tn(a, b):
    return lax.dot_general(a, b, (((0,), (0,)), ((), ())), preferred_element_type=F32)


def _split(x):
    hi = x.astype(BF16)
    lo = (x - hi.astype(F32)).astype(BF16)
    return hi, lo


def _rms(x, g):
    return x * lax.rsqrt(jnp.mean(x * x, axis=-1, keepdims=True) + RMS_EPS) * g


def _norm_linear_kernel(x_ref, g_ref, w_ref, o_ref):
    xn = _rms(x_ref[...], g_ref[...]).astype(BF16)
    o_ref[...] = _dot(xn, w_ref[...]).astype(o_ref.dtype)


def norm_linear(x, g, w, tm=256):
    n, k = x.shape
    dout = w.shape[1]
    return pl.pallas_call(
        _norm_linear_kernel,
        grid=(n // tm,),
        in_specs=[pl.BlockSpec((tm, k), lambda i: (i, 0)),
                  pl.BlockSpec((1, k), lambda i: (0, 0)),
                  pl.BlockSpec((k, dout), lambda i: (0, 0))],
        out_specs=pl.BlockSpec((tm, dout), lambda i: (i, 0)),
        out_shape=jax.ShapeDtypeStruct((n, dout), BF16),
        compiler_params=_params("parallel"),
        name="norm_linear",
    )(x, g.reshape(1, k), w)


def _head_sum(x_sq):
    r = _iota((PAIR, PAIR), 0) // HEAD_DIM
    c = _iota((PAIR, PAIR), 1) // HEAD_DIM
    bd = jnp.where(r == c, 1.0, 0.0).astype(BF16)
    hi, lo = _split(x_sq)
    return _dot(hi, bd) + _dot(lo, bd)


def _rwkv_mix_kernel(seq_len, x_ref, xp_ref, g_ref, mu_ref, vec_ref, wr_ref, wk_ref, wv_ref,
                     l1_ref, w2_ref, a2_ref, g2_ref,
                     r_out, k_out, v_out, kk_out, akk_out, lw_out, g_out):
    i = pl.program_id(0)
    tm = x_ref.shape[0]
    gain = g_ref[...]
    h = _rms(x_ref[...], gain)
    hp = _rms(xp_ref[...], gain)[ROW_ALIGN - 1:ROW_ALIGN, :]
    hp = jnp.where((i * tm) % seq_len == 0, 0.0, hp)
    hs = pltpu.roll(h, 1, 0)
    hs = jnp.where(_iota((tm, 1), 0) == 0, hp, hs)
    hb = h.astype(BF16)
    dxb = (hs - h).astype(BF16)
    mub = mu_ref[...].astype(BF16)

    def mix(s):
        return hb + dxb * mub[s:s + 1, :]

    vec = vec_ref[...]
    w0, a0, k_k, k_a = vec[0:1], vec[1:2], vec[2:3], vec[3:4]
    l1 = l1_ref[...]
    r = _dot(mix(0), wr_ref[...])
    k = _dot(mix(2), wk_ref[...])
    v = _dot(mix(3), wv_ref[...])
    tw = jnp.tanh(_dot(mix(1), l1[:, 0:64])).astype(BF16)
    lw = -math.exp(-0.5) * jax.nn.sigmoid(w0 + _dot(tw, w2_ref[...]))
    ta = _dot(mix(4), l1[:, 64:128]).astype(BF16)
    a = jax.nn.sigmoid(a0 + _dot(ta, a2_ref[...]))
    tg = jax.nn.sigmoid(_dot(mix(5), l1[:, 128:256])).astype(BF16)
    g = _dot(tg, g2_ref[...])

    kk = k * k_k
    for p in range(N_PAIRS):
        sl = slice(p * PAIR, (p + 1) * PAIR)
        kkp = kk[:, sl]
        kkn = kkp * lax.rsqrt(jnp.maximum(_head_sum(kkp * kkp), 1e-24))
        kk_out[:, sl] = kkn.astype(kk_out.dtype)
        akk_out[:, sl] = (kkn * a[:, sl]).astype(akk_out.dtype)
    r_out[...] = r.astype(r_out.dtype)
    k_out[...] = (k * (1.0 + (a - 1.0) * k_a)).astype(k_out.dtype)
    v_out[...] = v.astype(v_out.dtype)
    lw_out[...] = lw
    g_out[...] = g.astype(g_out.dtype)


def rwkv_mix(x, seq_len, gain, mu, w0, w1, w2, a0, a1, a2, g1, g2, k_k, k_a, w_rkv, tm=512):
    n, d = x.shape
    mu8 = jnp.zeros((8, d), F32).at[:6].set(mu)
    vec = jnp.zeros((8, d), F32).at[0].set(w0).at[1].set(a0).at[2].set(k_k).at[3].set(k_a)
    l1 = jnp.concatenate([w1, a1, g1], axis=1).astype(BF16)
    wb = w_rkv.astype(BF16)
    row = pl.BlockSpec((tm, d), lambda i: (i, 0))
    full = lambda a: pl.BlockSpec(a.shape, lambda i: (0,) * a.ndim)
    args = (x, x, gain.reshape(1, d), mu8, vec, wb[0], wb[1], wb[2], l1,
            w2.astype(BF16), a2.astype(BF16), g2.astype(BF16))
    in_specs = [row, pl.BlockSpec((ROW_ALIGN, d), lambda i: (jnp.maximum(i * (tm // ROW_ALIGN) - 1, 0), 0))]
    in_specs += [full(a) for a in args[2:]]
    outs = [jax.ShapeDtypeStruct((n, d), BF16)] * 5 + [jax.ShapeDtypeStruct((n, d), F32),
                                                       jax.ShapeDtypeStruct((n, d), BF16)]
    return pl.pallas_call(
        functools.partial(_rwkv_mix_kernel, seq_len),
        grid=(n // tm,),
        in_specs=in_specs,
        out_specs=[row] * 7,
        out_shape=outs,
        compiler_params=_params("parallel"),
        name="rwkv_mix",
    )(*args)


def _rwkv_rec_kernel(r_ref, k_ref, v_ref, kk_ref, akk_ref, lw_ref, g_ref, vec_ref, y_ref, s_ref):
    c = pl.program_id(1)
    L = RWKV_CHUNK

    @pl.when(c == 0)
    def _():
        s_ref[...] = jnp.zeros_like(s_ref)

    lane = _iota((1, PAIR), 1)
    m0 = lane < HEAD_DIM
    ri = _iota((2 * L, 2 * L), 0)
    ci = _iota((2 * L, 2 * L), 1)
    same = (ri // L) == (ci // L)
    strict = same & (ci < ri)
    incl = same & (ci <= ri)
    eye = jnp.where(ri == ci, 1.0, 0.0)
    tri = jnp.where(_iota((L, L), 1) <= _iota((L, L), 0), 1.0, 0.0).astype(BF16)
    rb = _iota((PAIR, PAIR), 0) // HEAD_DIM
    cb = _iota((PAIR, PAIR), 1) // HEAD_DIM
    bd = rb == cb

    def stack(x):
        return jnp.concatenate([jnp.where(m0, x, 0.0), jnp.where(m0, 0.0, x)], axis=0)

    def fold(x):
        return x[:L] + x[L:]

    def head_mean(x):
        s0 = jnp.sum(jnp.where(m0, x, 0.0), axis=-1, keepdims=True)
        s1 = jnp.sum(jnp.where(m0, 0.0, x), axis=-1, keepdims=True)
        return jnp.where(m0, s0, s1) * (1.0 / HEAD_DIM)

    vec = vec_ref[...]
    n_sub = r_ref.shape[0] // L
    pairs = range(n_sub * N_PAIRS)
    rws = [slice((q // N_PAIRS) * L, (q // N_PAIRS + 1) * L) for q in pairs]
    sls = [slice((q % N_PAIRS) * PAIR, (q % N_PAIRS + 1) * PAIR) for q in pairs]
    lw = [lw_ref[rws[p], sls[p]] for p in pairs]
    cum = []
    for p in pairs:
        lw_hi, lw_lo = _split(lw[p])
        cum.append(_dot(tri, lw_hi) + _dot(tri, lw_lo))
    dec_all, x_kap, x_r, a_t, k_t, sc = [], [], [], [], [], []
    for p in pairs:
        rw, sl = rws[p], sls[p]
        dec = jnp.exp(cum[p])
        inv = jnp.exp(-cum[p])
        dec_prev = jnp.exp(cum[p] - lw[p])
        dec_all.append(dec[L - 1:L, :])
        x_kap.append(stack(kk_ref[rw, sl].astype(F32) * dec_prev).astype(BF16))
        x_r.append(stack(r_ref[rw, sl].astype(F32) * dec))
        a_t.append((akk_ref[rw, sl].astype(F32) * inv).astype(BF16))
        k_t.append((k_ref[rw, sl].astype(F32) * inv).astype(BF16))
        xs = jnp.concatenate([x_kap[p], x_r[p].astype(BF16)], axis=0)
        ys = jnp.concatenate([a_t[p], a_t[p], k_t[p], k_t[p]], axis=0)
        sc.append(_dot_nt(xs, ys))
    mp = [jnp.where(strict, -sc[p][:2 * L, :2 * L], 0.0) for p in pairs]
    m_kk = [jnp.where(strict, sc[p][:2 * L, 2 * L:], 0.0).astype(BF16) for p in pairs]
    s_ra = [jnp.where(incl, sc[p][2 * L:, :2 * L], 0.0).astype(BF16) for p in pairs]
    s_rk = [jnp.where(incl, sc[p][2 * L:, 2 * L:], 0.0).astype(BF16) for p in pairs]
    v_st = [stack(v_ref[rws[p], sls[p]].astype(F32)).astype(BF16) for p in pairs]
    bo = [_dot(jnp.concatenate([m_kk[p], s_rk[p]], axis=0), v_st[p]) for p in pairs]
    b1 = [bo[p][:2 * L] for p in pairs]
    o1 = [bo[p][2 * L:] for p in pairs]

    t_inv = [eye + mp[p] for p in pairs]
    mp = [_dot(mp[p].astype(BF16), mp[p].astype(BF16)) for p in pairs]
    steps = int(math.log2(L)) - 1
    for j in range(steps):
        mpb = [mp[p].astype(BF16) for p in pairs]
        if j + 1 < steps:
            both = [_dot(mpb[p], jnp.concatenate([mpb[p], t_inv[p].astype(BF16)], axis=1)) for p in pairs]
            mp = [both[p][:, :2 * L] for p in pairs]
            t_inv = [t_inv[p] + both[p][:, 2 * L:] for p in pairs]
        else:
            t_inv = [t_inv[p] + _dot(mpb[p], t_inv[p].astype(BF16)) for p in pairs]

    wu = [_dot(t_inv[p].astype(BF16), jnp.concatenate([x_kap[p], b1[p].astype(BF16)], axis=1))
          for p in pairs]
    corr = [_dot(s_ra[p], wu[p].astype(BF16)) for p in pairs]
    g_m, h0t, r_hat, o0 = [], [], [], []
    for p in pairs:
        r_hat.append(fold(x_r[p] - corr[p][:, :PAIR]).astype(BF16))
        o0.append(fold(o1[p] - corr[p][:, PAIR:]))
        w_f = fold(wu[p][:, :PAIR]).astype(BF16)
        u0_f = fold(wu[p][:, PAIR:]).astype(BF16)
        g_m.append(jnp.where(bd, _dot_tn(a_t[p], w_f), 0.0).astype(BF16))
        h0t.append(jnp.where(bd, _dot_tn(v_ref[rws[p], sls[p]], k_t[p]) - _dot_tn(u0_f, a_t[p]), 0.0))
    o = []
    for p in pairs:
        hp = p % N_PAIRS
        s = s_ref[hp]
        sb = s.astype(BF16)
        o.append(_dot_nt(r_hat[p], sb) + o0[p])
        s_ref[hp] = (s - _dot_nt(sb, g_m[p]) + h0t[p]) * dec_all[p]
    for p in pairs:
        rw, sl = rws[p], sls[p]
        mean = head_mean(o[p])
        cen = o[p] - mean
        var = head_mean(cen * cen)
        gn = cen * lax.rsqrt(var + RWKV_GN_EPS) * vec[0:1, sl] + vec[1:2, sl]
        r = r_ref[rw, sl].astype(F32)
        k = k_ref[rw, sl].astype(F32)
        bonus = head_mean(r * k * vec[2:3, sl]) * HEAD_DIM * v_ref[rw, sl].astype(F32)
        y_ref[rw, sl] = ((gn + bonus) * g_ref[rw, sl].astype(F32)).astype(y_ref.dtype)


def rwkv_recurrence(r, k, v, kk, akk, lw, g, lnx_g, lnx_b, r_k, batch, seq_len):
    n, d = r.shape
    rows = RWKV_CHUNK * RWKV_CHUNKS_PER_STEP
    nc = seq_len // rows
    vec = jnp.zeros((8, d), F32).at[0].set(lnx_g).at[1].set(lnx_b).at[2].set(r_k.reshape(d))
    blk = pl.BlockSpec((rows, d), lambda b, c: (b * nc + c, 0))
    return pl.pallas_call(
        _rwkv_rec_kernel,
        grid=(batch, nc),
        in_specs=[blk] * 7 + [pl.BlockSpec((8, d), lambda b, c: (0, 0))],
        out_specs=blk,
        out_shape=jax.ShapeDtypeStruct((n, d), BF16),
        scratch_shapes=[pltpu.VMEM((N_PAIRS, PAIR, PAIR), F32)],
        compiler_params=_params("parallel", "arbitrary"),
        name="rwkv_recurrence",
    )(r, k, v, kk, akk, lw, g, vec)


def _qkv_rope_kernel(x_ref, g_ref, pos_ref, inv_ref, w_ref, o_ref):
    d = x_ref.shape[1]
    xn = _rms(x_ref[...], g_ref[...]).astype(BF16)
    y = [_dot(xn, w_ref[:, part * d:(part + 1) * d]) for part in range(3)]
    ang = pos_ref[...].astype(F32) * inv_ref[...]
    first = (_iota((1, PAIR), 1) % HEAD_DIM) < HEAD_DIM // 2
    sn = jnp.sin(ang)
    cos = jnp.cos(ang)
    sin_lo = jnp.where(first, -sn, 0.0)
    sin_hi = jnp.where(first, 0.0, sn)
    for part in range(2):
        for p in range(N_PAIRS):
            t = y[part][:, p * PAIR:(p + 1) * PAIR]
            rot = (t * cos + pltpu.roll(t, PAIR - HEAD_DIM // 2, 1) * sin_lo
                   + pltpu.roll(t, HEAD_DIM // 2, 1) * sin_hi)
            o_ref[:, part * d + p * PAIR:part * d + (p + 1) * PAIR] = rot.astype(o_ref.dtype)
    o_ref[:, 2 * d:] = y[2].astype(o_ref.dtype)


def qkv_rope(x, gain, positions, w_qkv, tm=512):
    n, d = x.shape
    half = HEAD_DIM // 2
    inv = ROPE_THETA ** (-jnp.arange(half, dtype=F32) * 2.0 / HEAD_DIM)
    inv128 = jnp.tile(inv, PAIR // half).reshape(1, PAIR)
    col_scale = jnp.where(jnp.arange(3 * d) < d, math.log2(math.e) / math.sqrt(HEAD_DIM), 1.0).astype(F32)
    return pl.pallas_call(
        _qkv_rope_kernel,
        grid=(n // tm,),
        in_specs=[pl.BlockSpec((tm, d), lambda i: (i, 0)),
                  pl.BlockSpec((1, d), lambda i: (0, 0)),
                  pl.BlockSpec((tm, 1), lambda i: (i, 0)),
                  pl.BlockSpec((1, PAIR), lambda i: (0, 0)),
                  pl.BlockSpec((d, 3 * d), lambda i: (0, 0))],
        out_specs=pl.BlockSpec((tm, 3 * d), lambda i: (i, 0)),
        out_shape=jax.ShapeDtypeStruct((n, 3 * d), BF16),
        compiler_params=_params("parallel"),
        name="qkv_rope",
    )(x, gain.reshape(1, d), positions.reshape(n, 1), inv128, (w_qkv * col_scale).astype(BF16))


def _moba_kernel(q_ref, k_ref, v_ref, o_ref, vt_ref):
    nblk = k_ref.shape[0] // MOBA_BLOCK
    bq = MOBA_BLOCK
    km_rows = 16

    means = []
    ones_row = jnp.where(_iota((VT_ROWS - HEAD_DIM, bq), 0) == 0, 1.0, 0.0).astype(BF16)
    for n in range(nblk):
        cols = slice(n * bq, (n + 1) * bq)
        means.append(jnp.mean(k_ref[cols, :].astype(F32), axis=0, keepdims=True))
        vt = v_ref[cols, :].astype(F32).T.astype(BF16)
        for h in range(2):
            vt_ref[h * VT_ROWS:h * VT_ROWS + HEAD_DIM, cols] = vt[h * HEAD_DIM:(h + 1) * HEAD_DIM]
            vt_ref[h * VT_ROWS + HEAD_DIM:(h + 1) * VT_ROWS, cols] = ones_row
    km_hi, km_lo = _split(jnp.concatenate(means + [jnp.zeros((km_rows - nblk, PAIR), F32)], axis=0))

    lane = _iota((1, PAIR), 1)
    own = (lane < HEAD_DIM, lane >= HEAD_DIM)
    spare = (HEAD_DIM, 0)
    blk = _iota((km_rows, bq), 0)
    causal = _iota((bq, bq), 0) <= _iota((bq, bq), 1)
    eye = jnp.where(_iota((bq, bq), 0) == _iota((bq, bq), 1), 1.0, 0.0).astype(BF16)

    def augmented_queries(c, h):
        q = q_ref[c * bq:(c + 1) * bq, :]
        qm = jnp.where(own[h], q, jnp.zeros_like(q))
        past = blk < c
        gate = _dot_nt(km_hi, qm) + _dot_nt(km_lo, qm)
        gate = jnp.where(past, gate, NEG_INF)
        rank = jnp.zeros(gate.shape, F32)
        for m in range(c):
            row = gate[m:m + 1, :]
            tie = jnp.where(m < blk, 1.0, 0.0)
            rank = rank + jnp.where(row > gate, 1.0, jnp.where(row == gate, tie, 0.0))
        drop = jnp.where((blk < nblk) & jnp.logical_not(past & (rank < MOBA_TOPK)), 1.0, 0.0)
        pad = [jnp.zeros((HEAD_DIM - km_rows, bq), F32)]
        rows = ([jnp.zeros((HEAD_DIM, bq), F32), drop] + pad) if spare[h] else ([drop] + pad + [jnp.zeros((HEAD_DIM, bq), F32)])
        dropped = _dot_nt(eye, jnp.concatenate(rows, axis=0).astype(BF16))
        return jnp.where(own[h], q, jnp.where(dropped > 0.5, NEG_INF, 0.0).astype(BF16))

    def score(c, h):
        k_aug = []
        for n in range(c + 1):
            kb = k_ref[n * bq:(n + 1) * bq, :]
            marker = jnp.where(lane == spare[h] + n, 1.0, 0.0) if n < c else jnp.zeros(lane.shape, F32)
            k_aug.append(jnp.where(own[h], kb, marker.astype(BF16)))
        return _dot_nt(jnp.concatenate(k_aug, axis=0), augmented_queries(c, h))

    def attend(c, h, s):
        parts = [s[n * bq:(n + 1) * bq] for n in range(c)]
        parts.append(jnp.where(causal, s[c * bq:], NEG_INF))
        top = parts[0].max(axis=0, keepdims=True)
        for part in parts[1:]:
            top = jnp.maximum(top, part.max(axis=0, keepdims=True))
        probs = jnp.concatenate([jnp.exp2((part - top).astype(BF16)) for part in parts], axis=0)
        acc = _dot(vt_ref[h * VT_ROWS:(h + 1) * VT_ROWS, 0:(c + 1) * bq], probs)
        return acc[:HEAD_DIM] / acc[HEAD_DIM:HEAD_DIM + 1]

    units = [(c, h) for c in range(nblk) for h in range(2)]
    ahead = 2
    pending = [score(*u) for u in units[:ahead]]
    outs = {}
    for i, (c, h) in enumerate(units):
        s = pending.pop(0)
        if i + ahead < len(units):
            pending.append(score(*units[i + ahead]))
        outs[h] = attend(c, h, s)
        if h == 1:
            o_ref[c * bq:(c + 1) * bq, :] = jnp.concatenate([outs[0], outs[1]], axis=0).T.astype(o_ref.dtype)


def moba_attention(qkv, batch, seq_len):
    n = qkv.shape[0]
    return pl.pallas_call(
        _moba_kernel,
        grid=(batch, N_PAIRS),
        in_specs=[pl.BlockSpec((seq_len, PAIR), lambda b, p: (b, p)),
                  pl.BlockSpec((seq_len, PAIR), lambda b, p: (b, N_PAIRS + p)),
                  pl.BlockSpec((seq_len, PAIR), lambda b, p: (b, 2 * N_PAIRS + p))],
        out_specs=pl.BlockSpec((seq_len, PAIR), lambda b, p: (b, p)),
        out_shape=jax.ShapeDtypeStruct((n, D_MODEL), BF16),
        scratch_shapes=[pltpu.VMEM((2 * VT_ROWS, seq_len), BF16)],
        compiler_params=_params("parallel", "parallel"),
        name="moba_attention",
    )(qkv, qkv, qkv)


def _mem_xattn_kernel(y_ref, wy_ref, x_ref, g_ref, wq_ref, kv_ref, wo_ref, o_ref):
    x = x_ref[...] + _dot(y_ref[...], wy_ref[...])
    xn = _rms(x, g_ref[...]).astype(BF16)
    q = (_dot(xn, wq_ref[...]) * (1.0 / math.sqrt(MEM_HEAD_DIM))).astype(BF16)
    outs = []
    for h in range(MEM_HEADS):
        sl = slice(h * MEM_HEAD_DIM, (h + 1) * MEM_HEAD_DIM)
        s = _dot_nt(q[:, sl], kv_ref[:, sl])
        s = s - jnp.max(s, axis=-1, keepdims=True)
        e = jnp.exp(s)
        pr = e / jnp.sum(e, axis=-1, keepdims=True)
        vh = kv_ref[:, D_MODEL + h * MEM_HEAD_DIM:D_MODEL + (h + 1) * MEM_HEAD_DIM]
        outs.append(_dot(pr.astype(BF16), vh).astype(BF16))
    o = jnp.concatenate(outs, axis=1)
    o_ref[...] = x + _dot(o, wo_ref[...])


def mem_cross_attention(y, wy, x, gain, wq, kv, wo, seq_len, tm=512):
    n, d = x.shape
    m = kv.shape[0] // (n // seq_len)
    per_seq = seq_len // tm
    return pl.pallas_call(
        _mem_xattn_kernel,
        grid=(n // tm,),
        in_specs=[pl.BlockSpec((tm, d), lambda i: (i, 0)),
                  pl.BlockSpec((d, d), lambda i: (0, 0)),
                  pl.BlockSpec((tm, d), lambda i: (i, 0)),
                  pl.BlockSpec((1, d), lambda i: (0, 0)),
                  pl.BlockSpec((d, d), lambda i: (0, 0)),
                  pl.BlockSpec((m, 2 * d), lambda i: (i // per_seq, 0)),
                  pl.BlockSpec((d, d), lambda i: (0, 0))],
        out_specs=pl.BlockSpec((tm, d), lambda i: (i, 0)),
        out_shape=jax.ShapeDtypeStruct((n, d), F32),
        compiler_params=_params("parallel"),
        name="mem_cross_attention",
    )(y, wy.astype(BF16), x, gain.reshape(1, d), wq.astype(BF16), kv, wo.astype(BF16))


def _router_kernel(x_ref, g_ref, w_ref, b_ref, h_out, meta_out, cnt_out):
    tm = x_ref.shape[0]
    h2 = _rms(x_ref[...], g_ref[...])
    hi = h2.astype(BF16)
    h_out[...] = hi
    lo = (h2 - hi.astype(F32)).astype(BF16)
    w_hi, w_lo = _split(w_ref[...])
    lg = _dot_nt(w_hi, hi) + _dot_nt(w_hi, lo) + _dot_nt(w_lo, hi) + b_ref[...]
    row = _iota((8, tm), 0).astype(F32)

    def first_argmax(val, vmax):
        return jnp.min(jnp.where(val == vmax, row, 8.0), axis=0, keepdims=True)

    gl = jnp.where(row < MOE_GROUPS, lg[0:8], -jnp.inf)
    gmax = jnp.max(gl, axis=0, keepdims=True)
    p_g = 1.0 / jnp.sum(jnp.exp(gl - gmax), axis=0, keepdims=True)
    gidx = first_argmax(gl, gmax)
    el = jnp.zeros((8, tm), F32)
    for g in range(MOE_GROUPS):
        el = el + jnp.where(gidx == g, lg[8 + 8 * g:16 + 8 * g], 0.0)
    ee = jnp.exp(el - jnp.max(el, axis=0, keepdims=True))
    pe = ee / jnp.sum(ee, axis=0, keepdims=True)
    p1 = jnp.max(pe, axis=0, keepdims=True)
    i1 = first_argmax(pe, p1)
    pe2 = jnp.where(row == i1, -1.0, pe)
    p2 = jnp.max(pe2, axis=0, keepdims=True)
    i2 = first_argmax(pe2, p2)
    e1 = gidx * MOE_EPG + i1
    e2 = gidx * MOE_EPG + i2
    gate1 = p_g * p1 / (p1 + p2)
    gate2 = p_g * p2 / (p1 + p2)

    erow = _iota((MOE_EXPERTS, tm), 0).astype(F32)
    oh1 = erow == e1
    oh2 = erow == e2
    oh = jnp.where(oh1 | oh2, 1.0, 0.0)
    before = jnp.where(_iota((tm, tm), 0) < _iota((tm, tm), 1), 1.0, 0.0).astype(BF16)
    cnt_before = _dot(oh.astype(BF16), before)
    lr1 = jnp.sum(jnp.where(oh1, cnt_before, 0.0), axis=0, keepdims=True)
    lr2 = jnp.sum(jnp.where(oh2, cnt_before, 0.0), axis=0, keepdims=True)
    zero = jnp.zeros((1, tm), F32)
    meta_out[...] = jnp.concatenate([e1, e2, gate1, gate2, lr1, lr2, zero, zero], axis=0)
    cnt_out[...] = jnp.broadcast_to(jnp.sum(oh, axis=1, keepdims=True), (MOE_EXPERTS, LANES))


def moe_router(x, gain, w_grp, b_grp, w_exp, b_exp):
    n, d = x.shape
    tm = MOE_TILE
    wt = jnp.zeros((LANES, d), F32).at[0:MOE_GROUPS].set(w_grp.T).at[8:8 + MOE_EXPERTS].set(w_exp.T)
    bt = jnp.zeros((LANES, 1), F32).at[0:MOE_GROUPS, 0].set(b_grp).at[8:8 + MOE_EXPERTS, 0].set(b_exp)
    return pl.pallas_call(
        _router_kernel,
        grid=(n // tm,),
        in_specs=[pl.BlockSpec((tm, d), lambda i: (i, 0)),
                  pl.BlockSpec((1, d), lambda i: (0, 0)),
                  pl.BlockSpec((LANES, d), lambda i: (0, 0)),
                  pl.BlockSpec((LANES, 1), lambda i: (0, 0))],
        out_specs=[pl.BlockSpec((tm, d), lambda i: (i, 0)),
                   pl.BlockSpec((8, tm), lambda i: (0, i)),
                   pl.BlockSpec((MOE_EXPERTS, LANES), lambda i: (i, 0))],
        out_shape=[jax.ShapeDtypeStruct((n, d), BF16),
                   jax.ShapeDtypeStruct((8, n), F32),
                   jax.ShapeDtypeStruct((n // tm * MOE_EXPERTS, LANES), F32)],
        compiler_params=_params("parallel"),
        name="moe_router",
    )(x, gain.reshape(1, d), wt, bt)


def _chunk_copies(tab_ref, t, make_copy, act):
    base = t * RUN_TAB
    for ci, ck in enumerate(RUN_CHUNKS):
        first = base + len(RUN_CHUNKS) + ci * 2 * MOE_EXPERTS

        def body(i, _, first=first, ck=ck):
            src = tab_ref[first + 2 * i]
            dst = tab_ref[first + 2 * i + 1]
            act(make_copy(pl.multiple_of(src, MOE_ALIGN), pl.multiple_of(dst, MOE_ALIGN), ck))
            return 0

        lax.fori_loop(0, tab_ref[base + ci], body, 0)


def _run_copies(tab_ref, first, count, chunks, make_copy, act):
    def body(e, _):
        base = first + e * 3
        length = tab_ref[base]
        src = tab_ref[base + 1]
        dst = tab_ref[base + 2]
        for ck in chunks:
            @pl.when((length & ck) != 0)
            def _():
                off = length & (-2 * ck)
                act(make_copy(pl.multiple_of(src + off, MOE_ALIGN), pl.multiple_of(dst + off, MOE_ALIGN), ck))
        return 0
    lax.fori_loop(0, count, body, 0)


def _start(cp):
    cp.start()


def _wait(cp):
    cp.wait()


def _dispatch_kernel(tab_ref, h_ref, pos_ref, rows_hbm, sorted_ref, zero_ref, sem, zsem):
    t = pl.program_id(0)
    nt = pl.num_programs(0)
    slot = t % 2
    r = _iota((MOE_SORTED, MOE_TILE), 0)
    perm = jnp.where((r == pos_ref[0:1, :]) | (r == pos_ref[1:2, :]), 1.0, 0.0).astype(BF16)
    sorted_ref[slot] = _dot(perm, h_ref[...]).astype(BF16)

    def run_copy(s):
        def make(src, dst, ck):
            return pltpu.make_async_copy(sorted_ref.at[s, pl.ds(src, ck)], rows_hbm.at[pl.ds(dst, ck)], sem.at[s])
        return make

    def zero_copy(src, dst, ck):
        del src
        return pltpu.make_async_copy(zero_ref.at[pl.ds(0, ck)], rows_hbm.at[pl.ds(dst, ck)], zsem)

    _chunk_copies(tab_ref, t, run_copy(slot), _start)

    @pl.when(t > 0)
    def _():
        _chunk_copies(tab_ref, t - 1, run_copy(1 - slot), _wait)

    @pl.when(t == nt - 1)
    def _():
        zero_ref[...] = jnp.zeros_like(zero_ref)
        pad = nt * RUN_TAB
        _run_copies(tab_ref, pad, MOE_EXPERTS, PAD_CHUNKS, zero_copy, _start)
        spare = pad + MOE_EXPERTS * 3
        n_spare = tab_ref[spare]

        def spare_copy(i):
            dst = pl.multiple_of(tab_ref[spare + 2] + i * PAD_ROWS, PAD_ROWS)
            return zero_copy(0, dst, PAD_ROWS)

        lax.fori_loop(0, n_spare, lambda i, c: (_start(spare_copy(i)), c)[1], 0)
        _chunk_copies(tab_ref, t, run_copy(slot), _wait)
        _run_copies(tab_ref, pad, MOE_EXPERTS, PAD_CHUNKS, zero_copy, _wait)
        lax.fori_loop(0, n_spare, lambda i, c: (_wait(spare_copy(i)), c)[1], 0)


def moe_dispatch(tab, h2, pos, n_rows):
    n, d = h2.shape
    tm = MOE_TILE
    return pl.pallas_call(
        _dispatch_kernel,
        grid_spec=pltpu.PrefetchScalarGridSpec(
            num_scalar_prefetch=1,
            grid=(n // tm,),
            in_specs=[pl.BlockSpec((tm, d), lambda i, tab: (i, 0)),
                      pl.BlockSpec((8, tm), lambda i, tab: (0, i))],
            out_specs=pl.BlockSpec(memory_space=pl.ANY),
            scratch_shapes=[pltpu.VMEM((2, MOE_SORTED, d), BF16), pltpu.VMEM((PAD_ROWS, d), BF16),
                            pltpu.SemaphoreType.DMA((2,)), pltpu.SemaphoreType.DMA],
        ),
        out_shape=jax.ShapeDtypeStruct((n_rows, d), BF16),
        compiler_params=_params("arbitrary"),
        name="moe_dispatch",
    )(tab, h2, pos)


def _expert_kernel(be_ref, nu_ref, x_ref, w1_ref, w3_ref, w2_ref, y_ref, w1b, w3b, w2b):
    b = pl.program_id(0)
    used = b < nu_ref[0]
    new_expert = (b == 0) | (be_ref[b] != be_ref[jnp.maximum(b - 1, 0)])

    @pl.when(used & new_expert)
    def _():
        w1b[...] = w1_ref[0, 0].astype(BF16)
        w3b[...] = w3_ref[0, 0].astype(BF16)
        w2b[...] = w2_ref[0, 0].astype(BF16)

    @pl.when(used)
    def _():
        xb = x_ref[...]
        a = _dot(xb, w1b[...])
        c = _dot(xb, w3b[...])
        hid = (a * jax.nn.sigmoid(a) * c).astype(BF16)
        y_ref[...] = _dot(hid, w2b[...]).astype(y_ref.dtype)

    @pl.when(jnp.logical_not(used))
    def _():
        y_ref[...] = jnp.zeros_like(y_ref)


def moe_experts(block_expert, n_used, rows, layer, w1, w3, w2):
    n_rows = rows.shape[0]
    nb = n_rows // MOE_ROWS
    d, ff = w1.shape[2], w1.shape[3]

    def xmap(b, be, nu):
        return (jnp.minimum(b, nu[0] - 1), 0)

    def wmap(b, be, nu):
        return (layer, be[jnp.minimum(b, nu[0] - 1)], 0, 0)

    return pl.pallas_call(
        _expert_kernel,
        grid_spec=pltpu.PrefetchScalarGridSpec(
            num_scalar_prefetch=2,
            grid=(nb,),
            in_specs=[pl.BlockSpec((MOE_ROWS, d), xmap),
                      pl.BlockSpec((1, 1, d, ff), wmap),
                      pl.BlockSpec((1, 1, d, ff), wmap),
                      pl.BlockSpec((1, 1, ff, d), wmap)],
            out_specs=pl.BlockSpec((MOE_ROWS, d), lambda b, be, nu: (b, 0)),
            scratch_shapes=[pltpu.VMEM((d, ff), BF16), pltpu.VMEM((d, ff), BF16), pltpu.VMEM((ff, d), BF16)],
        ),
        out_shape=jax.ShapeDtypeStruct((n_rows, d), BF16),
        compiler_params=_params("arbitrary"),
        name="moe_experts",
    )(block_expert, n_used, rows, w1, w3, w2)


def _combine_kernel(final_norm, tab_ref, y_hbm, pos_ref, gate_ref, x_ref, g_ref, o_ref, ys_ref, sem):
    t = pl.program_id(0)
    nt = pl.num_programs(0)
    slot = t % 2

    def run_copy(s):
        def make(src, dst, ck):
            return pltpu.make_async_copy(y_hbm.at[pl.ds(dst, ck)], ys_ref.at[s, pl.ds(src, ck)], sem.at[s])
        return make

    @pl.when(t == 0)
    def _():
        ys_ref[...] = jnp.zeros_like(ys_ref)
        _chunk_copies(tab_ref, 0, run_copy(0), _start)

    @pl.when(t + 1 < nt)
    def _():
        _chunk_copies(tab_ref, t + 1, run_copy(1 - slot), _start)

    r = _iota((MOE_SORTED, MOE_TILE), 0)
    hit1 = r == pos_ref[0:1, :]
    hit2 = r == pos_ref[1:2, :]
    wgt = jnp.where(hit1, gate_ref[2:3, :], jnp.where(hit2, gate_ref[3:4, :], 0.0))
    row_gate = jnp.sum(wgt, axis=1, keepdims=True)
    row_used = row_gate > 0.0
    perm = jnp.where(hit1 | hit2, 1.0, 0.0).astype(BF16)
    _chunk_copies(tab_ref, t, run_copy(slot), _wait)
    ys = jnp.where(row_used, ys_ref[slot].astype(F32) * row_gate, 0.0).astype(BF16)
    out = x_ref[...] + _dot_tn(perm, ys)
    if final_norm:
        out = _rms(out, g_ref[...])
    o_ref[...] = out


def moe_combine(tab, y, pos, meta, x, final_gain):
    n, d = x.shape
    tm = MOE_TILE
    final_norm = final_gain is not None
    gain = (final_gain if final_norm else jnp.ones((d,), F32)).reshape(1, d)
    return pl.pallas_call(
        functools.partial(_combine_kernel, final_norm),
        grid_spec=pltpu.PrefetchScalarGridSpec(
            num_scalar_prefetch=1,
            grid=(n // tm,),
            in_specs=[pl.BlockSpec(memory_space=pl.ANY),
                      pl.BlockSpec((8, tm), lambda i, tab: (0, i)),
                      pl.BlockSpec((8, tm), lambda i, tab: (0, i)),
                      pl.BlockSpec((tm, d), lambda i, tab: (i, 0)),
                      pl.BlockSpec((1, d), lambda i, tab: (0, 0))],
            out_specs=pl.BlockSpec((tm, d), lambda i, tab: (i, 0)),
            scratch_shapes=[pltpu.VMEM((2, MOE_SORTED, d), BF16), pltpu.SemaphoreType.DMA((2,))],
        ),
        out_shape=jax.ShapeDtypeStruct((n, d), F32),
        compiler_params=_params("arbitrary"),
        name="moe_combine",
    )(tab, y, pos, meta, x, gain)


def hierarchical_moe(x, gain, w_grp, b_grp, w_exp, b_exp, layer, w1, w3, w2, final_gain):
    n, d = x.shape
    tm = MOE_TILE
    nt = n // tm
    h2, meta, cnt = moe_router(x, gain, w_grp, b_grp, w_exp, b_exp)

    cnt = cnt.reshape(nt, MOE_EXPERTS, LANES)[:, :, 0].astype(I32)
    run = (cnt + MOE_ALIGN - 1) // MOE_ALIGN * MOE_ALIGN
    src = jnp.cumsum(run, axis=1) - run
    before = jnp.cumsum(run, axis=0) - run
    total = jnp.sum(run, axis=0)
    padded = (total + MOE_ROWS - 1) // MOE_ROWS * MOE_ROWS
    pad_end = jnp.cumsum(padded)
    dst = (pad_end - padded)[None, :] + before
    max_rows = 2 * n + nt * MOE_EXPERTS * (MOE_ALIGN - 1) + MOE_EXPERTS * (MOE_ROWS - 1)
    nb = -(-max_rows // MOE_ROWS)
    sizes = jnp.array(RUN_CHUNKS, I32)[None, :, None]
    has = (run[:, None, :] & sizes) != 0
    above = run[:, None, :] & (-2 * sizes)
    place = jnp.where(has, jnp.cumsum(has.astype(I32), axis=2) - 1, -1)
    front = place[..., None] == jnp.arange(MOE_EXPERTS, dtype=I32)
    c_src = jnp.sum(jnp.where(front, (src[:, None, :] + above)[..., None], 0), axis=2)
    c_dst = jnp.sum(jnp.where(front, (dst[:, None, :] + above)[..., None], 0), axis=2)
    chunks = jnp.stack([c_src, c_dst], axis=-1).reshape(nt, -1)
    runs = jnp.concatenate([jnp.sum(has.astype(I32), axis=2), chunks], axis=1).reshape(-1)
    zero = jnp.zeros((MOE_EXPERTS,), I32)
    pads = jnp.stack([padded - total, zero, pad_end - padded + total], axis=-1)
    spare = jnp.stack([(nb * MOE_ROWS - pad_end[-1]) // PAD_ROWS, zero[0], pad_end[-1]])
    tab = jnp.concatenate([runs, pads.reshape(-1), spare]).astype(I32)
    n_used = (pad_end[-1] // MOE_ROWS).astype(I32).reshape(1)
    block_start = jnp.arange(nb, dtype=I32) * MOE_ROWS
    block_expert = jnp.minimum(
        jnp.sum((pad_end[None, :] <= block_start[:, None]).astype(I32), axis=1), MOE_EXPERTS - 1).astype(I32)
    e = meta[0:2].astype(I32).reshape(2, nt, tm, 1)
    hit = e == jnp.arange(MOE_EXPERTS, dtype=I32)
    pos = jnp.sum(jnp.where(hit, src[None, :, None, :], 0), axis=-1).reshape(2, n) + meta[4:6].astype(I32)
    pos8 = jnp.zeros((8, n), I32).at[0:2].set(pos)

    rows = moe_dispatch(tab, h2, pos8, nb * MOE_ROWS)
    y = moe_experts(block_expert, n_used, rows, layer, w1, w3, w2)
    return moe_combine(tab, y, pos8, meta, x, final_gain)


def kernel(x, mem, positions, ln_mix, ln_mem, ln_memkv, ln_ffn, rw_mu, rw_w0, rw_w1, rw_w2, rw_a0, rw_a1, rw_a2, rw_g1, rw_g2, rw_kk, rw_ka, rw_rk, rw_wrkv, rw_lnx_g, rw_lnx_b, rw_wo, mb_wqkv, mb_wo, mx_wq, mx_wkv, mx_wo, moe_wg, moe_bg, moe_we, moe_be, moe_w1, moe_w3, moe_w2, ln_f):
    B, T, C = x.shape
    n = B * T
    depth = ln_mix.shape[0]
    xf = x.reshape(n, C)
    memf = mem.reshape(-1, C)
    for i in range(depth):
        j = i // 2
        if i % 2 == 0:
            r, k, v, kk, akk, lw, g = rwkv_mix(xf, T, ln_mix[i], rw_mu[j], rw_w0[j], rw_w1[j], rw_w2[j],
                                               rw_a0[j], rw_a1[j], rw_a2[j], rw_g1[j], rw_g2[j],
                                               rw_kk[j], rw_ka[j], rw_wrkv[j])
            y = rwkv_recurrence(r, k, v, kk, akk, lw, g, rw_lnx_g[j], rw_lnx_b[j], rw_rk[j], B, T)
            w_mix_o = rw_wo[j]
        else:
            qkv = qkv_rope(xf, ln_mix[i], positions, mb_wqkv[j])
            y = moba_attention(qkv, B, T)
            w_mix_o = mb_wo[j]
        kv = norm_linear(memf, ln_memkv[i], mx_wkv[i].astype(BF16))
        xf = mem_cross_attention(y, w_mix_o, xf, ln_mem[i], mx_wq[i], kv, mx_wo[i], T)
        xf = hierarchical_moe(xf, ln_ffn[i], moe_wg[i], moe_bg[i], moe_we[i], moe_be[i],
                              i, moe_w1, moe_w3, moe_w2, ln_f if i == depth - 1 else None)
    return xf.reshape(B, T, C)
```

```python
import functools
import math

import jax
import jax.numpy as jnp
from jax import lax
from jax.experimental import pallas as pl
from jax.experimental.pallas import tpu as pltpu

F32 = jnp.float32
BF16 = jnp.bfloat16
I32 = jnp.int32
U32 = jnp.uint32

D_MODEL = 1024
HEAD_DIM = 64
PAIR = 2 * HEAD_DIM
N_PAIRS = D_MODEL // PAIR
RWKV_GN_EPS = 64e-5
RWKV_CHUNK = 64
RWKV_CHUNKS_PER_STEP = 4
MOBA_BLOCK = 256
MOBA_TOPK = 3
VT_ROWS = HEAD_DIM + 16
ROPE_THETA = 10000.0
MEM_HEADS = 4
MEM_HEAD_DIM = D_MODEL // MEM_HEADS
MOE_GROUPS = 4
MOE_EPG = 8
MOE_EXPERTS = MOE_GROUPS * MOE_EPG
MOE_FF = D_MODEL // 2
RMS_EPS = 1e-6
NEG_INF = -1e30

LANES = 128
ROW_ALIGN = 8
MOE_TILE = 512
MOE_ROWS = 512
MOE_ALIGN = 16
MOE_SORTED = 2 * MOE_TILE + MOE_EXPERTS * (MOE_ALIGN - 1) + (-(2 * MOE_TILE + MOE_EXPERTS * (MOE_ALIGN - 1))) % MOE_ALIGN
RUN_CHUNKS = tuple(MOE_ALIGN << s for s in range((MOE_TILE // MOE_ALIGN).bit_length() - 1, -1, -1))
RUN_TAB = len(RUN_CHUNKS) * (1 + 2 * MOE_EXPERTS)
PAD_ROWS = MOE_ROWS // 2
PAD_CHUNKS = tuple(MOE_ALIGN << s for s in range((PAD_ROWS // MOE_ALIGN).bit_length() - 1, -1, -1))
VMEM_LIMIT = 56 * 2 ** 20


def _params(*sem):
    return pltpu.CompilerParams(dimension_semantics=sem, vmem_limit_bytes=VMEM_LIMIT)


def _iota(shape, dim):
    return lax.broadcasted_iota(I32, shape, dim)


def _dot(a, b):
    return jnp.dot(a, b, preferred_element_type=F32)


def _dot_nt(a, b):
    return lax.dot_general(a, b, (((1,), (1,)), ((), ())), preferred_element_type=F32)


def _dot_tn(a, b):
    return lax.dot_general(a, b, (((0,), (0,)), ((), ())), preferred_element_type=F32)


def _split(x):
    hi = x.astype(BF16)
    lo = (x - hi.astype(F32)).astype(BF16)
    return hi, lo


def _rms(x, g):
    return x * lax.rsqrt(jnp.mean(x * x, axis=-1, keepdims=True) + RMS_EPS) * g


def _norm_linear_kernel(x_ref, g_ref, w_ref, o_ref):
    xn = _rms(x_ref[...], g_ref[...]).astype(BF16)
    o_ref[...] = _dot(xn, w_ref[...]).astype(o_ref.dtype)


def norm_linear(x, g, w, tm=256):
    n, k = x.shape
    dout = w.shape[1]
    return pl.pallas_call(
        _norm_linear_kernel,
        grid=(n // tm,),
        in_specs=[pl.BlockSpec((tm, k), lambda i: (i, 0)),
                  pl.BlockSpec((1, k), lambda i: (0, 0)),
                  pl.BlockSpec((k, dout), lambda i: (0, 0))],
        out_specs=pl.BlockSpec((tm, dout), lambda i: (i, 0)),
        out_shape=jax.ShapeDtypeStruct((n, dout), BF16),
        compiler_params=_params("parallel"),
        name="norm_linear",
    )(x, g.reshape(1, k), w)


def _head_sum(x_sq):
    r = _iota((PAIR, PAIR), 0) // HEAD_DIM
    c = _iota((PAIR, PAIR), 1) // HEAD_DIM
    bd = jnp.where(r == c, 1.0, 0.0).astype(BF16)
    hi, lo = _split(x_sq)
    return _dot(hi, bd) + _dot(lo, bd)


def _rwkv_mix_kernel(seq_len, x_ref, xp_ref, g_ref, mu_ref, vec_ref, wr_ref, wk_ref, wv_ref,
                     l1_ref, w2_ref, a2_ref, g2_ref,
                     r_out, k_out, v_out, kk_out, akk_out, lw_out, g_out):
    i = pl.program_id(0)
    tm = x_ref.shape[0]
    gain = g_ref[...]
    h = _rms(x_ref[...], gain)
    hp = _rms(xp_ref[...], gain)[ROW_ALIGN - 1:ROW_ALIGN, :]
    hp = jnp.where((i * tm) % seq_len == 0, 0.0, hp)
    hs = pltpu.roll(h, 1, 0)
    hs = jnp.where(_iota((tm, 1), 0) == 0, hp, hs)
    hb = h.astype(BF16)
    dxb = (hs - h).astype(BF16)
    mub = mu_ref[...].astype(BF16)

    def mix(s):
        return hb + dxb * mub[s:s + 1, :]

    vec = vec_ref[...]
    w0, a0, k_k, k_a = vec[0:1], vec[1:2], vec[2:3], vec[3:4]
    l1 = l1_ref[...]
    r = _dot(mix(0), wr_ref[...])
    k = _dot(mix(2), wk_ref[...])
    v = _dot(mix(3), wv_ref[...])
    tw = jnp.tanh(_dot(mix(1), l1[:, 0:64])).astype(BF16)
    lw = -math.exp(-0.5) * jax.nn.sigmoid(w0 + _dot(tw, w2_ref[...]))
    ta = _dot(mix(4), l1[:, 64:128]).astype(BF16)
    a = jax.nn.sigmoid(a0 + _dot(ta, a2_ref[...]))
    tg = jax.nn.sigmoid(_dot(mix(5), l1[:, 128:256])).astype(BF16)
    g = _dot(tg, g2_ref[...])

    kk = k * k_k
    for p in range(N_PAIRS):
        sl = slice(p * PAIR, (p + 1) * PAIR)
        kkp = kk[:, sl]
        kkn = kkp * lax.rsqrt(jnp.maximum(_head_sum(kkp * kkp), 1e-24))
        kk_out[:, sl] = kkn.astype(kk_out.dtype)
        akk_out[:, sl] = (kkn * a[:, sl]).astype(akk_out.dtype)
    r_out[...] = r.astype(r_out.dtype)
    k_out[...] = (k * (1.0 + (a - 1.0) * k_a)).astype(k_out.dtype)
    v_out[...] = v.astype(v_out.dtype)
    lw_out[...] = lw
    g_out[...] = g.astype(g_out.dtype)


def rwkv_mix(x, seq_len, gain, mu, w0, w1, w2, a0, a1, a2, g1, g2, k_k, k_a, w_rkv, tm=512):
    n, d = x.shape
    mu8 = jnp.zeros((8, d), F32).at[:6].set(mu)
    vec = jnp.zeros((8, d), F32).at[0].set(w0).at[1].set(a0).at[2].set(k_k).at[3].set(k_a)
    l1 = jnp.concatenate([w1, a1, g1], axis=1).astype(BF16)
    wb = w_rkv.astype(BF16)
    row = pl.BlockSpec((tm, d), lambda i: (i, 0))
    full = lambda a: pl.BlockSpec(a.shape, lambda i: (0,) * a.ndim)
    args = (x, x, gain.reshape(1, d), mu8, vec, wb[0], wb[1], wb[2], l1,
            w2.astype(BF16), a2.astype(BF16), g2.astype(BF16))
    in_specs = [row, pl.BlockSpec((ROW_ALIGN, d), lambda i: (jnp.maximum(i * (tm // ROW_ALIGN) - 1, 0), 0))]
    in_specs += [full(a) for a in args[2:]]
    outs = [jax.ShapeDtypeStruct((n, d), BF16)] * 5 + [jax.ShapeDtypeStruct((n, d), F32),
                                                       jax.ShapeDtypeStruct((n, d), BF16)]
    return pl.pallas_call(
        functools.partial(_rwkv_mix_kernel, seq_len),
        grid=(n // tm,),
        in_specs=in_specs,
        out_specs=[row] * 7,
        out_shape=outs,
        compiler_params=_params("parallel"),
        name="rwkv_mix",
    )(*args)


def _rwkv_rec_kernel(r_ref, k_ref, v_ref, kk_ref, akk_ref, lw_ref, g_ref, vec_ref, y_ref, s_ref):
    c = pl.program_id(1)
    L = RWKV_CHUNK

    @pl.when(c == 0)
    def _():
        s_ref[...] = jnp.zeros_like(s_ref)

    lane = _iota((1, PAIR), 1)
    m0 = lane < HEAD_DIM
    ri = _iota((2 * L, 2 * L), 0)
    ci = _iota((2 * L, 2 * L), 1)
    same = (ri // L) == (ci // L)
    strict = same & (ci < ri)
    incl = same & (ci <= ri)
    eye = jnp.where(ri == ci, 1.0, 0.0)
    tri = jnp.where(_iota((L, L), 1) <= _iota((L, L), 0), 1.0, 0.0).astype(BF16)
    rb = _iota((PAIR, PAIR), 0) // HEAD_DIM
    cb = _iota((PAIR, PAIR), 1) // HEAD_DIM
    bd = rb == cb

    def stack(x):
        return jnp.concatenate([jnp.where(m0, x, 0.0), jnp.where(m0, 0.0, x)], axis=0)

    def fold(x):
        return x[:L] + x[L:]

    def head_mean(x):
        s0 = jnp.sum(jnp.where(m0, x, 0.0), axis=-1, keepdims=True)
        s1 = jnp.sum(jnp.where(m0, 0.0, x), axis=-1, keepdims=True)
        return jnp.where(m0, s0, s1) * (1.0 / HEAD_DIM)

    vec = vec_ref[...]
    n_sub = r_ref.shape[0] // L
    pairs = range(n_sub * N_PAIRS)
    rws = [slice((q // N_PAIRS) * L, (q // N_PAIRS + 1) * L) for q in pairs]
    sls = [slice((q % N_PAIRS) * PAIR, (q % N_PAIRS + 1) * PAIR) for q in pairs]
    lw = [lw_ref[rws[p], sls[p]] for p in pairs]
    cum = []
    for p in pairs:
        lw_hi, lw_lo = _split(lw[p])
        cum.append(_dot(tri, lw_hi) + _dot(tri, lw_lo))
    dec_all, x_kap, x_r, a_t, k_t, sc = [], [], [], [], [], []
    for p in pairs:
        rw, sl = rws[p], sls[p]
        dec = jnp.exp(cum[p])
        inv = jnp.exp(-cum[p])
        dec_prev = jnp.exp(cum[p] - lw[p])
        dec_all.append(dec[L - 1:L, :])
        x_kap.append(stack(kk_ref[rw, sl].astype(F32) * dec_prev).astype(BF16))
        x_r.append(stack(r_ref[rw, sl].astype(F32) * dec))
        a_t.append((akk_ref[rw, sl].astype(F32) * inv).astype(BF16))
        k_t.append((k_ref[rw, sl].astype(F32) * inv).astype(BF16))
        xs = jnp.concatenate([x_kap[p], x_r[p].astype(BF16)], axis=0)
        ys = jnp.concatenate([a_t[p], a_t[p], k_t[p], k_t[p]], axis=0)
        sc.append(_dot_nt(xs, ys))
    mp = [jnp.where(strict, -sc[p][:2 * L, :2 * L], 0.0) for p in pairs]
    m_kk = [jnp.where(strict, sc[p][:2 * L, 2 * L:], 0.0).astype(BF16) for p in pairs]
    s_ra = [jnp.where(incl, sc[p][2 * L:, :2 * L], 0.0).astype(BF16) for p in pairs]
    s_rk = [jnp.where(incl, sc[p][2 * L:, 2 * L:], 0.0).astype(BF16) for p in pairs]
    v_st = [stack(v_ref[rws[p], sls[p]].astype(F32)).astype(BF16) for p in pairs]
    bo = [_dot(jnp.concatenate([m_kk[p], s_rk[p]], axis=0), v_st[p]) for p in pairs]
    b1 = [bo[p][:2 * L] for p in pairs]
    o1 = [bo[p][2 * L:] for p in pairs]

    t_inv = [eye + mp[p] for p in pairs]
    mp = [_dot(mp[p].astype(BF16), mp[p].astype(BF16)) for p in pairs]
    steps = int(math.log2(L)) - 1
    for j in range(steps):
        mpb = [mp[p].astype(BF16) for p in pairs]
        if j + 1 < steps:
            both = [_dot(mpb[p], jnp.concatenate([mpb[p], t_inv[p].astype(BF16)], axis=1)) for p in pairs]
            mp = [both[p][:, :2 * L] for p in pairs]
            t_inv = [t_inv[p] + both[p][:, 2 * L:] for p in pairs]
        else:
            t_inv = [t_inv[p] + _dot(mpb[p], t_inv[p].astype(BF16)) for p in pairs]

    wu = [_dot(t_inv[p].astype(BF16), jnp.concatenate([x_kap[p], b1[p].astype(BF16)], axis=1))
          for p in pairs]
    corr = [_dot(s_ra[p], wu[p].astype(BF16)) for p in pairs]
    g_m, h0t, r_hat, o0 = [], [], [], []
    for p in pairs:
        r_hat.append(fold(x_r[p] - corr[p][:, :PAIR]).astype(BF16))
        o0.append(fold(o1[p] - corr[p][:, PAIR:]))
        w_f = fold(wu[p][:, :PAIR]).astype(BF16)
        u0_f = fold(wu[p][:, PAIR:]).astype(BF16)
        g_m.append(jnp.where(bd, _dot_tn(a_t[p], w_f), 0.0).astype(BF16))
        h0t.append(jnp.where(bd, _dot_tn(v_ref[rws[p], sls[p]], k_t[p]) - _dot_tn(u0_f, a_t[p]), 0.0))
    o = []
    for p in pairs:
        hp = p % N_PAIRS
        s = s_ref[hp]
        sb = s.astype(BF16)
        o.append(_dot_nt(r_hat[p], sb) + o0[p])
        s_ref[hp] = (s - _dot_nt(sb, g_m[p]) + h0t[p]) * dec_all[p]
    for p in pairs:
        rw, sl = rws[p], sls[p]
        mean = head_mean(o[p])
        cen = o[p] - mean
        var = head_mean(cen * cen)
        gn = cen * lax.rsqrt(var + RWKV_GN_EPS) * vec[0:1, sl] + vec[1:2, sl]
        r = r_ref[rw, sl].astype(F32)
        k = k_ref[rw, sl].astype(F32)
        bonus = head_mean(r * k * vec[2:3, sl]) * HEAD_DIM * v_ref[rw, sl].astype(F32)
        y_ref[rw, sl] = ((gn + bonus) * g_ref[rw, sl].astype(F32)).astype(y_ref.dtype)


def rwkv_recurrence(r, k, v, kk, akk, lw, g, lnx_g, lnx_b, r_k, batch, seq_len):
    n, d = r.shape
    rows = RWKV_CHUNK * RWKV_CHUNKS_PER_STEP
    nc = seq_len // rows
    vec = jnp.zeros((8, d), F32).at[0].set(lnx_g).at[1].set(lnx_b).at[2].set(r_k.reshape(d))
    blk = pl.BlockSpec((rows, d), lambda b, c: (b * nc + c, 0))
    return pl.pallas_call(
        _rwkv_rec_kernel,
        grid=(batch, nc),
        in_specs=[blk] * 7 + [pl.BlockSpec((8, d), lambda b, c: (0, 0))],
        out_specs=blk,
        out_shape=jax.ShapeDtypeStruct((n, d), BF16),
        scratch_shapes=[pltpu.VMEM((N_PAIRS, PAIR, PAIR), F32)],
        compiler_params=_params("parallel", "arbitrary"),
        name="rwkv_recurrence",
    )(r, k, v, kk, akk, lw, g, vec)


def _qkv_rope_kernel(x_ref, g_ref, pos_ref, inv_ref, w_ref, o_ref):
    d = x_ref.shape[1]
    xn = _rms(x_ref[...], g_ref[...]).astype(BF16)
    y = [_dot(xn, w_ref[:, part * d:(part + 1) * d]) for part in range(3)]
    ang = pos_ref[...].astype(F32) * inv_ref[...]
    first = (_iota((1, PAIR), 1) % HEAD_DIM) < HEAD_DIM // 2
    sn = jnp.sin(ang)
    cos = jnp.cos(ang)
    sin_lo = jnp.where(first, -sn, 0.0)
    sin_hi = jnp.where(first, 0.0, sn)
    for part in range(2):
        for p in range(N_PAIRS):
            t = y[part][:, p * PAIR:(p + 1) * PAIR]
            rot = (t * cos + pltpu.roll(t, PAIR - HEAD_DIM // 2, 1) * sin_lo
                   + pltpu.roll(t, HEAD_DIM // 2, 1) * sin_hi)
            o_ref[:, part * d + p * PAIR:part * d + (p + 1) * PAIR] = rot.astype(o_ref.dtype)
    o_ref[:, 2 * d:] = y[2].astype(o_ref.dtype)


def qkv_rope(x, gain, positions, w_qkv, tm=512):
    n, d = x.shape
    half = HEAD_DIM // 2
    inv = ROPE_THETA ** (-jnp.arange(half, dtype=F32) * 2.0 / HEAD_DIM)
    inv128 = jnp.tile(inv, PAIR // half).reshape(1, PAIR)
    col_scale = jnp.where(jnp.arange(3 * d) < d, math.log2(math.e) / math.sqrt(HEAD_DIM), 1.0).astype(F32)
    return pl.pallas_call(
        _qkv_rope_kernel,
        grid=(n // tm,),
        in_specs=[pl.BlockSpec((tm, d), lambda i: (i, 0)),
                  pl.BlockSpec((1, d), lambda i: (0, 0)),
                  pl.BlockSpec((tm, 1), lambda i: (i, 0)),
                  pl.BlockSpec((1, PAIR), lambda i: (0, 0)),
                  pl.BlockSpec((d, 3 * d), lambda i: (0, 0))],
        out_specs=pl.BlockSpec((tm, 3 * d), lambda i: (i, 0)),
        out_shape=jax.ShapeDtypeStruct((n, 3 * d), BF16),
        compiler_params=_params("parallel"),
        name="qkv_rope",
    )(x, gain.reshape(1, d), positions.reshape(n, 1), inv128, (w_qkv * col_scale).astype(BF16))


def _moba_kernel(q_ref, k_ref, v_ref, o_ref, vt_ref):
    nblk = k_ref.shape[0] // MOBA_BLOCK
    bq = MOBA_BLOCK
    km_rows = 16

    means = []
    ones_row = jnp.where(_iota((VT_ROWS - HEAD_DIM, bq), 0) == 0, 1.0, 0.0).astype(BF16)
    for n in range(nblk):
        cols = slice(n * bq, (n + 1) * bq)
        means.append(jnp.mean(k_ref[cols, :].astype(F32), axis=0, keepdims=True))
        vt = v_ref[cols, :].astype(F32).T.astype(BF16)
        for h in range(2):
            vt_ref[h * VT_ROWS:h * VT_ROWS + HEAD_DIM, cols] = vt[h * HEAD_DIM:(h + 1) * HEAD_DIM]
            vt_ref[h * VT_ROWS + HEAD_DIM:(h + 1) * VT_ROWS, cols] = ones_row
    km_hi, km_lo = _split(jnp.concatenate(means + [jnp.zeros((km_rows - nblk, PAIR), F32)], axis=0))

    lane = _iota((1, PAIR), 1)
    own = (lane < HEAD_DIM, lane >= HEAD_DIM)
    spare = (HEAD_DIM, 0)
    blk = _iota((km_rows, bq), 0)
    causal = _iota((bq, bq), 0) <= _iota((bq, bq), 1)
    eye = jnp.where(_iota((bq, bq), 0) == _iota((bq, bq), 1), 1.0, 0.0).astype(BF16)

    def augmented_queries(c, h):
        q = q_ref[c * bq:(c + 1) * bq, :]
        qm = jnp.where(own[h], q, jnp.zeros_like(q))
        past = blk < c
        gate = _dot_nt(km_hi, qm) + _dot_nt(km_lo, qm)
        gate = jnp.where(past, gate, NEG_INF)
        rank = jnp.zeros(gate.shape, F32)
        for m in range(c):
            row = gate[m:m + 1, :]
            tie = jnp.where(m < blk, 1.0, 0.0)
            rank = rank + jnp.where(row > gate, 1.0, jnp.where(row == gate, tie, 0.0))
        drop = jnp.where((blk < nblk) & jnp.logical_not(past & (rank < MOBA_TOPK)), 1.0, 0.0)
        pad = [jnp.zeros((HEAD_DIM - km_rows, bq), F32)]
        rows = ([jnp.zeros((HEAD_DIM, bq), F32), drop] + pad) if spare[h] else ([drop] + pad + [jnp.zeros((HEAD_DIM, bq), F32)])
        dropped = _dot_nt(eye, jnp.concatenate(rows, axis=0).astype(BF16))
        return jnp.where(own[h], q, jnp.where(dropped > 0.5, NEG_INF, 0.0).astype(BF16))

    def score(c, h):
        k_aug = []
        for n in range(c + 1):
            kb = k_ref[n * bq:(n + 1) * bq, :]
            marker = jnp.where(lane == spare[h] + n, 1.0, 0.0) if n < c else jnp.zeros(lane.shape, F32)
            k_aug.append(jnp.where(own[h], kb, marker.astype(BF16)))
        return _dot_nt(jnp.concatenate(k_aug, axis=0), augmented_queries(c, h))

    def attend(c, h, s):
        parts = [s[n * bq:(n + 1) * bq] for n in range(c)]
        parts.append(jnp.where(causal, s[c * bq:], NEG_INF))
        top = parts[0].max(axis=0, keepdims=True)
        for part in parts[1:]:
            top = jnp.maximum(top, part.max(axis=0, keepdims=True))
        probs = jnp.concatenate([jnp.exp2((part - top).astype(BF16)) for part in parts], axis=0)
        acc = _dot(vt_ref[h * VT_ROWS:(h + 1) * VT_ROWS, 0:(c + 1) * bq], probs)
        return acc[:HEAD_DIM] / acc[HEAD_DIM:HEAD_DIM + 1]

    units = [(c, h) for c in range(nblk) for h in range(2)]
    ahead = 2
    pending = [score(*u) for u in units[:ahead]]
    outs = {}
    for i, (c, h) in enumerate(units):
        s = pending.pop(0)
        if i + ahead < len(units):
            pending.append(score(*units[i + ahead]))
        outs[h] = attend(c, h, s)
        if h == 1:
            o_ref[c * bq:(c + 1) * bq, :] = jnp.concatenate([outs[0], outs[1]], axis=0).T.astype(o_ref.dtype)


def moba_attention(qkv, batch, seq_len):
    n = qkv.shape[0]
    return pl.pallas_call(
        _moba_kernel,
        grid=(batch, N_PAIRS),
        in_specs=[pl.BlockSpec((seq_len, PAIR), lambda b, p: (b, p)),
                  pl.BlockSpec((seq_len, PAIR), lambda b, p: (b, N_PAIRS + p)),
                  pl.BlockSpec((seq_len, PAIR), lambda b, p: (b, 2 * N_PAIRS + p))],
        out_specs=pl.BlockSpec((seq_len, PAIR), lambda b, p: (b, p)),
        out_shape=jax.ShapeDtypeStruct((n, D_MODEL), BF16),
        scratch_shapes=[pltpu.VMEM((2 * VT_ROWS, seq_len), BF16)],
        compiler_params=_params("parallel", "parallel"),
        name="moba_attention",
    )(qkv, qkv, qkv)


def _mem_xattn_kernel(y_ref, wy_ref, x_ref, g_ref, wq_ref, kv_ref, wo_ref, gf_ref, rw_ref, rb_ref,
                      o_ref, h_out, meta_out, cnt_out):
    x = x_ref[...] + _dot(y_ref[...], wy_ref[...])
    xn = _rms(x, g_ref[...]).astype(BF16)
    q = (_dot(xn, wq_ref[...]) * (1.0 / math.sqrt(MEM_HEAD_DIM))).astype(BF16)
    outs = []
    for h in range(MEM_HEADS):
        sl = slice(h * MEM_HEAD_DIM, (h + 1) * MEM_HEAD_DIM)
        s = _dot_nt(q[:, sl], kv_ref[:, sl])
        s = s - jnp.max(s, axis=-1, keepdims=True)
        e = jnp.exp(s)
        pr = e / jnp.sum(e, axis=-1, keepdims=True)
        vh = kv_ref[:, D_MODEL + h * MEM_HEAD_DIM:D_MODEL + (h + 1) * MEM_HEAD_DIM]
        outs.append(_dot(pr.astype(BF16), vh).astype(BF16))
    o = jnp.concatenate(outs, axis=1)
    x2 = x + _dot(o, wo_ref[...])
    o_ref[...] = x2
    _route(x2, gf_ref, rw_ref, rb_ref, h_out, meta_out, cnt_out)


def mem_cross_attention(y, wy, x, gain, wq, kv, wo, seq_len, ffn_gain, w_grp, b_grp, w_exp, b_exp):
    n, d = x.shape
    tm = MOE_TILE
    m = kv.shape[0] // (n // seq_len)
    per_seq = seq_len // tm
    wt = jnp.zeros((LANES, d), F32).at[0:MOE_GROUPS].set(w_grp.T).at[8:8 + MOE_EXPERTS].set(w_exp.T)
    bt = jnp.zeros((LANES, 1), F32).at[0:MOE_GROUPS, 0].set(b_grp).at[8:8 + MOE_EXPERTS, 0].set(b_exp)
    const = lambda shape: pl.BlockSpec(shape, lambda i: (0, 0))
    return pl.pallas_call(
        _mem_xattn_kernel,
        grid=(n // tm,),
        in_specs=[pl.BlockSpec((tm, d), lambda i: (i, 0)),
                  const((d, d)),
                  pl.BlockSpec((tm, d), lambda i: (i, 0)),
                  const((1, d)),
                  const((d, d)),
                  pl.BlockSpec((m, 2 * d), lambda i: (i // per_seq, 0)),
                  const((d, d)),
                  const((1, d)),
                  const((LANES, d)),
                  const((LANES, 1))],
        out_specs=[pl.BlockSpec((tm, d), lambda i: (i, 0)),
                   pl.BlockSpec((tm, d), lambda i: (i, 0)),
                   pl.BlockSpec((8, tm), lambda i: (0, i)),
                   pl.BlockSpec((MOE_EXPERTS, LANES), lambda i: (i, 0))],
        out_shape=[jax.ShapeDtypeStruct((n, d), F32),
                   jax.ShapeDtypeStruct((n, d), BF16),
                   jax.ShapeDtypeStruct((8, n), F32),
                   jax.ShapeDtypeStruct((n // tm * MOE_EXPERTS, LANES), F32)],
        compiler_params=_params("parallel"),
        name="mem_cross_attention",
    )(y, wy.astype(BF16), x, gain.reshape(1, d), wq.astype(BF16), kv, wo.astype(BF16),
      ffn_gain.reshape(1, d), wt, bt)


def _route(x, g_ref, w_ref, b_ref, h_out, meta_out, cnt_out):
    tm = x.shape[0]
    h2 = _rms(x, g_ref[...])
    hi = h2.astype(BF16)
    h_out[...] = hi
    lo = (h2 - hi.astype(F32)).astype(BF16)
    w_hi, w_lo = _split(w_ref[...])
    lg = _dot_nt(w_hi, hi) + _dot_nt(w_hi, lo) + _dot_nt(w_lo, hi) + b_ref[...]
    row = _iota((8, tm), 0).astype(F32)

    def first_argmax(val, vmax):
        return jnp.min(jnp.where(val == vmax, row, 8.0), axis=0, keepdims=True)

    gl = jnp.where(row < MOE_GROUPS, lg[0:8], -jnp.inf)
    gmax = jnp.max(gl, axis=0, keepdims=True)
    p_g = 1.0 / jnp.sum(jnp.exp(gl - gmax), axis=0, keepdims=True)
    gidx = first_argmax(gl, gmax)
    el = jnp.zeros((8, tm), F32)
    for g in range(MOE_GROUPS):
        el = el + jnp.where(gidx == g, lg[8 + 8 * g:16 + 8 * g], 0.0)
    ee = jnp.exp(el - jnp.max(el, axis=0, keepdims=True))
    pe = ee / jnp.sum(ee, axis=0, keepdims=True)
    p1 = jnp.max(pe, axis=0, keepdims=True)
    i1 = first_argmax(pe, p1)
    pe2 = jnp.where(row == i1, -1.0, pe)
    p2 = jnp.max(pe2, axis=0, keepdims=True)
    i2 = first_argmax(pe2, p2)
    e1 = gidx * MOE_EPG + i1
    e2 = gidx * MOE_EPG + i2
    gate1 = p_g * p1 / (p1 + p2)
    gate2 = p_g * p2 / (p1 + p2)

    erow = _iota((MOE_EXPERTS, tm), 0).astype(F32)
    oh1 = erow == e1
    oh2 = erow == e2
    oh = jnp.where(oh1 | oh2, 1.0, 0.0)
    before = jnp.where(_iota((tm, tm), 0) < _iota((tm, tm), 1), 1.0, 0.0).astype(BF16)
    cnt_before = _dot(oh.astype(BF16), before)
    lr1 = jnp.sum(jnp.where(oh1, cnt_before, 0.0), axis=0, keepdims=True)
    lr2 = jnp.sum(jnp.where(oh2, cnt_before, 0.0), axis=0, keepdims=True)
    zero = jnp.zeros((1, tm), F32)
    meta_out[...] = jnp.concatenate([e1, e2, gate1, gate2, lr1, lr2, zero, zero], axis=0)
    cnt_out[...] = jnp.broadcast_to(jnp.sum(oh, axis=1, keepdims=True), (MOE_EXPERTS, LANES))


def _chunk_copies(tab_ref, t, make_copy, act):
    base = t * RUN_TAB
    for ci, ck in enumerate(RUN_CHUNKS):
        first = base + len(RUN_CHUNKS) + ci * 2 * MOE_EXPERTS

        def body(i, _, first=first, ck=ck):
            src = tab_ref[first + 2 * i]
            dst = tab_ref[first + 2 * i + 1]
            act(make_copy(pl.multiple_of(src, MOE_ALIGN), pl.multiple_of(dst, MOE_ALIGN), ck))
            return 0

        lax.fori_loop(0, tab_ref[base + ci], body, 0)


def _run_copies(tab_ref, first, count, chunks, make_copy, act):
    def body(e, _):
        base = first + e * 3
        length = tab_ref[base]
        src = tab_ref[base + 1]
        dst = tab_ref[base + 2]
        for ck in chunks:
            @pl.when((length & ck) != 0)
            def _():
                off = length & (-2 * ck)
                act(make_copy(pl.multiple_of(src + off, MOE_ALIGN), pl.multiple_of(dst + off, MOE_ALIGN), ck))
        return 0
    lax.fori_loop(0, count, body, 0)


def _start(cp):
    cp.start()


def _wait(cp):
    cp.wait()


def _dispatch_kernel(tab_ref, h_ref, pos_ref, rows_hbm, sorted_ref, zero_ref, sem, zsem):
    t = pl.program_id(0)
    nt = pl.num_programs(0)
    slot = t % 2
    r = _iota((MOE_SORTED, MOE_TILE), 0)
    perm = jnp.where((r == pos_ref[0:1, :]) | (r == pos_ref[1:2, :]), 1.0, 0.0).astype(BF16)
    sorted_ref[slot] = _dot(perm, h_ref[...]).astype(BF16)

    def run_copy(s):
        def make(src, dst, ck):
            return pltpu.make_async_copy(sorted_ref.at[s, pl.ds(src, ck)], rows_hbm.at[pl.ds(dst, ck)], sem.at[s])
        return make

    def zero_copy(src, dst, ck):
        del src
        return pltpu.make_async_copy(zero_ref.at[pl.ds(0, ck)], rows_hbm.at[pl.ds(dst, ck)], zsem)

    _chunk_copies(tab_ref, t, run_copy(slot), _start)

    @pl.when(t > 0)
    def _():
        _chunk_copies(tab_ref, t - 1, run_copy(1 - slot), _wait)

    @pl.when(t == nt - 1)
    def _():
        zero_ref[...] = jnp.zeros_like(zero_ref)
        pad = nt * RUN_TAB
        _run_copies(tab_ref, pad, MOE_EXPERTS, PAD_CHUNKS, zero_copy, _start)
        spare = pad + MOE_EXPERTS * 3
        n_spare = tab_ref[spare]

        def spare_copy(i):
            dst = pl.multiple_of(tab_ref[spare + 2] + i * PAD_ROWS, PAD_ROWS)
            return zero_copy(0, dst, PAD_ROWS)

        lax.fori_loop(0, n_spare, lambda i, c: (_start(spare_copy(i)), c)[1], 0)
        _chunk_copies(tab_ref, t, run_copy(slot), _wait)
        _run_copies(tab_ref, pad, MOE_EXPERTS, PAD_CHUNKS, zero_copy, _wait)
        lax.fori_loop(0, n_spare, lambda i, c: (_wait(spare_copy(i)), c)[1], 0)


def moe_dispatch(tab, h2, pos, n_rows):
    n, d = h2.shape
    tm = MOE_TILE
    return pl.pallas_call(
        _dispatch_kernel,
        grid_spec=pltpu.PrefetchScalarGridSpec(
            num_scalar_prefetch=1,
            grid=(n // tm,),
            in_specs=[pl.BlockSpec((tm, d), lambda i, tab: (i, 0)),
                      pl.BlockSpec((8, tm), lambda i, tab: (0, i))],
            out_specs=pl.BlockSpec(memory_space=pl.ANY),
            scratch_shapes=[pltpu.VMEM((2, MOE_SORTED, d), BF16), pltpu.VMEM((PAD_ROWS, d), BF16),
                            pltpu.SemaphoreType.DMA((2,)), pltpu.SemaphoreType.DMA],
        ),
        out_shape=jax.ShapeDtypeStruct((n_rows, d), BF16),
        compiler_params=_params("arbitrary"),
        name="moe_dispatch",
    )(tab, h2, pos)


def _expert_kernel(be_ref, nu_ref, x_ref, w1_ref, w3_ref, w2_ref, y_ref, w1b, w3b, w2b):
    b = pl.program_id(0)
    used = b < nu_ref[0]
    new_expert = (b == 0) | (be_ref[b] != be_ref[jnp.maximum(b - 1, 0)])

    @pl.when(used & new_expert)
    def _():
        w1b[...] = w1_ref[0, 0].astype(BF16)
        w3b[...] = w3_ref[0, 0].astype(BF16)
        w2b[...] = w2_ref[0, 0].astype(BF16)

    @pl.when(used)
    def _():
        xb = x_ref[...]
        a = _dot(xb, w1b[...])
        c = _dot(xb, w3b[...])
        hid = (a * jax.nn.sigmoid(a) * c).astype(BF16)
        y_ref[...] = _dot(hid, w2b[...]).astype(y_ref.dtype)

    @pl.when(jnp.logical_not(used))
    def _():
        y_ref[...] = jnp.zeros_like(y_ref)


def moe_experts(block_expert, n_used, rows, layer, w1, w3, w2):
    n_rows = rows.shape[0]
    nb = n_rows // MOE_ROWS
    d, ff = w1.shape[2], w1.shape[3]

    def xmap(b, be, nu):
        return (jnp.minimum(b, nu[0] - 1), 0)

    def wmap(b, be, nu):
        return (layer, be[jnp.minimum(b, nu[0] - 1)], 0, 0)

    return pl.pallas_call(
        _expert_kernel,
        grid_spec=pltpu.PrefetchScalarGridSpec(
            num_scalar_prefetch=2,
            grid=(nb,),
            in_specs=[pl.BlockSpec((MOE_ROWS, d), xmap),
                      pl.BlockSpec((1, 1, d, ff), wmap),
                      pl.BlockSpec((1, 1, d, ff), wmap),
                      pl.BlockSpec((1, 1, ff, d), wmap)],
            out_specs=pl.BlockSpec((MOE_ROWS, d), lambda b, be, nu: (b, 0)),
            scratch_shapes=[pltpu.VMEM((d, ff), BF16), pltpu.VMEM((d, ff), BF16), pltpu.VMEM((ff, d), BF16)],
        ),
        out_shape=jax.ShapeDtypeStruct((n_rows, d), BF16),
        compiler_params=_params("arbitrary"),
        name="moe_experts",
    )(block_expert, n_used, rows, w1, w3, w2)


def _combine_kernel(final_norm, tab_ref, y_hbm, pos_ref, gate_ref, x_ref, g_ref, o_ref, ys_ref, sem):
    t = pl.program_id(0)
    nt = pl.num_programs(0)
    slot = t % 2

    def run_copy(s):
        def make(src, dst, ck):
            return pltpu.make_async_copy(y_hbm.at[pl.ds(dst, ck)], ys_ref.at[s, pl.ds(src, ck)], sem.at[s])
        return make

    @pl.when(t == 0)
    def _():
        ys_ref[...] = jnp.zeros_like(ys_ref)
        _chunk_copies(tab_ref, 0, run_copy(0), _start)

    @pl.when(t + 1 < nt)
    def _():
        _chunk_copies(tab_ref, t + 1, run_copy(1 - slot), _start)

    r = _iota((MOE_SORTED, MOE_TILE), 0)
    hit1 = r == pos_ref[0:1, :]
    hit2 = r == pos_ref[1:2, :]
    wgt = jnp.where(hit1, gate_ref[2:3, :], jnp.where(hit2, gate_ref[3:4, :], 0.0))
    row_gate = jnp.sum(wgt, axis=1, keepdims=True)
    row_used = row_gate > 0.0
    perm = jnp.where(hit1 | hit2, 1.0, 0.0).astype(BF16)
    _chunk_copies(tab_ref, t, run_copy(slot), _wait)
    ys = jnp.where(row_used, ys_ref[slot].astype(F32) * row_gate, 0.0).astype(BF16)
    out = x_ref[...] + _dot_tn(perm, ys)
    if final_norm:
        out = _rms(out, g_ref[...])
    o_ref[...] = out


def moe_combine(tab, y, pos, meta, x, final_gain):
    n, d = x.shape
    tm = MOE_TILE
    final_norm = final_gain is not None
    gain = (final_gain if final_norm else jnp.ones((d,), F32)).reshape(1, d)
    return pl.pallas_call(
        functools.partial(_combine_kernel, final_norm),
        grid_spec=pltpu.PrefetchScalarGridSpec(
            num_scalar_prefetch=1,
            grid=(n // tm,),
            in_specs=[pl.BlockSpec(memory_space=pl.ANY),
                      pl.BlockSpec((8, tm), lambda i, tab: (0, i)),
                      pl.BlockSpec((8, tm), lambda i, tab: (0, i)),
                      pl.BlockSpec((tm, d), lambda i, tab: (i, 0)),
                      pl.BlockSpec((1, d), lambda i, tab: (0, 0))],
            out_specs=pl.BlockSpec((tm, d), lambda i, tab: (i, 0)),
            scratch_shapes=[pltpu.VMEM((2, MOE_SORTED, d), BF16), pltpu.SemaphoreType.DMA((2,))],
        ),
        out_shape=jax.ShapeDtypeStruct((n, d), F32),
        compiler_params=_params("arbitrary"),
        name="moe_combine",
    )(tab, y, pos, meta, x, gain)


def hierarchical_moe(x, h2, meta, cnt, layer, w1, w3, w2, final_gain):
    n, d = x.shape
    tm = MOE_TILE
    nt = n // tm

    cnt = cnt.reshape(nt, MOE_EXPERTS, LANES)[:, :, 0].astype(I32)
    run = (cnt + MOE_ALIGN - 1) // MOE_ALIGN * MOE_ALIGN
    src = jnp.cumsum(run, axis=1) - run
    before = jnp.cumsum(run, axis=0) - run
    total = jnp.sum(run, axis=0)
    padded = (total + MOE_ROWS - 1) // MOE_ROWS * MOE_ROWS
    pad_end = jnp.cumsum(padded)
    dst = (pad_end - padded)[None, :] + before
    max_rows = 2 * n + nt * MOE_EXPERTS * (MOE_ALIGN - 1) + MOE_EXPERTS * (MOE_ROWS - 1)
    nb = -(-max_rows // MOE_ROWS)
    sizes = jnp.array(RUN_CHUNKS, I32)[None, :, None]
    has = (run[:, None, :] & sizes) != 0
    above = run[:, None, :] & (-2 * sizes)
    place = jnp.where(has, jnp.cumsum(has.astype(I32), axis=2) - 1, -1)
    front = place[..., None] == jnp.arange(MOE_EXPERTS, dtype=I32)
    c_src = jnp.sum(jnp.where(front, (src[:, None, :] + above)[..., None], 0), axis=2)
    c_dst = jnp.sum(jnp.where(front, (dst[:, None, :] + above)[..., None], 0), axis=2)
    chunks = jnp.stack([c_src, c_dst], axis=-1).reshape(nt, -1)
    runs = jnp.concatenate([jnp.sum(has.astype(I32), axis=2), chunks], axis=1).reshape(-1)
    zero = jnp.zeros((MOE_EXPERTS,), I32)
    pads = jnp.stack([padded - total, zero, pad_end - padded + total], axis=-1)
    spare = jnp.stack([(nb * MOE_ROWS - pad_end[-1]) // PAD_ROWS, zero[0], pad_end[-1]])
    tab = jnp.concatenate([runs, pads.reshape(-1), spare]).astype(I32)
    n_used = (pad_end[-1] // MOE_ROWS).astype(I32).reshape(1)
    block_start = jnp.arange(nb, dtype=I32) * MOE_ROWS
    block_expert = jnp.minimum(
        jnp.sum((pad_end[None, :] <= block_start[:, None]).astype(I32), axis=1), MOE_EXPERTS - 1).astype(I32)
    e = meta[0:2].astype(I32).reshape(2, nt, tm, 1)
    hit = e == jnp.arange(MOE_EXPERTS, dtype=I32)
    pos = jnp.sum(jnp.where(hit, src[None, :, None, :], 0), axis=-1).reshape(2, n) + meta[4:6].astype(I32)
    pos8 = jnp.zeros((8, n), I32).at[0:2].set(pos)

    rows = moe_dispatch(tab, h2, pos8, nb * MOE_ROWS)
    y = moe_experts(block_expert, n_used, rows, layer, w1, w3, w2)
    return moe_combine(tab, y, pos8, meta, x, final_gain)


def kernel(x, mem, positions, ln_mix, ln_mem, ln_memkv, ln_ffn, rw_mu, rw_w0, rw_w1, rw_w2, rw_a0, rw_a1, rw_a2, rw_g1, rw_g2, rw_kk, rw_ka, rw_rk, rw_wrkv, rw_lnx_g, rw_lnx_b, rw_wo, mb_wqkv, mb_wo, mx_wq, mx_wkv, mx_wo, moe_wg, moe_bg, moe_we, moe_be, moe_w1, moe_w3, moe_w2, ln_f):
    B, T, C = x.shape
    n = B * T
    depth = ln_mix.shape[0]
    xf = x.reshape(n, C)
    memf = mem.reshape(-1, C)
    for i in range(depth):
        j = i // 2
        if i % 2 == 0:
            r, k, v, kk, akk, lw, g = rwkv_mix(xf, T, ln_mix[i], rw_mu[j], rw_w0[j], rw_w1[j], rw_w2[j],
                                               rw_a0[j], rw_a1[j], rw_a2[j], rw_g1[j], rw_g2[j],
                                               rw_kk[j], rw_ka[j], rw_wrkv[j])
            y = rwkv_recurrence(r, k, v, kk, akk, lw, g, rw_lnx_g[j], rw_lnx_b[j], rw_rk[j], B, T)
            w_mix_o = rw_wo[j]
        else:
            qkv = qkv_rope(xf, ln_mix[i], positions, mb_wqkv[j])
            y = moba_attention(qkv, B, T)
            w_mix_o = mb_wo[j]
        kv = norm_linear(memf, ln_memkv[i], mx_wkv[i].astype(BF16))
        xf, h2, meta, cnt = mem_cross_attention(y, w_mix_o, xf, ln_mem[i], mx_wq[i], kv, mx_wo[i], T,
                                                ln_ffn[i], moe_wg[i], moe_bg[i], moe_we[i], moe_be[i])
        xf = hierarchical_moe(xf, h2, meta, cnt, i, moe_w1, moe_w3, moe_w2, ln_f if i == depth - 1 else None)
    return xf.reshape(B, T, C)
```

```python
import functools
import math

import jax
import jax.numpy as jnp
from jax import lax
from jax.experimental import pallas as pl
from jax.experimental.pallas import tpu as pltpu

F32 = jnp.float32
BF16 = jnp.bfloat16
I32 = jnp.int32
U32 = jnp.uint32

D_MODEL = 1024
HEAD_DIM = 64
PAIR = 2 * HEAD_DIM
N_PAIRS = D_MODEL // PAIR
RWKV_GN_EPS = 64e-5
RWKV_CHUNK = 64
RWKV_CHUNKS_PER_STEP = 4
MOBA_BLOCK = 256
MOBA_TOPK = 3
VT_ROWS = HEAD_DIM + 16
ROPE_THETA = 10000.0
MEM_HEADS = 4
MEM_HEAD_DIM = D_MODEL // MEM_HEADS
MOE_GROUPS = 4
MOE_EPG = 8
MOE_EXPERTS = MOE_GROUPS * MOE_EPG
MOE_FF = D_MODEL // 2
RMS_EPS = 1e-6
NEG_INF = -1e30

LANES = 128
ROW_ALIGN = 8
MOE_TILE = 512
MOE_ROWS = 512
MOE_ALIGN = 16
MOE_SORTED = 2 * MOE_TILE + MOE_EXPERTS * (MOE_ALIGN - 1) + (-(2 * MOE_TILE + MOE_EXPERTS * (MOE_ALIGN - 1))) % MOE_ALIGN
RUN_CHUNKS = tuple(MOE_ALIGN << s for s in range((MOE_TILE // MOE_ALIGN).bit_length() - 1, -1, -1))
RUN_TAB = len(RUN_CHUNKS) * (1 + 2 * MOE_EXPERTS)
PAD_ROWS = MOE_ROWS // 2
PAD_CHUNKS = tuple(MOE_ALIGN << s for s in range((PAD_ROWS // MOE_ALIGN).bit_length() - 1, -1, -1))
VMEM_LIMIT = 56 * 2 ** 20


def _params(*sem):
    return pltpu.CompilerParams(dimension_semantics=sem, vmem_limit_bytes=VMEM_LIMIT)


def _iota(shape, dim):
    return lax.broadcasted_iota(I32, shape, dim)


def _dot(a, b):
    return jnp.dot(a, b, preferred_element_type=F32)


def _dot_nt(a, b):
    return lax.dot_general(a, b, (((1,), (1,)), ((), ())), preferred_element_type=F32)


def _dot_tn(a, b):
    return lax.dot_general(a, b, (((0,), (0,)), ((), ())), preferred_element_type=F32)


def _split(x):
    hi = x.astype(BF16)
    lo = (x - hi.astype(F32)).astype(BF16)
    return hi, lo


def _rms(x, g):
    return x * lax.rsqrt(jnp.mean(x * x, axis=-1, keepdims=True) + RMS_EPS) * g


def _norm_linear_kernel(x_ref, g_ref, w_ref, o_ref):
    xn = _rms(x_ref[...], g_ref[...]).astype(BF16)
    o_ref[...] = _dot(xn, w_ref[...]).astype(o_ref.dtype)


def norm_linear(x, g, w, tm=256):
    n, k = x.shape
    dout = w.shape[1]
    return pl.pallas_call(
        _norm_linear_kernel,
        grid=(n // tm,),
        in_specs=[pl.BlockSpec((tm, k), lambda i: (i, 0)),
                  pl.BlockSpec((1, k), lambda i: (0, 0)),
                  pl.BlockSpec((k, dout), lambda i: (0, 0))],
        out_specs=pl.BlockSpec((tm, dout), lambda i: (i, 0)),
        out_shape=jax.ShapeDtypeStruct((n, dout), BF16),
        compiler_params=_params("parallel"),
        name="norm_linear",
    )(x, g.reshape(1, k), w)


def _head_sum(x_sq):
    r = _iota((PAIR, PAIR), 0) // HEAD_DIM
    c = _iota((PAIR, PAIR), 1) // HEAD_DIM
    bd = jnp.where(r == c, 1.0, 0.0).astype(BF16)
    hi, lo = _split(x_sq)
    return _dot(hi, bd) + _dot(lo, bd)


def _rwkv_mix_kernel(seq_len, x_ref, xp_ref, g_ref, mu_ref, vec_ref, wr_ref, wk_ref, wv_ref,
                     l1_ref, w2_ref, a2_ref, g2_ref,
                     r_out, k_out, v_out, kk_out, akk_out, lw_out, g_out):
    i = pl.program_id(0)
    tm = x_ref.shape[0]
    gain = g_ref[...]
    h = _rms(x_ref[...], gain)
    hp = _rms(xp_ref[...], gain)[ROW_ALIGN - 1:ROW_ALIGN, :]
    hp = jnp.where((i * tm) % seq_len == 0, 0.0, hp)
    hs = pltpu.roll(h, 1, 0)
    hs = jnp.where(_iota((tm, 1), 0) == 0, hp, hs)
    hb = h.astype(BF16)
    dxb = (hs - h).astype(BF16)
    mub = mu_ref[...].astype(BF16)

    def mix(s):
        return hb + dxb * mub[s:s + 1, :]

    vec = vec_ref[...]
    w0, a0, k_k, k_a = vec[0:1], vec[1:2], vec[2:3], vec[3:4]
    l1 = l1_ref[...]
    r = _dot(mix(0), wr_ref[...])
    k = _dot(mix(2), wk_ref[...])
    v = _dot(mix(3), wv_ref[...])
    tw = jnp.tanh(_dot(mix(1), l1[:, 0:64])).astype(BF16)
    lw = -math.exp(-0.5) * jax.nn.sigmoid(w0 + _dot(tw, w2_ref[...]))
    ta = _dot(mix(4), l1[:, 64:128]).astype(BF16)
    a = jax.nn.sigmoid(a0 + _dot(ta, a2_ref[...]))
    tg = jax.nn.sigmoid(_dot(mix(5), l1[:, 128:256])).astype(BF16)
    g = _dot(tg, g2_ref[...])

    kk = k * k_k
    for p in range(N_PAIRS):
        sl = slice(p * PAIR, (p + 1) * PAIR)
        kkp = kk[:, sl]
        kkn = kkp * lax.rsqrt(jnp.maximum(_head_sum(kkp * kkp), 1e-24))
        kk_out[:, sl] = kkn.astype(kk_out.dtype)
        akk_out[:, sl] = (kkn * a[:, sl]).astype(akk_out.dtype)
    r_out[...] = r.astype(r_out.dtype)
    k_out[...] = (k * (1.0 + (a - 1.0) * k_a)).astype(k_out.dtype)
    v_out[...] = v.astype(v_out.dtype)
    lw_out[...] = lw
    g_out[...] = g.astype(g_out.dtype)


def rwkv_mix(x, seq_len, gain, mu, w0, w1, w2, a0, a1, a2, g1, g2, k_k, k_a, w_rkv, tm=512):
    n, d = x.shape
    mu8 = jnp.zeros((8, d), F32).at[:6].set(mu)
    vec = jnp.zeros((8, d), F32).at[0].set(w0).at[1].set(a0).at[2].set(k_k).at[3].set(k_a)
    l1 = jnp.concatenate([w1, a1, g1], axis=1).astype(BF16)
    wb = w_rkv.astype(BF16)
    row = pl.BlockSpec((tm, d), lambda i: (i, 0))
    full = lambda a: pl.BlockSpec(a.shape, lambda i: (0,) * a.ndim)
    args = (x, x, gain.reshape(1, d), mu8, vec, wb[0], wb[1], wb[2], l1,
            w2.astype(BF16), a2.astype(BF16), g2.astype(BF16))
    in_specs = [row, pl.BlockSpec((ROW_ALIGN, d), lambda i: (jnp.maximum(i * (tm // ROW_ALIGN) - 1, 0), 0))]
    in_specs += [full(a) for a in args[2:]]
    outs = [jax.ShapeDtypeStruct((n, d), BF16)] * 5 + [jax.ShapeDtypeStruct((n, d), F32),
                                                       jax.ShapeDtypeStruct((n, d), BF16)]
    return pl.pallas_call(
        functools.partial(_rwkv_mix_kernel, seq_len),
        grid=(n // tm,),
        in_specs=in_specs,
        out_specs=[row] * 7,
        out_shape=outs,
        compiler_params=_params("parallel"),
        name="rwkv_mix",
    )(*args)


def _rwkv_rec_kernel(r_ref, k_ref, v_ref, kk_ref, akk_ref, lw_ref, g_ref, vec_ref, y_ref, s_ref):
    c = pl.program_id(1)
    L = RWKV_CHUNK

    @pl.when(c == 0)
    def _():
        s_ref[...] = jnp.zeros_like(s_ref)

    lane = _iota((1, PAIR), 1)
    m0 = lane < HEAD_DIM
    ri = _iota((2 * L, 2 * L), 0)
    ci = _iota((2 * L, 2 * L), 1)
    same = (ri // L) == (ci // L)
    strict = same & (ci < ri)
    incl = same & (ci <= ri)
    eye = jnp.where(ri == ci, 1.0, 0.0)
    tri = jnp.where(_iota((L, L), 1) <= _iota((L, L), 0), 1.0, 0.0).astype(BF16)
    rb = _iota((PAIR, PAIR), 0) // HEAD_DIM
    cb = _iota((PAIR, PAIR), 1) // HEAD_DIM
    bd = rb == cb

    def stack(x):
        return jnp.concatenate([jnp.where(m0, x, 0.0), jnp.where(m0, 0.0, x)], axis=0)

    def fold(x):
        return x[:L] + x[L:]

    def head_mean(x):
        s0 = jnp.sum(jnp.where(m0, x, 0.0), axis=-1, keepdims=True)
        s1 = jnp.sum(jnp.where(m0, 0.0, x), axis=-1, keepdims=True)
        return jnp.where(m0, s0, s1) * (1.0 / HEAD_DIM)

    vec = vec_ref[...]
    n_sub = r_ref.shape[0] // L
    pairs = range(n_sub * N_PAIRS)
    rws = [slice((q // N_PAIRS) * L, (q // N_PAIRS + 1) * L) for q in pairs]
    sls = [slice((q % N_PAIRS) * PAIR, (q % N_PAIRS + 1) * PAIR) for q in pairs]
    lw = [lw_ref[rws[p], sls[p]] for p in pairs]
    cum = []
    for p in pairs:
        lw_hi, lw_lo = _split(lw[p])
        cum.append(_dot(tri, lw_hi) + _dot(tri, lw_lo))
    dec_all, x_kap, x_r, a_t, k_t, sc = [], [], [], [], [], []
    for p in pairs:
        rw, sl = rws[p], sls[p]
        dec = jnp.exp(cum[p])
        inv = jnp.exp(-cum[p])
        dec_prev = jnp.exp(cum[p] - lw[p])
        dec_all.append(dec[L - 1:L, :])
        x_kap.append(stack(kk_ref[rw, sl].astype(F32) * dec_prev).astype(BF16))
        x_r.append(stack(r_ref[rw, sl].astype(F32) * dec))
        a_t.append((akk_ref[rw, sl].astype(F32) * inv).astype(BF16))
        k_t.append((k_ref[rw, sl].astype(F32) * inv).astype(BF16))
        xs = jnp.concatenate([x_kap[p], x_r[p].astype(BF16)], axis=0)
        ys = jnp.concatenate([a_t[p], a_t[p], k_t[p], k_t[p]], axis=0)
        sc.append(_dot_nt(xs, ys))
    mp = [jnp.where(strict, -sc[p][:2 * L, :2 * L], 0.0) for p in pairs]
    m_kk = [jnp.where(strict, sc[p][:2 * L, 2 * L:], 0.0).astype(BF16) for p in pairs]
    s_ra = [jnp.where(incl, sc[p][2 * L:, :2 * L], 0.0).astype(BF16) for p in pairs]
    s_rk = [jnp.where(incl, sc[p][2 * L:, 2 * L:], 0.0).astype(BF16) for p in pairs]
    v_st = [stack(v_ref[rws[p], sls[p]].astype(F32)).astype(BF16) for p in pairs]
    bo = [_dot(jnp.concatenate([m_kk[p], s_rk[p]], axis=0), v_st[p]) for p in pairs]
    b1 = [bo[p][:2 * L] for p in pairs]
    o1 = [bo[p][2 * L:] for p in pairs]

    t_inv = [eye + mp[p] for p in pairs]
    mp = [_dot(mp[p].astype(BF16), mp[p].astype(BF16)) for p in pairs]
    steps = int(math.log2(L)) - 1
    for j in range(steps):
        mpb = [mp[p].astype(BF16) for p in pairs]
        if j + 1 < steps:
            both = [_dot(mpb[p], jnp.concatenate([mpb[p], t_inv[p].astype(BF16)], axis=1)) for p in pairs]
            mp = [both[p][:, :2 * L] for p in pairs]
            t_inv = [t_inv[p] + both[p][:, 2 * L:] for p in pairs]
        else:
            t_inv = [t_inv[p] + _dot(mpb[p], t_inv[p].astype(BF16)) for p in pairs]

    wu = [_dot(t_inv[p].astype(BF16), jnp.concatenate([x_kap[p], b1[p].astype(BF16)], axis=1))
          for p in pairs]
    corr = [_dot(s_ra[p], wu[p].astype(BF16)) for p in pairs]
    g_m, h0t, r_hat, o0 = [], [], [], []
    for p in pairs:
        r_hat.append(fold(x_r[p] - corr[p][:, :PAIR]).astype(BF16))
        o0.append(fold(o1[p] - corr[p][:, PAIR:]))
        w_f = fold(wu[p][:, :PAIR]).astype(BF16)
        u0_f = fold(wu[p][:, PAIR:]).astype(BF16)
        g_m.append(jnp.where(bd, _dot_tn(a_t[p], w_f), 0.0).astype(BF16))
        vu = jnp.concatenate([v_ref[rws[p], sls[p]], -u0_f], axis=0)
        h0t.append(jnp.where(bd, _dot_tn(vu, jnp.concatenate([k_t[p], a_t[p]], axis=0)), 0.0))
    o = []
    for p in pairs:
        hp = p % N_PAIRS
        s = s_ref[hp]
        sb = s.astype(BF16)
        o.append(_dot_nt(r_hat[p], sb) + o0[p])
        s_ref[hp] = (s - _dot_nt(sb, g_m[p]) + h0t[p]) * dec_all[p]
    for p in pairs:
        rw, sl = rws[p], sls[p]
        mean = head_mean(o[p])
        cen = o[p] - mean
        var = head_mean(cen * cen)
        gn = cen * lax.rsqrt(var + RWKV_GN_EPS) * vec[0:1, sl] + vec[1:2, sl]
        r = r_ref[rw, sl].astype(F32)
        k = k_ref[rw, sl].astype(F32)
        bonus = head_mean(r * k * vec[2:3, sl]) * HEAD_DIM * v_ref[rw, sl].astype(F32)
        y_ref[rw, sl] = ((gn + bonus) * g_ref[rw, sl].astype(F32)).astype(y_ref.dtype)


def rwkv_recurrence(r, k, v, kk, akk, lw, g, lnx_g, lnx_b, r_k, batch, seq_len):
    n, d = r.shape
    rows = RWKV_CHUNK * RWKV_CHUNKS_PER_STEP
    nc = seq_len // rows
    vec = jnp.zeros((8, d), F32).at[0].set(lnx_g).at[1].set(lnx_b).at[2].set(r_k.reshape(d))
    blk = pl.BlockSpec((rows, d), lambda b, c: (b * nc + c, 0))
    return pl.pallas_call(
        _rwkv_rec_kernel,
        grid=(batch, nc),
        in_specs=[blk] * 7 + [pl.BlockSpec((8, d), lambda b, c: (0, 0))],
        out_specs=blk,
        out_shape=jax.ShapeDtypeStruct((n, d), BF16),
        scratch_shapes=[pltpu.VMEM((N_PAIRS, PAIR, PAIR), F32)],
        compiler_params=_params("parallel", "arbitrary"),
        name="rwkv_recurrence",
    )(r, k, v, kk, akk, lw, g, vec)


def _qkv_rope_kernel(x_ref, g_ref, pos_ref, inv_ref, w_ref, o_ref):
    d = x_ref.shape[1]
    xn = _rms(x_ref[...], g_ref[...]).astype(BF16)
    y = [_dot(xn, w_ref[:, part * d:(part + 1) * d]) for part in range(3)]
    ang = pos_ref[...].astype(F32) * inv_ref[...]
    first = (_iota((1, PAIR), 1) % HEAD_DIM) < HEAD_DIM // 2
    sn = jnp.sin(ang)
    cos = jnp.cos(ang)
    sin_lo = jnp.where(first, -sn, 0.0)
    sin_hi = jnp.where(first, 0.0, sn)
    for part in range(2):
        for p in range(N_PAIRS):
            t = y[part][:, p * PAIR:(p + 1) * PAIR]
            rot = (t * cos + pltpu.roll(t, PAIR - HEAD_DIM // 2, 1) * sin_lo
                   + pltpu.roll(t, HEAD_DIM // 2, 1) * sin_hi)
            o_ref[:, part * d + p * PAIR:part * d + (p + 1) * PAIR] = rot.astype(o_ref.dtype)
    o_ref[:, 2 * d:] = y[2].astype(o_ref.dtype)


def qkv_rope(x, gain, positions, w_qkv, tm=512):
    n, d = x.shape
    half = HEAD_DIM // 2
    inv = ROPE_THETA ** (-jnp.arange(half, dtype=F32) * 2.0 / HEAD_DIM)
    inv128 = jnp.tile(inv, PAIR // half).reshape(1, PAIR)
    col_scale = jnp.where(jnp.arange(3 * d) < d, math.log2(math.e) / math.sqrt(HEAD_DIM), 1.0).astype(F32)
    return pl.pallas_call(
        _qkv_rope_kernel,
        grid=(n // tm,),
        in_specs=[pl.BlockSpec((tm, d), lambda i: (i, 0)),
                  pl.BlockSpec((1, d), lambda i: (0, 0)),
                  pl.BlockSpec((tm, 1), lambda i: (i, 0)),
                  pl.BlockSpec((1, PAIR), lambda i: (0, 0)),
                  pl.BlockSpec((d, 3 * d), lambda i: (0, 0))],
        out_specs=pl.BlockSpec((tm, 3 * d), lambda i: (i, 0)),
        out_shape=jax.ShapeDtypeStruct((n, 3 * d), BF16),
        compiler_params=_params("parallel"),
        name="qkv_rope",
    )(x, gain.reshape(1, d), positions.reshape(n, 1), inv128, (w_qkv * col_scale).astype(BF16))


def _moba_kernel(q_ref, k_ref, v_ref, o_ref, vt_ref, ka_ref):
    nblk = k_ref.shape[0] // MOBA_BLOCK
    bq = MOBA_BLOCK
    km_rows = 16
    lane = _iota((1, PAIR), 1)
    own = (lane < HEAD_DIM, lane >= HEAD_DIM)
    spare = (HEAD_DIM, 0)

    means = []
    ones_row = jnp.where(_iota((VT_ROWS - HEAD_DIM, bq), 0) == 0, 1.0, 0.0).astype(BF16)
    for n in range(nblk):
        cols = slice(n * bq, (n + 1) * bq)
        kb = k_ref[cols, :]
        means.append(jnp.mean(kb.astype(F32), axis=0, keepdims=True))
        vt = v_ref[cols, :].astype(F32).T.astype(BF16)
        for h in range(2):
            vt_ref[h * VT_ROWS:h * VT_ROWS + HEAD_DIM, cols] = vt[h * HEAD_DIM:(h + 1) * HEAD_DIM]
            vt_ref[h * VT_ROWS + HEAD_DIM:(h + 1) * VT_ROWS, cols] = ones_row
            marker = jnp.where(lane == spare[h] + n, 1.0, 0.0).astype(BF16)
            ka_ref[h, cols, :] = jnp.where(own[h], kb, marker)
    km_hi, km_lo = _split(jnp.concatenate(means + [jnp.zeros((km_rows - nblk, PAIR), F32)], axis=0))

    blk = _iota((km_rows, bq), 0)
    causal = _iota((bq, bq), 0) <= _iota((bq, bq), 1)
    eye = jnp.where(_iota((bq, bq), 0) == _iota((bq, bq), 1), 1.0, 0.0).astype(BF16)

    def augmented_queries(c, h):
        q = q_ref[c * bq:(c + 1) * bq, :]
        qm = jnp.where(own[h], q, jnp.zeros_like(q))
        past = blk < c
        gate = _dot_nt(km_hi, qm) + _dot_nt(km_lo, qm)
        gate = jnp.where(past, gate, NEG_INF)
        rank = jnp.zeros(gate.shape, F32)
        for m in range(c):
            row = gate[m:m + 1, :]
            tie = jnp.where(m < blk, 1.0, 0.0)
            rank = rank + jnp.where(row > gate, 1.0, jnp.where(row == gate, tie, 0.0))
        drop = jnp.where(past & (rank >= MOBA_TOPK), 1.0, 0.0)
        pad = [jnp.zeros((HEAD_DIM - km_rows, bq), F32)]
        rows = ([jnp.zeros((HEAD_DIM, bq), F32), drop] + pad) if spare[h] else ([drop] + pad + [jnp.zeros((HEAD_DIM, bq), F32)])
        dropped = _dot_nt(eye, jnp.concatenate(rows, axis=0).astype(BF16))
        return jnp.where(own[h], q, jnp.where(dropped > 0.5, NEG_INF, 0.0).astype(BF16))

    def score(c, h):
        return _dot_nt(ka_ref[h, 0:(c + 1) * bq, :], augmented_queries(c, h))

    def attend(c, h, s):
        parts = [s[n * bq:(n + 1) * bq] for n in range(c)]
        parts.append(jnp.where(causal, s[c * bq:], NEG_INF))
        top = parts[0].max(axis=0, keepdims=True)
        for part in parts[1:]:
            top = jnp.maximum(top, part.max(axis=0, keepdims=True))
        probs = jnp.concatenate([jnp.exp2((part - top).astype(BF16)) for part in parts], axis=0)
        acc = _dot(vt_ref[h * VT_ROWS:(h + 1) * VT_ROWS, 0:(c + 1) * bq], probs)
        return acc[:HEAD_DIM] * (1.0 / acc[HEAD_DIM:HEAD_DIM + 1])

    units = [(c, h) for c in range(nblk) for h in range(2)]
    ahead = 2
    pending = [score(*u) for u in units[:ahead]]
    outs = {}
    for i, (c, h) in enumerate(units):
        s = pending.pop(0)
        if i + ahead < len(units):
            pending.append(score(*units[i + ahead]))
        outs[h] = attend(c, h, s)
        if h == 1:
            o_ref[c * bq:(c + 1) * bq, :] = jnp.concatenate([outs[0], outs[1]], axis=0).T.astype(o_ref.dtype)


def moba_attention(qkv, batch, seq_len):
    n = qkv.shape[0]
    return pl.pallas_call(
        _moba_kernel,
        grid=(batch, N_PAIRS),
        in_specs=[pl.BlockSpec((seq_len, PAIR), lambda b, p: (b, p)),
                  pl.BlockSpec((seq_len, PAIR), lambda b, p: (b, N_PAIRS + p)),
                  pl.BlockSpec((seq_len, PAIR), lambda b, p: (b, 2 * N_PAIRS + p))],
        out_specs=pl.BlockSpec((seq_len, PAIR), lambda b, p: (b, p)),
        out_shape=jax.ShapeDtypeStruct((n, D_MODEL), BF16),
        scratch_shapes=[pltpu.VMEM((2 * VT_ROWS, seq_len), BF16),
                        pltpu.VMEM((2, seq_len, PAIR), BF16)],
        compiler_params=_params("parallel", "parallel"),
        name="moba_attention",
    )(qkv, qkv, qkv)


def _mem_xattn_kernel(y_ref, wy_ref, x_ref, g_ref, wq_ref, kv_ref, wo_ref, gf_ref, rw_ref, rb_ref,
                      o_ref, h_out, meta_out, cnt_out):
    x = x_ref[...] + _dot(y_ref[...], wy_ref[...])
    xn = _rms(x, g_ref[...]).astype(BF16)
    q = (_dot(xn, wq_ref[...]) * (1.0 / math.sqrt(MEM_HEAD_DIM))).astype(BF16)
    outs = []
    for h in range(MEM_HEADS):
        sl = slice(h * MEM_HEAD_DIM, (h + 1) * MEM_HEAD_DIM)
        s = _dot_nt(q[:, sl], kv_ref[:, sl])
        s = s - jnp.max(s, axis=-1, keepdims=True)
        e = jnp.exp(s)
        pr = e * (1.0 / jnp.sum(e, axis=-1, keepdims=True))
        vh = kv_ref[:, D_MODEL + h * MEM_HEAD_DIM:D_MODEL + (h + 1) * MEM_HEAD_DIM]
        outs.append(_dot(pr.astype(BF16), vh).astype(BF16))
    o = jnp.concatenate(outs, axis=1)
    x2 = x + _dot(o, wo_ref[...])
    o_ref[...] = x2
    _route(x2, gf_ref, rw_ref, rb_ref, h_out, meta_out, cnt_out)


def mem_cross_attention(y, wy, x, gain, wq, kv, wo, seq_len, ffn_gain, w_grp, b_grp, w_exp, b_exp):
    n, d = x.shape
    tm = MOE_TILE
    m = kv.shape[0] // (n // seq_len)
    per_seq = seq_len // tm
    wt = jnp.zeros((LANES, d), F32).at[0:MOE_GROUPS].set(w_grp.T).at[8:8 + MOE_EXPERTS].set(w_exp.T)
    bt = jnp.zeros((LANES, 1), F32).at[0:MOE_GROUPS, 0].set(b_grp).at[8:8 + MOE_EXPERTS, 0].set(b_exp)
    const = lambda shape: pl.BlockSpec(shape, lambda i: (0, 0))
    return pl.pallas_call(
        _mem_xattn_kernel,
        grid=(n // tm,),
        in_specs=[pl.BlockSpec((tm, d), lambda i: (i, 0)),
                  const((d, d)),
                  pl.BlockSpec((tm, d), lambda i: (i, 0)),
                  const((1, d)),
                  const((d, d)),
                  pl.BlockSpec((m, 2 * d), lambda i: (i // per_seq, 0)),
                  const((d, d)),
                  const((1, d)),
                  const((LANES, d)),
                  const((LANES, 1))],
        out_specs=[pl.BlockSpec((tm, d), lambda i: (i, 0)),
                   pl.BlockSpec((tm, d), lambda i: (i, 0)),
                   pl.BlockSpec((8, tm), lambda i: (0, i)),
                   pl.BlockSpec((MOE_EXPERTS, LANES), lambda i: (i, 0))],
        out_shape=[jax.ShapeDtypeStruct((n, d), F32),
                   jax.ShapeDtypeStruct((n, d), BF16),
                   jax.ShapeDtypeStruct((8, n), F32),
                   jax.ShapeDtypeStruct((n // tm * MOE_EXPERTS, LANES), F32)],
        compiler_params=_params("parallel"),
        name="mem_cross_attention",
    )(y, wy.astype(BF16), x, gain.reshape(1, d), wq.astype(BF16), kv, wo.astype(BF16),
      ffn_gain.reshape(1, d), wt, bt)


def _route(x, g_ref, w_ref, b_ref, h_out, meta_out, cnt_out):
    tm = x.shape[0]
    h2 = _rms(x, g_ref[...])
    hi = h2.astype(BF16)
    h_out[...] = hi
    lo = (h2 - hi.astype(F32)).astype(BF16)
    w_hi, w_lo = _split(w_ref[...])
    lg = _dot_nt(w_hi, hi) + _dot_nt(w_hi, lo) + _dot_nt(w_lo, hi) + b_ref[...]
    row = _iota((8, tm), 0).astype(F32)

    def first_argmax(val, vmax):
        return jnp.min(jnp.where(val == vmax, row, 8.0), axis=0, keepdims=True)

    gl = jnp.where(row < MOE_GROUPS, lg[0:8], -jnp.inf)
    gmax = jnp.max(gl, axis=0, keepdims=True)
    p_g = 1.0 / jnp.sum(jnp.exp(gl - gmax), axis=0, keepdims=True)
    gidx = first_argmax(gl, gmax)
    el = jnp.zeros((8, tm), F32)
    for g in range(MOE_GROUPS):
        el = el + jnp.where(gidx == g, lg[8 + 8 * g:16 + 8 * g], 0.0)
    ee = jnp.exp(el - jnp.max(el, axis=0, keepdims=True))
    pe = ee / jnp.sum(ee, axis=0, keepdims=True)
    p1 = jnp.max(pe, axis=0, keepdims=True)
    i1 = first_argmax(pe, p1)
    pe2 = jnp.where(row == i1, -1.0, pe)
    p2 = jnp.max(pe2, axis=0, keepdims=True)
    i2 = first_argmax(pe2, p2)
    e1 = gidx * MOE_EPG + i1
    e2 = gidx * MOE_EPG + i2
    gate1 = p_g * p1 / (p1 + p2)
    gate2 = p_g * p2 / (p1 + p2)

    erow = _iota((MOE_EXPERTS, tm), 0).astype(F32)
    oh1 = erow == e1
    oh2 = erow == e2
    oh = jnp.where(oh1 | oh2, 1.0, 0.0)
    before = jnp.where(_iota((tm, tm), 0) < _iota((tm, tm), 1), 1.0, 0.0).astype(BF16)
    cnt_before = _dot(oh.astype(BF16), before)
    lr1 = jnp.sum(jnp.where(oh1, cnt_before, 0.0), axis=0, keepdims=True)
    lr2 = jnp.sum(jnp.where(oh2, cnt_before, 0.0), axis=0, keepdims=True)
    zero = jnp.zeros((1, tm), F32)
    meta_out[...] = jnp.concatenate([e1, e2, gate1, gate2, lr1, lr2, zero, zero], axis=0)
    cnt_out[...] = jnp.broadcast_to(jnp.sum(oh, axis=1, keepdims=True), (MOE_EXPERTS, LANES))


def _chunk_copies(tab_ref, t, make_copy, act):
    base = t * RUN_TAB
    for ci, ck in enumerate(RUN_CHUNKS):
        first = base + len(RUN_CHUNKS) + ci * 2 * MOE_EXPERTS

        def body(i, _, first=first, ck=ck):
            src = tab_ref[first + 2 * i]
            dst = tab_ref[first + 2 * i + 1]
            act(make_copy(pl.multiple_of(src, MOE_ALIGN), pl.multiple_of(dst, MOE_ALIGN), ck))
            return 0

        lax.fori_loop(0, tab_ref[base + ci], body, 0)


def _run_copies(tab_ref, first, count, chunks, make_copy, act):
    def body(e, _):
        base = first + e * 3
        length = tab_ref[base]
        src = tab_ref[base + 1]
        dst = tab_ref[base + 2]
        for ck in chunks:
            @pl.when((length & ck) != 0)
            def _():
                off = length & (-2 * ck)
                act(make_copy(pl.multiple_of(src + off, MOE_ALIGN), pl.multiple_of(dst + off, MOE_ALIGN), ck))
        return 0
    lax.fori_loop(0, count, body, 0)


def _start(cp):
    cp.start()


def _wait(cp):
    cp.wait()


def _dispatch_kernel(tab_ref, h_ref, pos_ref, rows_hbm, sorted_ref, zero_ref, sem, zsem):
    t = pl.program_id(0)
    nt = pl.num_programs(0)
    slot = t % 2
    r = _iota((MOE_SORTED, MOE_TILE), 0)
    perm = jnp.where((r == pos_ref[0:1, :]) | (r == pos_ref[1:2, :]), 1.0, 0.0).astype(BF16)
    sorted_ref[slot] = _dot(perm, h_ref[...]).astype(BF16)

    def run_copy(s):
        def make(src, dst, ck):
            return pltpu.make_async_copy(sorted_ref.at[s, pl.ds(src, ck)], rows_hbm.at[pl.ds(dst, ck)], sem.at[s])
        return make

    def zero_copy(src, dst, ck):
        del src
        return pltpu.make_async_copy(zero_ref.at[pl.ds(0, ck)], rows_hbm.at[pl.ds(dst, ck)], zsem)

    _chunk_copies(tab_ref, t, run_copy(slot), _start)

    @pl.when(t > 0)
    def _():
        _chunk_copies(tab_ref, t - 1, run_copy(1 - slot), _wait)

    @pl.when(t == nt - 1)
    def _():
        zero_ref[...] = jnp.zeros_like(zero_ref)
        pad = nt * RUN_TAB
        _run_copies(tab_ref, pad, MOE_EXPERTS, PAD_CHUNKS, zero_copy, _start)
        spare = pad + MOE_EXPERTS * 3
        n_spare = tab_ref[spare]

        def spare_copy(i):
            dst = pl.multiple_of(tab_ref[spare + 2] + i * PAD_ROWS, PAD_ROWS)
            return zero_copy(0, dst, PAD_ROWS)

        lax.fori_loop(0, n_spare, lambda i, c: (_start(spare_copy(i)), c)[1], 0)
        _chunk_copies(tab_ref, t, run_copy(slot), _wait)
        _run_copies(tab_ref, pad, MOE_EXPERTS, PAD_CHUNKS, zero_copy, _wait)
        lax.fori_loop(0, n_spare, lambda i, c: (_wait(spare_copy(i)), c)[1], 0)


def moe_dispatch(tab, h2, pos, n_rows):
    n, d = h2.shape
    tm = MOE_TILE
    return pl.pallas_call(
        _dispatch_kernel,
        grid_spec=pltpu.PrefetchScalarGridSpec(
            num_scalar_prefetch=1,
            grid=(n // tm,),
            in_specs=[pl.BlockSpec((tm, d), lambda i, tab: (i, 0)),
                      pl.BlockSpec((8, tm), lambda i, tab: (0, i))],
            out_specs=pl.BlockSpec(memory_space=pl.ANY),
            scratch_shapes=[pltpu.VMEM((2, MOE_SORTED, d), BF16), pltpu.VMEM((PAD_ROWS, d), BF16),
                            pltpu.SemaphoreType.DMA((2,)), pltpu.SemaphoreType.DMA],
        ),
        out_shape=jax.ShapeDtypeStruct((n_rows, d), BF16),
        compiler_params=_params("arbitrary"),
        name="moe_dispatch",
    )(tab, h2, pos)


def _expert_kernel(be_ref, nu_ref, x_ref, w1_ref, w3_ref, w2_ref, y_ref, w1b, w3b, w2b):
    b = pl.program_id(0)
    used = b < nu_ref[0]
    new_expert = (b == 0) | (be_ref[b] != be_ref[jnp.maximum(b - 1, 0)])

    @pl.when(used & new_expert)
    def _():
        w1b[...] = w1_ref[0, 0].astype(BF16)
        w3b[...] = w3_ref[0, 0].astype(BF16)
        w2b[...] = w2_ref[0, 0].astype(BF16)

    @pl.when(used)
    def _():
        xb = x_ref[...]
        a = _dot(xb, w1b[...])
        c = _dot(xb, w3b[...])
        hid = (a * jax.nn.sigmoid(a) * c).astype(BF16)
        y_ref[...] = _dot(hid, w2b[...]).astype(y_ref.dtype)

    @pl.when(jnp.logical_not(used))
    def _():
        y_ref[...] = jnp.zeros_like(y_ref)


def moe_experts(block_expert, n_used, rows, layer, w1, w3, w2):
    n_rows = rows.shape[0]
    nb = n_rows // MOE_ROWS
    d, ff = w1.shape[2], w1.shape[3]

    def xmap(b, be, nu):
        return (jnp.minimum(b, nu[0] - 1), 0)

    def wmap(b, be, nu):
        return (layer, be[jnp.minimum(b, nu[0] - 1)], 0, 0)

    return pl.pallas_call(
        _expert_kernel,
        grid_spec=pltpu.PrefetchScalarGridSpec(
            num_scalar_prefetch=2,
            grid=(nb,),
            in_specs=[pl.BlockSpec((MOE_ROWS, d), xmap),
                      pl.BlockSpec((1, 1, d, ff), wmap),
                      pl.BlockSpec((1, 1, d, ff), wmap),
                      pl.BlockSpec((1, 1, ff, d), wmap)],
            out_specs=pl.BlockSpec((MOE_ROWS, d), lambda b, be, nu: (b, 0)),
            scratch_shapes=[pltpu.VMEM((d, ff), BF16), pltpu.VMEM((d, ff), BF16), pltpu.VMEM((ff, d), BF16)],
        ),
        out_shape=jax.ShapeDtypeStruct((n_rows, d), BF16),
        compiler_params=_params("arbitrary"),
        name="moe_experts",
    )(block_expert, n_used, rows, w1, w3, w2)


def _combine_kernel(final_norm, tab_ref, y_hbm, pos_ref, gate_ref, x_ref, g_ref, o_ref, ys_ref, sem):
    t = pl.program_id(0)
    nt = pl.num_programs(0)
    slot = t % 2

    def run_copy(s):
        def make(src, dst, ck):
            return pltpu.make_async_copy(y_hbm.at[pl.ds(dst, ck)], ys_ref.at[s, pl.ds(src, ck)], sem.at[s])
        return make

    @pl.when(t == 0)
    def _():
        ys_ref[...] = jnp.zeros_like(ys_ref)
        _chunk_copies(tab_ref, 0, run_copy(0), _start)

    @pl.when(t + 1 < nt)
    def _():
        _chunk_copies(tab_ref, t + 1, run_copy(1 - slot), _start)

    r = _iota((MOE_SORTED, MOE_TILE), 0)
    hit1 = r == pos_ref[0:1, :]
    hit2 = r == pos_ref[1:2, :]
    wgt = jnp.where(hit1, gate_ref[2:3, :], jnp.where(hit2, gate_ref[3:4, :], 0.0)).astype(BF16)
    _chunk_copies(tab_ref, t, run_copy(slot), _wait)
    out = x_ref[...] + _dot_tn(wgt, ys_ref[slot])
    if final_norm:
        out = _rms(out, g_ref[...])
    o_ref[...] = out


def moe_combine(tab, y, pos, meta, x, final_gain):
    n, d = x.shape
    tm = MOE_TILE
    final_norm = final_gain is not None
    gain = (final_gain if final_norm else jnp.ones((d,), F32)).reshape(1, d)
    return pl.pallas_call(
        functools.partial(_combine_kernel, final_norm),
        grid_spec=pltpu.PrefetchScalarGridSpec(
            num_scalar_prefetch=1,
            grid=(n // tm,),
            in_specs=[pl.BlockSpec(memory_space=pl.ANY),
                      pl.BlockSpec((8, tm), lambda i, tab: (0, i)),
                      pl.BlockSpec((8, tm), lambda i, tab: (0, i)),
                      pl.BlockSpec((tm, d), lambda i, tab: (i, 0)),
                      pl.BlockSpec((1, d), lambda i, tab: (0, 0))],
            out_specs=pl.BlockSpec((tm, d), lambda i, tab: (i, 0)),
            scratch_shapes=[pltpu.VMEM((2, MOE_SORTED, d), BF16), pltpu.SemaphoreType.DMA((2,))],
        ),
        out_shape=jax.ShapeDtypeStruct((n, d), F32),
        compiler_params=_params("arbitrary"),
        name="moe_combine",
    )(tab, y, pos, meta, x, gain)


def hierarchical_moe(x, h2, meta, cnt, layer, w1, w3, w2, final_gain):
    n, d = x.shape
    tm = MOE_TILE
    nt = n // tm

    cnt = cnt.reshape(nt, MOE_EXPERTS, LANES)[:, :, 0].astype(I32)
    run = (cnt + MOE_ALIGN - 1) // MOE_ALIGN * MOE_ALIGN
    src = jnp.cumsum(run, axis=1) - run
    before = jnp.cumsum(run, axis=0) - run
    total = jnp.sum(run, axis=0)
    padded = (total + MOE_ROWS - 1) // MOE_ROWS * MOE_ROWS
    pad_end = jnp.cumsum(padded)
    dst = (pad_end - padded)[None, :] + before
    max_rows = 2 * n + nt * MOE_EXPERTS * (MOE_ALIGN - 1) + MOE_EXPERTS * (MOE_ROWS - 1)
    nb = -(-max_rows // MOE_ROWS)
    sizes = jnp.array(RUN_CHUNKS, I32)[None, :, None]
    has = (run[:, None, :] & sizes) != 0
    above = run[:, None, :] & (-2 * sizes)
    place = jnp.where(has, jnp.cumsum(has.astype(I32), axis=2) - 1, -1)
    front = place[..., None] == jnp.arange(MOE_EXPERTS, dtype=I32)
    c_src = jnp.sum(jnp.where(front, (src[:, None, :] + above)[..., None], 0), axis=2)
    c_dst = jnp.sum(jnp.where(front, (dst[:, None, :] + above)[..., None], 0), axis=2)
    chunks = jnp.stack([c_src, c_dst], axis=-1).reshape(nt, -1)
    runs = jnp.concatenate([jnp.sum(has.astype(I32), axis=2), chunks], axis=1).reshape(-1)
    zero = jnp.zeros((MOE_EXPERTS,), I32)
    pads = jnp.stack([padded - total, zero, pad_end - padded + total], axis=-1)
    spare = jnp.stack([(nb * MOE_ROWS - pad_end[-1]) // PAD_ROWS, zero[0], pad_end[-1]])
    tab = jnp.concatenate([runs, pads.reshape(-1), spare]).astype(I32)
    n_used = (pad_end[-1] // MOE_ROWS).astype(I32).reshape(1)
    block_start = jnp.arange(nb, dtype=I32) * MOE_ROWS
    block_expert = jnp.minimum(
        jnp.sum((pad_end[None, :] <= block_start[:, None]).astype(I32), axis=1), MOE_EXPERTS - 1).astype(I32)
    e = meta[0:2].astype(I32).reshape(2, nt, tm, 1)
    hit = e == jnp.arange(MOE_EXPERTS, dtype=I32)
    pos = jnp.sum(jnp.where(hit, src[None, :, None, :], 0), axis=-1).reshape(2, n) + meta[4:6].astype(I32)
    pos8 = jnp.zeros((8, n), I32).at[0:2].set(pos)

    rows = moe_dispatch(tab, h2, pos8, nb * MOE_ROWS)
    y = moe_experts(block_expert, n_used, rows, layer, w1, w3, w2)
    return moe_combine(tab, y, pos8, meta, x, final_gain)


def kernel(x, mem, positions, ln_mix, ln_mem, ln_memkv, ln_ffn, rw_mu, rw_w0, rw_w1, rw_w2, rw_a0, rw_a1, rw_a2, rw_g1, rw_g2, rw_kk, rw_ka, rw_rk, rw_wrkv, rw_lnx_g, rw_lnx_b, rw_wo, mb_wqkv, mb_wo, mx_wq, mx_wkv, mx_wo, moe_wg, moe_bg, moe_we, moe_be, moe_w1, moe_w3, moe_w2, ln_f):
    B, T, C = x.shape
    n = B * T
    depth = ln_mix.shape[0]
    xf = x.reshape(n, C)
    memf = mem.reshape(-1, C)
    for i in range(depth):
        j = i // 2
        if i % 2 == 0:
            r, k, v, kk, akk, lw, g = rwkv_mix(xf, T, ln_mix[i], rw_mu[j], rw_w0[j], rw_w1[j], rw_w2[j],
                                               rw_a0[j], rw_a1[j], rw_a2[j], rw_g1[j], rw_g2[j],
                                               rw_kk[j], rw_ka[j], rw_wrkv[j])
            y = rwkv_recurrence(r, k, v, kk, akk, lw, g, rw_lnx_g[j], rw_lnx_b[j], rw_rk[j], B, T)
            w_mix_o = rw_wo[j]
        else:
            qkv = qkv_rope(xf, ln_mix[i], positions, mb_wqkv[j])
            y = moba_attention(qkv, B, T)
            w_mix_o = mb_wo[j]
        kv = norm_linear(memf, ln_memkv[i], mx_wkv[i].astype(BF16))
        xf, h2, meta, cnt = mem_cross_attention(y, w_mix_o, xf, ln_mem[i], mx_wq[i], kv, mx_wo[i], T,
                                                ln_ffn[i], moe_wg[i], moe_bg[i], moe_we[i], moe_be[i])
        xf = hierarchical_moe(xf, h2, meta, cnt, i, moe_w1, moe_w3, moe_w2, ln_f if i == depth - 1 else None)
    return xf.reshape(B, T, C)
```

```python
import functools
import math

import jax
import jax.numpy as jnp
from jax import lax
from jax.experimental import pallas as pl
from jax.experimental.pallas import tpu as pltpu

F32 = jnp.float32
BF16 = jnp.bfloat16
I32 = jnp.int32

D_MODEL = 1024
HEAD_DIM = 64
PAIR = 2 * HEAD_DIM
N_PAIRS = D_MODEL // PAIR
RWKV_GN_EPS = 64e-5
RWKV_CHUNK = 64
RWKV_CHUNKS_PER_STEP = 4
MOBA_BLOCK = 256
MOBA_TOPK = 3
VT_ROWS = HEAD_DIM + 16
ROPE_THETA = 10000.0
MEM_HEADS = 4
MEM_HEAD_DIM = D_MODEL // MEM_HEADS
MOE_GROUPS = 4
MOE_EPG = 8
MOE_EXPERTS = MOE_GROUPS * MOE_EPG
RMS_EPS = 1e-6
NEG_INF = -1e30

LANES = 128
ROW_ALIGN = 8
MOE_TILE = 512
MOE_ROWS = 512
MOE_ALIGN = 16
MOE_SORTED = -(-(2 * MOE_TILE + MOE_EXPERTS * (MOE_ALIGN - 1)) // MOE_ALIGN) * MOE_ALIGN
RUN_CHUNKS = tuple(MOE_ALIGN << s for s in range((MOE_TILE // MOE_ALIGN).bit_length() - 1, -1, -1))
RUN_TAB = len(RUN_CHUNKS) * (1 + 2 * MOE_EXPERTS)
PAD_ROWS = MOE_ROWS // 2
PAD_CHUNKS = tuple(MOE_ALIGN << s for s in range((PAD_ROWS // MOE_ALIGN).bit_length() - 1, -1, -1))
VMEM_LIMIT = 56 * 2 ** 20


def _params(*sem):
    return pltpu.CompilerParams(dimension_semantics=sem, vmem_limit_bytes=VMEM_LIMIT)


def _iota(shape, dim):
    return lax.broadcasted_iota(I32, shape, dim)


def _dot(a, b):
    return jnp.dot(a, b, preferred_element_type=F32)


def _dot_nt(a, b):
    return lax.dot_general(a, b, (((1,), (1,)), ((), ())), preferred_element_type=F32)


def _dot_tn(a, b):
    return lax.dot_general(a, b, (((0,), (0,)), ((), ())), preferred_element_type=F32)


def _split(x):
    hi = x.astype(BF16)
    lo = (x - hi.astype(F32)).astype(BF16)
    return hi, lo


def _rms(x, g):
    return x * lax.rsqrt(jnp.mean(x * x, axis=-1, keepdims=True) + RMS_EPS) * g


def _norm_linear_kernel(x_ref, g_ref, w_ref, o_ref):
    xn = _rms(x_ref[...], g_ref[...]).astype(BF16)
    o_ref[...] = _dot(xn, w_ref[...]).astype(o_ref.dtype)


def norm_linear(x, g, w, tm=256):
    n, k = x.shape
    dout = w.shape[1]
    return pl.pallas_call(
        _norm_linear_kernel,
        grid=(n // tm,),
        in_specs=[pl.BlockSpec((tm, k), lambda i: (i, 0)),
                  pl.BlockSpec((1, k), lambda i: (0, 0)),
                  pl.BlockSpec((k, dout), lambda i: (0, 0))],
        out_specs=pl.BlockSpec((tm, dout), lambda i: (i, 0)),
        out_shape=jax.ShapeDtypeStruct((n, dout), BF16),
        compiler_params=_params("parallel"),
        name="norm_linear",
    )(x, g.reshape(1, k), w)


def _head_sum(x_sq):
    r = _iota((PAIR, PAIR), 0) // HEAD_DIM
    c = _iota((PAIR, PAIR), 1) // HEAD_DIM
    bd = jnp.where(r == c, 1.0, 0.0).astype(BF16)
    hi, lo = _split(x_sq)
    return _dot(hi, bd) + _dot(lo, bd)


def _rwkv_mix_kernel(seq_len, x_ref, xp_ref, g_ref, mu_ref, vec_ref, wr_ref, wk_ref, wv_ref,
                     l1_ref, w2_ref, a2_ref, g2_ref,
                     r_out, k_out, v_out, kk_out, akk_out, lw_out, g_out):
    i = pl.program_id(0)
    tm = x_ref.shape[0]
    gain = g_ref[...]
    h = _rms(x_ref[...], gain)
    hp = _rms(xp_ref[...], gain)[ROW_ALIGN - 1:ROW_ALIGN, :]
    hp = jnp.where((i * tm) % seq_len == 0, 0.0, hp)
    hs = pltpu.roll(h, 1, 0)
    hs = jnp.where(_iota((tm, 1), 0) == 0, hp, hs)
    hb = h.astype(BF16)
    dxb = (hs - h).astype(BF16)
    mub = mu_ref[...].astype(BF16)

    def mix(s):
        return hb + dxb * mub[s:s + 1, :]

    vec = vec_ref[...]
    w0, a0, k_k, k_a = vec[0:1], vec[1:2], vec[2:3], vec[3:4]
    l1 = l1_ref[...]
    r = _dot(mix(0), wr_ref[...])
    k = _dot(mix(2), wk_ref[...])
    v = _dot(mix(3), wv_ref[...])
    tw = jnp.tanh(_dot(mix(1), l1[:, 0:64])).astype(BF16)
    lw = -math.exp(-0.5) * jax.nn.sigmoid(w0 + _dot(tw, w2_ref[...]))
    ta = _dot(mix(4), l1[:, 64:128]).astype(BF16)
    a = jax.nn.sigmoid(a0 + _dot(ta, a2_ref[...]))
    tg = jax.nn.sigmoid(_dot(mix(5), l1[:, 128:256])).astype(BF16)
    g = _dot(tg, g2_ref[...])

    kk = k * k_k
    for p in range(N_PAIRS):
        sl = slice(p * PAIR, (p + 1) * PAIR)
        kkp = kk[:, sl]
        kkn = kkp * lax.rsqrt(jnp.maximum(_head_sum(kkp * kkp), 1e-24))
        kk_out[:, sl] = kkn.astype(kk_out.dtype)
        akk_out[:, sl] = (kkn * a[:, sl]).astype(akk_out.dtype)
    r_out[...] = r.astype(r_out.dtype)
    k_out[...] = (k * (1.0 + (a - 1.0) * k_a)).astype(k_out.dtype)
    v_out[...] = v.astype(v_out.dtype)
    lw_out[...] = lw
    g_out[...] = g.astype(g_out.dtype)


def rwkv_mix(x, seq_len, gain, mu, w0, w1, w2, a0, a1, a2, g1, g2, k_k, k_a, w_rkv, tm=512):
    n, d = x.shape
    mu8 = jnp.zeros((8, d), F32).at[:6].set(mu)
    vec = jnp.zeros((8, d), F32).at[0].set(w0).at[1].set(a0).at[2].set(k_k).at[3].set(k_a)
    l1 = jnp.concatenate([w1, a1, g1], axis=1).astype(BF16)
    wb = w_rkv.astype(BF16)
    row = pl.BlockSpec((tm, d), lambda i: (i, 0))
    full = lambda a: pl.BlockSpec(a.shape, lambda i: (0,) * a.ndim)
    args = (x, x, gain.reshape(1, d), mu8, vec, wb[0], wb[1], wb[2], l1,
            w2.astype(BF16), a2.astype(BF16), g2.astype(BF16))
    in_specs = [row, pl.BlockSpec((ROW_ALIGN, d), lambda i: (jnp.maximum(i * (tm // ROW_ALIGN) - 1, 0), 0))]
    in_specs += [full(a) for a in args[2:]]
    outs = [jax.ShapeDtypeStruct((n, d), BF16)] * 5 + [jax.ShapeDtypeStruct((n, d), F32),
                                                       jax.ShapeDtypeStruct((n, d), BF16)]
    return pl.pallas_call(
        functools.partial(_rwkv_mix_kernel, seq_len),
        grid=(n // tm,),
        in_specs=in_specs,
        out_specs=[row] * 7,
        out_shape=outs,
        compiler_params=_params("parallel"),
        name="rwkv_mix",
    )(*args)


def _rwkv_rec_kernel(r_ref, k_ref, v_ref, kk_ref, akk_ref, lw_ref, g_ref, vec_ref, y_ref, s_ref):
    c = pl.program_id(1)
    L = RWKV_CHUNK

    @pl.when(c == 0)
    def _():
        s_ref[...] = jnp.zeros_like(s_ref)

    lane = _iota((1, PAIR), 1)
    m0 = lane < HEAD_DIM
    ri = _iota((2 * L, 2 * L), 0)
    ci = _iota((2 * L, 2 * L), 1)
    same = (ri // L) == (ci // L)
    strict = same & (ci < ri)
    incl = same & (ci <= ri)
    eye = jnp.where(ri == ci, 1.0, 0.0)
    tri = jnp.where(_iota((L, L), 1) <= _iota((L, L), 0), 1.0, 0.0).astype(BF16)
    rb = _iota((PAIR, PAIR), 0) // HEAD_DIM
    cb = _iota((PAIR, PAIR), 1) // HEAD_DIM
    bd = rb == cb

    def stack(x):
        return jnp.concatenate([jnp.where(m0, x, 0.0), jnp.where(m0, 0.0, x)], axis=0)

    def fold(x):
        return x[:L] + x[L:]

    def head_mean(x):
        s0 = jnp.sum(jnp.where(m0, x, 0.0), axis=-1, keepdims=True)
        s1 = jnp.sum(jnp.where(m0, 0.0, x), axis=-1, keepdims=True)
        return jnp.where(m0, s0, s1) * (1.0 / HEAD_DIM)

    vec = vec_ref[...]
    n_sub = r_ref.shape[0] // L
    pairs = range(n_sub * N_PAIRS)
    rws = [slice((q // N_PAIRS) * L, (q // N_PAIRS + 1) * L) for q in pairs]
    sls = [slice((q % N_PAIRS) * PAIR, (q % N_PAIRS + 1) * PAIR) for q in pairs]
    lw = [lw_ref[rws[p], sls[p]] for p in pairs]
    cum = []
    for p in pairs:
        lw_hi, lw_lo = _split(lw[p])
        cum.append(_dot(tri, lw_hi) + _dot(tri, lw_lo))
    dec_all, x_kap, x_r, a_t, k_t, sc = [], [], [], [], [], []
    for p in pairs:
        rw, sl = rws[p], sls[p]
        dec = jnp.exp(cum[p])
        inv = jnp.exp(-cum[p])
        dec_prev = jnp.exp(cum[p] - lw[p])
        dec_all.append(dec[L - 1:L, :])
        x_kap.append(stack(kk_ref[rw, sl].astype(F32) * dec_prev).astype(BF16))
        x_r.append(stack(r_ref[rw, sl].astype(F32) * dec))
        a_t.append((akk_ref[rw, sl].astype(F32) * inv).astype(BF16))
        k_t.append((k_ref[rw, sl].astype(F32) * inv).astype(BF16))
        xs = jnp.concatenate([x_kap[p], x_r[p].astype(BF16)], axis=0)
        ys = jnp.concatenate([a_t[p], k_t[p]], axis=0)
        sc.append(_dot_nt(xs, ys))

    def both_heads(first, second, keep):
        upper_left = keep & (ri < L)
        lower_right = keep & (ri >= L)
        return jnp.where(upper_left, first, jnp.where(lower_right, second, 0.0))

    mp, m_kk = [], []
    for p in pairs:
        top = sc[p][:2 * L]
        top_r = pltpu.roll(top, L, 1)
        mp.append(-both_heads(top, top_r, strict))
        m_kk.append(both_heads(top_r, top, strict).astype(BF16))
    s_ra, s_rk = [], []
    for p in pairs:
        bot = sc[p][2 * L:]
        bot_r = pltpu.roll(bot, L, 1)
        s_ra.append(both_heads(bot, bot_r, incl).astype(BF16))
        s_rk.append(both_heads(bot_r, bot, incl).astype(BF16))
    v_st = [stack(v_ref[rws[p], sls[p]].astype(F32)).astype(BF16) for p in pairs]
    bo = [_dot(jnp.concatenate([m_kk[p], s_rk[p]], axis=0), v_st[p]) for p in pairs]
    b1 = [bo[p][:2 * L].astype(BF16) for p in pairs]
    o1 = [bo[p][2 * L:] for p in pairs]

    t_inv = [eye + mp[p] for p in pairs]
    mpb = [mp[p].astype(BF16) for p in pairs]
    mpb = [_dot(mpb[p], mpb[p]).astype(BF16) for p in pairs]
    steps = int(math.log2(L)) - 1
    for j in range(steps):
        if j + 1 < steps:
            both = [_dot(mpb[p], jnp.concatenate([mpb[p], t_inv[p].astype(BF16)], axis=1)) for p in pairs]
            mpb = [both[p][:, :2 * L].astype(BF16) for p in pairs]
            t_inv = [t_inv[p] + both[p][:, 2 * L:] for p in pairs]
        else:
            t_inv = [t_inv[p] + _dot(mpb[p], t_inv[p].astype(BF16)) for p in pairs]

    wu = [_dot(t_inv[p].astype(BF16), jnp.concatenate([x_kap[p], b1[p]], axis=1)).astype(BF16)
          for p in pairs]
    corr = [_dot(s_ra[p], wu[p]) for p in pairs]
    g_m, h0t, r_hat, o0 = [], [], [], []
    for p in pairs:
        r_hat.append(fold(x_r[p] - corr[p][:, :PAIR]).astype(BF16))
        o0.append(fold(o1[p] - corr[p][:, PAIR:]))
        w_f = fold(wu[p][:, :PAIR])
        u0_f = fold(wu[p][:, PAIR:])
        g_m.append(jnp.where(bd, _dot_tn(a_t[p], w_f), 0.0).astype(BF16))
        vu = jnp.concatenate([v_ref[rws[p], sls[p]], -u0_f], axis=0)
        h0t.append(jnp.where(bd, _dot_tn(vu, jnp.concatenate([k_t[p], a_t[p]], axis=0)), 0.0))
    o = []
    for p in pairs:
        hp = p % N_PAIRS
        s = s_ref[hp]
        sb = s.astype(BF16)
        o.append(_dot_nt(r_hat[p], sb) + o0[p])
        s_ref[hp] = (s - _dot_nt(sb, g_m[p]) + h0t[p]) * dec_all[p]
    for p in pairs:
        rw, sl = rws[p], sls[p]
        mean = head_mean(o[p])
        cen = o[p] - mean
        var = head_mean(cen * cen)
        gn = cen * lax.rsqrt(var + RWKV_GN_EPS) * vec[0:1, sl] + vec[1:2, sl]
        r = r_ref[rw, sl].astype(F32)
        k = k_ref[rw, sl].astype(F32)
        bonus = head_mean(r * k * vec[2:3, sl]) * HEAD_DIM * v_ref[rw, sl].astype(F32)
        y_ref[rw, sl] = ((gn + bonus) * g_ref[rw, sl].astype(F32)).astype(y_ref.dtype)


def rwkv_recurrence(r, k, v, kk, akk, lw, g, lnx_g, lnx_b, r_k, batch, seq_len):
    n, d = r.shape
    rows = RWKV_CHUNK * RWKV_CHUNKS_PER_STEP
    nc = seq_len // rows
    vec = jnp.zeros((8, d), F32).at[0].set(lnx_g).at[1].set(lnx_b).at[2].set(r_k.reshape(d))
    blk = pl.BlockSpec((rows, d), lambda b, c: (b * nc + c, 0))
    return pl.pallas_call(
        _rwkv_rec_kernel,
        grid=(batch, nc),
        in_specs=[blk] * 7 + [pl.BlockSpec((8, d), lambda b, c: (0, 0))],
        out_specs=blk,
        out_shape=jax.ShapeDtypeStruct((n, d), BF16),
        scratch_shapes=[pltpu.VMEM((N_PAIRS, PAIR, PAIR), F32)],
        compiler_params=_params("parallel", "arbitrary"),
        name="rwkv_recurrence",
    )(r, k, v, kk, akk, lw, g, vec)


def _qkv_rope_kernel(x_ref, g_ref, pos_ref, inv_ref, w_ref, o_ref):
    d = x_ref.shape[1]
    xn = _rms(x_ref[...], g_ref[...]).astype(BF16)
    y = [_dot(xn, w_ref[:, part * d:(part + 1) * d]) for part in range(3)]
    ang = pos_ref[...].astype(F32) * inv_ref[...]
    first = (_iota((1, PAIR), 1) % HEAD_DIM) < HEAD_DIM // 2
    sn = jnp.sin(ang)
    cos = jnp.cos(ang)
    sin_lo = jnp.where(first, -sn, 0.0)
    sin_hi = jnp.where(first, 0.0, sn)
    for part in range(2):
        for p in range(N_PAIRS):
            t = y[part][:, p * PAIR:(p + 1) * PAIR]
            rot = (t * cos + pltpu.roll(t, PAIR - HEAD_DIM // 2, 1) * sin_lo
                   + pltpu.roll(t, HEAD_DIM // 2, 1) * sin_hi)
            o_ref[:, part * d + p * PAIR:part * d + (p + 1) * PAIR] = rot.astype(o_ref.dtype)
    o_ref[:, 2 * d:] = y[2].astype(o_ref.dtype)


def qkv_rope(x, gain, positions, w_qkv, tm=512):
    n, d = x.shape
    half = HEAD_DIM // 2
    inv = ROPE_THETA ** (-jnp.arange(half, dtype=F32) * 2.0 / HEAD_DIM)
    inv128 = jnp.tile(inv, PAIR // half).reshape(1, PAIR)
    col_scale = jnp.where(jnp.arange(3 * d) < d, math.log2(math.e) / math.sqrt(HEAD_DIM), 1.0).astype(F32)
    return pl.pallas_call(
        _qkv_rope_kernel,
        grid=(n // tm,),
        in_specs=[pl.BlockSpec((tm, d), lambda i: (i, 0)),
                  pl.BlockSpec((1, d), lambda i: (0, 0)),
                  pl.BlockSpec((tm, 1), lambda i: (i, 0)),
                  pl.BlockSpec((1, PAIR), lambda i: (0, 0)),
                  pl.BlockSpec((d, 3 * d), lambda i: (0, 0))],
        out_specs=pl.BlockSpec((tm, 3 * d), lambda i: (i, 0)),
        out_shape=jax.ShapeDtypeStruct((n, 3 * d), BF16),
        compiler_params=_params("parallel"),
        name="qkv_rope",
    )(x, gain.reshape(1, d), positions.reshape(n, 1), inv128, (w_qkv * col_scale).astype(BF16))


def _moba_kernel(q_ref, k_ref, v_ref, o_ref, vt_ref, ka_ref):
    nblk = k_ref.shape[0] // MOBA_BLOCK
    bq = MOBA_BLOCK
    km_rows = 16
    lane = _iota((1, PAIR), 1)
    own = (lane < HEAD_DIM, lane >= HEAD_DIM)
    spare = (HEAD_DIM, 0)

    means = []
    ones_row = jnp.where(_iota((VT_ROWS - HEAD_DIM, bq), 0) == 0, 1.0, 0.0).astype(BF16)
    for n in range(nblk):
        cols = slice(n * bq, (n + 1) * bq)
        kb = k_ref[cols, :]
        means.append(jnp.mean(kb.astype(F32), axis=0, keepdims=True))
        vt = v_ref[cols, :].astype(F32).T.astype(BF16)
        for h in range(2):
            vt_ref[h * VT_ROWS:h * VT_ROWS + HEAD_DIM, cols] = vt[h * HEAD_DIM:(h + 1) * HEAD_DIM]
            vt_ref[h * VT_ROWS + HEAD_DIM:(h + 1) * VT_ROWS, cols] = ones_row
            marker = jnp.where(lane == spare[h] + n, 1.0, 0.0).astype(BF16)
            ka_ref[h, cols, :] = jnp.where(own[h], kb, marker)
    km_hi, km_lo = _split(jnp.concatenate(means + [jnp.zeros((km_rows - nblk, PAIR), F32)], axis=0))

    blk = _iota((km_rows, bq), 0)
    causal = _iota((bq, bq), 0) <= _iota((bq, bq), 1)
    eye = jnp.where(_iota((bq, bq), 0) == _iota((bq, bq), 1), 1.0, 0.0).astype(BF16)

    def augmented_queries(c, h):
        q = q_ref[c * bq:(c + 1) * bq, :]
        qm = jnp.where(own[h], q, jnp.zeros_like(q))
        past = blk < c
        gate = _dot_nt(km_hi, qm) + _dot_nt(km_lo, qm)
        gate = jnp.where(past, gate, NEG_INF)
        rank = jnp.zeros(gate.shape, F32)
        for m in range(c):
            row = gate[m:m + 1, :]
            tie = jnp.where(m < blk, 1.0, 0.0)
            rank = rank + jnp.where(row > gate, 1.0, jnp.where(row == gate, tie, 0.0))
        drop = jnp.where(past & (rank >= MOBA_TOPK), 1.0, 0.0)
        pad = [jnp.zeros((HEAD_DIM - km_rows, bq), F32)]
        rows = ([jnp.zeros((HEAD_DIM, bq), F32), drop] + pad) if spare[h] else ([drop] + pad + [jnp.zeros((HEAD_DIM, bq), F32)])
        dropped = _dot_nt(eye, jnp.concatenate(rows, axis=0).astype(BF16))
        return jnp.where(own[h], q, jnp.where(dropped > 0.5, NEG_INF, 0.0).astype(BF16))

    def score(c, h):
        return _dot_nt(ka_ref[h, 0:(c + 1) * bq, :], augmented_queries(c, h))

    def attend(c, h, s):
        parts = [s[n * bq:(n + 1) * bq] for n in range(c)]
        parts.append(jnp.where(causal, s[c * bq:], NEG_INF))
        top = parts[0].max(axis=0, keepdims=True)
        for part in parts[1:]:
            top = jnp.maximum(top, part.max(axis=0, keepdims=True))
        probs = jnp.concatenate([jnp.exp2((part - top).astype(BF16)) for part in parts], axis=0)
        acc = _dot(vt_ref[h * VT_ROWS:(h + 1) * VT_ROWS, 0:(c + 1) * bq], probs)
        return acc[:HEAD_DIM] * (1.0 / acc[HEAD_DIM:HEAD_DIM + 1])

    units = [(c, h) for c in range(nblk) for h in range(2)]
    ahead = 2
    pending = [score(*u) for u in units[:ahead]]
    outs = {}
    for i, (c, h) in enumerate(units):
        s = pending.pop(0)
        if i + ahead < len(units):
            pending.append(score(*units[i + ahead]))
        outs[h] = attend(c, h, s)
        if h == 1:
            o_ref[c * bq:(c + 1) * bq, :] = jnp.concatenate([outs[0], outs[1]], axis=0).T.astype(o_ref.dtype)


def moba_attention(qkv, batch, seq_len):
    n = qkv.shape[0]
    return pl.pallas_call(
        _moba_kernel,
        grid=(batch, N_PAIRS),
        in_specs=[pl.BlockSpec((seq_len, PAIR), lambda b, p: (b, p)),
                  pl.BlockSpec((seq_len, PAIR), lambda b, p: (b, N_PAIRS + p)),
                  pl.BlockSpec((seq_len, PAIR), lambda b, p: (b, 2 * N_PAIRS + p))],
        out_specs=pl.BlockSpec((seq_len, PAIR), lambda b, p: (b, p)),
        out_shape=jax.ShapeDtypeStruct((n, D_MODEL), BF16),
        scratch_shapes=[pltpu.VMEM((2 * VT_ROWS, seq_len), BF16),
                        pltpu.VMEM((2, seq_len, PAIR), BF16)],
        compiler_params=_params("parallel", "parallel"),
        name="moba_attention",
    )(qkv, qkv, qkv)


def _mem_xattn_kernel(y_ref, wy_ref, x_ref, g_ref, wq_ref, kv_ref, wo_ref, gf_ref, rw_ref, rb_ref,
                      o_ref, h_out, meta_out, cnt_out):
    x = x_ref[...] + _dot(y_ref[...], wy_ref[...])
    xn = _rms(x, g_ref[...]).astype(BF16)
    q = (_dot(xn, wq_ref[...]) * (1.0 / math.sqrt(MEM_HEAD_DIM))).astype(BF16)
    outs = []
    for h in range(MEM_HEADS):
        sl = slice(h * MEM_HEAD_DIM, (h + 1) * MEM_HEAD_DIM)
        s = _dot_nt(q[:, sl], kv_ref[:, sl])
        s = s - jnp.max(s, axis=-1, keepdims=True)
        e = jnp.exp(s)
        pr = e * (1.0 / jnp.sum(e, axis=-1, keepdims=True))
        vh = kv_ref[:, D_MODEL + h * MEM_HEAD_DIM:D_MODEL + (h + 1) * MEM_HEAD_DIM]
        outs.append(_dot(pr.astype(BF16), vh).astype(BF16))
    o = jnp.concatenate(outs, axis=1)
    x2 = x + _dot(o, wo_ref[...])
    o_ref[...] = x2
    _route(x2, gf_ref, rw_ref, rb_ref, h_out, meta_out, cnt_out)


def mem_cross_attention(y, wy, x, gain, wq, kv, wo, seq_len, ffn_gain, w_grp, b_grp, w_exp, b_exp):
    n, d = x.shape
    tm = MOE_TILE
    m = kv.shape[0] // (n // seq_len)
    per_seq = seq_len // tm
    wt = jnp.zeros((LANES, d), F32).at[0:MOE_GROUPS].set(w_grp.T).at[8:8 + MOE_EXPERTS].set(w_exp.T)
    bt = jnp.zeros((LANES, 1), F32).at[0:MOE_GROUPS, 0].set(b_grp).at[8:8 + MOE_EXPERTS, 0].set(b_exp)
    const = lambda shape: pl.BlockSpec(shape, lambda i: (0, 0))
    return pl.pallas_call(
        _mem_xattn_kernel,
        grid=(n // tm,),
        in_specs=[pl.BlockSpec((tm, d), lambda i: (i, 0)),
                  const((d, d)),
                  pl.BlockSpec((tm, d), lambda i: (i, 0)),
                  const((1, d)),
                  const((d, d)),
                  pl.BlockSpec((m, 2 * d), lambda i: (i // per_seq, 0)),
                  const((d, d)),
                  const((1, d)),
                  const((LANES, d)),
                  const((LANES, 1))],
        out_specs=[pl.BlockSpec((tm, d), lambda i: (i, 0)),
                   pl.BlockSpec((tm, d), lambda i: (i, 0)),
                   pl.BlockSpec((8, tm), lambda i: (0, i)),
                   pl.BlockSpec((MOE_EXPERTS, LANES), lambda i: (i, 0))],
        out_shape=[jax.ShapeDtypeStruct((n, d), F32),
                   jax.ShapeDtypeStruct((n, d), BF16),
                   jax.ShapeDtypeStruct((8, n), F32),
                   jax.ShapeDtypeStruct((n // tm * MOE_EXPERTS, LANES), F32)],
        compiler_params=_params("parallel"),
        name="mem_cross_attention",
    )(y, wy.astype(BF16), x, gain.reshape(1, d), wq.astype(BF16), kv, wo.astype(BF16),
      ffn_gain.reshape(1, d), wt, bt)


def _route(x, g_ref, w_ref, b_ref, h_out, meta_out, cnt_out):
    tm = x.shape[0]
    h2 = _rms(x, g_ref[...])
    hi = h2.astype(BF16)
    h_out[...] = hi
    lo = (h2 - hi.astype(F32)).astype(BF16)
    w_hi, w_lo = _split(w_ref[...])
    lg = _dot_nt(w_hi, hi) + _dot_nt(w_hi, lo) + _dot_nt(w_lo, hi) + b_ref[...]
    row = _iota((8, tm), 0).astype(F32)

    def first_argmax(val, vmax):
        return jnp.min(jnp.where(val == vmax, row, 8.0), axis=0, keepdims=True)

    gl = jnp.where(row < MOE_GROUPS, lg[0:8], -jnp.inf)
    gmax = jnp.max(gl, axis=0, keepdims=True)
    p_g = 1.0 / jnp.sum(jnp.exp(gl - gmax), axis=0, keepdims=True)
    gidx = first_argmax(gl, gmax)
    el = jnp.zeros((8, tm), F32)
    for g in range(MOE_GROUPS):
        el = el + jnp.where(gidx == g, lg[8 + 8 * g:16 + 8 * g], 0.0)
    ee = jnp.exp(el - jnp.max(el, axis=0, keepdims=True))
    pe = ee / jnp.sum(ee, axis=0, keepdims=True)
    p1 = jnp.max(pe, axis=0, keepdims=True)
    i1 = first_argmax(pe, p1)
    pe2 = jnp.where(row == i1, -1.0, pe)
    p2 = jnp.max(pe2, axis=0, keepdims=True)
    i2 = first_argmax(pe2, p2)
    e1 = gidx * MOE_EPG + i1
    e2 = gidx * MOE_EPG + i2
    gate1 = p_g * p1 / (p1 + p2)
    gate2 = p_g * p2 / (p1 + p2)

    erow = _iota((MOE_EXPERTS, tm), 0).astype(F32)
    oh1 = erow == e1
    oh2 = erow == e2
    oh = jnp.where(oh1 | oh2, 1.0, 0.0)
    before = jnp.where(_iota((tm, tm), 0) < _iota((tm, tm), 1), 1.0, 0.0).astype(BF16)
    cnt_before = _dot(oh.astype(BF16), before)
    lr1 = jnp.sum(jnp.where(oh1, cnt_before, 0.0), axis=0, keepdims=True)
    lr2 = jnp.sum(jnp.where(oh2, cnt_before, 0.0), axis=0, keepdims=True)
    zero = jnp.zeros((1, tm), F32)
    meta_out[...] = jnp.concatenate([e1, e2, gate1, gate2, lr1, lr2, zero, zero], axis=0)
    cnt_out[...] = jnp.broadcast_to(jnp.sum(oh, axis=1, keepdims=True), (MOE_EXPERTS, LANES))


def _chunk_copies(tab_ref, t, make_copy, act):
    base = t * RUN_TAB
    for ci, ck in enumerate(RUN_CHUNKS):
        first = base + len(RUN_CHUNKS) + ci * 2 * MOE_EXPERTS

        def body(i, _, first=first, ck=ck):
            src = tab_ref[first + 2 * i]
            dst = tab_ref[first + 2 * i + 1]
            act(make_copy(pl.multiple_of(src, MOE_ALIGN), pl.multiple_of(dst, MOE_ALIGN), ck))
            return 0

        lax.fori_loop(0, tab_ref[base + ci], body, 0)


def _run_copies(tab_ref, first, count, chunks, make_copy, act):
    def body(e, _):
        base = first + e * 3
        length = tab_ref[base]
        src = tab_ref[base + 1]
        dst = tab_ref[base + 2]
        for ck in chunks:
            @pl.when((length & ck) != 0)
            def _():
                off = length & (-2 * ck)
                act(make_copy(pl.multiple_of(src + off, MOE_ALIGN), pl.multiple_of(dst + off, MOE_ALIGN), ck))
        return 0
    lax.fori_loop(0, count, body, 0)


def _start(cp):
    cp.start()


def _wait(cp):
    cp.wait()


def _dispatch_kernel(tab_ref, h_ref, pos_ref, rows_hbm, sorted_ref, zero_ref, sem, zsem):
    t = pl.program_id(0)
    nt = pl.num_programs(0)
    slot = t % 2
    r = _iota((MOE_SORTED, MOE_TILE), 0)
    perm = jnp.where((r == pos_ref[0:1, :]) | (r == pos_ref[1:2, :]), 1.0, 0.0).astype(BF16)
    sorted_ref[slot] = _dot(perm, h_ref[...]).astype(BF16)

    def run_copy(s):
        def make(src, dst, ck):
            return pltpu.make_async_copy(sorted_ref.at[s, pl.ds(src, ck)], rows_hbm.at[pl.ds(dst, ck)], sem.at[s])
        return make

    def zero_copy(src, dst, ck):
        del src
        return pltpu.make_async_copy(zero_ref.at[pl.ds(0, ck)], rows_hbm.at[pl.ds(dst, ck)], zsem)

    _chunk_copies(tab_ref, t, run_copy(slot), _start)

    @pl.when(t > 0)
    def _():
        _chunk_copies(tab_ref, t - 1, run_copy(1 - slot), _wait)

    @pl.when(t == nt - 1)
    def _():
        zero_ref[...] = jnp.zeros_like(zero_ref)
        pad = nt * RUN_TAB
        _run_copies(tab_ref, pad, MOE_EXPERTS, PAD_CHUNKS, zero_copy, _start)
        spare = pad + MOE_EXPERTS * 3
        n_spare = tab_ref[spare]

        def spare_copy(i):
            dst = pl.multiple_of(tab_ref[spare + 2] + i * PAD_ROWS, PAD_ROWS)
            return zero_copy(0, dst, PAD_ROWS)

        lax.fori_loop(0, n_spare, lambda i, c: (_start(spare_copy(i)), c)[1], 0)
        _chunk_copies(tab_ref, t, run_copy(slot), _wait)
        _run_copies(tab_ref, pad, MOE_EXPERTS, PAD_CHUNKS, zero_copy, _wait)
        lax.fori_loop(0, n_spare, lambda i, c: (_wait(spare_copy(i)), c)[1], 0)


def moe_dispatch(tab, h2, pos, n_rows):
    n, d = h2.shape
    tm = MOE_TILE
    return pl.pallas_call(
        _dispatch_kernel,
        grid_spec=pltpu.PrefetchScalarGridSpec(
            num_scalar_prefetch=1,
            grid=(n // tm,),
            in_specs=[pl.BlockSpec((tm, d), lambda i, tab: (i, 0)),
                      pl.BlockSpec((8, tm), lambda i, tab: (0, i))],
            out_specs=pl.BlockSpec(memory_space=pl.ANY),
            scratch_shapes=[pltpu.VMEM((2, MOE_SORTED, d), BF16), pltpu.VMEM((PAD_ROWS, d), BF16),
                            pltpu.SemaphoreType.DMA((2,)), pltpu.SemaphoreType.DMA],
        ),
        out_shape=jax.ShapeDtypeStruct((n_rows, d), BF16),
        compiler_params=_params("arbitrary"),
        name="moe_dispatch",
    )(tab, h2, pos)


def _expert_kernel(be_ref, nu_ref, x_ref, w1_ref, w3_ref, w2_ref, y_ref, w1b, w3b, w2b):
    b = pl.program_id(0)
    used = b < nu_ref[0]
    new_expert = (b == 0) | (be_ref[b] != be_ref[jnp.maximum(b - 1, 0)])

    @pl.when(used & new_expert)
    def _():
        w1b[...] = w1_ref[0, 0].astype(BF16)
        w3b[...] = w3_ref[0, 0].astype(BF16)
        w2b[...] = w2_ref[0, 0].astype(BF16)

    @pl.when(used)
    def _():
        xb = x_ref[...]
        a = _dot(xb, w1b[...])
        c = _dot(xb, w3b[...])
        hid = (a * jax.nn.sigmoid(a) * c).astype(BF16)
        y_ref[...] = _dot(hid, w2b[...]).astype(y_ref.dtype)

    @pl.when(jnp.logical_not(used))
    def _():
        y_ref[...] = jnp.zeros_like(y_ref)


def moe_experts(block_expert, n_used, rows, layer, w1, w3, w2):
    n_rows = rows.shape[0]
    nb = n_rows // MOE_ROWS
    d, ff = w1.shape[2], w1.shape[3]

    def xmap(b, be, nu):
        return (jnp.minimum(b, nu[0] - 1), 0)

    def wmap(b, be, nu):
        return (layer, be[jnp.minimum(b, nu[0] - 1)], 0, 0)

    return pl.pallas_call(
        _expert_kernel,
        grid_spec=pltpu.PrefetchScalarGridSpec(
            num_scalar_prefetch=2,
            grid=(nb,),
            in_specs=[pl.BlockSpec((MOE_ROWS, d), xmap),
                      pl.BlockSpec((1, 1, d, ff), wmap),
                      pl.BlockSpec((1, 1, d, ff), wmap),
                      pl.BlockSpec((1, 1, ff, d), wmap)],
            out_specs=pl.BlockSpec((MOE_ROWS, d), lambda b, be, nu: (b, 0)),
            scratch_shapes=[pltpu.VMEM((d, ff), BF16), pltpu.VMEM((d, ff), BF16), pltpu.VMEM((ff, d), BF16)],
        ),
        out_shape=jax.ShapeDtypeStruct((n_rows, d), BF16),
        compiler_params=_params("arbitrary"),
        name="moe_experts",
    )(block_expert, n_used, rows, w1, w3, w2)


def _combine_kernel(final_norm, tab_ref, y_hbm, pos_ref, gate_ref, x_ref, g_ref, o_ref, ys_ref, sem):
    t = pl.program_id(0)
    nt = pl.num_programs(0)
    slot = t % 2

    def run_copy(s):
        def make(src, dst, ck):
            return pltpu.make_async_copy(y_hbm.at[pl.ds(dst, ck)], ys_ref.at[s, pl.ds(src, ck)], sem.at[s])
        return make

    @pl.when(t == 0)
    def _():
        ys_ref[...] = jnp.zeros_like(ys_ref)
        _chunk_copies(tab_ref, 0, run_copy(0), _start)

    @pl.when(t + 1 < nt)
    def _():
        _chunk_copies(tab_ref, t + 1, run_copy(1 - slot), _start)

    r = _iota((MOE_SORTED, MOE_TILE), 0)
    hit1 = r == pos_ref[0:1, :]
    hit2 = r == pos_ref[1:2, :]
    wgt = jnp.where(hit1, gate_ref[2:3, :], jnp.where(hit2, gate_ref[3:4, :], 0.0)).astype(BF16)
    _chunk_copies(tab_ref, t, run_copy(slot), _wait)
    out = x_ref[...] + _dot_tn(wgt, ys_ref[slot])
    if final_norm:
        out = _rms(out, g_ref[...])
    o_ref[...] = out


def moe_combine(tab, y, pos, meta, x, final_gain):
    n, d = x.shape
    tm = MOE_TILE
    final_norm = final_gain is not None
    gain = (final_gain if final_norm else jnp.ones((d,), F32)).reshape(1, d)
    return pl.pallas_call(
        functools.partial(_combine_kernel, final_norm),
        grid_spec=pltpu.PrefetchScalarGridSpec(
            num_scalar_prefetch=1,
            grid=(n // tm,),
            in_specs=[pl.BlockSpec(memory_space=pl.ANY),
                      pl.BlockSpec((8, tm), lambda i, tab: (0, i)),
                      pl.BlockSpec((8, tm), lambda i, tab: (0, i)),
                      pl.BlockSpec((tm, d), lambda i, tab: (i, 0)),
                      pl.BlockSpec((1, d), lambda i, tab: (0, 0))],
            out_specs=pl.BlockSpec((tm, d), lambda i, tab: (i, 0)),
            scratch_shapes=[pltpu.VMEM((2, MOE_SORTED, d), BF16), pltpu.SemaphoreType.DMA((2,))],
        ),
        out_shape=jax.ShapeDtypeStruct((n, d), F32),
        compiler_params=_params("arbitrary"),
        name="moe_combine",
    )(tab, y, pos, meta, x, gain)


def hierarchical_moe(x, h2, meta, cnt, layer, w1, w3, w2, final_gain):
    n, d = x.shape
    tm = MOE_TILE
    nt = n // tm

    cnt = cnt.reshape(nt, MOE_EXPERTS, LANES)[:, :, 0].astype(I32)
    run = (cnt + MOE_ALIGN - 1) // MOE_ALIGN * MOE_ALIGN
    src = jnp.cumsum(run, axis=1) - run
    before = jnp.cumsum(run, axis=0) - run
    total = jnp.sum(run, axis=0)
    padded = (total + MOE_ROWS - 1) // MOE_ROWS * MOE_ROWS
    pad_end = jnp.cumsum(padded)
    dst = (pad_end - padded)[None, :] + before
    max_rows = 2 * n + nt * MOE_EXPERTS * (MOE_ALIGN - 1) + MOE_EXPERTS * (MOE_ROWS - 1)
    nb = -(-max_rows // MOE_ROWS)
    sizes = jnp.array(RUN_CHUNKS, I32)[None, :, None]
    has = (run[:, None, :] & sizes) != 0
    above = run[:, None, :] & (-2 * sizes)
    place = jnp.where(has, jnp.cumsum(has.astype(I32), axis=2) - 1, -1)
    front = place[..., None] == jnp.arange(MOE_EXPERTS, dtype=I32)
    c_src = jnp.sum(jnp.where(front, (src[:, None, :] + above)[..., None], 0), axis=2)
    c_dst = jnp.sum(jnp.where(front, (dst[:, None, :] + above)[..., None], 0), axis=2)
    chunks = jnp.stack([c_src, c_dst], axis=-1).reshape(nt, -1)
    runs = jnp.concatenate([jnp.sum(has.astype(I32), axis=2), chunks], axis=1).reshape(-1)
    zero = jnp.zeros((MOE_EXPERTS,), I32)
    pads = jnp.stack([padded - total, zero, pad_end - padded + total], axis=-1)
    spare = jnp.stack([(nb * MOE_ROWS - pad_end[-1]) // PAD_ROWS, zero[0], pad_end[-1]])
    tab = jnp.concatenate([runs, pads.reshape(-1), spare]).astype(I32)
    n_used = (pad_end[-1] // MOE_ROWS).astype(I32).reshape(1)
    block_start = jnp.arange(nb, dtype=I32) * MOE_ROWS
    block_expert = jnp.minimum(
        jnp.sum((pad_end[None, :] <= block_start[:, None]).astype(I32), axis=1), MOE_EXPERTS - 1).astype(I32)
    e = meta[0:2].astype(I32).reshape(2, nt, tm, 1)
    hit = e == jnp.arange(MOE_EXPERTS, dtype=I32)
    pos = jnp.sum(jnp.where(hit, src[None, :, None, :], 0), axis=-1).reshape(2, n) + meta[4:6].astype(I32)
    pos8 = jnp.zeros((8, n), I32).at[0:2].set(pos)

    rows = moe_dispatch(tab, h2, pos8, nb * MOE_ROWS)
    y = moe_experts(block_expert, n_used, rows, layer, w1, w3, w2)
    return moe_combine(tab, y, pos8, meta, x, final_gain)


def kernel(x, mem, positions, ln_mix, ln_mem, ln_memkv, ln_ffn, rw_mu, rw_w0, rw_w1, rw_w2, rw_a0, rw_a1, rw_a2, rw_g1, rw_g2, rw_kk, rw_ka, rw_rk, rw_wrkv, rw_lnx_g, rw_lnx_b, rw_wo, mb_wqkv, mb_wo, mx_wq, mx_wkv, mx_wo, moe_wg, moe_bg, moe_we, moe_be, moe_w1, moe_w3, moe_w2, ln_f):
    B, T, C = x.shape
    n = B * T
    depth = ln_mix.shape[0]
    xf = x.reshape(n, C)
    memf = mem.reshape(-1, C)
    for i in range(depth):
        j = i // 2
        if i % 2 == 0:
            r, k, v, kk, akk, lw, g = rwkv_mix(xf, T, ln_mix[i], rw_mu[j], rw_w0[j], rw_w1[j], rw_w2[j],
                                               rw_a0[j], rw_a1[j], rw_a2[j], rw_g1[j], rw_g2[j],
                                               rw_kk[j], rw_ka[j], rw_wrkv[j])
            y = rwkv_recurrence(r, k, v, kk, akk, lw, g, rw_lnx_g[j], rw_lnx_b[j], rw_rk[j], B, T)
            w_mix_o = rw_wo[j]
        else:
            qkv = qkv_rope(xf, ln_mix[i], positions, mb_wqkv[j])
            y = moba_attention(qkv, B, T)
            w_mix_o = mb_wo[j]
        kv = norm_linear(memf, ln_memkv[i], mx_wkv[i].astype(BF16))
        xf, h2, meta, cnt = mem_cross_attention(y, w_mix_o, xf, ln_mem[i], mx_wq[i], kv, mx_wo[i], T,
                                                ln_ffn[i], moe_wg[i], moe_bg[i], moe_we[i], moe_be[i])
        xf = hierarchical_moe(xf, h2, meta, cnt, i, moe_w1, moe_w3, moe_w2, ln_f if i == depth - 1 else None)
    return xf.reshape(B, T, C)
```

```python
import functools
import math

import jax
import jax.numpy as jnp
from jax import lax
from jax.experimental import pallas as pl
from jax.experimental.pallas import tpu as pltpu

F32 = jnp.float32
BF16 = jnp.bfloat16
I32 = jnp.int32

D_MODEL = 1024
HEAD_DIM = 64
PAIR = 2 * HEAD_DIM
N_PAIRS = D_MODEL // PAIR
RWKV_GN_EPS = 64e-5
RWKV_CHUNK = 64
RWKV_CHUNKS_PER_STEP = 8
MOBA_BLOCK = 256
MOBA_TOPK = 3
VT_ROWS = HEAD_DIM + 16
ROPE_THETA = 10000.0
MEM_HEADS = 4
MEM_HEAD_DIM = D_MODEL // MEM_HEADS
MOE_GROUPS = 4
MOE_EPG = 8
MOE_EXPERTS = MOE_GROUPS * MOE_EPG
RMS_EPS = 1e-6
NEG_INF = -1e30

LANES = 128
ROW_ALIGN = 8
MOE_TILE = 512
MOE_ROWS = 512
MOE_ALIGN = 16
MOE_SORTED = -(-(2 * MOE_TILE + MOE_EXPERTS * (MOE_ALIGN - 1)) // MOE_ALIGN) * MOE_ALIGN
RUN_CHUNKS = tuple(MOE_ALIGN << s for s in range((MOE_TILE // MOE_ALIGN).bit_length() - 1, -1, -1))
RUN_TAB = len(RUN_CHUNKS) * (1 + 2 * MOE_EXPERTS)
PAD_ROWS = MOE_ROWS // 2
PAD_CHUNKS = tuple(MOE_ALIGN << s for s in range((PAD_ROWS // MOE_ALIGN).bit_length() - 1, -1, -1))
VMEM_LIMIT = 56 * 2 ** 20


def _params(*sem):
    return pltpu.CompilerParams(dimension_semantics=sem, vmem_limit_bytes=VMEM_LIMIT)


def _iota(shape, dim):
    return lax.broadcasted_iota(I32, shape, dim)


def _dot(a, b):
    return jnp.dot(a, b, preferred_element_type=F32)


def _dot_nt(a, b):
    return lax.dot_general(a, b, (((1,), (1,)), ((), ())), preferred_element_type=F32)


def _dot_tn(a, b):
    return lax.dot_general(a, b, (((0,), (0,)), ((), ())), preferred_element_type=F32)


def _split(x):
    hi = x.astype(BF16)
    lo = (x - hi.astype(F32)).astype(BF16)
    return hi, lo


def _rms(x, g):
    return x * lax.rsqrt(jnp.mean(x * x, axis=-1, keepdims=True) + RMS_EPS) * g


def _norm_linear_kernel(x_ref, g_ref, w_ref, o_ref):
    xn = _rms(x_ref[...], g_ref[...]).astype(BF16)
    o_ref[...] = _dot(xn, w_ref[...]).astype(o_ref.dtype)


def norm_linear(x, g, w, tm=256):
    n, k = x.shape
    dout = w.shape[1]
    return pl.pallas_call(
        _norm_linear_kernel,
        grid=(n // tm,),
        in_specs=[pl.BlockSpec((tm, k), lambda i: (i, 0)),
                  pl.BlockSpec((1, k), lambda i: (0, 0)),
                  pl.BlockSpec((k, dout), lambda i: (0, 0))],
        out_specs=pl.BlockSpec((tm, dout), lambda i: (i, 0)),
        out_shape=jax.ShapeDtypeStruct((n, dout), BF16),
        compiler_params=_params("parallel"),
        name="norm_linear",
    )(x, g.reshape(1, k), w)


def _head_sum(x_sq):
    r = _iota((PAIR, PAIR), 0) // HEAD_DIM
    c = _iota((PAIR, PAIR), 1) // HEAD_DIM
    bd = jnp.where(r == c, 1.0, 0.0).astype(BF16)
    hi, lo = _split(x_sq)
    return _dot(hi, bd) + _dot(lo, bd)


def _rwkv_mix_kernel(seq_len, x_ref, xp_ref, g_ref, mu_ref, vec_ref, wr_ref, wk_ref, wv_ref,
                     l1_ref, w2_ref, a2_ref, g2_ref,
                     r_out, k_out, v_out, kk_out, akk_out, lw_out, g_out):
    i = pl.program_id(0)
    tm = x_ref.shape[0]
    gain = g_ref[...]
    h = _rms(x_ref[...], gain)
    hp = _rms(xp_ref[...], gain)[ROW_ALIGN - 1:ROW_ALIGN, :]
    hp = jnp.where((i * tm) % seq_len == 0, 0.0, hp)
    hs = pltpu.roll(h, 1, 0)
    hs = jnp.where(_iota((tm, 1), 0) == 0, hp, hs)
    hb = h.astype(BF16)
    dxb = (hs - h).astype(BF16)
    mub = mu_ref[...].astype(BF16)

    def mix(s):
        return hb + dxb * mub[s:s + 1, :]

    vec = vec_ref[...]
    w0, a0, k_k, k_a = vec[0:1], vec[1:2], vec[2:3], vec[3:4]
    l1 = l1_ref[...]
    r = _dot(mix(0), wr_ref[...])
    k = _dot(mix(2), wk_ref[...])
    v = _dot(mix(3), wv_ref[...])
    tw = jnp.tanh(_dot(mix(1), l1[:, 0:64])).astype(BF16)
    lw = -math.exp(-0.5) * jax.nn.sigmoid(w0 + _dot(tw, w2_ref[...]))
    ta = _dot(mix(4), l1[:, 64:128]).astype(BF16)
    a = jax.nn.sigmoid(a0 + _dot(ta, a2_ref[...]))
    tg = jax.nn.sigmoid(_dot(mix(5), l1[:, 128:256])).astype(BF16)
    g = _dot(tg, g2_ref[...])

    kk = k * k_k
    for p in range(N_PAIRS):
        sl = slice(p * PAIR, (p + 1) * PAIR)
        kkp = kk[:, sl]
        kkn = kkp * lax.rsqrt(jnp.maximum(_head_sum(kkp * kkp), 1e-24))
        kk_out[:, sl] = kkn.astype(kk_out.dtype)
        akk_out[:, sl] = (kkn * a[:, sl]).astype(akk_out.dtype)
    r_out[...] = r.astype(r_out.dtype)
    k_out[...] = (k * (1.0 + (a - 1.0) * k_a)).astype(k_out.dtype)
    v_out[...] = v.astype(v_out.dtype)
    lw_out[...] = lw
    g_out[...] = g.astype(g_out.dtype)


def rwkv_mix(x, seq_len, gain, mu, w0, w1, w2, a0, a1, a2, g1, g2, k_k, k_a, w_rkv, tm=512):
    n, d = x.shape
    mu8 = jnp.zeros((8, d), F32).at[:6].set(mu)
    vec = jnp.zeros((8, d), F32).at[0].set(w0).at[1].set(a0).at[2].set(k_k).at[3].set(k_a)
    l1 = jnp.concatenate([w1, a1, g1], axis=1).astype(BF16)
    wb = w_rkv.astype(BF16)
    row = pl.BlockSpec((tm, d), lambda i: (i, 0))
    full = lambda a: pl.BlockSpec(a.shape, lambda i: (0,) * a.ndim)
    args = (x, x, gain.reshape(1, d), mu8, vec, wb[0], wb[1], wb[2], l1,
            w2.astype(BF16), a2.astype(BF16), g2.astype(BF16))
    in_specs = [row, pl.BlockSpec((ROW_ALIGN, d), lambda i: (jnp.maximum(i * (tm // ROW_ALIGN) - 1, 0), 0))]
    in_specs += [full(a) for a in args[2:]]
    outs = [jax.ShapeDtypeStruct((n, d), BF16)] * 5 + [jax.ShapeDtypeStruct((n, d), F32),
                                                       jax.ShapeDtypeStruct((n, d), BF16)]
    return pl.pallas_call(
        functools.partial(_rwkv_mix_kernel, seq_len),
        grid=(n // tm,),
        in_specs=in_specs,
        out_specs=[row] * 7,
        out_shape=outs,
        compiler_params=_params("parallel"),
        name="rwkv_mix",
    )(*args)


def _rwkv_rec_kernel(r_ref, k_ref, v_ref, kk_ref, akk_ref, lw_ref, g_ref, vec_ref, y_ref, s_ref):
    c = pl.program_id(1)
    L = RWKV_CHUNK

    @pl.when(c == 0)
    def _():
        s_ref[...] = jnp.zeros_like(s_ref)

    lane = _iota((1, PAIR), 1)
    m0 = lane < HEAD_DIM
    ri = _iota((2 * L, 2 * L), 0)
    ci = _iota((2 * L, 2 * L), 1)
    same = (ri // L) == (ci // L)
    strict = same & (ci < ri)
    incl = same & (ci <= ri)
    eye = jnp.where(ri == ci, 1.0, 0.0)
    tri = jnp.where(_iota((L, L), 1) <= _iota((L, L), 0), 1.0, 0.0).astype(BF16)
    rb = _iota((PAIR, PAIR), 0) // HEAD_DIM
    cb = _iota((PAIR, PAIR), 1) // HEAD_DIM
    bd = rb == cb

    def stack(x):
        return jnp.concatenate([jnp.where(m0, x, 0.0), jnp.where(m0, 0.0, x)], axis=0)

    def fold(x):
        return x[:L] + x[L:]

    def head_mean(x):
        s0 = jnp.sum(jnp.where(m0, x, 0.0), axis=-1, keepdims=True)
        s1 = jnp.sum(jnp.where(m0, 0.0, x), axis=-1, keepdims=True)
        return jnp.where(m0, s0, s1) * (1.0 / HEAD_DIM)

    vec = vec_ref[...]
    n_sub = r_ref.shape[0] // L
    pairs = range(n_sub * N_PAIRS)
    rws = [slice((q // N_PAIRS) * L, (q // N_PAIRS + 1) * L) for q in pairs]
    sls = [slice((q % N_PAIRS) * PAIR, (q % N_PAIRS + 1) * PAIR) for q in pairs]
    lw = [lw_ref[rws[p], sls[p]] for p in pairs]
    cum = []
    for p in pairs:
        lw_hi, lw_lo = _split(lw[p])
        cum.append(_dot(tri, lw_hi) + _dot(tri, lw_lo))
    dec_all, x_kap, x_r, a_t, k_t, sc = [], [], [], [], [], []
    for p in pairs:
        rw, sl = rws[p], sls[p]
        dec = jnp.exp(cum[p])
        inv = jnp.exp(-cum[p])
        dec_prev = jnp.exp(cum[p] - lw[p])
        dec_all.append(dec[L - 1:L, :])
        x_kap.append(stack(kk_ref[rw, sl].astype(F32) * dec_prev).astype(BF16))
        x_r.append(stack(r_ref[rw, sl].astype(F32) * dec))
        a_t.append((akk_ref[rw, sl].astype(F32) * inv).astype(BF16))
        k_t.append((k_ref[rw, sl].astype(F32) * inv).astype(BF16))
        xs = jnp.concatenate([x_kap[p], x_r[p].astype(BF16)], axis=0)
        ys = jnp.concatenate([a_t[p], k_t[p]], axis=0)
        sc.append(_dot_nt(xs, ys))

    def both_heads(first, second, keep):
        upper_left = keep & (ri < L)
        lower_right = keep & (ri >= L)
        return jnp.where(upper_left, first, jnp.where(lower_right, second, 0.0))

    mp, m_kk = [], []
    for p in pairs:
        top = sc[p][:2 * L]
        top_r = pltpu.roll(top, L, 1)
        mp.append(-both_heads(top, top_r, strict))
        m_kk.append(both_heads(top_r, top, strict).astype(BF16))
    s_ra, s_rk = [], []
    for p in pairs:
        bot = sc[p][2 * L:]
        bot_r = pltpu.roll(bot, L, 1)
        s_ra.append(both_heads(bot, bot_r, incl).astype(BF16))
        s_rk.append(both_heads(bot_r, bot, incl).astype(BF16))
    v_st = [stack(v_ref[rws[p], sls[p]].astype(F32)).astype(BF16) for p in pairs]
    bo = [_dot(jnp.concatenate([m_kk[p], s_rk[p]], axis=0), v_st[p]) for p in pairs]
    b1 = [bo[p][:2 * L].astype(BF16) for p in pairs]
    o1 = [bo[p][2 * L:] for p in pairs]

    t_inv = [eye + mp[p] for p in pairs]
    mpb = [mp[p].astype(BF16) for p in pairs]
    mpb = [_dot(mpb[p], mpb[p]).astype(BF16) for p in pairs]
    steps = int(math.log2(L)) - 1
    for j in range(steps):
        if j + 1 < steps:
            both = [_dot(mpb[p], jnp.concatenate([mpb[p], t_inv[p].astype(BF16)], axis=1)) for p in pairs]
            mpb = [both[p][:, :2 * L].astype(BF16) for p in pairs]
            t_inv = [t_inv[p] + both[p][:, 2 * L:] for p in pairs]
        else:
            t_inv = [t_inv[p] + _dot(mpb[p], t_inv[p].astype(BF16)) for p in pairs]

    wu = [_dot(t_inv[p].astype(BF16), jnp.concatenate([x_kap[p], b1[p]], axis=1)).astype(BF16)
          for p in pairs]
    corr = [_dot(s_ra[p], wu[p]) for p in pairs]
    g_m, h0t, r_hat, o0 = [], [], [], []
    for p in pairs:
        r_hat.append(fold(x_r[p] - corr[p][:, :PAIR]).astype(BF16))
        o0.append(fold(o1[p] - corr[p][:, PAIR:]))
        w_f = fold(wu[p][:, :PAIR])
        u0_f = fold(wu[p][:, PAIR:])
        g_m.append(jnp.where(bd, _dot_tn(a_t[p], w_f), 0.0).astype(BF16))
        vu = jnp.concatenate([v_ref[rws[p], sls[p]], -u0_f], axis=0)
        h0t.append(jnp.where(bd, _dot_tn(vu, jnp.concatenate([k_t[p], a_t[p]], axis=0)), 0.0))
    o = []
    for p in pairs:
        hp = p % N_PAIRS
        s = s_ref[hp]
        sb = s.astype(BF16)
        o.append(_dot_nt(r_hat[p], sb) + o0[p])
        s_ref[hp] = (s - _dot_nt(sb, g_m[p]) + h0t[p]) * dec_all[p]
    for p in pairs:
        rw, sl = rws[p], sls[p]
        mean = head_mean(o[p])
        cen = o[p] - mean
        var = head_mean(cen * cen)
        gn = cen * lax.rsqrt(var + RWKV_GN_EPS) * vec[0:1, sl] + vec[1:2, sl]
        r = r_ref[rw, sl].astype(F32)
        k = k_ref[rw, sl].astype(F32)
        bonus = head_mean(r * k * vec[2:3, sl]) * HEAD_DIM * v_ref[rw, sl].astype(F32)
        y_ref[rw, sl] = ((gn + bonus) * g_ref[rw, sl].astype(F32)).astype(y_ref.dtype)


def rwkv_recurrence(r, k, v, kk, akk, lw, g, lnx_g, lnx_b, r_k, batch, seq_len):
    n, d = r.shape
    rows = RWKV_CHUNK * RWKV_CHUNKS_PER_STEP
    nc = seq_len // rows
    vec = jnp.zeros((8, d), F32).at[0].set(lnx_g).at[1].set(lnx_b).at[2].set(r_k.reshape(d))
    blk = pl.BlockSpec((rows, d), lambda b, c: (b * nc + c, 0))
    return pl.pallas_call(
        _rwkv_rec_kernel,
        grid=(batch, nc),
        in_specs=[blk] * 7 + [pl.BlockSpec((8, d), lambda b, c: (0, 0))],
        out_specs=blk,
        out_shape=jax.ShapeDtypeStruct((n, d), BF16),
        scratch_shapes=[pltpu.VMEM((N_PAIRS, PAIR, PAIR), F32)],
        compiler_params=_params("parallel", "arbitrary"),
        name="rwkv_recurrence",
    )(r, k, v, kk, akk, lw, g, vec)


def _qkv_rope_kernel(x_ref, g_ref, pos_ref, inv_ref, w_ref, o_ref):
    d = x_ref.shape[1]
    xn = _rms(x_ref[...], g_ref[...]).astype(BF16)
    y = [_dot(xn, w_ref[:, part * d:(part + 1) * d]) for part in range(3)]
    ang = pos_ref[...].astype(F32) * inv_ref[...]
    first = (_iota((1, PAIR), 1) % HEAD_DIM) < HEAD_DIM // 2
    sn = jnp.sin(ang)
    cos = jnp.cos(ang)
    sin_lo = jnp.where(first, -sn, 0.0)
    sin_hi = jnp.where(first, 0.0, sn)
    for part in range(2):
        for p in range(N_PAIRS):
            t = y[part][:, p * PAIR:(p + 1) * PAIR]
            rot = (t * cos + pltpu.roll(t, PAIR - HEAD_DIM // 2, 1) * sin_lo
                   + pltpu.roll(t, HEAD_DIM // 2, 1) * sin_hi)
            o_ref[:, part * d + p * PAIR:part * d + (p + 1) * PAIR] = rot.astype(o_ref.dtype)
    o_ref[:, 2 * d:] = y[2].astype(o_ref.dtype)


def qkv_rope(x, gain, positions, w_qkv, tm=512):
    n, d = x.shape
    half = HEAD_DIM // 2
    inv = ROPE_THETA ** (-jnp.arange(half, dtype=F32) * 2.0 / HEAD_DIM)
    inv128 = jnp.tile(inv, PAIR // half).reshape(1, PAIR)
    col_scale = jnp.where(jnp.arange(3 * d) < d, math.log2(math.e) / math.sqrt(HEAD_DIM), 1.0).astype(F32)
    return pl.pallas_call(
        _qkv_rope_kernel,
        grid=(n // tm,),
        in_specs=[pl.BlockSpec((tm, d), lambda i: (i, 0)),
                  pl.BlockSpec((1, d), lambda i: (0, 0)),
                  pl.BlockSpec((tm, 1), lambda i: (i, 0)),
                  pl.BlockSpec((1, PAIR), lambda i: (0, 0)),
                  pl.BlockSpec((d, 3 * d), lambda i: (0, 0))],
        out_specs=pl.BlockSpec((tm, 3 * d), lambda i: (i, 0)),
        out_shape=jax.ShapeDtypeStruct((n, 3 * d), BF16),
        compiler_params=_params("parallel"),
        name="qkv_rope",
    )(x, gain.reshape(1, d), positions.reshape(n, 1), inv128, (w_qkv * col_scale).astype(BF16))


def _moba_kernel(q_ref, k_ref, v_ref, o_ref, vt_ref, ka_ref):
    nblk = k_ref.shape[0] // MOBA_BLOCK
    bq = MOBA_BLOCK
    km_rows = 16
    lane = _iota((1, PAIR), 1)
    own = (lane < HEAD_DIM, lane >= HEAD_DIM)
    spare = (HEAD_DIM, 0)

    means = []
    ones_row = jnp.where(_iota((VT_ROWS - HEAD_DIM, bq), 0) == 0, 1.0, 0.0).astype(BF16)
    for n in range(nblk):
        cols = slice(n * bq, (n + 1) * bq)
        kb = k_ref[cols, :]
        means.append(jnp.mean(kb.astype(F32), axis=0, keepdims=True))
        vt = v_ref[cols, :].astype(F32).T.astype(BF16)
        for h in range(2):
            vt_ref[h * VT_ROWS:h * VT_ROWS + HEAD_DIM, cols] = vt[h * HEAD_DIM:(h + 1) * HEAD_DIM]
            vt_ref[h * VT_ROWS + HEAD_DIM:(h + 1) * VT_ROWS, cols] = ones_row
            marker = jnp.where(lane == spare[h] + n, 1.0, 0.0).astype(BF16)
            ka_ref[h, cols, :] = jnp.where(own[h], kb, marker)
    km_hi, km_lo = _split(jnp.concatenate(means + [jnp.zeros((km_rows - nblk, PAIR), F32)], axis=0))

    blk = _iota((km_rows, bq), 0)
    causal = _iota((bq, bq), 0) <= _iota((bq, bq), 1)
    eye = jnp.where(_iota((bq, bq), 0) == _iota((bq, bq), 1), 1.0, 0.0).astype(BF16)

    def augmented_queries(c, h):
        q = q_ref[c * bq:(c + 1) * bq, :]
        qm = jnp.where(own[h], q, jnp.zeros_like(q))
        past = blk < c
        gate = _dot_nt(km_hi, qm) + _dot_nt(km_lo, qm)
        gate = jnp.where(past, gate, NEG_INF)
        rank = jnp.zeros(gate.shape, F32)
        for m in range(c):
            row = gate[m:m + 1, :]
            tie = jnp.where(m < blk, 1.0, 0.0)
            rank = rank + jnp.where(row > gate, 1.0, jnp.where(row == gate, tie, 0.0))
        drop = jnp.where(past & (rank >= MOBA_TOPK), 1.0, 0.0)
        pad = [jnp.zeros((HEAD_DIM - km_rows, bq), F32)]
        rows = ([jnp.zeros((HEAD_DIM, bq), F32), drop] + pad) if spare[h] else ([drop] + pad + [jnp.zeros((HEAD_DIM, bq), F32)])
        dropped = _dot_nt(eye, jnp.concatenate(rows, axis=0).astype(BF16))
        return jnp.where(own[h], q, jnp.where(dropped > 0.5, NEG_INF, 0.0).astype(BF16))

    def score(c, h):
        return _dot_nt(ka_ref[h, 0:(c + 1) * bq, :], augmented_queries(c, h))

    def attend(c, h, s):
        parts = [s[n * bq:(n + 1) * bq] for n in range(c)]
        parts.append(jnp.where(causal, s[c * bq:], NEG_INF))
        top = parts[0].max(axis=0, keepdims=True)
        for part in parts[1:]:
            top = jnp.maximum(top, part.max(axis=0, keepdims=True))
        probs = jnp.concatenate([jnp.exp2((part - top).astype(BF16)) for part in parts], axis=0)
        acc = _dot(vt_ref[h * VT_ROWS:(h + 1) * VT_ROWS, 0:(c + 1) * bq], probs)
        return acc[:HEAD_DIM] * (1.0 / acc[HEAD_DIM:HEAD_DIM + 1])

    units = [(c, h) for c in range(nblk) for h in range(2)]
    ahead = 2
    pending = [score(*u) for u in units[:ahead]]
    outs = {}
    for i, (c, h) in enumerate(units):
        s = pending.pop(0)
        if i + ahead < len(units):
            pending.append(score(*units[i + ahead]))
        outs[h] = attend(c, h, s)
        if h == 1:
            o_ref[c * bq:(c + 1) * bq, :] = jnp.concatenate([outs[0], outs[1]], axis=0).T.astype(o_ref.dtype)


def moba_attention(qkv, batch, seq_len):
    n = qkv.shape[0]
    return pl.pallas_call(
        _moba_kernel,
        grid=(batch, N_PAIRS),
        in_specs=[pl.BlockSpec((seq_len, PAIR), lambda b, p: (b, p)),
                  pl.BlockSpec((seq_len, PAIR), lambda b, p: (b, N_PAIRS + p)),
                  pl.BlockSpec((seq_len, PAIR), lambda b, p: (b, 2 * N_PAIRS + p))],
        out_specs=pl.BlockSpec((seq_len, PAIR), lambda b, p: (b, p)),
        out_shape=jax.ShapeDtypeStruct((n, D_MODEL), BF16),
        scratch_shapes=[pltpu.VMEM((2 * VT_ROWS, seq_len), BF16),
                        pltpu.VMEM((2, seq_len, PAIR), BF16)],
        compiler_params=_params("parallel", "parallel"),
        name="moba_attention",
    )(qkv, qkv, qkv)


def _mem_xattn_kernel(y_ref, wy_ref, x_ref, g_ref, wq_ref, kv_ref, wo_ref, gf_ref, rw_ref, rb_ref,
                      o_ref, h_out, meta_out, cnt_out):
    x = x_ref[...] + _dot(y_ref[...], wy_ref[...])
    xn = _rms(x, g_ref[...]).astype(BF16)
    q = (_dot(xn, wq_ref[...]) * (1.0 / math.sqrt(MEM_HEAD_DIM))).astype(BF16)
    outs = []
    for h in range(MEM_HEADS):
        sl = slice(h * MEM_HEAD_DIM, (h + 1) * MEM_HEAD_DIM)
        s = _dot_nt(q[:, sl], kv_ref[:, sl])
        s = s - jnp.max(s, axis=-1, keepdims=True)
        e = jnp.exp(s)
        pr = e * (1.0 / jnp.sum(e, axis=-1, keepdims=True))
        vh = kv_ref[:, D_MODEL + h * MEM_HEAD_DIM:D_MODEL + (h + 1) * MEM_HEAD_DIM]
        outs.append(_dot(pr.astype(BF16), vh).astype(BF16))
    o = jnp.concatenate(outs, axis=1)
    x2 = x + _dot(o, wo_ref[...])
    o_ref[...] = x2
    _route(x2, gf_ref, rw_ref, rb_ref, h_out, meta_out, cnt_out)


def mem_cross_attention(y, wy, x, gain, wq, kv, wo, seq_len, ffn_gain, w_grp, b_grp, w_exp, b_exp):
    n, d = x.shape
    tm = MOE_TILE
    m = kv.shape[0] // (n // seq_len)
    per_seq = seq_len // tm
    wt = jnp.zeros((LANES, d), F32).at[0:MOE_GROUPS].set(w_grp.T).at[8:8 + MOE_EXPERTS].set(w_exp.T)
    bt = jnp.zeros((LANES, 1), F32).at[0:MOE_GROUPS, 0].set(b_grp).at[8:8 + MOE_EXPERTS, 0].set(b_exp)
    const = lambda shape: pl.BlockSpec(shape, lambda i: (0, 0))
    return pl.pallas_call(
        _mem_xattn_kernel,
        grid=(n // tm,),
        in_specs=[pl.BlockSpec((tm, d), lambda i: (i, 0)),
                  const((d, d)),
                  pl.BlockSpec((tm, d), lambda i: (i, 0)),
                  const((1, d)),
                  const((d, d)),
                  pl.BlockSpec((m, 2 * d), lambda i: (i // per_seq, 0)),
                  const((d, d)),
                  const((1, d)),
                  const((LANES, d)),
                  const((LANES, 1))],
        out_specs=[pl.BlockSpec((tm, d), lambda i: (i, 0)),
                   pl.BlockSpec((tm, d), lambda i: (i, 0)),
                   pl.BlockSpec((8, tm), lambda i: (0, i)),
                   pl.BlockSpec((MOE_EXPERTS, LANES), lambda i: (i, 0))],
        out_shape=[jax.ShapeDtypeStruct((n, d), F32),
                   jax.ShapeDtypeStruct((n, d), BF16),
                   jax.ShapeDtypeStruct((8, n), F32),
                   jax.ShapeDtypeStruct((n // tm * MOE_EXPERTS, LANES), F32)],
        compiler_params=_params("parallel"),
        name="mem_cross_attention",
    )(y, wy.astype(BF16), x, gain.reshape(1, d), wq.astype(BF16), kv, wo.astype(BF16),
      ffn_gain.reshape(1, d), wt, bt)


def _route(x, g_ref, w_ref, b_ref, h_out, meta_out, cnt_out):
    tm = x.shape[0]
    h2 = _rms(x, g_ref[...])
    hi = h2.astype(BF16)
    h_out[...] = hi
    lo = (h2 - hi.astype(F32)).astype(BF16)
    w_hi, w_lo = _split(w_ref[...])
    lg = _dot_nt(w_hi, hi) + _dot_nt(w_hi, lo) + _dot_nt(w_lo, hi) + b_ref[...]
    row = _iota((8, tm), 0).astype(F32)

    def first_argmax(val, vmax):
        return jnp.min(jnp.where(val == vmax, row, 8.0), axis=0, keepdims=True)

    gl = jnp.where(row < MOE_GROUPS, lg[0:8], -jnp.inf)
    gmax = jnp.max(gl, axis=0, keepdims=True)
    p_g = 1.0 / jnp.sum(jnp.exp(gl - gmax), axis=0, keepdims=True)
    gidx = first_argmax(gl, gmax)
    el = jnp.zeros((8, tm), F32)
    for g in range(MOE_GROUPS):
        el = el + jnp.where(gidx == g, lg[8 + 8 * g:16 + 8 * g], 0.0)
    ee = jnp.exp(el - jnp.max(el, axis=0, keepdims=True))
    pe = ee / jnp.sum(ee, axis=0, keepdims=True)
    p1 = jnp.max(pe, axis=0, keepdims=True)
    i1 = first_argmax(pe, p1)
    pe2 = jnp.where(row == i1, -1.0, pe)
    p2 = jnp.max(pe2, axis=0, keepdims=True)
    i2 = first_argmax(pe2, p2)
    e1 = gidx * MOE_EPG + i1
    e2 = gidx * MOE_EPG + i2
    gate1 = p_g * p1 / (p1 + p2)
    gate2 = p_g * p2 / (p1 + p2)

    erow = _iota((MOE_EXPERTS, tm), 0).astype(F32)
    oh1 = erow == e1
    oh2 = erow == e2
    oh = jnp.where(oh1 | oh2, 1.0, 0.0)
    before = jnp.where(_iota((tm, tm), 0) < _iota((tm, tm), 1), 1.0, 0.0).astype(BF16)
    cnt_before = _dot(oh.astype(BF16), before)
    lr1 = jnp.sum(jnp.where(oh1, cnt_before, 0.0), axis=0, keepdims=True)
    lr2 = jnp.sum(jnp.where(oh2, cnt_before, 0.0), axis=0, keepdims=True)
    zero = jnp.zeros((1, tm), F32)
    meta_out[...] = jnp.concatenate([e1, e2, gate1, gate2, lr1, lr2, zero, zero], axis=0)
    cnt_out[...] = jnp.broadcast_to(jnp.sum(oh, axis=1, keepdims=True), (MOE_EXPERTS, LANES))


def _chunk_copies(tab_ref, t, make_copy, act):
    base = t * RUN_TAB
    for ci, ck in enumerate(RUN_CHUNKS):
        first = base + len(RUN_CHUNKS) + ci * 2 * MOE_EXPERTS

        def body(i, _, first=first, ck=ck):
            src = tab_ref[first + 2 * i]
            dst = tab_ref[first + 2 * i + 1]
            act(make_copy(pl.multiple_of(src, MOE_ALIGN), pl.multiple_of(dst, MOE_ALIGN), ck))
            return 0

        lax.fori_loop(0, tab_ref[base + ci], body, 0)


def _run_copies(tab_ref, first, count, chunks, make_copy, act):
    def body(e, _):
        base = first + e * 3
        length = tab_ref[base]
        src = tab_ref[base + 1]
        dst = tab_ref[base + 2]
        for ck in chunks:
            @pl.when((length & ck) != 0)
            def _():
                off = length & (-2 * ck)
                act(make_copy(pl.multiple_of(src + off, MOE_ALIGN), pl.multiple_of(dst + off, MOE_ALIGN), ck))
        return 0
    lax.fori_loop(0, count, body, 0)


def _start(cp):
    cp.start()


def _wait(cp):
    cp.wait()


def _dispatch_kernel(tab_ref, h_ref, pos_ref, rows_hbm, sorted_ref, zero_ref, sem, zsem):
    t = pl.program_id(0)
    nt = pl.num_programs(0)
    slot = t % 2
    r = _iota((MOE_SORTED, MOE_TILE), 0)
    perm = jnp.where((r == pos_ref[0:1, :]) | (r == pos_ref[1:2, :]), 1.0, 0.0).astype(BF16)
    sorted_ref[slot] = _dot(perm, h_ref[...]).astype(BF16)

    def run_copy(s):
        def make(src, dst, ck):
            return pltpu.make_async_copy(sorted_ref.at[s, pl.ds(src, ck)], rows_hbm.at[pl.ds(dst, ck)], sem.at[s])
        return make

    def zero_copy(src, dst, ck):
        del src
        return pltpu.make_async_copy(zero_ref.at[pl.ds(0, ck)], rows_hbm.at[pl.ds(dst, ck)], zsem)

    _chunk_copies(tab_ref, t, run_copy(slot), _start)

    @pl.when(t > 0)
    def _():
        _chunk_copies(tab_ref, t - 1, run_copy(1 - slot), _wait)

    @pl.when(t == nt - 1)
    def _():
        zero_ref[...] = jnp.zeros_like(zero_ref)
        pad = nt * RUN_TAB
        _run_copies(tab_ref, pad, MOE_EXPERTS, PAD_CHUNKS, zero_copy, _start)
        spare = pad + MOE_EXPERTS * 3
        n_spare = tab_ref[spare]

        def spare_copy(i):
            dst = pl.multiple_of(tab_ref[spare + 2] + i * PAD_ROWS, PAD_ROWS)
            return zero_copy(0, dst, PAD_ROWS)

        lax.fori_loop(0, n_spare, lambda i, c: (_start(spare_copy(i)), c)[1], 0)
        _chunk_copies(tab_ref, t, run_copy(slot), _wait)
        _run_copies(tab_ref, pad, MOE_EXPERTS, PAD_CHUNKS, zero_copy, _wait)
        lax.fori_loop(0, n_spare, lambda i, c: (_wait(spare_copy(i)), c)[1], 0)


def moe_dispatch(tab, h2, pos, n_rows):
    n, d = h2.shape
    tm = MOE_TILE
    return pl.pallas_call(
        _dispatch_kernel,
        grid_spec=pltpu.PrefetchScalarGridSpec(
            num_scalar_prefetch=1,
            grid=(n // tm,),
            in_specs=[pl.BlockSpec((tm, d), lambda i, tab: (i, 0)),
                      pl.BlockSpec((8, tm), lambda i, tab: (0, i))],
            out_specs=pl.BlockSpec(memory_space=pl.ANY),
            scratch_shapes=[pltpu.VMEM((2, MOE_SORTED, d), BF16), pltpu.VMEM((PAD_ROWS, d), BF16),
                            pltpu.SemaphoreType.DMA((2,)), pltpu.SemaphoreType.DMA],
        ),
        out_shape=jax.ShapeDtypeStruct((n_rows, d), BF16),
        compiler_params=_params("arbitrary"),
        name="moe_dispatch",
    )(tab, h2, pos)


def _expert_kernel(be_ref, nu_ref, x_ref, w1_ref, w3_ref, w2_ref, y_ref, w1b, w3b, w2b):
    b = pl.program_id(0)
    used = b < nu_ref[0]
    new_expert = (b == 0) | (be_ref[b] != be_ref[jnp.maximum(b - 1, 0)])

    @pl.when(used & new_expert)
    def _():
        w1b[...] = w1_ref[0, 0].astype(BF16)
        w3b[...] = w3_ref[0, 0].astype(BF16)
        w2b[...] = w2_ref[0, 0].astype(BF16)

    @pl.when(used)
    def _():
        xb = x_ref[...]
        a = _dot(xb, w1b[...])
        c = _dot(xb, w3b[...])
        hid = (a * jax.nn.sigmoid(a) * c).astype(BF16)
        y_ref[...] = _dot(hid, w2b[...]).astype(y_ref.dtype)

    @pl.when(jnp.logical_not(used))
    def _():
        y_ref[...] = jnp.zeros_like(y_ref)


def moe_experts(block_expert, n_used, rows, layer, w1, w3, w2):
    n_rows = rows.shape[0]
    nb = n_rows // MOE_ROWS
    d, ff = w1.shape[2], w1.shape[3]

    def xmap(b, be, nu):
        return (jnp.minimum(b, nu[0] - 1), 0)

    def wmap(b, be, nu):
        return (layer, be[jnp.minimum(b, nu[0] - 1)], 0, 0)

    return pl.pallas_call(
        _expert_kernel,
        grid_spec=pltpu.PrefetchScalarGridSpec(
            num_scalar_prefetch=2,
            grid=(nb,),
            in_specs=[pl.BlockSpec((MOE_ROWS, d), xmap),
                      pl.BlockSpec((1, 1, d, ff), wmap),
                      pl.BlockSpec((1, 1, d, ff), wmap),
                      pl.BlockSpec((1, 1, ff, d), wmap)],
            out_specs=pl.BlockSpec((MOE_ROWS, d), lambda b, be, nu: (b, 0)),
            scratch_shapes=[pltpu.VMEM((d, ff), BF16), pltpu.VMEM((d, ff), BF16), pltpu.VMEM((ff, d), BF16)],
        ),
        out_shape=jax.ShapeDtypeStruct((n_rows, d), BF16),
        compiler_params=_params("arbitrary"),
        name="moe_experts",
    )(block_expert, n_used, rows, w1, w3, w2)


def _combine_kernel(final_norm, tab_ref, y_hbm, pos_ref, gate_ref, x_ref, g_ref, o_ref, ys_ref, sem):
    t = pl.program_id(0)
    nt = pl.num_programs(0)
    slot = t % 2

    def run_copy(s):
        def make(src, dst, ck):
            return pltpu.make_async_copy(y_hbm.at[pl.ds(dst, ck)], ys_ref.at[s, pl.ds(src, ck)], sem.at[s])
        return make

    @pl.when(t == 0)
    def _():
        ys_ref[...] = jnp.zeros_like(ys_ref)
        _chunk_copies(tab_ref, 0, run_copy(0), _start)

    @pl.when(t + 1 < nt)
    def _():
        _chunk_copies(tab_ref, t + 1, run_copy(1 - slot), _start)

    r = _iota((MOE_SORTED, MOE_TILE), 0)
    hit1 = r == pos_ref[0:1, :]
    hit2 = r == pos_ref[1:2, :]
    wgt = jnp.where(hit1, gate_ref[2:3, :], jnp.where(hit2, gate_ref[3:4, :], 0.0)).astype(BF16)
    _chunk_copies(tab_ref, t, run_copy(slot), _wait)
    out = x_ref[...] + _dot_tn(wgt, ys_ref[slot])
    if final_norm:
        out = _rms(out, g_ref[...])
    o_ref[...] = out


def moe_combine(tab, y, pos, meta, x, final_gain):
    n, d = x.shape
    tm = MOE_TILE
    final_norm = final_gain is not None
    gain = (final_gain if final_norm else jnp.ones((d,), F32)).reshape(1, d)
    return pl.pallas_call(
        functools.partial(_combine_kernel, final_norm),
        grid_spec=pltpu.PrefetchScalarGridSpec(
            num_scalar_prefetch=1,
            grid=(n // tm,),
            in_specs=[pl.BlockSpec(memory_space=pl.ANY),
                      pl.BlockSpec((8, tm), lambda i, tab: (0, i)),
                      pl.BlockSpec((8, tm), lambda i, tab: (0, i)),
                      pl.BlockSpec((tm, d), lambda i, tab: (i, 0)),
                      pl.BlockSpec((1, d), lambda i, tab: (0, 0))],
            out_specs=pl.BlockSpec((tm, d), lambda i, tab: (i, 0)),
            scratch_shapes=[pltpu.VMEM((2, MOE_SORTED, d), BF16), pltpu.SemaphoreType.DMA((2,))],
        ),
        out_shape=jax.ShapeDtypeStruct((n, d), F32),
        compiler_params=_params("arbitrary"),
        name="moe_combine",
    )(tab, y, pos, meta, x, gain)


def hierarchical_moe(x, h2, meta, cnt, layer, w1, w3, w2, final_gain):
    n, d = x.shape
    tm = MOE_TILE
    nt = n // tm

    cnt = cnt.reshape(nt, MOE_EXPERTS, LANES)[:, :, 0].astype(I32)
    run = (cnt + MOE_ALIGN - 1) // MOE_ALIGN * MOE_ALIGN
    src = jnp.cumsum(run, axis=1) - run
    before = jnp.cumsum(run, axis=0) - run
    total = jnp.sum(run, axis=0)
    padded = (total + MOE_ROWS - 1) // MOE_ROWS * MOE_ROWS
    pad_end = jnp.cumsum(padded)
    dst = (pad_end - padded)[None, :] + before
    max_rows = 2 * n + nt * MOE_EXPERTS * (MOE_ALIGN - 1) + MOE_EXPERTS * (MOE_ROWS - 1)
    nb = -(-max_rows // MOE_ROWS)
    sizes = jnp.array(RUN_CHUNKS, I32)[None, :, None]
    has = (run[:, None, :] & sizes) != 0
    above = run[:, None, :] & (-2 * sizes)
    place = jnp.where(has, jnp.cumsum(has.astype(I32), axis=2) - 1, -1)
    front = place[..., None] == jnp.arange(MOE_EXPERTS, dtype=I32)
    c_src = jnp.sum(jnp.where(front, (src[:, None, :] + above)[..., None], 0), axis=2)
    c_dst = jnp.sum(jnp.where(front, (dst[:, None, :] + above)[..., None], 0), axis=2)
    chunks = jnp.stack([c_src, c_dst], axis=-1).reshape(nt, -1)
    runs = jnp.concatenate([jnp.sum(has.astype(I32), axis=2), chunks], axis=1).reshape(-1)
    zero = jnp.zeros((MOE_EXPERTS,), I32)
    pads = jnp.stack([padded - total, zero, pad_end - padded + total], axis=-1)
    spare = jnp.stack([(nb * MOE_ROWS - pad_end[-1]) // PAD_ROWS, zero[0], pad_end[-1]])
    tab = jnp.concatenate([runs, pads.reshape(-1), spare]).astype(I32)
    n_used = (pad_end[-1] // MOE_ROWS).astype(I32).reshape(1)
    block_start = jnp.arange(nb, dtype=I32) * MOE_ROWS
    block_expert = jnp.minimum(
        jnp.sum((pad_end[None, :] <= block_start[:, None]).astype(I32), axis=1), MOE_EXPERTS - 1).astype(I32)
    e = meta[0:2].astype(I32).reshape(2, nt, tm, 1)
    hit = e == jnp.arange(MOE_EXPERTS, dtype=I32)
    pos = jnp.sum(jnp.where(hit, src[None, :, None, :], 0), axis=-1).reshape(2, n) + meta[4:6].astype(I32)
    pos8 = jnp.zeros((8, n), I32).at[0:2].set(pos)

    rows = moe_dispatch(tab, h2, pos8, nb * MOE_ROWS)
    y = moe_experts(block_expert, n_used, rows, layer, w1, w3, w2)
    return moe_combine(tab, y, pos8, meta, x, final_gain)


def kernel(x, mem, positions, ln_mix, ln_mem, ln_memkv, ln_ffn, rw_mu, rw_w0, rw_w1, rw_w2, rw_a0, rw_a1, rw_a2, rw_g1, rw_g2, rw_kk, rw_ka, rw_rk, rw_wrkv, rw_lnx_g, rw_lnx_b, rw_wo, mb_wqkv, mb_wo, mx_wq, mx_wkv, mx_wo, moe_wg, moe_bg, moe_we, moe_be, moe_w1, moe_w3, moe_w2, ln_f):
    B, T, C = x.shape
    n = B * T
    depth = ln_mix.shape[0]
    xf = x.reshape(n, C)
    memf = mem.reshape(-1, C)
    for i in range(depth):
        j = i // 2
        if i % 2 == 0:
            r, k, v, kk, akk, lw, g = rwkv_mix(xf, T, ln_mix[i], rw_mu[j], rw_w0[j], rw_w1[j], rw_w2[j],
                                               rw_a0[j], rw_a1[j], rw_a2[j], rw_g1[j], rw_g2[j],
                                               rw_kk[j], rw_ka[j], rw_wrkv[j])
            y = rwkv_recurrence(r, k, v, kk, akk, lw, g, rw_lnx_g[j], rw_lnx_b[j], rw_rk[j], B, T)
            w_mix_o = rw_wo[j]
        else:
            qkv = qkv_rope(xf, ln_mix[i], positions, mb_wqkv[j])
            y = moba_attention(qkv, B, T)
            w_mix_o = mb_wo[j]
        kv = norm_linear(memf, ln_memkv[i], mx_wkv[i].astype(BF16))
        xf, h2, meta, cnt = mem_cross_attention(y, w_mix_o, xf, ln_mem[i], mx_wq[i], kv, mx_wo[i], T,
                                                ln_ffn[i], moe_wg[i], moe_bg[i], moe_we[i], moe_be[i])
        xf = hierarchical_moe(xf, h2, meta, cnt, i, moe_w1, moe_w3, moe_w2, ln_f if i == depth - 1 else None)
    return xf.reshape(B, T, C)
```

```python
import functools
import math

import jax
import jax.numpy as jnp
from jax import lax
from jax.experimental import pallas as pl
from jax.experimental.pallas import tpu as pltpu

F32 = jnp.float32
BF16 = jnp.bfloat16
I32 = jnp.int32

D_MODEL = 1024
HEAD_DIM = 64
PAIR = 2 * HEAD_DIM
N_PAIRS = D_MODEL // PAIR
RWKV_GN_EPS = 64e-5
RWKV_CHUNK = 64
RWKV_CHUNKS_PER_STEP = 8
MOBA_BLOCK = 256
MOBA_TOPK = 3
VT_ROWS = HEAD_DIM + 16
ROPE_THETA = 10000.0
MEM_HEADS = 4
MEM_HEAD_DIM = D_MODEL // MEM_HEADS
MOE_GROUPS = 4
MOE_EPG = 8
MOE_EXPERTS = MOE_GROUPS * MOE_EPG
RMS_EPS = 1e-6
NEG_INF = -1e30

LANES = 128
ROW_ALIGN = 8
MOE_TILE = 512
MOE_ROWS = 512
MOE_ALIGN = 16
MOE_SORTED = -(-(2 * MOE_TILE + MOE_EXPERTS * (MOE_ALIGN - 1)) // MOE_ALIGN) * MOE_ALIGN
RUN_CHUNKS = tuple(MOE_ALIGN << s for s in range((MOE_TILE // MOE_ALIGN).bit_length() - 1, -1, -1))
RUN_TAB = len(RUN_CHUNKS) * (1 + 2 * MOE_EXPERTS)
PAD_ROWS = MOE_ROWS // 2
PAD_CHUNKS = tuple(MOE_ALIGN << s for s in range((PAD_ROWS // MOE_ALIGN).bit_length() - 1, -1, -1))
VMEM_LIMIT = 56 * 2 ** 20


def _params(*sem):
    return pltpu.CompilerParams(dimension_semantics=sem, vmem_limit_bytes=VMEM_LIMIT)


def _iota(shape, dim):
    return lax.broadcasted_iota(I32, shape, dim)


def _dot(a, b):
    return jnp.dot(a, b, preferred_element_type=F32)


def _dot_nt(a, b):
    return lax.dot_general(a, b, (((1,), (1,)), ((), ())), preferred_element_type=F32)


def _dot_tn(a, b):
    return lax.dot_general(a, b, (((0,), (0,)), ((), ())), preferred_element_type=F32)


def _split(x):
    hi = x.astype(BF16)
    lo = (x - hi.astype(F32)).astype(BF16)
    return hi, lo


def _rms(x, g):
    return x * lax.rsqrt(jnp.mean(x * x, axis=-1, keepdims=True) + RMS_EPS) * g


def _norm_linear_kernel(x_ref, g_ref, w_ref, o_ref):
    xn = _rms(x_ref[...], g_ref[...]).astype(BF16)
    o_ref[...] = _dot(xn, w_ref[...]).astype(o_ref.dtype)


def norm_linear(x, g, w, tm=256):
    n, k = x.shape
    dout = w.shape[1]
    return pl.pallas_call(
        _norm_linear_kernel,
        grid=(n // tm,),
        in_specs=[pl.BlockSpec((tm, k), lambda i: (i, 0)),
                  pl.BlockSpec((1, k), lambda i: (0, 0)),
                  pl.BlockSpec((k, dout), lambda i: (0, 0))],
        out_specs=pl.BlockSpec((tm, dout), lambda i: (i, 0)),
        out_shape=jax.ShapeDtypeStruct((n, dout), BF16),
        compiler_params=_params("parallel"),
        name="norm_linear",
    )(x, g.reshape(1, k), w)


def _head_sum(x_sq):
    r = _iota((PAIR, PAIR), 0) // HEAD_DIM
    c = _iota((PAIR, PAIR), 1) // HEAD_DIM
    bd = jnp.where(r == c, 1.0, 0.0).astype(BF16)
    hi, lo = _split(x_sq)
    return _dot(hi, bd) + _dot(lo, bd)


def _rwkv_mix_kernel(seq_len, x_ref, xp_ref, g_ref, mu_ref, vec_ref, wr_ref, wk_ref, wv_ref,
                     l1_ref, w2_ref, a2_ref, g2_ref,
                     r_out, k_out, v_out, kk_out, akk_out, lw_out, g_out):
    i = pl.program_id(0)
    tm = x_ref.shape[0]
    gain = g_ref[...]
    h = _rms(x_ref[...], gain)
    hp = _rms(xp_ref[...], gain)[ROW_ALIGN - 1:ROW_ALIGN, :]
    hp = jnp.where((i * tm) % seq_len == 0, 0.0, hp)
    hs = pltpu.roll(h, 1, 0)
    hs = jnp.where(_iota((tm, 1), 0) == 0, hp, hs)
    hb = h.astype(BF16)
    dxb = (hs - h).astype(BF16)
    mub = mu_ref[...].astype(BF16)

    def mix(s):
        return hb + dxb * mub[s:s + 1, :]

    vec = vec_ref[...]
    w0, a0, k_k, k_a = vec[0:1], vec[1:2], vec[2:3], vec[3:4]
    l1 = l1_ref[...]
    tw = _dot(mix(1), l1[:, 0:64])
    ta = _dot(mix(4), l1[:, 64:128])
    tg = _dot(mix(5), l1[:, 128:256])
    r = _dot(mix(0), wr_ref[...])
    k = _dot(mix(2), wk_ref[...])
    v = _dot(mix(3), wv_ref[...])
    lw = -math.exp(-0.5) * jax.nn.sigmoid(w0 + _dot(jnp.tanh(tw).astype(BF16), w2_ref[...]))
    a = jax.nn.sigmoid(a0 + _dot(ta.astype(BF16), a2_ref[...]))
    g = _dot(jax.nn.sigmoid(tg).astype(BF16), g2_ref[...])

    kk = k * k_k
    for p in range(N_PAIRS):
        sl = slice(p * PAIR, (p + 1) * PAIR)
        kkp = kk[:, sl]
        kkn = kkp * lax.rsqrt(jnp.maximum(_head_sum(kkp * kkp), 1e-24))
        kk_out[:, sl] = kkn.astype(kk_out.dtype)
        akk_out[:, sl] = (kkn * a[:, sl]).astype(akk_out.dtype)
    r_out[...] = r.astype(r_out.dtype)
    k_out[...] = (k * (1.0 + (a - 1.0) * k_a)).astype(k_out.dtype)
    v_out[...] = v.astype(v_out.dtype)
    lw_out[...] = lw
    g_out[...] = g.astype(g_out.dtype)


def rwkv_mix(x, seq_len, gain, mu, w0, w1, w2, a0, a1, a2, g1, g2, k_k, k_a, w_rkv, tm=512):
    n, d = x.shape
    mu8 = jnp.zeros((8, d), F32).at[:6].set(mu)
    vec = jnp.zeros((8, d), F32).at[0].set(w0).at[1].set(a0).at[2].set(k_k).at[3].set(k_a)
    l1 = jnp.concatenate([w1, a1, g1], axis=1).astype(BF16)
    wb = w_rkv.astype(BF16)
    row = pl.BlockSpec((tm, d), lambda i: (i, 0))
    full = lambda a: pl.BlockSpec(a.shape, lambda i: (0,) * a.ndim)
    args = (x, x, gain.reshape(1, d), mu8, vec, wb[0], wb[1], wb[2], l1,
            w2.astype(BF16), a2.astype(BF16), g2.astype(BF16))
    in_specs = [row, pl.BlockSpec((ROW_ALIGN, d), lambda i: (jnp.maximum(i * (tm // ROW_ALIGN) - 1, 0), 0))]
    in_specs += [full(a) for a in args[2:]]
    outs = [jax.ShapeDtypeStruct((n, d), BF16)] * 5 + [jax.ShapeDtypeStruct((n, d), F32),
                                                       jax.ShapeDtypeStruct((n, d), BF16)]
    return pl.pallas_call(
        functools.partial(_rwkv_mix_kernel, seq_len),
        grid=(n // tm,),
        in_specs=in_specs,
        out_specs=[row] * 7,
        out_shape=outs,
        compiler_params=_params("parallel"),
        name="rwkv_mix",
    )(*args)


def _rwkv_rec_kernel(r_ref, k_ref, v_ref, kk_ref, akk_ref, lw_ref, g_ref, vec_ref, y_ref, s_ref):
    c = pl.program_id(1)
    L = RWKV_CHUNK

    @pl.when(c == 0)
    def _():
        s_ref[...] = jnp.zeros_like(s_ref)

    lane = _iota((1, PAIR), 1)
    m0 = lane < HEAD_DIM
    ri = _iota((2 * L, 2 * L), 0)
    ci = _iota((2 * L, 2 * L), 1)
    same = (ri // L) == (ci // L)
    strict = same & (ci < ri)
    incl = same & (ci <= ri)
    eye = jnp.where(ri == ci, 1.0, 0.0)
    tri = jnp.where(_iota((L, L), 1) <= _iota((L, L), 0), 1.0, 0.0).astype(BF16)
    rb = _iota((PAIR, PAIR), 0) // HEAD_DIM
    cb = _iota((PAIR, PAIR), 1) // HEAD_DIM
    bd = rb == cb

    def stack(x):
        return jnp.concatenate([jnp.where(m0, x, 0.0), jnp.where(m0, 0.0, x)], axis=0)

    def fold(x):
        return x[:L] + x[L:]

    def head_mean(x):
        s0 = jnp.sum(jnp.where(m0, x, 0.0), axis=-1, keepdims=True)
        s1 = jnp.sum(jnp.where(m0, 0.0, x), axis=-1, keepdims=True)
        return jnp.where(m0, s0, s1) * (1.0 / HEAD_DIM)

    vec = vec_ref[...]
    n_sub = r_ref.shape[0] // L
    pairs = range(n_sub * N_PAIRS)
    rws = [slice((q // N_PAIRS) * L, (q // N_PAIRS + 1) * L) for q in pairs]
    sls = [slice((q % N_PAIRS) * PAIR, (q % N_PAIRS + 1) * PAIR) for q in pairs]
    lw = [lw_ref[rws[p], sls[p]] for p in pairs]
    cum = []
    for p in pairs:
        lw_hi, lw_lo = _split(lw[p])
        cum.append(_dot(tri, lw_hi) + _dot(tri, lw_lo))
    dec_all, x_kap, x_r, a_t, k_t, sc = [], [], [], [], [], []
    for p in pairs:
        rw, sl = rws[p], sls[p]
        dec = jnp.exp(cum[p])
        inv = jnp.exp(-cum[p])
        dec_prev = jnp.exp(cum[p] - lw[p])
        dec_all.append(dec[L - 1:L, :])
        x_kap.append(stack(kk_ref[rw, sl].astype(F32) * dec_prev).astype(BF16))
        x_r.append(stack(r_ref[rw, sl].astype(F32) * dec))
        a_t.append((akk_ref[rw, sl].astype(F32) * inv).astype(BF16))
        k_t.append((k_ref[rw, sl].astype(F32) * inv).astype(BF16))
        xs = jnp.concatenate([x_kap[p], x_r[p].astype(BF16)], axis=0)
        ys = jnp.concatenate([a_t[p], k_t[p]], axis=0)
        sc.append(_dot_nt(xs, ys))

    def both_heads(first, second, keep):
        upper_left = keep & (ri < L)
        lower_right = keep & (ri >= L)
        return jnp.where(upper_left, first, jnp.where(lower_right, second, 0.0))

    mp, m_kk = [], []
    for p in pairs:
        top = sc[p][:2 * L]
        top_r = pltpu.roll(top, L, 1)
        mp.append(-both_heads(top, top_r, strict))
        m_kk.append(both_heads(top_r, top, strict).astype(BF16))
    s_ra, s_rk = [], []
    for p in pairs:
        bot = sc[p][2 * L:]
        bot_r = pltpu.roll(bot, L, 1)
        s_ra.append(both_heads(bot, bot_r, incl).astype(BF16))
        s_rk.append(both_heads(bot_r, bot, incl).astype(BF16))
    v_st = [stack(v_ref[rws[p], sls[p]].astype(F32)).astype(BF16) for p in pairs]
    bo = [_dot(jnp.concatenate([m_kk[p], s_rk[p]], axis=0), v_st[p]) for p in pairs]
    b1 = [bo[p][:2 * L].astype(BF16) for p in pairs]
    o1 = [bo[p][2 * L:] for p in pairs]

    t_inv = [eye + mp[p] for p in pairs]
    mpb = [mp[p].astype(BF16) for p in pairs]
    mpb = [_dot(mpb[p], mpb[p]).astype(BF16) for p in pairs]
    steps = int(math.log2(L)) - 1
    for j in range(steps):
        if j + 1 < steps:
            both = [_dot(mpb[p], jnp.concatenate([mpb[p], t_inv[p].astype(BF16)], axis=1)) for p in pairs]
            mpb = [both[p][:, :2 * L].astype(BF16) for p in pairs]
            t_inv = [t_inv[p] + both[p][:, 2 * L:] for p in pairs]
        else:
            t_inv = [t_inv[p] + _dot(mpb[p], t_inv[p].astype(BF16)) for p in pairs]

    wu = [_dot(t_inv[p].astype(BF16), jnp.concatenate([x_kap[p], b1[p]], axis=1)).astype(BF16)
          for p in pairs]
    corr = [_dot(s_ra[p], wu[p]) for p in pairs]
    g_m, h0t, r_hat, o0 = [], [], [], []
    for p in pairs:
        r_hat.append(fold(x_r[p] - corr[p][:, :PAIR]).astype(BF16))
        o0.append(fold(o1[p] - corr[p][:, PAIR:]))
        w_f = fold(wu[p][:, :PAIR])
        u0_f = fold(wu[p][:, PAIR:])
        g_m.append(jnp.where(bd, _dot_tn(a_t[p], w_f), 0.0).astype(BF16))
        vu = jnp.concatenate([v_ref[rws[p], sls[p]], -u0_f], axis=0)
        h0t.append(jnp.where(bd, _dot_tn(vu, jnp.concatenate([k_t[p], a_t[p]], axis=0)), 0.0))
    o = []
    for p in pairs:
        hp = p % N_PAIRS
        s = s_ref[hp]
        sb = s.astype(BF16)
        o.append(_dot_nt(r_hat[p], sb) + o0[p])
        s_ref[hp] = (s - _dot_nt(sb, g_m[p]) + h0t[p]) * dec_all[p]
    for p in pairs:
        rw, sl = rws[p], sls[p]
        mean = head_mean(o[p])
        cen = o[p] - mean
        var = head_mean(cen * cen)
        gn = cen * lax.rsqrt(var + RWKV_GN_EPS) * vec[0:1, sl] + vec[1:2, sl]
        r = r_ref[rw, sl].astype(F32)
        k = k_ref[rw, sl].astype(F32)
        bonus = head_mean(r * k * vec[2:3, sl]) * HEAD_DIM * v_ref[rw, sl].astype(F32)
        y_ref[rw, sl] = ((gn + bonus) * g_ref[rw, sl].astype(F32)).astype(y_ref.dtype)


def rwkv_recurrence(r, k, v, kk, akk, lw, g, lnx_g, lnx_b, r_k, batch, seq_len):
    n, d = r.shape
    rows = RWKV_CHUNK * RWKV_CHUNKS_PER_STEP
    nc = seq_len // rows
    vec = jnp.zeros((8, d), F32).at[0].set(lnx_g).at[1].set(lnx_b).at[2].set(r_k.reshape(d))
    blk = pl.BlockSpec((rows, d), lambda b, c: (b * nc + c, 0))
    return pl.pallas_call(
        _rwkv_rec_kernel,
        grid=(batch, nc),
        in_specs=[blk] * 7 + [pl.BlockSpec((8, d), lambda b, c: (0, 0))],
        out_specs=blk,
        out_shape=jax.ShapeDtypeStruct((n, d), BF16),
        scratch_shapes=[pltpu.VMEM((N_PAIRS, PAIR, PAIR), F32)],
        compiler_params=_params("parallel", "arbitrary"),
        name="rwkv_recurrence",
    )(r, k, v, kk, akk, lw, g, vec)


def _qkv_rope_kernel(x_ref, g_ref, pos_ref, inv_ref, w_ref, o_ref):
    d = x_ref.shape[1]
    xn = _rms(x_ref[...], g_ref[...]).astype(BF16)
    y = [_dot(xn, w_ref[:, part * d:(part + 1) * d]) for part in range(3)]
    ang = pos_ref[...].astype(F32) * inv_ref[...]
    first = (_iota((1, PAIR), 1) % HEAD_DIM) < HEAD_DIM // 2
    sn = jnp.sin(ang)
    cos = jnp.cos(ang)
    sin_lo = jnp.where(first, -sn, 0.0)
    sin_hi = jnp.where(first, 0.0, sn)
    for part in range(2):
        for p in range(N_PAIRS):
            t = y[part][:, p * PAIR:(p + 1) * PAIR]
            rot = (t * cos + pltpu.roll(t, PAIR - HEAD_DIM // 2, 1) * sin_lo
                   + pltpu.roll(t, HEAD_DIM // 2, 1) * sin_hi)
            o_ref[:, part * d + p * PAIR:part * d + (p + 1) * PAIR] = rot.astype(o_ref.dtype)
    o_ref[:, 2 * d:] = y[2].astype(o_ref.dtype)


def qkv_rope(x, gain, positions, w_qkv, tm=512):
    n, d = x.shape
    half = HEAD_DIM // 2
    inv = ROPE_THETA ** (-jnp.arange(half, dtype=F32) * 2.0 / HEAD_DIM)
    inv128 = jnp.tile(inv, PAIR // half).reshape(1, PAIR)
    col_scale = jnp.where(jnp.arange(3 * d) < d, math.log2(math.e) / math.sqrt(HEAD_DIM), 1.0).astype(F32)
    return pl.pallas_call(
        _qkv_rope_kernel,
        grid=(n // tm,),
        in_specs=[pl.BlockSpec((tm, d), lambda i: (i, 0)),
                  pl.BlockSpec((1, d), lambda i: (0, 0)),
                  pl.BlockSpec((tm, 1), lambda i: (i, 0)),
                  pl.BlockSpec((1, PAIR), lambda i: (0, 0)),
                  pl.BlockSpec((d, 3 * d), lambda i: (0, 0))],
        out_specs=pl.BlockSpec((tm, 3 * d), lambda i: (i, 0)),
        out_shape=jax.ShapeDtypeStruct((n, 3 * d), BF16),
        compiler_params=_params("parallel"),
        name="qkv_rope",
    )(x, gain.reshape(1, d), positions.reshape(n, 1), inv128, (w_qkv * col_scale).astype(BF16))


def _moba_kernel(q_ref, k_ref, v_ref, o_ref, vt_ref, ka_ref, qa_ref):
    nblk = k_ref.shape[0] // MOBA_BLOCK
    bq = MOBA_BLOCK
    km_rows = 16
    lane = _iota((1, PAIR), 1)
    own = (lane < HEAD_DIM, lane >= HEAD_DIM)
    spare = (HEAD_DIM, 0)

    means = []
    ones_row = jnp.where(_iota((VT_ROWS - HEAD_DIM, bq), 0) == 0, 1.0, 0.0).astype(BF16)
    for n in range(nblk):
        cols = slice(n * bq, (n + 1) * bq)
        kb = k_ref[cols, :]
        means.append(jnp.mean(kb.astype(F32), axis=0, keepdims=True))
        vt = v_ref[cols, :].astype(F32).T.astype(BF16)
        for h in range(2):
            vt_ref[h * VT_ROWS:h * VT_ROWS + HEAD_DIM, cols] = vt[h * HEAD_DIM:(h + 1) * HEAD_DIM]
            vt_ref[h * VT_ROWS + HEAD_DIM:(h + 1) * VT_ROWS, cols] = ones_row
            marker = jnp.where(lane == spare[h] + n, 1.0, 0.0).astype(BF16)
            ka_ref[h, cols, :] = jnp.where(own[h], kb, marker)
    km_hi, km_lo = _split(jnp.concatenate(means + [jnp.zeros((km_rows - nblk, PAIR), F32)], axis=0))

    blk = _iota((km_rows, bq), 0)
    causal = _iota((bq, bq), 0) <= _iota((bq, bq), 1)
    eye = jnp.where(_iota((bq, bq), 0) == _iota((bq, bq), 1), 1.0, 0.0).astype(BF16)

    q_all = q_ref[...]
    gates = []
    for h in range(2):
        qm = jnp.where(own[h], q_all, jnp.zeros_like(q_all))
        gates.append(_dot_nt(km_hi, qm) + _dot_nt(km_lo, qm))
    for c in range(nblk):
        past = blk < c
        for h in range(2):
            gate = jnp.where(past, gates[h][:, c * bq:(c + 1) * bq], NEG_INF)
            rank = jnp.zeros(gate.shape, F32)
            for m in range(c):
                row = gate[m:m + 1, :]
                tie = jnp.where(m < blk, 1.0, 0.0)
                rank = rank + jnp.where(row > gate, 1.0, jnp.where(row == gate, tie, 0.0))
            drop = jnp.where(past & (rank >= MOBA_TOPK), 1.0, 0.0)
            pad = [jnp.zeros((HEAD_DIM - km_rows, bq), F32)]
            other = [jnp.zeros((HEAD_DIM, bq), F32)]
            rows = (other + [drop] + pad) if spare[h] else ([drop] + pad + other)
            dropped = _dot_nt(eye, jnp.concatenate(rows, axis=0).astype(BF16))
            q = q_ref[c * bq:(c + 1) * bq, :]
            qa_ref[h, c * bq:(c + 1) * bq, :] = jnp.where(
                own[h], q, jnp.where(dropped > 0.5, NEG_INF, 0.0).astype(BF16))

    def score(c, h):
        return _dot_nt(ka_ref[h, 0:(c + 1) * bq, :], qa_ref[h, c * bq:(c + 1) * bq, :])

    def attend(c, h, s):
        parts = [s[n * bq:(n + 1) * bq] for n in range(c)]
        parts.append(jnp.where(causal, s[c * bq:], NEG_INF))
        top = parts[0].max(axis=0, keepdims=True)
        for part in parts[1:]:
            top = jnp.maximum(top, part.max(axis=0, keepdims=True))
        probs = jnp.concatenate([jnp.exp2((part - top).astype(BF16)) for part in parts], axis=0)
        acc = _dot(vt_ref[h * VT_ROWS:(h + 1) * VT_ROWS, 0:(c + 1) * bq], probs)
        return acc[:HEAD_DIM] * (1.0 / acc[HEAD_DIM:HEAD_DIM + 1])

    units = [(c, h) for c in range(nblk) for h in range(2)]
    ahead = 2
    pending = [score(*u) for u in units[:ahead]]
    outs = {}
    for i, (c, h) in enumerate(units):
        s = pending.pop(0)
        if i + ahead < len(units):
            pending.append(score(*units[i + ahead]))
        outs[h] = attend(c, h, s)
        if h == 1:
            o_ref[c * bq:(c + 1) * bq, :] = jnp.concatenate([outs[0], outs[1]], axis=0).T.astype(o_ref.dtype)


def moba_attention(qkv, batch, seq_len):
    n = qkv.shape[0]
    return pl.pallas_call(
        _moba_kernel,
        grid=(batch, N_PAIRS),
        in_specs=[pl.BlockSpec((seq_len, PAIR), lambda b, p: (b, p)),
                  pl.BlockSpec((seq_len, PAIR), lambda b, p: (b, N_PAIRS + p)),
                  pl.BlockSpec((seq_len, PAIR), lambda b, p: (b, 2 * N_PAIRS + p))],
        out_specs=pl.BlockSpec((seq_len, PAIR), lambda b, p: (b, p)),
        out_shape=jax.ShapeDtypeStruct((n, D_MODEL), BF16),
        scratch_shapes=[pltpu.VMEM((2 * VT_ROWS, seq_len), BF16),
                        pltpu.VMEM((2, seq_len, PAIR), BF16),
                        pltpu.VMEM((2, seq_len, PAIR), BF16)],
        compiler_params=_params("parallel", "parallel"),
        name="moba_attention",
    )(qkv, qkv, qkv)


def _mem_xattn_kernel(y_ref, wy_ref, x_ref, g_ref, wq_ref, kv_ref, wo_ref, gf_ref, rw_ref, rb_ref,
                      o_ref, h_out, meta_out, cnt_out):
    x = x_ref[...] + _dot(y_ref[...], wy_ref[...])
    xn = _rms(x, g_ref[...]).astype(BF16)
    q = (_dot(xn, wq_ref[...]) * (1.0 / math.sqrt(MEM_HEAD_DIM))).astype(BF16)
    heads = [slice(h * MEM_HEAD_DIM, (h + 1) * MEM_HEAD_DIM) for h in range(MEM_HEADS)]
    scores = [_dot_nt(q[:, sl], kv_ref[:, sl]) for sl in heads]
    outs = []
    for sl, s in zip(heads, scores):
        e = jnp.exp(s - jnp.max(s, axis=-1, keepdims=True))
        pr = e * (1.0 / jnp.sum(e, axis=-1, keepdims=True))
        outs.append(_dot(pr.astype(BF16), kv_ref[:, D_MODEL + sl.start:D_MODEL + sl.stop]).astype(BF16))
    o = jnp.concatenate(outs, axis=1)
    x2 = x + _dot(o, wo_ref[...])
    o_ref[...] = x2
    _route(x2, gf_ref, rw_ref, rb_ref, h_out, meta_out, cnt_out)


def mem_cross_attention(y, wy, x, gain, wq, kv, wo, seq_len, ffn_gain, w_grp, b_grp, w_exp, b_exp):
    n, d = x.shape
    tm = MOE_TILE
    m = kv.shape[0] // (n // seq_len)
    per_seq = seq_len // tm
    wt = jnp.zeros((LANES, d), F32).at[0:MOE_GROUPS].set(w_grp.T).at[8:8 + MOE_EXPERTS].set(w_exp.T)
    bt = jnp.zeros((LANES, 1), F32).at[0:MOE_GROUPS, 0].set(b_grp).at[8:8 + MOE_EXPERTS, 0].set(b_exp)
    const = lambda shape: pl.BlockSpec(shape, lambda i: (0, 0))
    return pl.pallas_call(
        _mem_xattn_kernel,
        grid=(n // tm,),
        in_specs=[pl.BlockSpec((tm, d), lambda i: (i, 0)),
                  const((d, d)),
                  pl.BlockSpec((tm, d), lambda i: (i, 0)),
                  const((1, d)),
                  const((d, d)),
                  pl.BlockSpec((m, 2 * d), lambda i: (i // per_seq, 0)),
                  const((d, d)),
                  const((1, d)),
                  const((LANES, d)),
                  const((LANES, 1))],
        out_specs=[pl.BlockSpec((tm, d), lambda i: (i, 0)),
                   pl.BlockSpec((tm, d), lambda i: (i, 0)),
                   pl.BlockSpec((8, tm), lambda i: (0, i)),
                   pl.BlockSpec((MOE_EXPERTS, LANES), lambda i: (i, 0))],
        out_shape=[jax.ShapeDtypeStruct((n, d), F32),
                   jax.ShapeDtypeStruct((n, d), BF16),
                   jax.ShapeDtypeStruct((8, n), F32),
                   jax.ShapeDtypeStruct((n // tm * MOE_EXPERTS, LANES), F32)],
        compiler_params=_params("parallel"),
        name="mem_cross_attention",
    )(y, wy.astype(BF16), x, gain.reshape(1, d), wq.astype(BF16), kv, wo.astype(BF16),
      ffn_gain.reshape(1, d), wt, bt)


def _route(x, g_ref, w_ref, b_ref, h_out, meta_out, cnt_out):
    tm = x.shape[0]
    h2 = _rms(x, g_ref[...])
    hi = h2.astype(BF16)
    h_out[...] = hi
    lo = (h2 - hi.astype(F32)).astype(BF16)
    w_hi, w_lo = _split(w_ref[...])
    lg = _dot_nt(w_hi, hi) + _dot_nt(w_hi, lo) + _dot_nt(w_lo, hi) + b_ref[...]
    row = _iota((8, tm), 0).astype(F32)

    def first_argmax(val, vmax):
        return jnp.min(jnp.where(val == vmax, row, 8.0), axis=0, keepdims=True)

    gl = jnp.where(row < MOE_GROUPS, lg[0:8], -jnp.inf)
    gmax = jnp.max(gl, axis=0, keepdims=True)
    p_g = 1.0 / jnp.sum(jnp.exp(gl - gmax), axis=0, keepdims=True)
    gidx = first_argmax(gl, gmax)
    el = jnp.zeros((8, tm), F32)
    for g in range(MOE_GROUPS):
        el = el + jnp.where(gidx == g, lg[8 + 8 * g:16 + 8 * g], 0.0)
    ee = jnp.exp(el - jnp.max(el, axis=0, keepdims=True))
    pe = ee / jnp.sum(ee, axis=0, keepdims=True)
    p1 = jnp.max(pe, axis=0, keepdims=True)
    i1 = first_argmax(pe, p1)
    pe2 = jnp.where(row == i1, -1.0, pe)
    p2 = jnp.max(pe2, axis=0, keepdims=True)
    i2 = first_argmax(pe2, p2)
    e1 = gidx * MOE_EPG + i1
    e2 = gidx * MOE_EPG + i2
    gate1 = p_g * p1 / (p1 + p2)
    gate2 = p_g * p2 / (p1 + p2)

    erow = _iota((MOE_EXPERTS, tm), 0).astype(F32)
    oh1 = erow == e1
    oh2 = erow == e2
    oh = jnp.where(oh1 | oh2, 1.0, 0.0)
    before = jnp.where(_iota((tm, tm), 0) < _iota((tm, tm), 1), 1.0, 0.0).astype(BF16)
    cnt_before = _dot(oh.astype(BF16), before)
    lr1 = jnp.sum(jnp.where(oh1, cnt_before, 0.0), axis=0, keepdims=True)
    lr2 = jnp.sum(jnp.where(oh2, cnt_before, 0.0), axis=0, keepdims=True)
    zero = jnp.zeros((1, tm), F32)
    meta_out[...] = jnp.concatenate([e1, e2, gate1, gate2, lr1, lr2, zero, zero], axis=0)
    cnt_out[...] = jnp.broadcast_to(jnp.sum(oh, axis=1, keepdims=True), (MOE_EXPERTS, LANES))


def _chunk_copies(tab_ref, t, make_copy, act):
    base = t * RUN_TAB
    for ci, ck in enumerate(RUN_CHUNKS):
        first = base + len(RUN_CHUNKS) + ci * 2 * MOE_EXPERTS

        def body(i, _, first=first, ck=ck):
            src = tab_ref[first + 2 * i]
            dst = tab_ref[first + 2 * i + 1]
            act(make_copy(pl.multiple_of(src, MOE_ALIGN), pl.multiple_of(dst, MOE_ALIGN), ck))
            return 0

        lax.fori_loop(0, tab_ref[base + ci], body, 0)


def _run_copies(tab_ref, first, count, chunks, make_copy, act):
    def body(e, _):
        base = first + e * 3
        length = tab_ref[base]
        src = tab_ref[base + 1]
        dst = tab_ref[base + 2]
        for ck in chunks:
            @pl.when((length & ck) != 0)
            def _():
                off = length & (-2 * ck)
                act(make_copy(pl.multiple_of(src + off, MOE_ALIGN), pl.multiple_of(dst + off, MOE_ALIGN), ck))
        return 0
    lax.fori_loop(0, count, body, 0)


def _start(cp):
    cp.start()


def _wait(cp):
    cp.wait()


def _dispatch_kernel(tab_ref, h_ref, pos_ref, rows_hbm, sorted_ref, zero_ref, sem, zsem):
    t = pl.program_id(0)
    nt = pl.num_programs(0)
    slot = t % 2
    r = _iota((MOE_SORTED, MOE_TILE), 0)
    perm = jnp.where((r == pos_ref[0:1, :]) | (r == pos_ref[1:2, :]), 1.0, 0.0).astype(BF16)
    sorted_ref[slot] = _dot(perm, h_ref[...]).astype(BF16)

    def run_copy(s):
        def make(src, dst, ck):
            return pltpu.make_async_copy(sorted_ref.at[s, pl.ds(src, ck)], rows_hbm.at[pl.ds(dst, ck)], sem.at[s])
        return make

    def zero_copy(src, dst, ck):
        del src
        return pltpu.make_async_copy(zero_ref.at[pl.ds(0, ck)], rows_hbm.at[pl.ds(dst, ck)], zsem)

    _chunk_copies(tab_ref, t, run_copy(slot), _start)

    @pl.when(t > 0)
    def _():
        _chunk_copies(tab_ref, t - 1, run_copy(1 - slot), _wait)

    @pl.when(t == nt - 1)
    def _():
        zero_ref[...] = jnp.zeros_like(zero_ref)
        pad = nt * RUN_TAB
        _run_copies(tab_ref, pad, MOE_EXPERTS, PAD_CHUNKS, zero_copy, _start)
        spare = pad + MOE_EXPERTS * 3
        n_spare = tab_ref[spare]

        def spare_copy(i):
            dst = pl.multiple_of(tab_ref[spare + 2] + i * PAD_ROWS, PAD_ROWS)
            return zero_copy(0, dst, PAD_ROWS)

        lax.fori_loop(0, n_spare, lambda i, c: (_start(spare_copy(i)), c)[1], 0)
        _chunk_copies(tab_ref, t, run_copy(slot), _wait)
        _run_copies(tab_ref, pad, MOE_EXPERTS, PAD_CHUNKS, zero_copy, _wait)
        lax.fori_loop(0, n_spare, lambda i, c: (_wait(spare_copy(i)), c)[1], 0)


def moe_dispatch(tab, h2, pos, n_rows):
    n, d = h2.shape
    tm = MOE_TILE
    return pl.pallas_call(
        _dispatch_kernel,
        grid_spec=pltpu.PrefetchScalarGridSpec(
            num_scalar_prefetch=1,
            grid=(n // tm,),
            in_specs=[pl.BlockSpec((tm, d), lambda i, tab: (i, 0)),
                      pl.BlockSpec((8, tm), lambda i, tab: (0, i))],
            out_specs=pl.BlockSpec(memory_space=pl.ANY),
            scratch_shapes=[pltpu.VMEM((2, MOE_SORTED, d), BF16), pltpu.VMEM((PAD_ROWS, d), BF16),
                            pltpu.SemaphoreType.DMA((2,)), pltpu.SemaphoreType.DMA],
        ),
        out_shape=jax.ShapeDtypeStruct((n_rows, d), BF16),
        compiler_params=_params("arbitrary"),
        name="moe_dispatch",
    )(tab, h2, pos)


def _expert_kernel(be_ref, nu_ref, x_ref, w1_ref, w3_ref, w2_ref, y_ref, w1b, w3b, w2b):
    b = pl.program_id(0)
    used = b < nu_ref[0]
    new_expert = (b == 0) | (be_ref[b] != be_ref[jnp.maximum(b - 1, 0)])

    @pl.when(used & new_expert)
    def _():
        w1b[...] = w1_ref[0, 0].astype(BF16)
        w3b[...] = w3_ref[0, 0].astype(BF16)
        w2b[...] = w2_ref[0, 0].astype(BF16)

    @pl.when(used)
    def _():
        xb = x_ref[...]
        a = _dot(xb, w1b[...])
        c = _dot(xb, w3b[...])
        hid = (a * jax.nn.sigmoid(a) * c).astype(BF16)
        y_ref[...] = _dot(hid, w2b[...]).astype(y_ref.dtype)

    @pl.when(jnp.logical_not(used))
    def _():
        y_ref[...] = jnp.zeros_like(y_ref)


def moe_experts(block_expert, n_used, rows, layer, w1, w3, w2):
    n_rows = rows.shape[0]
    nb = n_rows // MOE_ROWS
    d, ff = w1.shape[2], w1.shape[3]

    def xmap(b, be, nu):
        return (jnp.minimum(b, nu[0] - 1), 0)

    def wmap(b, be, nu):
        return (layer, be[jnp.minimum(b, nu[0] - 1)], 0, 0)

    return pl.pallas_call(
        _expert_kernel,
        grid_spec=pltpu.PrefetchScalarGridSpec(
            num_scalar_prefetch=2,
            grid=(nb,),
            in_specs=[pl.BlockSpec((MOE_ROWS, d), xmap),
                      pl.BlockSpec((1, 1, d, ff), wmap),
                      pl.BlockSpec((1, 1, d, ff), wmap),
                      pl.BlockSpec((1, 1, ff, d), wmap)],
            out_specs=pl.BlockSpec((MOE_ROWS, d), lambda b, be, nu: (b, 0)),
            scratch_shapes=[pltpu.VMEM((d, ff), BF16), pltpu.VMEM((d, ff), BF16), pltpu.VMEM((ff, d), BF16)],
        ),
        out_shape=jax.ShapeDtypeStruct((n_rows, d), BF16),
        compiler_params=_params("arbitrary"),
        name="moe_experts",
    )(block_expert, n_used, rows, w1, w3, w2)


def _combine_kernel(final_norm, tab_ref, y_hbm, pos_ref, gate_ref, x_ref, g_ref, o_ref, ys_ref, sem):
    t = pl.program_id(0)
    nt = pl.num_programs(0)
    slot = t % 2

    def run_copy(s):
        def make(src, dst, ck):
            return pltpu.make_async_copy(y_hbm.at[pl.ds(dst, ck)], ys_ref.at[s, pl.ds(src, ck)], sem.at[s])
        return make

    @pl.when(t == 0)
    def _():
        ys_ref[...] = jnp.zeros_like(ys_ref)
        _chunk_copies(tab_ref, 0, run_copy(0), _start)

    @pl.when(t + 1 < nt)
    def _():
        _chunk_copies(tab_ref, t + 1, run_copy(1 - slot), _start)

    r = _iota((MOE_SORTED, MOE_TILE), 0)
    hit1 = r == pos_ref[0:1, :]
    hit2 = r == pos_ref[1:2, :]
    wgt = jnp.where(hit1, gate_ref[2:3, :], jnp.where(hit2, gate_ref[3:4, :], 0.0)).astype(BF16)
    _chunk_copies(tab_ref, t, run_copy(slot), _wait)
    out = x_ref[...] + _dot_tn(wgt, ys_ref[slot])
    if final_norm:
        out = _rms(out, g_ref[...])
    o_ref[...] = out


def moe_combine(tab, y, pos, meta, x, final_gain):
    n, d = x.shape
    tm = MOE_TILE
    final_norm = final_gain is not None
    gain = (final_gain if final_norm else jnp.ones((d,), F32)).reshape(1, d)
    return pl.pallas_call(
        functools.partial(_combine_kernel, final_norm),
        grid_spec=pltpu.PrefetchScalarGridSpec(
            num_scalar_prefetch=1,
            grid=(n // tm,),
            in_specs=[pl.BlockSpec(memory_space=pl.ANY),
                      pl.BlockSpec((8, tm), lambda i, tab: (0, i)),
                      pl.BlockSpec((8, tm), lambda i, tab: (0, i)),
                      pl.BlockSpec((tm, d), lambda i, tab: (i, 0)),
                      pl.BlockSpec((1, d), lambda i, tab: (0, 0))],
            out_specs=pl.BlockSpec((tm, d), lambda i, tab: (i, 0)),
            scratch_shapes=[pltpu.VMEM((2, MOE_SORTED, d), BF16), pltpu.SemaphoreType.DMA((2,))],
        ),
        out_shape=jax.ShapeDtypeStruct((n, d), F32),
        compiler_params=_params("arbitrary"),
        name="moe_combine",
    )(tab, y, pos, meta, x, gain)


def hierarchical_moe(x, h2, meta, cnt, layer, w1, w3, w2, final_gain):
    n, d = x.shape
    tm = MOE_TILE
    nt = n // tm

    cnt = cnt.reshape(nt, MOE_EXPERTS, LANES)[:, :, 0].astype(I32)
    run = (cnt + MOE_ALIGN - 1) // MOE_ALIGN * MOE_ALIGN
    src = jnp.cumsum(run, axis=1) - run
    before = jnp.cumsum(run, axis=0) - run
    total = jnp.sum(run, axis=0)
    padded = (total + MOE_ROWS - 1) // MOE_ROWS * MOE_ROWS
    pad_end = jnp.cumsum(padded)
    dst = (pad_end - padded)[None, :] + before
    max_rows = 2 * n + nt * MOE_EXPERTS * (MOE_ALIGN - 1) + MOE_EXPERTS * (MOE_ROWS - 1)
    nb = -(-max_rows // MOE_ROWS)
    sizes = jnp.array(RUN_CHUNKS, I32)[None, :, None]
    has = (run[:, None, :] & sizes) != 0
    above = run[:, None, :] & (-2 * sizes)
    place = jnp.where(has, jnp.cumsum(has.astype(I32), axis=2) - 1, -1)
    front = place[..., None] == jnp.arange(MOE_EXPERTS, dtype=I32)
    c_src = jnp.sum(jnp.where(front, (src[:, None, :] + above)[..., None], 0), axis=2)
    c_dst = jnp.sum(jnp.where(front, (dst[:, None, :] + above)[..., None], 0), axis=2)
    chunks = jnp.stack([c_src, c_dst], axis=-1).reshape(nt, -1)
    runs = jnp.concatenate([jnp.sum(has.astype(I32), axis=2), chunks], axis=1).reshape(-1)
    zero = jnp.zeros((MOE_EXPERTS,), I32)
    pads = jnp.stack([padded - total, zero, pad_end - padded + total], axis=-1)
    spare = jnp.stack([(nb * MOE_ROWS - pad_end[-1]) // PAD_ROWS, zero[0], pad_end[-1]])
    tab = jnp.concatenate([runs, pads.reshape(-1), spare]).astype(I32)
    n_used = (pad_end[-1] // MOE_ROWS).astype(I32).reshape(1)
    block_start = jnp.arange(nb, dtype=I32) * MOE_ROWS
    block_expert = jnp.minimum(
        jnp.sum((pad_end[None, :] <= block_start[:, None]).astype(I32), axis=1), MOE_EXPERTS - 1).astype(I32)
    e = meta[0:2].astype(I32).reshape(2, nt, tm, 1)
    hit = e == jnp.arange(MOE_EXPERTS, dtype=I32)
    pos = jnp.sum(jnp.where(hit, src[None, :, None, :], 0), axis=-1).reshape(2, n) + meta[4:6].astype(I32)
    pos8 = jnp.zeros((8, n), I32).at[0:2].set(pos)

    rows = moe_dispatch(tab, h2, pos8, nb * MOE_ROWS)
    y = moe_experts(block_expert, n_used, rows, layer, w1, w3, w2)
    return moe_combine(tab, y, pos8, meta, x, final_gain)


def kernel(x, mem, positions, ln_mix, ln_mem, ln_memkv, ln_ffn, rw_mu, rw_w0, rw_w1, rw_w2, rw_a0, rw_a1, rw_a2, rw_g1, rw_g2, rw_kk, rw_ka, rw_rk, rw_wrkv, rw_lnx_g, rw_lnx_b, rw_wo, mb_wqkv, mb_wo, mx_wq, mx_wkv, mx_wo, moe_wg, moe_bg, moe_we, moe_be, moe_w1, moe_w3, moe_w2, ln_f):
    B, T, C = x.shape
    n = B * T
    depth = ln_mix.shape[0]
    xf = x.reshape(n, C)
    memf = mem.reshape(-1, C)
    for i in range(depth):
        j = i // 2
        if i % 2 == 0:
            r, k, v, kk, akk, lw, g = rwkv_mix(xf, T, ln_mix[i], rw_mu[j], rw_w0[j], rw_w1[j], rw_w2[j],
                                               rw_a0[j], rw_a1[j], rw_a2[j], rw_g1[j], rw_g2[j],
                                               rw_kk[j], rw_ka[j], rw_wrkv[j])
            y = rwkv_recurrence(r, k, v, kk, akk, lw, g, rw_lnx_g[j], rw_lnx_b[j], rw_rk[j], B, T)
            w_mix_o = rw_wo[j]
        else:
            qkv = qkv_rope(xf, ln_mix[i], positions, mb_wqkv[j])
            y = moba_attention(qkv, B, T)
            w_mix_o = mb_wo[j]
        kv = norm_linear(memf, ln_memkv[i], mx_wkv[i].astype(BF16))
        xf, h2, meta, cnt = mem_cross_attention(y, w_mix_o, xf, ln_mem[i], mx_wq[i], kv, mx_wo[i], T,
                                                ln_ffn[i], moe_wg[i], moe_bg[i], moe_we[i], moe_be[i])
        xf = hierarchical_moe(xf, h2, meta, cnt, i, moe_w1, moe_w3, moe_w2, ln_f if i == depth - 1 else None)
    return xf.reshape(B, T, C)
```

```python
import functools
import math

import jax
import jax.numpy as jnp
from jax import lax
from jax.experimental import pallas as pl
from jax.experimental.pallas import tpu as pltpu

F32 = jnp.float32
BF16 = jnp.bfloat16
I32 = jnp.int32

D_MODEL = 1024
HEAD_DIM = 64
PAIR = 2 * HEAD_DIM
N_PAIRS = D_MODEL // PAIR
RWKV_GN_EPS = 64e-5
RWKV_CHUNK = 64
RWKV_CHUNKS_PER_STEP = 8
MOBA_BLOCK = 256
MOBA_TOPK = 3
VT_ROWS = HEAD_DIM + 16
ROPE_THETA = 10000.0
MEM_HEADS = 4
MEM_HEAD_DIM = D_MODEL // MEM_HEADS
MOE_GROUPS = 4
MOE_EPG = 8
MOE_EXPERTS = MOE_GROUPS * MOE_EPG
RMS_EPS = 1e-6
NEG_INF = -1e30

LANES = 128
ROW_ALIGN = 8
MOE_TILE = 512
MOE_ROWS = 1024
MOE_ALIGN = 16
MOE_SORTED = -(-(2 * MOE_TILE + MOE_EXPERTS * (MOE_ALIGN - 1)) // MOE_ALIGN) * MOE_ALIGN
RUN_CHUNKS = tuple(MOE_ALIGN << s for s in range((MOE_TILE // MOE_ALIGN).bit_length() - 1, -1, -1))
RUN_TAB = len(RUN_CHUNKS) * (1 + 2 * MOE_EXPERTS)
PAD_ROWS = MOE_ROWS // 2
PAD_CHUNKS = tuple(MOE_ALIGN << s for s in range((PAD_ROWS // MOE_ALIGN).bit_length() - 1, -1, -1))
VMEM_LIMIT = 56 * 2 ** 20


def _params(*sem):
    return pltpu.CompilerParams(dimension_semantics=sem, vmem_limit_bytes=VMEM_LIMIT)


def _iota(shape, dim):
    return lax.broadcasted_iota(I32, shape, dim)


def _dot(a, b):
    return jnp.dot(a, b, preferred_element_type=F32)


def _dot_nt(a, b):
    return lax.dot_general(a, b, (((1,), (1,)), ((), ())), preferred_element_type=F32)


def _dot_tn(a, b):
    return lax.dot_general(a, b, (((0,), (0,)), ((), ())), preferred_element_type=F32)


def _split(x):
    hi = x.astype(BF16)
    lo = (x - hi.astype(F32)).astype(BF16)
    return hi, lo


def _rms(x, g):
    return x * lax.rsqrt(jnp.mean(x * x, axis=-1, keepdims=True) + RMS_EPS) * g


def _norm_linear_kernel(x_ref, g_ref, w_ref, o_ref):
    xn = _rms(x_ref[...], g_ref[...]).astype(BF16)
    o_ref[...] = _dot(xn, w_ref[...]).astype(o_ref.dtype)


def norm_linear(x, g, w, tm=256):
    n, k = x.shape
    dout = w.shape[1]
    return pl.pallas_call(
        _norm_linear_kernel,
        grid=(n // tm,),
        in_specs=[pl.BlockSpec((tm, k), lambda i: (i, 0)),
                  pl.BlockSpec((1, k), lambda i: (0, 0)),
                  pl.BlockSpec((k, dout), lambda i: (0, 0))],
        out_specs=pl.BlockSpec((tm, dout), lambda i: (i, 0)),
        out_shape=jax.ShapeDtypeStruct((n, dout), BF16),
        compiler_params=_params("parallel"),
        name="norm_linear",
    )(x, g.reshape(1, k), w)


def _head_sum(x_sq):
    r = _iota((PAIR, PAIR), 0) // HEAD_DIM
    c = _iota((PAIR, PAIR), 1) // HEAD_DIM
    bd = jnp.where(r == c, 1.0, 0.0).astype(BF16)
    hi, lo = _split(x_sq)
    return _dot(hi, bd) + _dot(lo, bd)


def _rwkv_mix_kernel(seq_len, x_ref, xp_ref, g_ref, mu_ref, vec_ref, wr_ref, wk_ref, wv_ref,
                     l1_ref, w2_ref, a2_ref, g2_ref,
                     r_out, k_out, v_out, kk_out, akk_out, lw_out, g_out):
    i = pl.program_id(0)
    tm = x_ref.shape[0]
    gain = g_ref[...]
    h = _rms(x_ref[...], gain)
    hp = _rms(xp_ref[...], gain)[ROW_ALIGN - 1:ROW_ALIGN, :]
    hp = jnp.where((i * tm) % seq_len == 0, 0.0, hp)
    hs = pltpu.roll(h, 1, 0)
    hs = jnp.where(_iota((tm, 1), 0) == 0, hp, hs)
    hb = h.astype(BF16)
    dxb = (hs - h).astype(BF16)
    mub = mu_ref[...].astype(BF16)

    def mix(s):
        return hb + dxb * mub[s:s + 1, :]

    vec = vec_ref[...]
    w0, a0, k_k, k_a = vec[0:1], vec[1:2], vec[2:3], vec[3:4]
    l1 = l1_ref[...]
    tw = _dot(mix(1), l1[:, 0:64])
    ta = _dot(mix(4), l1[:, 64:128])
    tg = _dot(mix(5), l1[:, 128:256])
    r = _dot(mix(0), wr_ref[...])
    k = _dot(mix(2), wk_ref[...])
    v = _dot(mix(3), wv_ref[...])
    lw = -math.exp(-0.5) * jax.nn.sigmoid(w0 + _dot(jnp.tanh(tw).astype(BF16), w2_ref[...]))
    a = jax.nn.sigmoid(a0 + _dot(ta.astype(BF16), a2_ref[...]))
    g = _dot(jax.nn.sigmoid(tg).astype(BF16), g2_ref[...])

    kk = k * k_k
    for p in range(N_PAIRS):
        sl = slice(p * PAIR, (p + 1) * PAIR)
        kkp = kk[:, sl]
        kkn = kkp * lax.rsqrt(jnp.maximum(_head_sum(kkp * kkp), 1e-24))
        kk_out[:, sl] = kkn.astype(kk_out.dtype)
        akk_out[:, sl] = (kkn * a[:, sl]).astype(akk_out.dtype)
    r_out[...] = r.astype(r_out.dtype)
    k_out[...] = (k * (1.0 + (a - 1.0) * k_a)).astype(k_out.dtype)
    v_out[...] = v.astype(v_out.dtype)
    lw_out[...] = lw
    g_out[...] = g.astype(g_out.dtype)


def rwkv_mix(x, seq_len, gain, mu, w0, w1, w2, a0, a1, a2, g1, g2, k_k, k_a, w_rkv, tm=512):
    n, d = x.shape
    mu8 = jnp.zeros((8, d), F32).at[:6].set(mu)
    vec = jnp.zeros((8, d), F32).at[0].set(w0).at[1].set(a0).at[2].set(k_k).at[3].set(k_a)
    l1 = jnp.concatenate([w1, a1, g1], axis=1).astype(BF16)
    wb = w_rkv.astype(BF16)
    row = pl.BlockSpec((tm, d), lambda i: (i, 0))
    full = lambda a: pl.BlockSpec(a.shape, lambda i: (0,) * a.ndim)
    args = (x, x, gain.reshape(1, d), mu8, vec, wb[0], wb[1], wb[2], l1,
            w2.astype(BF16), a2.astype(BF16), g2.astype(BF16))
    in_specs = [row, pl.BlockSpec((ROW_ALIGN, d), lambda i: (jnp.maximum(i * (tm // ROW_ALIGN) - 1, 0), 0))]
    in_specs += [full(a) for a in args[2:]]
    outs = [jax.ShapeDtypeStruct((n, d), BF16)] * 5 + [jax.ShapeDtypeStruct((n, d), F32),
                                                       jax.ShapeDtypeStruct((n, d), BF16)]
    return pl.pallas_call(
        functools.partial(_rwkv_mix_kernel, seq_len),
        grid=(n // tm,),
        in_specs=in_specs,
        out_specs=[row] * 7,
        out_shape=outs,
        compiler_params=_params("parallel"),
        name="rwkv_mix",
    )(*args)


def _rwkv_rec_kernel(r_ref, k_ref, v_ref, kk_ref, akk_ref, lw_ref, g_ref, vec_ref, y_ref, s_ref):
    c = pl.program_id(1)
    L = RWKV_CHUNK

    @pl.when(c == 0)
    def _():
        s_ref[...] = jnp.zeros_like(s_ref)

    lane = _iota((1, PAIR), 1)
    m0 = lane < HEAD_DIM
    ri = _iota((2 * L, 2 * L), 0)
    ci = _iota((2 * L, 2 * L), 1)
    same = (ri // L) == (ci // L)
    strict = same & (ci < ri)
    incl = same & (ci <= ri)
    eye = jnp.where(ri == ci, 1.0, 0.0)
    tri = jnp.where(_iota((L, L), 1) <= _iota((L, L), 0), 1.0, 0.0).astype(BF16)
    rb = _iota((PAIR, PAIR), 0) // HEAD_DIM
    cb = _iota((PAIR, PAIR), 1) // HEAD_DIM
    bd = rb == cb

    def stack(x):
        return jnp.concatenate([jnp.where(m0, x, 0.0), jnp.where(m0, 0.0, x)], axis=0)

    def fold(x):
        return x[:L] + x[L:]

    def head_mean(x):
        s0 = jnp.sum(jnp.where(m0, x, 0.0), axis=-1, keepdims=True)
        s1 = jnp.sum(jnp.where(m0, 0.0, x), axis=-1, keepdims=True)
        return jnp.where(m0, s0, s1) * (1.0 / HEAD_DIM)

    vec = vec_ref[...]
    n_sub = r_ref.shape[0] // L
    pairs = range(n_sub * N_PAIRS)
    rws = [slice((q // N_PAIRS) * L, (q // N_PAIRS + 1) * L) for q in pairs]
    sls = [slice((q % N_PAIRS) * PAIR, (q % N_PAIRS + 1) * PAIR) for q in pairs]
    lw = [lw_ref[rws[p], sls[p]] for p in pairs]
    cum = []
    for p in pairs:
        lw_hi, lw_lo = _split(lw[p])
        cum.append(_dot(tri, lw_hi) + _dot(tri, lw_lo))
    dec_all, x_kap, x_r, a_t, k_t, sc = [], [], [], [], [], []
    for p in pairs:
        rw, sl = rws[p], sls[p]
        dec = jnp.exp(cum[p])
        inv = jnp.exp(-cum[p])
        dec_prev = jnp.exp(cum[p] - lw[p])
        dec_all.append(dec[L - 1:L, :])
        x_kap.append(stack(kk_ref[rw, sl].astype(F32) * dec_prev).astype(BF16))
        x_r.append(stack(r_ref[rw, sl].astype(F32) * dec))
        a_t.append((akk_ref[rw, sl].astype(F32) * inv).astype(BF16))
        k_t.append((k_ref[rw, sl].astype(F32) * inv).astype(BF16))
        xs = jnp.concatenate([x_kap[p], x_r[p].astype(BF16)], axis=0)
        ys = jnp.concatenate([a_t[p], k_t[p]], axis=0)
        sc.append(_dot_nt(xs, ys))

    def both_heads(first, second, keep):
        upper_left = keep & (ri < L)
        lower_right = keep & (ri >= L)
        return jnp.where(upper_left, first, jnp.where(lower_right, second, 0.0))

    mp, m_kk = [], []
    for p in pairs:
        top = sc[p][:2 * L]
        top_r = pltpu.roll(top, L, 1)
        mp.append(-both_heads(top, top_r, strict))
        m_kk.append(both_heads(top_r, top, strict).astype(BF16))
    s_ra, s_rk = [], []
    for p in pairs:
        bot = sc[p][2 * L:]
        bot_r = pltpu.roll(bot, L, 1)
        s_ra.append(both_heads(bot, bot_r, incl).astype(BF16))
        s_rk.append(both_heads(bot_r, bot, incl).astype(BF16))
    v_st = [stack(v_ref[rws[p], sls[p]].astype(F32)).astype(BF16) for p in pairs]
    bo = [_dot(jnp.concatenate([m_kk[p], s_rk[p]], axis=0), v_st[p]) for p in pairs]
    b1 = [bo[p][:2 * L].astype(BF16) for p in pairs]
    o1 = [bo[p][2 * L:] for p in pairs]

    t_inv = [eye + mp[p] for p in pairs]
    mpb = [mp[p].astype(BF16) for p in pairs]
    mpb = [_dot(mpb[p], mpb[p]).astype(BF16) for p in pairs]
    steps = int(math.log2(L)) - 1
    for j in range(steps):
        if j + 1 < steps:
            both = [_dot(mpb[p], jnp.concatenate([mpb[p], t_inv[p].astype(BF16)], axis=1)) for p in pairs]
            mpb = [both[p][:, :2 * L].astype(BF16) for p in pairs]
            t_inv = [t_inv[p] + both[p][:, 2 * L:] for p in pairs]
        else:
            t_inv = [t_inv[p] + _dot(mpb[p], t_inv[p].astype(BF16)) for p in pairs]

    wu = [_dot(t_inv[p].astype(BF16), jnp.concatenate([x_kap[p], b1[p]], axis=1)).astype(BF16)
          for p in pairs]
    corr = [_dot(s_ra[p], wu[p]) for p in pairs]
    g_m, h0t, r_hat, o0 = [], [], [], []
    for p in pairs:
        r_hat.append(fold(x_r[p] - corr[p][:, :PAIR]).astype(BF16))
        o0.append(fold(o1[p] - corr[p][:, PAIR:]))
        w_f = fold(wu[p][:, :PAIR])
        u0_f = fold(wu[p][:, PAIR:])
        g_m.append(jnp.where(bd, _dot_tn(a_t[p], w_f), 0.0).astype(BF16))
        vu = jnp.concatenate([v_ref[rws[p], sls[p]], -u0_f], axis=0)
        h0t.append(jnp.where(bd, _dot_tn(vu, jnp.concatenate([k_t[p], a_t[p]], axis=0)), 0.0))
    o = []
    for p in pairs:
        hp = p % N_PAIRS
        s = s_ref[hp]
        sb = s.astype(BF16)
        o.append(_dot_nt(r_hat[p], sb) + o0[p])
        s_ref[hp] = (s - _dot_nt(sb, g_m[p]) + h0t[p]) * dec_all[p]
    for p in pairs:
        rw, sl = rws[p], sls[p]
        mean = head_mean(o[p])
        cen = o[p] - mean
        var = head_mean(cen * cen)
        gn = cen * lax.rsqrt(var + RWKV_GN_EPS) * vec[0:1, sl] + vec[1:2, sl]
        r = r_ref[rw, sl].astype(F32)
        k = k_ref[rw, sl].astype(F32)
        bonus = head_mean(r * k * vec[2:3, sl]) * HEAD_DIM * v_ref[rw, sl].astype(F32)
        y_ref[rw, sl] = ((gn + bonus) * g_ref[rw, sl].astype(F32)).astype(y_ref.dtype)


def rwkv_recurrence(r, k, v, kk, akk, lw, g, lnx_g, lnx_b, r_k, batch, seq_len):
    n, d = r.shape
    rows = RWKV_CHUNK * RWKV_CHUNKS_PER_STEP
    nc = seq_len // rows
    vec = jnp.zeros((8, d), F32).at[0].set(lnx_g).at[1].set(lnx_b).at[2].set(r_k.reshape(d))
    blk = pl.BlockSpec((rows, d), lambda b, c: (b * nc + c, 0))
    return pl.pallas_call(
        _rwkv_rec_kernel,
        grid=(batch, nc),
        in_specs=[blk] * 7 + [pl.BlockSpec((8, d), lambda b, c: (0, 0))],
        out_specs=blk,
        out_shape=jax.ShapeDtypeStruct((n, d), BF16),
        scratch_shapes=[pltpu.VMEM((N_PAIRS, PAIR, PAIR), F32)],
        compiler_params=_params("parallel", "arbitrary"),
        name="rwkv_recurrence",
    )(r, k, v, kk, akk, lw, g, vec)


def _qkv_rope_kernel(x_ref, g_ref, pos_ref, inv_ref, w_ref, o_ref):
    d = x_ref.shape[1]
    xn = _rms(x_ref[...], g_ref[...]).astype(BF16)
    y = [_dot(xn, w_ref[:, part * d:(part + 1) * d]) for part in range(3)]
    ang = pos_ref[...].astype(F32) * inv_ref[...]
    first = (_iota((1, PAIR), 1) % HEAD_DIM) < HEAD_DIM // 2
    sn = jnp.sin(ang)
    cos = jnp.cos(ang)
    sin_lo = jnp.where(first, -sn, 0.0)
    sin_hi = jnp.where(first, 0.0, sn)
    for part in range(2):
        for p in range(N_PAIRS):
            t = y[part][:, p * PAIR:(p + 1) * PAIR]
            rot = (t * cos + pltpu.roll(t, PAIR - HEAD_DIM // 2, 1) * sin_lo
                   + pltpu.roll(t, HEAD_DIM // 2, 1) * sin_hi)
            o_ref[:, part * d + p * PAIR:part * d + (p + 1) * PAIR] = rot.astype(o_ref.dtype)
    o_ref[:, 2 * d:] = y[2].astype(o_ref.dtype)


def qkv_rope(x, gain, positions, w_qkv, tm=512):
    n, d = x.shape
    half = HEAD_DIM // 2
    inv = ROPE_THETA ** (-jnp.arange(half, dtype=F32) * 2.0 / HEAD_DIM)
    inv128 = jnp.tile(inv, PAIR // half).reshape(1, PAIR)
    col_scale = jnp.where(jnp.arange(3 * d) < d, math.log2(math.e) / math.sqrt(HEAD_DIM), 1.0).astype(F32)
    return pl.pallas_call(
        _qkv_rope_kernel,
        grid=(n // tm,),
        in_specs=[pl.BlockSpec((tm, d), lambda i: (i, 0)),
                  pl.BlockSpec((1, d), lambda i: (0, 0)),
                  pl.BlockSpec((tm, 1), lambda i: (i, 0)),
                  pl.BlockSpec((1, PAIR), lambda i: (0, 0)),
                  pl.BlockSpec((d, 3 * d), lambda i: (0, 0))],
        out_specs=pl.BlockSpec((tm, 3 * d), lambda i: (i, 0)),
        out_shape=jax.ShapeDtypeStruct((n, 3 * d), BF16),
        compiler_params=_params("parallel"),
        name="qkv_rope",
    )(x, gain.reshape(1, d), positions.reshape(n, 1), inv128, (w_qkv * col_scale).astype(BF16))


def _moba_kernel(q_ref, k_ref, v_ref, o_ref, vt_ref, ka_ref, qa_ref):
    nblk = k_ref.shape[0] // MOBA_BLOCK
    bq = MOBA_BLOCK
    km_rows = 16
    lane = _iota((1, PAIR), 1)
    own = (lane < HEAD_DIM, lane >= HEAD_DIM)
    spare = (HEAD_DIM, 0)

    means = []
    ones_row = jnp.where(_iota((VT_ROWS - HEAD_DIM, bq), 0) == 0, 1.0, 0.0).astype(BF16)
    for n in range(nblk):
        cols = slice(n * bq, (n + 1) * bq)
        kb = k_ref[cols, :]
        means.append(jnp.mean(kb.astype(F32), axis=0, keepdims=True))
        vt = v_ref[cols, :].astype(F32).T.astype(BF16)
        for h in range(2):
            vt_ref[h * VT_ROWS:h * VT_ROWS + HEAD_DIM, cols] = vt[h * HEAD_DIM:(h + 1) * HEAD_DIM]
            vt_ref[h * VT_ROWS + HEAD_DIM:(h + 1) * VT_ROWS, cols] = ones_row
            marker = jnp.where(lane == spare[h] + n, 1.0, 0.0).astype(BF16)
            ka_ref[h, cols, :] = jnp.where(own[h], kb, marker)
    km_hi, km_lo = _split(jnp.concatenate(means + [jnp.zeros((km_rows - nblk, PAIR), F32)], axis=0))

    blk = _iota((km_rows, bq), 0)
    causal = _iota((bq, bq), 0) <= _iota((bq, bq), 1)
    eye = jnp.where(_iota((bq, bq), 0) == _iota((bq, bq), 1), 1.0, 0.0).astype(BF16)

    q_all = q_ref[...]
    gates = []
    for h in range(2):
        qm = jnp.where(own[h], q_all, jnp.zeros_like(q_all))
        gates.append(_dot_nt(km_hi, qm) + _dot_nt(km_lo, qm))
    for c in range(nblk):
        past = blk < c
        for h in range(2):
            gate = jnp.where(past, gates[h][:, c * bq:(c + 1) * bq], NEG_INF)
            rank = jnp.zeros(gate.shape, F32)
            for m in range(c):
                row = gate[m:m + 1, :]
                tie = jnp.where(m < blk, 1.0, 0.0)
                rank = rank + jnp.where(row > gate, 1.0, jnp.where(row == gate, tie, 0.0))
            drop = jnp.where(past & (rank >= MOBA_TOPK), 1.0, 0.0)
            pad = [jnp.zeros((HEAD_DIM - km_rows, bq), F32)]
            other = [jnp.zeros((HEAD_DIM, bq), F32)]
            rows = (other + [drop] + pad) if spare[h] else ([drop] + pad + other)
            dropped = _dot_nt(eye, jnp.concatenate(rows, axis=0).astype(BF16))
            q = q_ref[c * bq:(c + 1) * bq, :]
            qa_ref[h, c * bq:(c + 1) * bq, :] = jnp.where(
                own[h], q, jnp.where(dropped > 0.5, NEG_INF, 0.0).astype(BF16))

    def score(c, h):
        return _dot_nt(ka_ref[h, 0:(c + 1) * bq, :], qa_ref[h, c * bq:(c + 1) * bq, :])

    def attend(c, h, s):
        parts = [s[n * bq:(n + 1) * bq] for n in range(c)]
        parts.append(jnp.where(causal, s[c * bq:], NEG_INF))
        top = parts[0].max(axis=0, keepdims=True)
        for part in parts[1:]:
            top = jnp.maximum(top, part.max(axis=0, keepdims=True))
        probs = jnp.concatenate([jnp.exp2((part - top).astype(BF16)) for part in parts], axis=0)
        acc = _dot(vt_ref[h * VT_ROWS:(h + 1) * VT_ROWS, 0:(c + 1) * bq], probs)
        return acc[:HEAD_DIM] * (1.0 / acc[HEAD_DIM:HEAD_DIM + 1])

    units = [(c, h) for c in range(nblk) for h in range(2)]
    ahead = 2
    pending = [score(*u) for u in units[:ahead]]
    outs = {}
    for i, (c, h) in enumerate(units):
        s = pending.pop(0)
        if i + ahead < len(units):
            pending.append(score(*units[i + ahead]))
        outs[h] = attend(c, h, s)
        if h == 1:
            o_ref[c * bq:(c + 1) * bq, :] = jnp.concatenate([outs[0], outs[1]], axis=0).T.astype(o_ref.dtype)


def moba_attention(qkv, batch, seq_len):
    n = qkv.shape[0]
    return pl.pallas_call(
        _moba_kernel,
        grid=(batch, N_PAIRS),
        in_specs=[pl.BlockSpec((seq_len, PAIR), lambda b, p: (b, p)),
                  pl.BlockSpec((seq_len, PAIR), lambda b, p: (b, N_PAIRS + p)),
                  pl.BlockSpec((seq_len, PAIR), lambda b, p: (b, 2 * N_PAIRS + p))],
        out_specs=pl.BlockSpec((seq_len, PAIR), lambda b, p: (b, p)),
        out_shape=jax.ShapeDtypeStruct((n, D_MODEL), BF16),
        scratch_shapes=[pltpu.VMEM((2 * VT_ROWS, seq_len), BF16),
                        pltpu.VMEM((2, seq_len, PAIR), BF16),
                        pltpu.VMEM((2, seq_len, PAIR), BF16)],
        compiler_params=_params("parallel", "parallel"),
        name="moba_attention",
    )(qkv, qkv, qkv)


def _mem_xattn_kernel(y_ref, wy_ref, x_ref, g_ref, wq_ref, kv_ref, wo_ref, gf_ref, rw_ref, rb_ref,
                      o_ref, h_out, meta_out, cnt_out):
    x = x_ref[...] + _dot(y_ref[...], wy_ref[...])
    xn = _rms(x, g_ref[...]).astype(BF16)
    q = (_dot(xn, wq_ref[...]) * (1.0 / math.sqrt(MEM_HEAD_DIM))).astype(BF16)
    heads = [slice(h * MEM_HEAD_DIM, (h + 1) * MEM_HEAD_DIM) for h in range(MEM_HEADS)]
    scores = [_dot_nt(q[:, sl], kv_ref[:, sl]) for sl in heads]
    outs = []
    for sl, s in zip(heads, scores):
        e = jnp.exp(s - jnp.max(s, axis=-1, keepdims=True))
        pr = e * (1.0 / jnp.sum(e, axis=-1, keepdims=True))
        outs.append(_dot(pr.astype(BF16), kv_ref[:, D_MODEL + sl.start:D_MODEL + sl.stop]).astype(BF16))
    o = jnp.concatenate(outs, axis=1)
    x2 = x + _dot(o, wo_ref[...])
    o_ref[...] = x2
    _route(x2, gf_ref, rw_ref, rb_ref, h_out, meta_out, cnt_out)


def mem_cross_attention(y, wy, x, gain, wq, kv, wo, seq_len, ffn_gain, w_grp, b_grp, w_exp, b_exp):
    n, d = x.shape
    tm = MOE_TILE
    m = kv.shape[0] // (n // seq_len)
    per_seq = seq_len // tm
    wt = jnp.zeros((LANES, d), F32).at[0:MOE_GROUPS].set(w_grp.T).at[8:8 + MOE_EXPERTS].set(w_exp.T)
    bt = jnp.zeros((LANES, 1), F32).at[0:MOE_GROUPS, 0].set(b_grp).at[8:8 + MOE_EXPERTS, 0].set(b_exp)
    const = lambda shape: pl.BlockSpec(shape, lambda i: (0, 0))
    return pl.pallas_call(
        _mem_xattn_kernel,
        grid=(n // tm,),
        in_specs=[pl.BlockSpec((tm, d), lambda i: (i, 0)),
                  const((d, d)),
                  pl.BlockSpec((tm, d), lambda i: (i, 0)),
                  const((1, d)),
                  const((d, d)),
                  pl.BlockSpec((m, 2 * d), lambda i: (i // per_seq, 0)),
                  const((d, d)),
                  const((1, d)),
                  const((LANES, d)),
                  const((LANES, 1))],
        out_specs=[pl.BlockSpec((tm, d), lambda i: (i, 0)),
                   pl.BlockSpec((tm, d), lambda i: (i, 0)),
                   pl.BlockSpec((8, tm), lambda i: (0, i)),
                   pl.BlockSpec((MOE_EXPERTS, LANES), lambda i: (i, 0))],
        out_shape=[jax.ShapeDtypeStruct((n, d), F32),
                   jax.ShapeDtypeStruct((n, d), BF16),
                   jax.ShapeDtypeStruct((8, n), F32),
                   jax.ShapeDtypeStruct((n // tm * MOE_EXPERTS, LANES), F32)],
        compiler_params=_params("parallel"),
        name="mem_cross_attention",
    )(y, wy.astype(BF16), x, gain.reshape(1, d), wq.astype(BF16), kv, wo.astype(BF16),
      ffn_gain.reshape(1, d), wt, bt)


def _route(x, g_ref, w_ref, b_ref, h_out, meta_out, cnt_out):
    tm = x.shape[0]
    h2 = _rms(x, g_ref[...])
    hi = h2.astype(BF16)
    h_out[...] = hi
    lo = (h2 - hi.astype(F32)).astype(BF16)
    w_hi, w_lo = _split(w_ref[...])
    lg = _dot_nt(w_hi, hi) + _dot_nt(w_hi, lo) + _dot_nt(w_lo, hi) + b_ref[...]
    row = _iota((8, tm), 0).astype(F32)

    def first_argmax(val, vmax):
        return jnp.min(jnp.where(val == vmax, row, 8.0), axis=0, keepdims=True)

    gl = jnp.where(row < MOE_GROUPS, lg[0:8], -jnp.inf)
    gmax = jnp.max(gl, axis=0, keepdims=True)
    p_g = 1.0 / jnp.sum(jnp.exp(gl - gmax), axis=0, keepdims=True)
    gidx = first_argmax(gl, gmax)
    el = jnp.zeros((8, tm), F32)
    for g in range(MOE_GROUPS):
        el = el + jnp.where(gidx == g, lg[8 + 8 * g:16 + 8 * g], 0.0)
    ee = jnp.exp(el - jnp.max(el, axis=0, keepdims=True))
    pe = ee / jnp.sum(ee, axis=0, keepdims=True)
    p1 = jnp.max(pe, axis=0, keepdims=True)
    i1 = first_argmax(pe, p1)
    pe2 = jnp.where(row == i1, -1.0, pe)
    p2 = jnp.max(pe2, axis=0, keepdims=True)
    i2 = first_argmax(pe2, p2)
    e1 = gidx * MOE_EPG + i1
    e2 = gidx * MOE_EPG + i2
    gate1 = p_g * p1 / (p1 + p2)
    gate2 = p_g * p2 / (p1 + p2)

    erow = _iota((MOE_EXPERTS, tm), 0).astype(F32)
    oh1 = erow == e1
    oh2 = erow == e2
    oh = jnp.where(oh1 | oh2, 1.0, 0.0)
    before = jnp.where(_iota((tm, tm), 0) < _iota((tm, tm), 1), 1.0, 0.0).astype(BF16)
    cnt_before = _dot(oh.astype(BF16), before)
    lr1 = jnp.sum(jnp.where(oh1, cnt_before, 0.0), axis=0, keepdims=True)
    lr2 = jnp.sum(jnp.where(oh2, cnt_before, 0.0), axis=0, keepdims=True)
    zero = jnp.zeros((1, tm), F32)
    meta_out[...] = jnp.concatenate([e1, e2, gate1, gate2, lr1, lr2, zero, zero], axis=0)
    cnt_out[...] = jnp.broadcast_to(jnp.sum(oh, axis=1, keepdims=True), (MOE_EXPERTS, LANES))


def _chunk_copies(tab_ref, t, make_copy, act):
    base = t * RUN_TAB
    for ci, ck in enumerate(RUN_CHUNKS):
        first = base + len(RUN_CHUNKS) + ci * 2 * MOE_EXPERTS

        def body(i, _, first=first, ck=ck):
            src = tab_ref[first + 2 * i]
            dst = tab_ref[first + 2 * i + 1]
            act(make_copy(pl.multiple_of(src, MOE_ALIGN), pl.multiple_of(dst, MOE_ALIGN), ck))
            return 0

        lax.fori_loop(0, tab_ref[base + ci], body, 0)


def _run_copies(tab_ref, first, count, chunks, make_copy, act):
    def body(e, _):
        base = first + e * 3
        length = tab_ref[base]
        src = tab_ref[base + 1]
        dst = tab_ref[base + 2]
        for ck in chunks:
            @pl.when((length & ck) != 0)
            def _():
                off = length & (-2 * ck)
                act(make_copy(pl.multiple_of(src + off, MOE_ALIGN), pl.multiple_of(dst + off, MOE_ALIGN), ck))
        return 0
    lax.fori_loop(0, count, body, 0)


def _start(cp):
    cp.start()


def _wait(cp):
    cp.wait()


def _dispatch_kernel(tab_ref, h_ref, pos_ref, rows_hbm, sorted_ref, zero_ref, sem, zsem):
    t = pl.program_id(0)
    nt = pl.num_programs(0)
    slot = t % 2
    r = _iota((MOE_SORTED, MOE_TILE), 0)
    perm = jnp.where((r == pos_ref[0:1, :]) | (r == pos_ref[1:2, :]), 1.0, 0.0).astype(BF16)
    sorted_ref[slot] = _dot(perm, h_ref[...]).astype(BF16)

    def run_copy(s):
        def make(src, dst, ck):
            return pltpu.make_async_copy(sorted_ref.at[s, pl.ds(src, ck)], rows_hbm.at[pl.ds(dst, ck)], sem.at[s])
        return make

    def zero_copy(src, dst, ck):
        del src
        return pltpu.make_async_copy(zero_ref.at[pl.ds(0, ck)], rows_hbm.at[pl.ds(dst, ck)], zsem)

    _chunk_copies(tab_ref, t, run_copy(slot), _start)

    @pl.when(t > 0)
    def _():
        _chunk_copies(tab_ref, t - 1, run_copy(1 - slot), _wait)

    @pl.when(t == nt - 1)
    def _():
        zero_ref[...] = jnp.zeros_like(zero_ref)
        pad = nt * RUN_TAB
        _run_copies(tab_ref, pad, MOE_EXPERTS, PAD_CHUNKS, zero_copy, _start)
        spare = pad + MOE_EXPERTS * 3
        n_spare = tab_ref[spare]

        def spare_copy(i):
            dst = pl.multiple_of(tab_ref[spare + 2] + i * PAD_ROWS, PAD_ROWS)
            return zero_copy(0, dst, PAD_ROWS)

        lax.fori_loop(0, n_spare, lambda i, c: (_start(spare_copy(i)), c)[1], 0)
        _chunk_copies(tab_ref, t, run_copy(slot), _wait)
        _run_copies(tab_ref, pad, MOE_EXPERTS, PAD_CHUNKS, zero_copy, _wait)
        lax.fori_loop(0, n_spare, lambda i, c: (_wait(spare_copy(i)), c)[1], 0)


def moe_dispatch(tab, h2, pos, n_rows):
    n, d = h2.shape
    tm = MOE_TILE
    return pl.pallas_call(
        _dispatch_kernel,
        grid_spec=pltpu.PrefetchScalarGridSpec(
            num_scalar_prefetch=1,
            grid=(n // tm,),
            in_specs=[pl.BlockSpec((tm, d), lambda i, tab: (i, 0)),
                      pl.BlockSpec((8, tm), lambda i, tab: (0, i))],
            out_specs=pl.BlockSpec(memory_space=pl.ANY),
            scratch_shapes=[pltpu.VMEM((2, MOE_SORTED, d), BF16), pltpu.VMEM((PAD_ROWS, d), BF16),
                            pltpu.SemaphoreType.DMA((2,)), pltpu.SemaphoreType.DMA],
        ),
        out_shape=jax.ShapeDtypeStruct((n_rows, d), BF16),
        compiler_params=_params("arbitrary"),
        name="moe_dispatch",
    )(tab, h2, pos)


def _expert_kernel(be_ref, nu_ref, x_ref, w1_ref, w3_ref, w2_ref, y_ref, w1b, w3b, w2b):
    b = pl.program_id(0)
    used = b < nu_ref[0]
    new_expert = (b == 0) | (be_ref[b] != be_ref[jnp.maximum(b - 1, 0)])

    @pl.when(used & new_expert)
    def _():
        w1b[...] = w1_ref[0, 0].astype(BF16)
        w3b[...] = w3_ref[0, 0].astype(BF16)
        w2b[...] = w2_ref[0, 0].astype(BF16)

    @pl.when(used)
    def _():
        xb = x_ref[...]
        a = _dot(xb, w1b[...])
        c = _dot(xb, w3b[...])
        hid = (a * jax.nn.sigmoid(a) * c).astype(BF16)
        y_ref[...] = _dot(hid, w2b[...]).astype(y_ref.dtype)

    @pl.when(jnp.logical_not(used))
    def _():
        y_ref[...] = jnp.zeros_like(y_ref)


def moe_experts(block_expert, n_used, rows, layer, w1, w3, w2):
    n_rows = rows.shape[0]
    nb = n_rows // MOE_ROWS
    d, ff = w1.shape[2], w1.shape[3]

    def xmap(b, be, nu):
        return (jnp.minimum(b, nu[0] - 1), 0)

    def wmap(b, be, nu):
        return (layer, be[jnp.minimum(b, nu[0] - 1)], 0, 0)

    return pl.pallas_call(
        _expert_kernel,
        grid_spec=pltpu.PrefetchScalarGridSpec(
            num_scalar_prefetch=2,
            grid=(nb,),
            in_specs=[pl.BlockSpec((MOE_ROWS, d), xmap),
                      pl.BlockSpec((1, 1, d, ff), wmap),
                      pl.BlockSpec((1, 1, d, ff), wmap),
                      pl.BlockSpec((1, 1, ff, d), wmap)],
            out_specs=pl.BlockSpec((MOE_ROWS, d), lambda b, be, nu: (b, 0)),
            scratch_shapes=[pltpu.VMEM((d, ff), BF16), pltpu.VMEM((d, ff), BF16), pltpu.VMEM((ff, d), BF16)],
        ),
        out_shape=jax.ShapeDtypeStruct((n_rows, d), BF16),
        compiler_params=_params("arbitrary"),
        name="moe_experts",
    )(block_expert, n_used, rows, w1, w3, w2)


def _combine_kernel(final_norm, tab_ref, y_hbm, pos_ref, gate_ref, x_ref, g_ref, o_ref, ys_ref, sem):
    t = pl.program_id(0)
    nt = pl.num_programs(0)
    slot = t % 2

    def run_copy(s):
        def make(src, dst, ck):
            return pltpu.make_async_copy(y_hbm.at[pl.ds(dst, ck)], ys_ref.at[s, pl.ds(src, ck)], sem.at[s])
        return make

    @pl.when(t == 0)
    def _():
        ys_ref[...] = jnp.zeros_like(ys_ref)
        _chunk_copies(tab_ref, 0, run_copy(0), _start)

    @pl.when(t + 1 < nt)
    def _():
        _chunk_copies(tab_ref, t + 1, run_copy(1 - slot), _start)

    r = _iota((MOE_SORTED, MOE_TILE), 0)
    hit1 = r == pos_ref[0:1, :]
    hit2 = r == pos_ref[1:2, :]
    wgt = jnp.where(hit1, gate_ref[2:3, :], jnp.where(hit2, gate_ref[3:4, :], 0.0)).astype(BF16)
    _chunk_copies(tab_ref, t, run_copy(slot), _wait)
    out = x_ref[...] + _dot_tn(wgt, ys_ref[slot])
    if final_norm:
        out = _rms(out, g_ref[...])
    o_ref[...] = out


def moe_combine(tab, y, pos, meta, x, final_gain):
    n, d = x.shape
    tm = MOE_TILE
    final_norm = final_gain is not None
    gain = (final_gain if final_norm else jnp.ones((d,), F32)).reshape(1, d)
    return pl.pallas_call(
        functools.partial(_combine_kernel, final_norm),
        grid_spec=pltpu.PrefetchScalarGridSpec(
            num_scalar_prefetch=1,
            grid=(n // tm,),
            in_specs=[pl.BlockSpec(memory_space=pl.ANY),
                      pl.BlockSpec((8, tm), lambda i, tab: (0, i)),
                      pl.BlockSpec((8, tm), lambda i, tab: (0, i)),
                      pl.BlockSpec((tm, d), lambda i, tab: (i, 0)),
                      pl.BlockSpec((1, d), lambda i, tab: (0, 0))],
            out_specs=pl.BlockSpec((tm, d), lambda i, tab: (i, 0)),
            scratch_shapes=[pltpu.VMEM((2, MOE_SORTED, d), BF16), pltpu.SemaphoreType.DMA((2,))],
        ),
        out_shape=jax.ShapeDtypeStruct((n, d), F32),
        compiler_params=_params("arbitrary"),
        name="moe_combine",
    )(tab, y, pos, meta, x, gain)


def hierarchical_moe(x, h2, meta, cnt, layer, w1, w3, w2, final_gain):
    n, d = x.shape
    tm = MOE_TILE
    nt = n // tm

    cnt = cnt.reshape(nt, MOE_EXPERTS, LANES)[:, :, 0].astype(I32)
    run = (cnt + MOE_ALIGN - 1) // MOE_ALIGN * MOE_ALIGN
    src = jnp.cumsum(run, axis=1) - run
    before = jnp.cumsum(run, axis=0) - run
    total = jnp.sum(run, axis=0)
    padded = (total + MOE_ROWS - 1) // MOE_ROWS * MOE_ROWS
    pad_end = jnp.cumsum(padded)
    dst = (pad_end - padded)[None, :] + before
    max_rows = 2 * n + nt * MOE_EXPERTS * (MOE_ALIGN - 1) + MOE_EXPERTS * (MOE_ROWS - 1)
    nb = -(-max_rows // MOE_ROWS)
    sizes = jnp.array(RUN_CHUNKS, I32)[None, :, None]
    has = (run[:, None, :] & sizes) != 0
    above = run[:, None, :] & (-2 * sizes)
    place = jnp.where(has, jnp.cumsum(has.astype(I32), axis=2) - 1, -1)
    front = place[..., None] == jnp.arange(MOE_EXPERTS, dtype=I32)
    c_src = jnp.sum(jnp.where(front, (src[:, None, :] + above)[..., None], 0), axis=2)
    c_dst = jnp.sum(jnp.where(front, (dst[:, None, :] + above)[..., None], 0), axis=2)
    chunks = jnp.stack([c_src, c_dst], axis=-1).reshape(nt, -1)
    runs = jnp.concatenate([jnp.sum(has.astype(I32), axis=2), chunks], axis=1).reshape(-1)
    zero = jnp.zeros((MOE_EXPERTS,), I32)
    pads = jnp.stack([padded - total, zero, pad_end - padded + total], axis=-1)
    spare = jnp.stack([(nb * MOE_ROWS - pad_end[-1]) // PAD_ROWS, zero[0], pad_end[-1]])
    tab = jnp.concatenate([runs, pads.reshape(-1), spare]).astype(I32)
    n_used = (pad_end[-1] // MOE_ROWS).astype(I32).reshape(1)
    block_start = jnp.arange(nb, dtype=I32) * MOE_ROWS
    block_expert = jnp.minimum(
        jnp.sum((pad_end[None, :] <= block_start[:, None]).astype(I32), axis=1), MOE_EXPERTS - 1).astype(I32)
    e = meta[0:2].astype(I32).reshape(2, nt, tm, 1)
    hit = e == jnp.arange(MOE_EXPERTS, dtype=I32)
    pos = jnp.sum(jnp.where(hit, src[None, :, None, :], 0), axis=-1).reshape(2, n) + meta[4:6].astype(I32)
    pos8 = jnp.zeros((8, n), I32).at[0:2].set(pos)

    rows = moe_dispatch(tab, h2, pos8, nb * MOE_ROWS)
    y = moe_experts(block_expert, n_used, rows, layer, w1, w3, w2)
    return moe_combine(tab, y, pos8, meta, x, final_gain)


def kernel(x, mem, positions, ln_mix, ln_mem, ln_memkv, ln_ffn, rw_mu, rw_w0, rw_w1, rw_w2, rw_a0, rw_a1, rw_a2, rw_g1, rw_g2, rw_kk, rw_ka, rw_rk, rw_wrkv, rw_lnx_g, rw_lnx_b, rw_wo, mb_wqkv, mb_wo, mx_wq, mx_wkv, mx_wo, moe_wg, moe_bg, moe_we, moe_be, moe_w1, moe_w3, moe_w2, ln_f):
    B, T, C = x.shape
    n = B * T
    depth = ln_mix.shape[0]
    xf = x.reshape(n, C)
    memf = mem.reshape(-1, C)
    for i in range(depth):
        j = i // 2
        if i % 2 == 0:
            r, k, v, kk, akk, lw, g = rwkv_mix(xf, T, ln_mix[i], rw_mu[j], rw_w0[j], rw_w1[j], rw_w2[j],
                                               rw_a0[j], rw_a1[j], rw_a2[j], rw_g1[j], rw_g2[j],
                                               rw_kk[j], rw_ka[j], rw_wrkv[j])
            y = rwkv_recurrence(r, k, v, kk, akk, lw, g, rw_lnx_g[j], rw_lnx_b[j], rw_rk[j], B, T)
            w_mix_o = rw_wo[j]
        else:
            qkv = qkv_rope(xf, ln_mix[i], positions, mb_wqkv[j])
            y = moba_attention(qkv, B, T)
            w_mix_o = mb_wo[j]
        kv = norm_linear(memf, ln_memkv[i], mx_wkv[i].astype(BF16))
        xf, h2, meta, cnt = mem_cross_attention(y, w_mix_o, xf, ln_mem[i], mx_wq[i], kv, mx_wo[i], T,
                                                ln_ffn[i], moe_wg[i], moe_bg[i], moe_we[i], moe_be[i])
        xf = hierarchical_moe(xf, h2, meta, cnt, i, moe_w1, moe_w3, moe_w2, ln_f if i == depth - 1 else None)
    return xf.reshape(B, T, C)
```

```python
import functools
import math

import jax
import jax.numpy as jnp
from jax import lax
from jax.experimental import pallas as pl
from jax.experimental.pallas import tpu as pltpu

F32 = jnp.float32
BF16 = jnp.bfloat16
I32 = jnp.int32

D_MODEL = 1024
HEAD_DIM = 64
PAIR = 2 * HEAD_DIM
N_PAIRS = D_MODEL // PAIR
RWKV_GN_EPS = 64e-5
RWKV_CHUNK = 64
RWKV_CHUNKS_PER_STEP = 8
MOBA_BLOCK = 256
MOBA_TOPK = 3
VT_ROWS = HEAD_DIM + 16
ROPE_THETA = 10000.0
MEM_HEADS = 4
MEM_HEAD_DIM = D_MODEL // MEM_HEADS
MOE_GROUPS = 4
MOE_EPG = 8
MOE_EXPERTS = MOE_GROUPS * MOE_EPG
RMS_EPS = 1e-6
NEG_INF = -1e30

LANES = 128
ROW_ALIGN = 8
MOE_TILE = 512
MOE_ROWS = 1024
MOE_ALIGN = 16
MOE_SORTED = -(-(2 * MOE_TILE + MOE_EXPERTS * (MOE_ALIGN - 1)) // MOE_ALIGN) * MOE_ALIGN
RUN_CHUNKS = tuple(MOE_ALIGN << s for s in range((MOE_TILE // MOE_ALIGN).bit_length() - 1, -1, -1))
RUN_TAB = len(RUN_CHUNKS) * (1 + 2 * MOE_EXPERTS)
PAD_ROWS = MOE_ROWS // 2
PAD_CHUNKS = tuple(MOE_ALIGN << s for s in range((PAD_ROWS // MOE_ALIGN).bit_length() - 1, -1, -1))
VMEM_LIMIT = 56 * 2 ** 20


def _params(*sem):
    return pltpu.CompilerParams(dimension_semantics=sem, vmem_limit_bytes=VMEM_LIMIT)


def _iota(shape, dim):
    return lax.broadcasted_iota(I32, shape, dim)


def _dot(a, b):
    return jnp.dot(a, b, preferred_element_type=F32)


def _dot_nt(a, b):
    return lax.dot_general(a, b, (((1,), (1,)), ((), ())), preferred_element_type=F32)


def _dot_tn(a, b):
    return lax.dot_general(a, b, (((0,), (0,)), ((), ())), preferred_element_type=F32)


def _split(x):
    hi = x.astype(BF16)
    lo = (x - hi.astype(F32)).astype(BF16)
    return hi, lo


def _rms(x, g):
    return x * lax.rsqrt(jnp.mean(x * x, axis=-1, keepdims=True) + RMS_EPS) * g


def _norm_linear_kernel(x_ref, g_ref, w_ref, o_ref):
    xn = _rms(x_ref[...], g_ref[...]).astype(BF16)
    o_ref[...] = _dot(xn, w_ref[...]).astype(o_ref.dtype)


def norm_linear(x, g, w, tm=256):
    n, k = x.shape
    dout = w.shape[1]
    return pl.pallas_call(
        _norm_linear_kernel,
        grid=(n // tm,),
        in_specs=[pl.BlockSpec((tm, k), lambda i: (i, 0)),
                  pl.BlockSpec((1, k), lambda i: (0, 0)),
                  pl.BlockSpec((k, dout), lambda i: (0, 0))],
        out_specs=pl.BlockSpec((tm, dout), lambda i: (i, 0)),
        out_shape=jax.ShapeDtypeStruct((n, dout), BF16),
        compiler_params=_params("parallel"),
        name="norm_linear",
    )(x, g.reshape(1, k), w)


def _head_sums(x_sq):
    rows = x_sq.shape[0]
    r = _iota((PAIR, PAIR), 0) // HEAD_DIM
    c = _iota((PAIR, PAIR), 1) // HEAD_DIM
    bd = jnp.where(r == c, 1.0, 0.0).astype(BF16)
    hi, lo = _split(x_sq)
    stacked = jnp.concatenate([part[:, p * PAIR:(p + 1) * PAIR] for part in (hi, lo) for p in range(N_PAIRS)], axis=0)
    sums = _dot(stacked, bd)
    sums = sums[:N_PAIRS * rows] + sums[N_PAIRS * rows:]
    return jnp.concatenate([sums[p * rows:(p + 1) * rows] for p in range(N_PAIRS)], axis=1)


def _rwkv_mix_kernel(seq_len, x_ref, xp_ref, g_ref, mu_ref, vec_ref, wr_ref, wk_ref, wv_ref,
                     l1_ref, w2_ref, a2_ref, g2_ref,
                     r_out, k_out, v_out, kk_out, akk_out, lw_out, g_out):
    i = pl.program_id(0)
    tm = x_ref.shape[0]
    gain = g_ref[...]
    h = _rms(x_ref[...], gain)
    hp = _rms(xp_ref[...], gain)[ROW_ALIGN - 1:ROW_ALIGN, :]
    hp = jnp.where((i * tm) % seq_len == 0, 0.0, hp)
    hs = pltpu.roll(h, 1, 0)
    hs = jnp.where(_iota((tm, 1), 0) == 0, hp, hs)
    hb = h.astype(BF16)
    dxb = (hs - h).astype(BF16)
    mub = mu_ref[...].astype(BF16)

    def mix(s):
        return hb + dxb * mub[s:s + 1, :]

    vec = vec_ref[...]
    w0, a0, k_k, k_a = vec[0:1], vec[1:2], vec[2:3], vec[3:4]
    l1 = l1_ref[...]
    tw = _dot(mix(1), l1[:, 0:64])
    ta = _dot(mix(4), l1[:, 64:128])
    tg = _dot(mix(5), l1[:, 128:256])
    r = _dot(mix(0), wr_ref[...])
    k = _dot(mix(2), wk_ref[...])
    v = _dot(mix(3), wv_ref[...])
    lw = -math.exp(-0.5) * jax.nn.sigmoid(w0 + _dot(jnp.tanh(tw).astype(BF16), w2_ref[...]))
    a = jax.nn.sigmoid(a0 + _dot(ta.astype(BF16), a2_ref[...]))
    g = _dot(jax.nn.sigmoid(tg).astype(BF16), g2_ref[...])

    kk = k * k_k
    kkn = kk * lax.rsqrt(jnp.maximum(_head_sums(kk * kk), 1e-24))
    kk_out[...] = kkn.astype(kk_out.dtype)
    akk_out[...] = (kkn * a).astype(akk_out.dtype)
    r_out[...] = r.astype(r_out.dtype)
    k_out[...] = (k * (1.0 + (a - 1.0) * k_a)).astype(k_out.dtype)
    v_out[...] = v.astype(v_out.dtype)
    lw_out[...] = lw
    g_out[...] = g.astype(g_out.dtype)


def rwkv_mix(x, seq_len, gain, mu, w0, w1, w2, a0, a1, a2, g1, g2, k_k, k_a, w_rkv, tm=512):
    n, d = x.shape
    mu8 = jnp.zeros((8, d), F32).at[:6].set(mu)
    vec = jnp.zeros((8, d), F32).at[0].set(w0).at[1].set(a0).at[2].set(k_k).at[3].set(k_a)
    l1 = jnp.concatenate([w1, a1, g1], axis=1).astype(BF16)
    wb = w_rkv.astype(BF16)
    row = pl.BlockSpec((tm, d), lambda i: (i, 0))
    full = lambda a: pl.BlockSpec(a.shape, lambda i: (0,) * a.ndim)
    args = (x, x, gain.reshape(1, d), mu8, vec, wb[0], wb[1], wb[2], l1,
            w2.astype(BF16), a2.astype(BF16), g2.astype(BF16))
    in_specs = [row, pl.BlockSpec((ROW_ALIGN, d), lambda i: (jnp.maximum(i * (tm // ROW_ALIGN) - 1, 0), 0))]
    in_specs += [full(a) for a in args[2:]]
    outs = [jax.ShapeDtypeStruct((n, d), BF16)] * 5 + [jax.ShapeDtypeStruct((n, d), F32),
                                                       jax.ShapeDtypeStruct((n, d), BF16)]
    return pl.pallas_call(
        functools.partial(_rwkv_mix_kernel, seq_len),
        grid=(n // tm,),
        in_specs=in_specs,
        out_specs=[row] * 7,
        out_shape=outs,
        compiler_params=_params("parallel"),
        name="rwkv_mix",
    )(*args)


def _rwkv_rec_kernel(r_ref, k_ref, v_ref, kk_ref, akk_ref, lw_ref, g_ref, vec_ref, y_ref, s_ref):
    c = pl.program_id(1)
    L = RWKV_CHUNK

    @pl.when(c == 0)
    def _():
        s_ref[...] = jnp.zeros_like(s_ref)

    lane = _iota((1, PAIR), 1)
    m0 = lane < HEAD_DIM
    ri = _iota((2 * L, 2 * L), 0)
    ci = _iota((2 * L, 2 * L), 1)
    same = (ri // L) == (ci // L)
    strict = same & (ci < ri)
    incl = same & (ci <= ri)
    eye = jnp.where(ri == ci, 1.0, 0.0)
    tri = jnp.where(_iota((L, L), 1) <= _iota((L, L), 0), 1.0, 0.0).astype(BF16)
    rb = _iota((PAIR, PAIR), 0) // HEAD_DIM
    cb = _iota((PAIR, PAIR), 1) // HEAD_DIM
    bd = rb == cb

    def stack(x):
        return jnp.concatenate([jnp.where(m0, x, 0.0), jnp.where(m0, 0.0, x)], axis=0)

    def fold(x):
        return x[:L] + x[L:]

    def head_mean(x):
        s0 = jnp.sum(jnp.where(m0, x, 0.0), axis=-1, keepdims=True)
        s1 = jnp.sum(jnp.where(m0, 0.0, x), axis=-1, keepdims=True)
        return jnp.where(m0, s0, s1) * (1.0 / HEAD_DIM)

    vec = vec_ref[...]
    n_sub = r_ref.shape[0] // L
    pairs = range(n_sub * N_PAIRS)
    rws = [slice((q // N_PAIRS) * L, (q // N_PAIRS + 1) * L) for q in pairs]
    sls = [slice((q % N_PAIRS) * PAIR, (q % N_PAIRS + 1) * PAIR) for q in pairs]
    lw = [lw_ref[rws[p], sls[p]] for p in pairs]
    cum = []
    for p in pairs:
        lw_hi, lw_lo = _split(lw[p])
        cum.append(_dot(tri, lw_hi) + _dot(tri, lw_lo))
    dec_all, x_kap, x_r, a_t, k_t, sc = [], [], [], [], [], []
    for p in pairs:
        rw, sl = rws[p], sls[p]
        dec = jnp.exp(cum[p])
        inv = jnp.exp(-cum[p])
        dec_prev = jnp.exp(cum[p] - lw[p])
        dec_all.append(dec[L - 1:L, :])
        x_kap.append(stack(kk_ref[rw, sl].astype(F32) * dec_prev).astype(BF16))
        x_r.append(stack(r_ref[rw, sl].astype(F32) * dec))
        a_t.append((akk_ref[rw, sl].astype(F32) * inv).astype(BF16))
        k_t.append((k_ref[rw, sl].astype(F32) * inv).astype(BF16))
        xs = jnp.concatenate([x_kap[p], x_r[p].astype(BF16)], axis=0)
        ys = jnp.concatenate([a_t[p], k_t[p]], axis=0)
        sc.append(_dot_nt(xs, ys))

    def both_heads(first, second, keep):
        upper_left = keep & (ri < L)
        lower_right = keep & (ri >= L)
        return jnp.where(upper_left, first, jnp.where(lower_right, second, 0.0))

    mp, m_kk = [], []
    for p in pairs:
        top = sc[p][:2 * L]
        top_r = pltpu.roll(top, L, 1)
        mp.append(-both_heads(top, top_r, strict))
        m_kk.append(both_heads(top_r, top, strict).astype(BF16))
    s_ra, s_rk = [], []
    for p in pairs:
        bot = sc[p][2 * L:]
        bot_r = pltpu.roll(bot, L, 1)
        s_ra.append(both_heads(bot, bot_r, incl).astype(BF16))
        s_rk.append(both_heads(bot_r, bot, incl).astype(BF16))
    v_st = [stack(v_ref[rws[p], sls[p]].astype(F32)).astype(BF16) for p in pairs]
    bo = [_dot(jnp.concatenate([m_kk[p], s_rk[p]], axis=0), v_st[p]) for p in pairs]
    b1 = [bo[p][:2 * L].astype(BF16) for p in pairs]
    o1 = [bo[p][2 * L:] for p in pairs]

    t_inv = [eye + mp[p] for p in pairs]
    mpb = [mp[p].astype(BF16) for p in pairs]
    mpb = [_dot(mpb[p], mpb[p]).astype(BF16) for p in pairs]
    steps = int(math.log2(L)) - 1
    for j in range(steps):
        if j + 1 < steps:
            both = [_dot(mpb[p], jnp.concatenate([mpb[p], t_inv[p].astype(BF16)], axis=1)) for p in pairs]
            mpb = [both[p][:, :2 * L].astype(BF16) for p in pairs]
            t_inv = [t_inv[p] + both[p][:, 2 * L:] for p in pairs]
        else:
            t_inv = [t_inv[p] + _dot(mpb[p], t_inv[p].astype(BF16)) for p in pairs]

    wu = [_dot(t_inv[p].astype(BF16), jnp.concatenate([x_kap[p], b1[p]], axis=1)).astype(BF16)
          for p in pairs]
    corr = [_dot(s_ra[p], wu[p]) for p in pairs]
    g_m, h0t, r_hat, o0 = [], [], [], []
    for p in pairs:
        r_hat.append(fold(x_r[p] - corr[p][:, :PAIR]).astype(BF16))
        o0.append(fold(o1[p] - corr[p][:, PAIR:]))
        w_f = fold(wu[p][:, :PAIR])
        u0_f = fold(wu[p][:, PAIR:])
        g_m.append(jnp.where(bd, _dot_tn(a_t[p], w_f), 0.0).astype(BF16))
        vu = jnp.concatenate([v_ref[rws[p], sls[p]], -u0_f], axis=0)
        h0t.append(jnp.where(bd, _dot_tn(vu, jnp.concatenate([k_t[p], a_t[p]], axis=0)), 0.0))
    o = []
    for p in pairs:
        hp = p % N_PAIRS
        s = s_ref[hp]
        sb = s.astype(BF16)
        o.append(_dot_nt(r_hat[p], sb) + o0[p])
        s_ref[hp] = (s - _dot_nt(sb, g_m[p]) + h0t[p]) * dec_all[p]
    for p in pairs:
        rw, sl = rws[p], sls[p]
        mean = head_mean(o[p])
        cen = o[p] - mean
        var = head_mean(cen * cen)
        gn = cen * lax.rsqrt(var + RWKV_GN_EPS) * vec[0:1, sl] + vec[1:2, sl]
        r = r_ref[rw, sl].astype(F32)
        k = k_ref[rw, sl].astype(F32)
        bonus = head_mean(r * k * vec[2:3, sl]) * HEAD_DIM * v_ref[rw, sl].astype(F32)
        y_ref[rw, sl] = ((gn + bonus) * g_ref[rw, sl].astype(F32)).astype(y_ref.dtype)


def rwkv_recurrence(r, k, v, kk, akk, lw, g, lnx_g, lnx_b, r_k, batch, seq_len):
    n, d = r.shape
    rows = RWKV_CHUNK * RWKV_CHUNKS_PER_STEP
    nc = seq_len // rows
    vec = jnp.zeros((8, d), F32).at[0].set(lnx_g).at[1].set(lnx_b).at[2].set(r_k.reshape(d))
    blk = pl.BlockSpec((rows, d), lambda b, c: (b * nc + c, 0))
    return pl.pallas_call(
        _rwkv_rec_kernel,
        grid=(batch, nc),
        in_specs=[blk] * 7 + [pl.BlockSpec((8, d), lambda b, c: (0, 0))],
        out_specs=blk,
        out_shape=jax.ShapeDtypeStruct((n, d), BF16),
        scratch_shapes=[pltpu.VMEM((N_PAIRS, PAIR, PAIR), F32)],
        compiler_params=_params("parallel", "arbitrary"),
        name="rwkv_recurrence",
    )(r, k, v, kk, akk, lw, g, vec)


def _qkv_rope_kernel(x_ref, g_ref, pos_ref, inv_ref, w_ref, o_ref):
    d = x_ref.shape[1]
    xn = _rms(x_ref[...], g_ref[...]).astype(BF16)
    y = [_dot(xn, w_ref[:, part * d:(part + 1) * d]) for part in range(3)]
    ang = pos_ref[...].astype(F32) * inv_ref[...]
    first = (_iota((1, PAIR), 1) % HEAD_DIM) < HEAD_DIM // 2
    sn = jnp.sin(ang)
    cos = jnp.cos(ang)
    sin_lo = jnp.where(first, -sn, 0.0)
    sin_hi = jnp.where(first, 0.0, sn)
    for part in range(2):
        for p in range(N_PAIRS):
            t = y[part][:, p * PAIR:(p + 1) * PAIR]
            rot = (t * cos + pltpu.roll(t, PAIR - HEAD_DIM // 2, 1) * sin_lo
                   + pltpu.roll(t, HEAD_DIM // 2, 1) * sin_hi)
            o_ref[:, part * d + p * PAIR:part * d + (p + 1) * PAIR] = rot.astype(o_ref.dtype)
    o_ref[:, 2 * d:] = y[2].astype(o_ref.dtype)


def qkv_rope(x, gain, positions, w_qkv, tm=1024):
    n, d = x.shape
    half = HEAD_DIM // 2
    inv = ROPE_THETA ** (-jnp.arange(half, dtype=F32) * 2.0 / HEAD_DIM)
    inv128 = jnp.tile(inv, PAIR // half).reshape(1, PAIR)
    col_scale = jnp.where(jnp.arange(3 * d) < d, math.log2(math.e) / math.sqrt(HEAD_DIM), 1.0).astype(F32)
    return pl.pallas_call(
        _qkv_rope_kernel,
        grid=(n // tm,),
        in_specs=[pl.BlockSpec((tm, d), lambda i: (i, 0)),
                  pl.BlockSpec((1, d), lambda i: (0, 0)),
                  pl.BlockSpec((tm, 1), lambda i: (i, 0)),
                  pl.BlockSpec((1, PAIR), lambda i: (0, 0)),
                  pl.BlockSpec((d, 3 * d), lambda i: (0, 0))],
        out_specs=pl.BlockSpec((tm, 3 * d), lambda i: (i, 0)),
        out_shape=jax.ShapeDtypeStruct((n, 3 * d), BF16),
        compiler_params=_params("parallel"),
        name="qkv_rope",
    )(x, gain.reshape(1, d), positions.reshape(n, 1), inv128, (w_qkv * col_scale).astype(BF16))


def _moba_kernel(q_ref, k_ref, v_ref, o_ref, vt_ref, ka_ref, qa_ref):
    nblk = k_ref.shape[0] // MOBA_BLOCK
    bq = MOBA_BLOCK
    km_rows = 16
    lane = _iota((1, PAIR), 1)
    own = (lane < HEAD_DIM, lane >= HEAD_DIM)
    spare = (HEAD_DIM, 0)

    means = []
    ones_row = jnp.where(_iota((VT_ROWS - HEAD_DIM, bq), 0) == 0, 1.0, 0.0).astype(BF16)
    for n in range(nblk):
        cols = slice(n * bq, (n + 1) * bq)
        kb = k_ref[cols, :]
        means.append(jnp.mean(kb.astype(F32), axis=0, keepdims=True))
        vt = v_ref[cols, :].astype(F32).T.astype(BF16)
        for h in range(2):
            vt_ref[h * VT_ROWS:h * VT_ROWS + HEAD_DIM, cols] = vt[h * HEAD_DIM:(h + 1) * HEAD_DIM]
            vt_ref[h * VT_ROWS + HEAD_DIM:(h + 1) * VT_ROWS, cols] = ones_row
            marker = jnp.where(lane == spare[h] + n, 1.0, 0.0).astype(BF16)
            ka_ref[h, cols, :] = jnp.where(own[h], kb, marker)
    km_hi, km_lo = _split(jnp.concatenate(means + [jnp.zeros((km_rows - nblk, PAIR), F32)], axis=0))

    blk = _iota((km_rows, bq), 0)
    causal = _iota((bq, bq), 0) <= _iota((bq, bq), 1)
    eye = jnp.where(_iota((bq, bq), 0) == _iota((bq, bq), 1), 1.0, 0.0).astype(BF16)

    q_all = q_ref[...]
    gates = []
    for h in range(2):
        qm = jnp.where(own[h], q_all, jnp.zeros_like(q_all))
        gates.append(_dot_nt(km_hi, qm) + _dot_nt(km_lo, qm))
    for c in range(nblk):
        past = blk < c
        for h in range(2):
            gate = jnp.where(past, gates[h][:, c * bq:(c + 1) * bq], NEG_INF)
            rank = jnp.zeros(gate.shape, F32)
            for m in range(c):
                row = gate[m:m + 1, :]
                tie = jnp.where(m < blk, 1.0, 0.0)
                rank = rank + jnp.where(row > gate, 1.0, jnp.where(row == gate, tie, 0.0))
            drop = jnp.where(past & (rank >= MOBA_TOPK), 1.0, 0.0)
            pad = [jnp.zeros((HEAD_DIM - km_rows, bq), F32)]
            other = [jnp.zeros((HEAD_DIM, bq), F32)]
            rows = (other + [drop] + pad) if spare[h] else ([drop] + pad + other)
            dropped = _dot_nt(eye, jnp.concatenate(rows, axis=0).astype(BF16))
            q = q_ref[c * bq:(c + 1) * bq, :]
            qa_ref[h, c * bq:(c + 1) * bq, :] = jnp.where(
                own[h], q, jnp.where(dropped > 0.5, NEG_INF, 0.0).astype(BF16))

    def score(c, h):
        return _dot_nt(ka_ref[h, 0:(c + 1) * bq, :], qa_ref[h, c * bq:(c + 1) * bq, :])

    def attend(c, h, s):
        parts = [s[n * bq:(n + 1) * bq] for n in range(c)]
        parts.append(jnp.where(causal, s[c * bq:], NEG_INF))
        top = parts[0].max(axis=0, keepdims=True)
        for part in parts[1:]:
            top = jnp.maximum(top, part.max(axis=0, keepdims=True))
        probs = jnp.concatenate([jnp.exp2((part - top).astype(BF16)) for part in parts], axis=0)
        acc = _dot(vt_ref[h * VT_ROWS:(h + 1) * VT_ROWS, 0:(c + 1) * bq], probs)
        return acc[:HEAD_DIM] * (1.0 / acc[HEAD_DIM:HEAD_DIM + 1])

    units = [(c, h) for c in range(nblk) for h in range(2)]
    ahead = 2
    pending = [score(*u) for u in units[:ahead]]
    outs = {}
    for i, (c, h) in enumerate(units):
        s = pending.pop(0)
        if i + ahead < len(units):
            pending.append(score(*units[i + ahead]))
        outs[h] = attend(c, h, s)
        if h == 1:
            o_ref[c * bq:(c + 1) * bq, :] = jnp.concatenate([outs[0], outs[1]], axis=0).T.astype(o_ref.dtype)


def moba_attention(qkv, batch, seq_len):
    n = qkv.shape[0]
    return pl.pallas_call(
        _moba_kernel,
        grid=(batch, N_PAIRS),
        in_specs=[pl.BlockSpec((seq_len, PAIR), lambda b, p: (b, p)),
                  pl.BlockSpec((seq_len, PAIR), lambda b, p: (b, N_PAIRS + p)),
                  pl.BlockSpec((seq_len, PAIR), lambda b, p: (b, 2 * N_PAIRS + p))],
        out_specs=pl.BlockSpec((seq_len, PAIR), lambda b, p: (b, p)),
        out_shape=jax.ShapeDtypeStruct((n, D_MODEL), BF16),
        scratch_shapes=[pltpu.VMEM((2 * VT_ROWS, seq_len), BF16),
                        pltpu.VMEM((2, seq_len, PAIR), BF16),
                        pltpu.VMEM((2, seq_len, PAIR), BF16)],
        compiler_params=_params("parallel", "parallel"),
        name="moba_attention",
    )(qkv, qkv, qkv)


def _mem_xattn_kernel(y_ref, wy_ref, x_ref, g_ref, wq_ref, kv_ref, wo_ref, gf_ref, rw_ref, rb_ref,
                      o_ref, h_out, meta_out, cnt_out):
    x = x_ref[...] + _dot(y_ref[...], wy_ref[...])
    xn = _rms(x, g_ref[...]).astype(BF16)
    q = (_dot(xn, wq_ref[...]) * (1.0 / math.sqrt(MEM_HEAD_DIM))).astype(BF16)
    heads = [slice(h * MEM_HEAD_DIM, (h + 1) * MEM_HEAD_DIM) for h in range(MEM_HEADS)]
    scores = [_dot_nt(q[:, sl], kv_ref[:, sl]) for sl in heads]
    outs = []
    for sl, s in zip(heads, scores):
        e = jnp.exp(s - jnp.max(s, axis=-1, keepdims=True))
        pr = e * (1.0 / jnp.sum(e, axis=-1, keepdims=True))
        outs.append(_dot(pr.astype(BF16), kv_ref[:, D_MODEL + sl.start:D_MODEL + sl.stop]).astype(BF16))
    o = jnp.concatenate(outs, axis=1)
    x2 = x + _dot(o, wo_ref[...])
    o_ref[...] = x2
    _route(x2, gf_ref, rw_ref, rb_ref, h_out, meta_out, cnt_out)


def mem_cross_attention(y, wy, x, gain, wq, kv, wo, seq_len, ffn_gain, w_grp, b_grp, w_exp, b_exp):
    n, d = x.shape
    tm = MOE_TILE
    m = kv.shape[0] // (n // seq_len)
    per_seq = seq_len // tm
    wt = jnp.zeros((LANES, d), F32).at[0:MOE_GROUPS].set(w_grp.T).at[8:8 + MOE_EXPERTS].set(w_exp.T)
    bt = jnp.zeros((LANES, 1), F32).at[0:MOE_GROUPS, 0].set(b_grp).at[8:8 + MOE_EXPERTS, 0].set(b_exp)
    const = lambda shape: pl.BlockSpec(shape, lambda i: (0, 0))
    return pl.pallas_call(
        _mem_xattn_kernel,
        grid=(n // tm,),
        in_specs=[pl.BlockSpec((tm, d), lambda i: (i, 0)),
                  const((d, d)),
                  pl.BlockSpec((tm, d), lambda i: (i, 0)),
                  const((1, d)),
                  const((d, d)),
                  pl.BlockSpec((m, 2 * d), lambda i: (i // per_seq, 0)),
                  const((d, d)),
                  const((1, d)),
                  const((LANES, d)),
                  const((LANES, 1))],
        out_specs=[pl.BlockSpec((tm, d), lambda i: (i, 0)),
                   pl.BlockSpec((tm, d), lambda i: (i, 0)),
                   pl.BlockSpec((8, tm), lambda i: (0, i)),
                   pl.BlockSpec((MOE_EXPERTS, LANES), lambda i: (i, 0))],
        out_shape=[jax.ShapeDtypeStruct((n, d), F32),
                   jax.ShapeDtypeStruct((n, d), BF16),
                   jax.ShapeDtypeStruct((8, n), F32),
                   jax.ShapeDtypeStruct((n // tm * MOE_EXPERTS, LANES), F32)],
        compiler_params=_params("parallel"),
        name="mem_cross_attention",
    )(y, wy.astype(BF16), x, gain.reshape(1, d), wq.astype(BF16), kv, wo.astype(BF16),
      ffn_gain.reshape(1, d), wt, bt)


def _route(x, g_ref, w_ref, b_ref, h_out, meta_out, cnt_out):
    tm = x.shape[0]
    h2 = _rms(x, g_ref[...])
    hi = h2.astype(BF16)
    h_out[...] = hi
    lo = (h2 - hi.astype(F32)).astype(BF16)
    w_hi, w_lo = _split(w_ref[...])
    lg = _dot_nt(w_hi, hi) + _dot_nt(w_hi, lo) + _dot_nt(w_lo, hi) + b_ref[...]
    row = _iota((8, tm), 0).astype(F32)

    def first_argmax(val, vmax):
        return jnp.min(jnp.where(val == vmax, row, 8.0), axis=0, keepdims=True)

    gl = jnp.where(row < MOE_GROUPS, lg[0:8], -jnp.inf)
    gmax = jnp.max(gl, axis=0, keepdims=True)
    p_g = 1.0 / jnp.sum(jnp.exp(gl - gmax), axis=0, keepdims=True)
    gidx = first_argmax(gl, gmax)
    el = jnp.zeros((8, tm), F32)
    for g in range(MOE_GROUPS):
        el = el + jnp.where(gidx == g, lg[8 + 8 * g:16 + 8 * g], 0.0)
    ee = jnp.exp(el - jnp.max(el, axis=0, keepdims=True))
    pe = ee / jnp.sum(ee, axis=0, keepdims=True)
    p1 = jnp.max(pe, axis=0, keepdims=True)
    i1 = first_argmax(pe, p1)
    pe2 = jnp.where(row == i1, -1.0, pe)
    p2 = jnp.max(pe2, axis=0, keepdims=True)
    i2 = first_argmax(pe2, p2)
    e1 = gidx * MOE_EPG + i1
    e2 = gidx * MOE_EPG + i2
    gate1 = p_g * p1 / (p1 + p2)
    gate2 = p_g * p2 / (p1 + p2)

    erow = _iota((MOE_EXPERTS, tm), 0).astype(F32)
    oh1 = erow == e1
    oh2 = erow == e2
    oh = jnp.where(oh1 | oh2, 1.0, 0.0)
    before = jnp.where(_iota((tm, tm), 0) < _iota((tm, tm), 1), 1.0, 0.0).astype(BF16)
    cnt_before = _dot(oh.astype(BF16), before)
    lr1 = jnp.sum(jnp.where(oh1, cnt_before, 0.0), axis=0, keepdims=True)
    lr2 = jnp.sum(jnp.where(oh2, cnt_before, 0.0), axis=0, keepdims=True)
    zero = jnp.zeros((1, tm), F32)
    meta_out[...] = jnp.concatenate([e1, e2, gate1, gate2, lr1, lr2, zero, zero], axis=0)
    cnt_out[...] = jnp.broadcast_to(jnp.sum(oh, axis=1, keepdims=True), (MOE_EXPERTS, LANES))


def _chunk_copies(tab_ref, t, make_copy, act):
    base = t * RUN_TAB
    for ci, ck in enumerate(RUN_CHUNKS):
        first = base + len(RUN_CHUNKS) + ci * 2 * MOE_EXPERTS

        def body(i, _, first=first, ck=ck):
            src = tab_ref[first + 2 * i]
            dst = tab_ref[first + 2 * i + 1]
            act(make_copy(pl.multiple_of(src, MOE_ALIGN), pl.multiple_of(dst, MOE_ALIGN), ck))
            return 0

        lax.fori_loop(0, tab_ref[base + ci], body, 0)


def _run_copies(tab_ref, first, count, chunks, make_copy, act):
    def body(e, _):
        base = first + e * 3
        length = tab_ref[base]
        src = tab_ref[base + 1]
        dst = tab_ref[base + 2]
        for ck in chunks:
            @pl.when((length & ck) != 0)
            def _():
                off = length & (-2 * ck)
                act(make_copy(pl.multiple_of(src + off, MOE_ALIGN), pl.multiple_of(dst + off, MOE_ALIGN), ck))
        return 0
    lax.fori_loop(0, count, body, 0)


def _start(cp):
    cp.start()


def _wait(cp):
    cp.wait()


def _dispatch_kernel(tab_ref, h_ref, pos_ref, rows_hbm, sorted_ref, zero_ref, sem, zsem):
    t = pl.program_id(0)
    nt = pl.num_programs(0)
    slot = t % 2
    r = _iota((MOE_SORTED, MOE_TILE), 0)
    perm = jnp.where((r == pos_ref[0:1, :]) | (r == pos_ref[1:2, :]), 1.0, 0.0).astype(BF16)
    sorted_ref[slot] = _dot(perm, h_ref[...]).astype(BF16)

    def run_copy(s):
        def make(src, dst, ck):
            return pltpu.make_async_copy(sorted_ref.at[s, pl.ds(src, ck)], rows_hbm.at[pl.ds(dst, ck)], sem.at[s])
        return make

    def zero_copy(src, dst, ck):
        del src
        return pltpu.make_async_copy(zero_ref.at[pl.ds(0, ck)], rows_hbm.at[pl.ds(dst, ck)], zsem)

    _chunk_copies(tab_ref, t, run_copy(slot), _start)

    @pl.when(t > 0)
    def _():
        _chunk_copies(tab_ref, t - 1, run_copy(1 - slot), _wait)

    @pl.when(t == nt - 1)
    def _():
        zero_ref[...] = jnp.zeros_like(zero_ref)
        pad = nt * RUN_TAB
        _run_copies(tab_ref, pad, MOE_EXPERTS, PAD_CHUNKS, zero_copy, _start)
        spare = pad + MOE_EXPERTS * 3
        n_spare = tab_ref[spare]

        def spare_copy(i):
            dst = pl.multiple_of(tab_ref[spare + 2] + i * PAD_ROWS, PAD_ROWS)
            return zero_copy(0, dst, PAD_ROWS)

        lax.fori_loop(0, n_spare, lambda i, c: (_start(spare_copy(i)), c)[1], 0)
        _chunk_copies(tab_ref, t, run_copy(slot), _wait)
        _run_copies(tab_ref, pad, MOE_EXPERTS, PAD_CHUNKS, zero_copy, _wait)
        lax.fori_loop(0, n_spare, lambda i, c: (_wait(spare_copy(i)), c)[1], 0)


def moe_dispatch(tab, h2, pos, n_rows):
    n, d = h2.shape
    tm = MOE_TILE
    return pl.pallas_call(
        _dispatch_kernel,
        grid_spec=pltpu.PrefetchScalarGridSpec(
            num_scalar_prefetch=1,
            grid=(n // tm,),
            in_specs=[pl.BlockSpec((tm, d), lambda i, tab: (i, 0)),
                      pl.BlockSpec((8, tm), lambda i, tab: (0, i))],
            out_specs=pl.BlockSpec(memory_space=pl.ANY),
            scratch_shapes=[pltpu.VMEM((2, MOE_SORTED, d), BF16), pltpu.VMEM((PAD_ROWS, d), BF16),
                            pltpu.SemaphoreType.DMA((2,)), pltpu.SemaphoreType.DMA],
        ),
        out_shape=jax.ShapeDtypeStruct((n_rows, d), BF16),
        compiler_params=_params("arbitrary"),
        name="moe_dispatch",
    )(tab, h2, pos)


def _expert_kernel(be_ref, nu_ref, x_ref, w1_ref, w3_ref, w2_ref, y_ref, w1b, w3b, w2b):
    b = pl.program_id(0)
    used = b < nu_ref[0]
    new_expert = (b == 0) | (be_ref[b] != be_ref[jnp.maximum(b - 1, 0)])

    @pl.when(used & new_expert)
    def _():
        w1b[...] = w1_ref[0, 0].astype(BF16)
        w3b[...] = w3_ref[0, 0].astype(BF16)
        w2b[...] = w2_ref[0, 0].astype(BF16)

    @pl.when(used)
    def _():
        xb = x_ref[...]
        a = _dot(xb, w1b[...])
        c = _dot(xb, w3b[...])
        hid = (a * jax.nn.sigmoid(a) * c).astype(BF16)
        y_ref[...] = _dot(hid, w2b[...]).astype(y_ref.dtype)

    @pl.when(jnp.logical_not(used))
    def _():
        y_ref[...] = jnp.zeros_like(y_ref)


def moe_experts(block_expert, n_used, rows, layer, w1, w3, w2):
    n_rows = rows.shape[0]
    nb = n_rows // MOE_ROWS
    d, ff = w1.shape[2], w1.shape[3]

    def xmap(b, be, nu):
        return (jnp.minimum(b, nu[0] - 1), 0)

    def wmap(b, be, nu):
        return (layer, be[jnp.minimum(b, nu[0] - 1)], 0, 0)

    return pl.pallas_call(
        _expert_kernel,
        grid_spec=pltpu.PrefetchScalarGridSpec(
            num_scalar_prefetch=2,
            grid=(nb,),
            in_specs=[pl.BlockSpec((MOE_ROWS, d), xmap),
                      pl.BlockSpec((1, 1, d, ff), wmap),
                      pl.BlockSpec((1, 1, d, ff), wmap),
                      pl.BlockSpec((1, 1, ff, d), wmap)],
            out_specs=pl.BlockSpec((MOE_ROWS, d), lambda b, be, nu: (b, 0)),
            scratch_shapes=[pltpu.VMEM((d, ff), BF16), pltpu.VMEM((d, ff), BF16), pltpu.VMEM((ff, d), BF16)],
        ),
        out_shape=jax.ShapeDtypeStruct((n_rows, d), BF16),
        compiler_params=_params("arbitrary"),
        name="moe_experts",
    )(block_expert, n_used, rows, w1, w3, w2)


def _combine_kernel(final_norm, tab_ref, y_hbm, pos_ref, gate_ref, x_ref, g_ref, o_ref, ys_ref, sem):
    t = pl.program_id(0)
    nt = pl.num_programs(0)
    slot = t % 2

    def run_copy(s):
        def make(src, dst, ck):
            return pltpu.make_async_copy(y_hbm.at[pl.ds(dst, ck)], ys_ref.at[s, pl.ds(src, ck)], sem.at[s])
        return make

    @pl.when(t == 0)
    def _():
        ys_ref[...] = jnp.zeros_like(ys_ref)
        _chunk_copies(tab_ref, 0, run_copy(0), _start)

    @pl.when(t + 1 < nt)
    def _():
        _chunk_copies(tab_ref, t + 1, run_copy(1 - slot), _start)

    r = _iota((MOE_SORTED, MOE_TILE), 0)
    hit1 = r == pos_ref[0:1, :]
    hit2 = r == pos_ref[1:2, :]
    wgt = jnp.where(hit1, gate_ref[2:3, :], jnp.where(hit2, gate_ref[3:4, :], 0.0)).astype(BF16)
    _chunk_copies(tab_ref, t, run_copy(slot), _wait)
    out = x_ref[...] + _dot_tn(wgt, ys_ref[slot])
    if final_norm:
        out = _rms(out, g_ref[...])
    o_ref[...] = out


def moe_combine(tab, y, pos, meta, x, final_gain):
    n, d = x.shape
    tm = MOE_TILE
    final_norm = final_gain is not None
    gain = (final_gain if final_norm else jnp.ones((d,), F32)).reshape(1, d)
    return pl.pallas_call(
        functools.partial(_combine_kernel, final_norm),
        grid_spec=pltpu.PrefetchScalarGridSpec(
            num_scalar_prefetch=1,
            grid=(n // tm,),
            in_specs=[pl.BlockSpec(memory_space=pl.ANY),
                      pl.BlockSpec((8, tm), lambda i, tab: (0, i)),
                      pl.BlockSpec((8, tm), lambda i, tab: (0, i)),
                      pl.BlockSpec((tm, d), lambda i, tab: (i, 0)),
                      pl.BlockSpec((1, d), lambda i, tab: (0, 0))],
            out_specs=pl.BlockSpec((tm, d), lambda i, tab: (i, 0)),
            scratch_shapes=[pltpu.VMEM((2, MOE_SORTED, d), BF16), pltpu.SemaphoreType.DMA((2,))],
        ),
        out_shape=jax.ShapeDtypeStruct((n, d), F32),
        compiler_params=_params("arbitrary"),
        name="moe_combine",
    )(tab, y, pos, meta, x, gain)


def hierarchical_moe(x, h2, meta, cnt, layer, w1, w3, w2, final_gain):
    n, d = x.shape
    tm = MOE_TILE
    nt = n // tm

    cnt = cnt.reshape(nt, MOE_EXPERTS, LANES)[:, :, 0].astype(I32)
    run = (cnt + MOE_ALIGN - 1) // MOE_ALIGN * MOE_ALIGN
    src = jnp.cumsum(run, axis=1) - run
    before = jnp.cumsum(run, axis=0) - run
    total = jnp.sum(run, axis=0)
    padded = (total + MOE_ROWS - 1) // MOE_ROWS * MOE_ROWS
    pad_end = jnp.cumsum(padded)
    dst = (pad_end - padded)[None, :] + before
    max_rows = 2 * n + nt * MOE_EXPERTS * (MOE_ALIGN - 1) + MOE_EXPERTS * (MOE_ROWS - 1)
    nb = -(-max_rows // MOE_ROWS)
    sizes = jnp.array(RUN_CHUNKS, I32)[None, :, None]
    has = (run[:, None, :] & sizes) != 0
    above = run[:, None, :] & (-2 * sizes)
    place = jnp.where(has, jnp.cumsum(has.astype(I32), axis=2) - 1, -1)
    front = place[..., None] == jnp.arange(MOE_EXPERTS, dtype=I32)
    c_src = jnp.sum(jnp.where(front, (src[:, None, :] + above)[..., None], 0), axis=2)
    c_dst = jnp.sum(jnp.where(front, (dst[:, None, :] + above)[..., None], 0), axis=2)
    chunks = jnp.stack([c_src, c_dst], axis=-1).reshape(nt, -1)
    runs = jnp.concatenate([jnp.sum(has.astype(I32), axis=2), chunks], axis=1).reshape(-1)
    zero = jnp.zeros((MOE_EXPERTS,), I32)
    pads = jnp.stack([padded - total, zero, pad_end - padded + total], axis=-1)
    spare = jnp.stack([(nb * MOE_ROWS - pad_end[-1]) // PAD_ROWS, zero[0], pad_end[-1]])
    tab = jnp.concatenate([runs, pads.reshape(-1), spare]).astype(I32)
    n_used = (pad_end[-1] // MOE_ROWS).astype(I32).reshape(1)
    block_start = jnp.arange(nb, dtype=I32) * MOE_ROWS
    block_expert = jnp.minimum(
        jnp.sum((pad_end[None, :] <= block_start[:, None]).astype(I32), axis=1), MOE_EXPERTS - 1).astype(I32)
    e = meta[0:2].astype(I32).reshape(2, nt, tm, 1)
    hit = e == jnp.arange(MOE_EXPERTS, dtype=I32)
    pos = jnp.sum(jnp.where(hit, src[None, :, None, :], 0), axis=-1).reshape(2, n) + meta[4:6].astype(I32)
    pos8 = jnp.zeros((8, n), I32).at[0:2].set(pos)

    rows = moe_dispatch(tab, h2, pos8, nb * MOE_ROWS)
    y = moe_experts(block_expert, n_used, rows, layer, w1, w3, w2)
    return moe_combine(tab, y, pos8, meta, x, final_gain)


def kernel(x, mem, positions, ln_mix, ln_mem, ln_memkv, ln_ffn, rw_mu, rw_w0, rw_w1, rw_w2, rw_a0, rw_a1, rw_a2, rw_g1, rw_g2, rw_kk, rw_ka, rw_rk, rw_wrkv, rw_lnx_g, rw_lnx_b, rw_wo, mb_wqkv, mb_wo, mx_wq, mx_wkv, mx_wo, moe_wg, moe_bg, moe_we, moe_be, moe_w1, moe_w3, moe_w2, ln_f):
    B, T, C = x.shape
    n = B * T
    depth = ln_mix.shape[0]
    xf = x.reshape(n, C)
    memf = mem.reshape(-1, C)
    for i in range(depth):
        j = i // 2
        if i % 2 == 0:
            r, k, v, kk, akk, lw, g = rwkv_mix(xf, T, ln_mix[i], rw_mu[j], rw_w0[j], rw_w1[j], rw_w2[j],
                                               rw_a0[j], rw_a1[j], rw_a2[j], rw_g1[j], rw_g2[j],
                                               rw_kk[j], rw_ka[j], rw_wrkv[j])
            y = rwkv_recurrence(r, k, v, kk, akk, lw, g, rw_lnx_g[j], rw_lnx_b[j], rw_rk[j], B, T)
            w_mix_o = rw_wo[j]
        else:
            qkv = qkv_rope(xf, ln_mix[i], positions, mb_wqkv[j])
            y = moba_attention(qkv, B, T)
            w_mix_o = mb_wo[j]
        kv = norm_linear(memf, ln_memkv[i], mx_wkv[i].astype(BF16))
        xf, h2, meta, cnt = mem_cross_attention(y, w_mix_o, xf, ln_mem[i], mx_wq[i], kv, mx_wo[i], T,
                                                ln_ffn[i], moe_wg[i], moe_bg[i], moe_we[i], moe_be[i])
        xf = hierarchical_moe(xf, h2, meta, cnt, i, moe_w1, moe_w3, moe_w2, ln_f if i == depth - 1 else None)
    return xf.reshape(B, T, C)
```

```python
import functools
import math

import jax
import jax.numpy as jnp
from jax import lax
from jax.experimental import pallas as pl
from jax.experimental.pallas import tpu as pltpu

F32 = jnp.float32
BF16 = jnp.bfloat16
I32 = jnp.int32

D_MODEL = 1024
HEAD_DIM = 64
PAIR = 2 * HEAD_DIM
N_PAIRS = D_MODEL // PAIR
RWKV_GN_EPS = 64e-5
RWKV_CHUNK = 64
RWKV_CHUNKS_PER_STEP = 8
MOBA_BLOCK = 256
MOBA_TOPK = 3
VT_ROWS = HEAD_DIM + 16
ROPE_THETA = 10000.0
MEM_HEADS = 4
MEM_HEAD_DIM = D_MODEL // MEM_HEADS
MOE_GROUPS = 4
MOE_EPG = 8
MOE_EXPERTS = MOE_GROUPS * MOE_EPG
RMS_EPS = 1e-6
NEG_INF = -1e30

LANES = 128
ROW_ALIGN = 8
MOE_TILE = 512
MOE_ROWS = 1024
MOE_ALIGN = 16
MOE_SORTED = -(-(2 * MOE_TILE + MOE_EXPERTS * (MOE_ALIGN - 1)) // MOE_ALIGN) * MOE_ALIGN
RUN_CHUNKS = tuple(MOE_ALIGN << s for s in range((MOE_TILE // MOE_ALIGN).bit_length() - 1, -1, -1))
RUN_TAB = len(RUN_CHUNKS) * (1 + 2 * MOE_EXPERTS)
PAD_ROWS = MOE_ROWS // 2
PAD_CHUNKS = tuple(MOE_ALIGN << s for s in range((PAD_ROWS // MOE_ALIGN).bit_length() - 1, -1, -1))
VMEM_LIMIT = 56 * 2 ** 20


def _params(*sem):
    return pltpu.CompilerParams(dimension_semantics=sem, vmem_limit_bytes=VMEM_LIMIT)


def _iota(shape, dim):
    return lax.broadcasted_iota(I32, shape, dim)


def _dot(a, b):
    return jnp.dot(a, b, preferred_element_type=F32)


def _dot_nt(a, b):
    return lax.dot_general(a, b, (((1,), (1,)), ((), ())), preferred_element_type=F32)


def _dot_tn(a, b):
    return lax.dot_general(a, b, (((0,), (0,)), ((), ())), preferred_element_type=F32)


def _split(x):
    hi = x.astype(BF16)
    lo = (x - hi.astype(F32)).astype(BF16)
    return hi, lo


def _rms(x, g):
    return x * lax.rsqrt(jnp.mean(x * x, axis=-1, keepdims=True) + RMS_EPS) * g


def _norm_linear_kernel(x_ref, g_ref, w_ref, o_ref):
    xn = _rms(x_ref[...], g_ref[...]).astype(BF16)
    o_ref[...] = _dot(xn, w_ref[...]).astype(o_ref.dtype)


def norm_linear(x, g, w, tm=256):
    n, k = x.shape
    dout = w.shape[1]
    return pl.pallas_call(
        _norm_linear_kernel,
        grid=(n // tm,),
        in_specs=[pl.BlockSpec((tm, k), lambda i: (i, 0)),
                  pl.BlockSpec((1, k), lambda i: (0, 0)),
                  pl.BlockSpec((k, dout), lambda i: (0, 0))],
        out_specs=pl.BlockSpec((tm, dout), lambda i: (i, 0)),
        out_shape=jax.ShapeDtypeStruct((n, dout), BF16),
        compiler_params=_params("parallel"),
        name="norm_linear",
    )(x, g.reshape(1, k), w)


def _head_sums(x_sq):
    rows = x_sq.shape[0]
    r = _iota((PAIR, PAIR), 0) // HEAD_DIM
    c = _iota((PAIR, PAIR), 1) // HEAD_DIM
    bd = jnp.where(r == c, 1.0, 0.0).astype(BF16)
    hi, lo = _split(x_sq)
    stacked = jnp.concatenate([part[:, p * PAIR:(p + 1) * PAIR] for part in (hi, lo) for p in range(N_PAIRS)], axis=0)
    sums = _dot(stacked, bd)
    sums = sums[:N_PAIRS * rows] + sums[N_PAIRS * rows:]
    return jnp.concatenate([sums[p * rows:(p + 1) * rows] for p in range(N_PAIRS)], axis=1)


def _rwkv_mix_kernel(seq_len, x_ref, xp_ref, g_ref, mu_ref, vec_ref, wr_ref, wk_ref, wv_ref,
                     l1_ref, w2_ref, a2_ref, g2_ref,
                     r_out, k_out, v_out, kk_out, akk_out, lw_out, g_out):
    i = pl.program_id(0)
    tm = x_ref.shape[0]
    gain = g_ref[...]
    h = _rms(x_ref[...], gain)
    hp = _rms(xp_ref[...], gain)[ROW_ALIGN - 1:ROW_ALIGN, :]
    hp = jnp.where((i * tm) % seq_len == 0, 0.0, hp)
    hs = pltpu.roll(h, 1, 0)
    hs = jnp.where(_iota((tm, 1), 0) == 0, hp, hs)
    hb = h.astype(BF16)
    dxb = (hs - h).astype(BF16)
    mub = mu_ref[...].astype(BF16)

    def mix(s):
        return hb + dxb * mub[s:s + 1, :]

    vec = vec_ref[...]
    w0, a0, k_k, k_a = vec[0:1], vec[1:2], vec[2:3], vec[3:4]
    l1 = l1_ref[...]
    tw = _dot(mix(1), l1[:, 0:64])
    ta = _dot(mix(4), l1[:, 64:128])
    tg = _dot(mix(5), l1[:, 128:256])
    r = _dot(mix(0), wr_ref[...])
    k = _dot(mix(2), wk_ref[...])
    v = _dot(mix(3), wv_ref[...])
    lw = -math.exp(-0.5) * jax.nn.sigmoid(w0 + _dot(jnp.tanh(tw).astype(BF16), w2_ref[...]))
    a = jax.nn.sigmoid(a0 + _dot(ta.astype(BF16), a2_ref[...]))
    g = _dot(jax.nn.sigmoid(tg).astype(BF16), g2_ref[...])

    kk = k * k_k
    kkn = kk * lax.rsqrt(jnp.maximum(_head_sums(kk * kk), 1e-24))
    kk_out[...] = kkn.astype(kk_out.dtype)
    akk_out[...] = (kkn * a).astype(akk_out.dtype)
    r_out[...] = r.astype(r_out.dtype)
    k_out[...] = (k * (1.0 + (a - 1.0) * k_a)).astype(k_out.dtype)
    v_out[...] = v.astype(v_out.dtype)
    lw_out[...] = lw
    g_out[...] = g.astype(g_out.dtype)


def rwkv_mix(x, seq_len, gain, mu, w0, w1, w2, a0, a1, a2, g1, g2, k_k, k_a, w_rkv, tm=512):
    n, d = x.shape
    mu8 = jnp.zeros((8, d), F32).at[:6].set(mu)
    vec = jnp.zeros((8, d), F32).at[0].set(w0).at[1].set(a0).at[2].set(k_k).at[3].set(k_a)
    l1 = jnp.concatenate([w1, a1, g1], axis=1).astype(BF16)
    wb = w_rkv.astype(BF16)
    row = pl.BlockSpec((tm, d), lambda i: (i, 0))
    full = lambda a: pl.BlockSpec(a.shape, lambda i: (0,) * a.ndim)
    args = (x, x, gain.reshape(1, d), mu8, vec, wb[0], wb[1], wb[2], l1,
            w2.astype(BF16), a2.astype(BF16), g2.astype(BF16))
    in_specs = [row, pl.BlockSpec((ROW_ALIGN, d), lambda i: (jnp.maximum(i * (tm // ROW_ALIGN) - 1, 0), 0))]
    in_specs += [full(a) for a in args[2:]]
    outs = [jax.ShapeDtypeStruct((n, d), BF16)] * 5 + [jax.ShapeDtypeStruct((n, d), F32),
                                                       jax.ShapeDtypeStruct((n, d), BF16)]
    return pl.pallas_call(
        functools.partial(_rwkv_mix_kernel, seq_len),
        grid=(n // tm,),
        in_specs=in_specs,
        out_specs=[row] * 7,
        out_shape=outs,
        compiler_params=_params("parallel"),
        name="rwkv_mix",
    )(*args)


def _rwkv_rec_kernel(r_ref, k_ref, v_ref, kk_ref, akk_ref, lw_ref, g_ref, vec_ref, y_ref, s_ref):
    c = pl.program_id(1)
    L = RWKV_CHUNK

    @pl.when(c == 0)
    def _():
        s_ref[...] = jnp.zeros_like(s_ref)

    lane = _iota((1, PAIR), 1)
    m0 = lane < HEAD_DIM
    ri = _iota((2 * L, 2 * L), 0)
    ci = _iota((2 * L, 2 * L), 1)
    same = (ri // L) == (ci // L)
    strict = same & (ci < ri)
    incl = same & (ci <= ri)
    eye = jnp.where(ri == ci, 1.0, 0.0)
    tri = jnp.where(_iota((L, L), 1) <= _iota((L, L), 0), 1.0, 0.0).astype(BF16)
    rb = _iota((PAIR, PAIR), 0) // HEAD_DIM
    cb = _iota((PAIR, PAIR), 1) // HEAD_DIM
    bd = rb == cb

    def stack(x):
        return jnp.concatenate([jnp.where(m0, x, 0.0), jnp.where(m0, 0.0, x)], axis=0)

    def fold(x):
        return x[:L] + x[L:]

    def head_mean(x):
        s0 = jnp.sum(jnp.where(m0, x, 0.0), axis=-1, keepdims=True)
        s1 = jnp.sum(jnp.where(m0, 0.0, x), axis=-1, keepdims=True)
        return jnp.where(m0, s0, s1) * (1.0 / HEAD_DIM)

    vec = vec_ref[...]
    n_sub = r_ref.shape[0] // L
    pairs = range(n_sub * N_PAIRS)
    rws = [slice((q // N_PAIRS) * L, (q // N_PAIRS + 1) * L) for q in pairs]
    sls = [slice((q % N_PAIRS) * PAIR, (q % N_PAIRS + 1) * PAIR) for q in pairs]
    lw = [lw_ref[rws[p], sls[p]] for p in pairs]
    cum = []
    for p in pairs:
        lw_hi, lw_lo = _split(lw[p])
        cum.append(_dot(tri, lw_hi) + _dot(tri, lw_lo))
    dec_all, x_kap, x_r, a_t, k_t, sc = [], [], [], [], [], []
    for p in pairs:
        rw, sl = rws[p], sls[p]
        dec = jnp.exp(cum[p])
        inv = jnp.exp(-cum[p])
        dec_prev = jnp.exp(cum[p] - lw[p])
        dec_all.append(dec[L - 1:L, :])
        x_kap.append(stack(kk_ref[rw, sl].astype(F32) * dec_prev).astype(BF16))
        x_r.append(stack(r_ref[rw, sl].astype(F32) * dec))
        a_t.append((akk_ref[rw, sl].astype(F32) * inv).astype(BF16))
        k_t.append((k_ref[rw, sl].astype(F32) * inv).astype(BF16))
        xs = jnp.concatenate([x_kap[p], x_r[p].astype(BF16)], axis=0)
        ys = jnp.concatenate([a_t[p], k_t[p]], axis=0)
        sc.append(_dot_nt(xs, ys))

    def both_heads(first, second, keep):
        upper_left = keep & (ri < L)
        lower_right = keep & (ri >= L)
        return jnp.where(upper_left, first, jnp.where(lower_right, second, 0.0))

    mp, m_kk = [], []
    for p in pairs:
        top = sc[p][:2 * L]
        top_r = pltpu.roll(top, L, 1)
        mp.append(-both_heads(top, top_r, strict))
        m_kk.append(both_heads(top_r, top, strict).astype(BF16))
    s_ra, s_rk = [], []
    for p in pairs:
        bot = sc[p][2 * L:]
        bot_r = pltpu.roll(bot, L, 1)
        s_ra.append(both_heads(bot, bot_r, incl).astype(BF16))
        s_rk.append(both_heads(bot_r, bot, incl).astype(BF16))
    v_st = [stack(v_ref[rws[p], sls[p]].astype(F32)).astype(BF16) for p in pairs]
    bo = [_dot(jnp.concatenate([m_kk[p], s_rk[p]], axis=0), v_st[p]) for p in pairs]
    b1 = [bo[p][:2 * L].astype(BF16) for p in pairs]
    o1 = [bo[p][2 * L:] for p in pairs]

    t_inv = [eye + mp[p] for p in pairs]
    mpb = [mp[p].astype(BF16) for p in pairs]
    mpb = [_dot(mpb[p], mpb[p]).astype(BF16) for p in pairs]
    steps = int(math.log2(L)) - 1
    for j in range(steps):
        if j + 1 < steps:
            both = [_dot(mpb[p], jnp.concatenate([mpb[p], t_inv[p].astype(BF16)], axis=1)) for p in pairs]
            mpb = [both[p][:, :2 * L].astype(BF16) for p in pairs]
            t_inv = [t_inv[p] + both[p][:, 2 * L:] for p in pairs]
        else:
            t_inv = [t_inv[p] + _dot(mpb[p], t_inv[p].astype(BF16)) for p in pairs]

    wu = [_dot(t_inv[p].astype(BF16), jnp.concatenate([x_kap[p], b1[p]], axis=1)).astype(BF16)
          for p in pairs]
    corr = [_dot(s_ra[p], wu[p]) for p in pairs]
    g_m, h0t, r_hat, o0 = [], [], [], []
    for p in pairs:
        r_hat.append(fold(x_r[p] - corr[p][:, :PAIR]).astype(BF16))
        o0.append(fold(o1[p] - corr[p][:, PAIR:]))
        w_f = fold(wu[p][:, :PAIR])
        u0_f = fold(wu[p][:, PAIR:])
        g_m.append(jnp.where(bd, _dot_tn(a_t[p], w_f), 0.0).astype(BF16))
        vu = jnp.concatenate([v_ref[rws[p], sls[p]], -u0_f], axis=0)
        h0t.append(jnp.where(bd, _dot_tn(vu, jnp.concatenate([k_t[p], a_t[p]], axis=0)), 0.0))
    o = []
    for p in pairs:
        hp = p % N_PAIRS
        s = s_ref[hp]
        sb = s.astype(BF16)
        o.append(_dot_nt(r_hat[p], sb) + o0[p])
        s_ref[hp] = (s - _dot_nt(sb, g_m[p]) + h0t[p]) * dec_all[p]
    for p in pairs:
        rw, sl = rws[p], sls[p]
        mean = head_mean(o[p])
        cen = o[p] - mean
        var = head_mean(cen * cen)
        gn = cen * lax.rsqrt(var + RWKV_GN_EPS) * vec[0:1, sl] + vec[1:2, sl]
        r = r_ref[rw, sl].astype(F32)
        k = k_ref[rw, sl].astype(F32)
        bonus = head_mean(r * k * vec[2:3, sl]) * HEAD_DIM * v_ref[rw, sl].astype(F32)
        y_ref[rw, sl] = ((gn + bonus) * g_ref[rw, sl].astype(F32)).astype(y_ref.dtype)


def rwkv_recurrence(r, k, v, kk, akk, lw, g, lnx_g, lnx_b, r_k, batch, seq_len):
    n, d = r.shape
    rows = RWKV_CHUNK * RWKV_CHUNKS_PER_STEP
    nc = seq_len // rows
    vec = jnp.zeros((8, d), F32).at[0].set(lnx_g).at[1].set(lnx_b).at[2].set(r_k.reshape(d))
    blk = pl.BlockSpec((rows, d), lambda b, c: (b * nc + c, 0))
    return pl.pallas_call(
        _rwkv_rec_kernel,
        grid=(batch, nc),
        in_specs=[blk] * 7 + [pl.BlockSpec((8, d), lambda b, c: (0, 0))],
        out_specs=blk,
        out_shape=jax.ShapeDtypeStruct((n, d), BF16),
        scratch_shapes=[pltpu.VMEM((N_PAIRS, PAIR, PAIR), F32)],
        compiler_params=_params("parallel", "arbitrary"),
        name="rwkv_recurrence",
    )(r, k, v, kk, akk, lw, g, vec)


def _qkv_rope_kernel(x_ref, g_ref, pos_ref, inv_ref, w_ref, o_ref):
    d = x_ref.shape[1]
    xn = _rms(x_ref[...], g_ref[...]).astype(BF16)
    y = [_dot(xn, w_ref[:, part * d:(part + 1) * d]) for part in range(3)]
    ang = pos_ref[...].astype(F32) * inv_ref[...]
    first = (_iota((1, PAIR), 1) % HEAD_DIM) < HEAD_DIM // 2
    sn = jnp.sin(ang)
    cos = jnp.cos(ang)
    sin_lo = jnp.where(first, -sn, 0.0)
    sin_hi = jnp.where(first, 0.0, sn)
    for part in range(2):
        for p in range(N_PAIRS):
            t = y[part][:, p * PAIR:(p + 1) * PAIR]
            rot = (t * cos + pltpu.roll(t, PAIR - HEAD_DIM // 2, 1) * sin_lo
                   + pltpu.roll(t, HEAD_DIM // 2, 1) * sin_hi)
            o_ref[:, part * d + p * PAIR:part * d + (p + 1) * PAIR] = rot.astype(o_ref.dtype)
    o_ref[:, 2 * d:] = y[2].astype(o_ref.dtype)


def qkv_rope(x, gain, positions, w_qkv, tm=1024):
    n, d = x.shape
    half = HEAD_DIM // 2
    inv = ROPE_THETA ** (-jnp.arange(half, dtype=F32) * 2.0 / HEAD_DIM)
    inv128 = jnp.tile(inv, PAIR // half).reshape(1, PAIR)
    col_scale = jnp.where(jnp.arange(3 * d) < d, math.log2(math.e) / math.sqrt(HEAD_DIM), 1.0).astype(F32)
    return pl.pallas_call(
        _qkv_rope_kernel,
        grid=(n // tm,),
        in_specs=[pl.BlockSpec((tm, d), lambda i: (i, 0)),
                  pl.BlockSpec((1, d), lambda i: (0, 0)),
                  pl.BlockSpec((tm, 1), lambda i: (i, 0)),
                  pl.BlockSpec((1, PAIR), lambda i: (0, 0)),
                  pl.BlockSpec((d, 3 * d), lambda i: (0, 0))],
        out_specs=pl.BlockSpec((tm, 3 * d), lambda i: (i, 0)),
        out_shape=jax.ShapeDtypeStruct((n, 3 * d), BF16),
        compiler_params=_params("parallel"),
        name="qkv_rope",
    )(x, gain.reshape(1, d), positions.reshape(n, 1), inv128, (w_qkv * col_scale).astype(BF16))


def _moba_kernel(q_ref, k_ref, v_ref, o_ref, vt_ref, ka_ref, qa_ref):
    nblk = k_ref.shape[0] // MOBA_BLOCK
    bq = MOBA_BLOCK
    km_rows = 16
    lane = _iota((1, PAIR), 1)
    own = (lane < HEAD_DIM, lane >= HEAD_DIM)
    spare = (HEAD_DIM, 0)

    means = []
    ones_row = jnp.where(_iota((VT_ROWS - HEAD_DIM, bq), 0) == 0, 1.0, 0.0).astype(BF16)
    for n in range(nblk):
        cols = slice(n * bq, (n + 1) * bq)
        kb = k_ref[cols, :]
        means.append(jnp.mean(kb.astype(F32), axis=0, keepdims=True))
        vt = v_ref[cols, :].astype(F32).T.astype(BF16)
        for h in range(2):
            vt_ref[h * VT_ROWS:h * VT_ROWS + HEAD_DIM, cols] = vt[h * HEAD_DIM:(h + 1) * HEAD_DIM]
            vt_ref[h * VT_ROWS + HEAD_DIM:(h + 1) * VT_ROWS, cols] = ones_row
            marker = jnp.where(lane == spare[h] + n, 1.0, 0.0).astype(BF16)
            ka_ref[h, cols, :] = jnp.where(own[h], kb, marker)
    km_hi, km_lo = _split(jnp.concatenate(means + [jnp.zeros((km_rows - nblk, PAIR), F32)], axis=0))

    blk = _iota((km_rows, bq), 0)
    causal = _iota((bq, bq), 0) <= _iota((bq, bq), 1)
    eye = jnp.where(_iota((bq, bq), 0) == _iota((bq, bq), 1), 1.0, 0.0).astype(BF16)

    q_all = q_ref[...]
    gates = []
    for h in range(2):
        qm = jnp.where(own[h], q_all, jnp.zeros_like(q_all))
        gates.append(_dot_nt(km_hi, qm) + _dot_nt(km_lo, qm))
    pad = jnp.zeros((HEAD_DIM - km_rows, bq), F32)
    for c in range(nblk):
        past = blk < c
        drop = []
        for h in range(2):
            gate = jnp.where(past, gates[h][:, c * bq:(c + 1) * bq], NEG_INF)
            rank = jnp.zeros(gate.shape, F32)
            for m in range(c):
                row = gate[m:m + 1, :]
                tie = jnp.where(m < blk, 1.0, 0.0)
                rank = rank + jnp.where(row > gate, 1.0, jnp.where(row == gate, tie, 0.0))
            drop.append(jnp.where(past & (rank >= MOBA_TOPK), 1.0, 0.0))
        rows = jnp.concatenate([drop[1], pad, drop[0], pad], axis=0).astype(BF16)
        bias = jnp.where(_dot_nt(eye, rows) > 0.5, NEG_INF, 0.0).astype(BF16)
        q = q_ref[c * bq:(c + 1) * bq, :]
        for h in range(2):
            qa_ref[h, c * bq:(c + 1) * bq, :] = jnp.where(own[h], q, bias)

    def score(c, h):
        return _dot_nt(ka_ref[h, 0:(c + 1) * bq, :], qa_ref[h, c * bq:(c + 1) * bq, :])

    def attend(c, h, s):
        parts = [s[n * bq:(n + 1) * bq] for n in range(c)]
        parts.append(jnp.where(causal, s[c * bq:], NEG_INF))
        top = parts[0].max(axis=0, keepdims=True)
        for part in parts[1:]:
            top = jnp.maximum(top, part.max(axis=0, keepdims=True))
        probs = jnp.concatenate([jnp.exp2((part - top).astype(BF16)) for part in parts], axis=0)
        acc = _dot(vt_ref[h * VT_ROWS:(h + 1) * VT_ROWS, 0:(c + 1) * bq], probs)
        return acc[:HEAD_DIM] * (1.0 / acc[HEAD_DIM:HEAD_DIM + 1])

    units = [(c, h) for c in range(nblk) for h in range(2)]
    ahead = 2
    pending = [score(*u) for u in units[:ahead]]
    outs = {}
    for i, (c, h) in enumerate(units):
        s = pending.pop(0)
        if i + ahead < len(units):
            pending.append(score(*units[i + ahead]))
        outs[h] = attend(c, h, s)
        if h == 1:
            o_ref[c * bq:(c + 1) * bq, :] = jnp.concatenate([outs[0], outs[1]], axis=0).T.astype(o_ref.dtype)


def moba_attention(qkv, batch, seq_len):
    n = qkv.shape[0]
    return pl.pallas_call(
        _moba_kernel,
        grid=(batch, N_PAIRS),
        in_specs=[pl.BlockSpec((seq_len, PAIR), lambda b, p: (b, p)),
                  pl.BlockSpec((seq_len, PAIR), lambda b, p: (b, N_PAIRS + p)),
                  pl.BlockSpec((seq_len, PAIR), lambda b, p: (b, 2 * N_PAIRS + p))],
        out_specs=pl.BlockSpec((seq_len, PAIR), lambda b, p: (b, p)),
        out_shape=jax.ShapeDtypeStruct((n, D_MODEL), BF16),
        scratch_shapes=[pltpu.VMEM((2 * VT_ROWS, seq_len), BF16),
                        pltpu.VMEM((2, seq_len, PAIR), BF16),
                        pltpu.VMEM((2, seq_len, PAIR), BF16)],
        compiler_params=_params("parallel", "parallel"),
        name="moba_attention",
    )(qkv, qkv, qkv)


def _mem_xattn_kernel(y_ref, wy_ref, x_ref, g_ref, wq_ref, kv_ref, wo_ref, gf_ref, rw_ref, rb_ref,
                      o_ref, h_out, meta_out, cnt_out):
    x = x_ref[...] + _dot(y_ref[...], wy_ref[...])
    xn = _rms(x, g_ref[...]).astype(BF16)
    q = (_dot(xn, wq_ref[...]) * (1.0 / math.sqrt(MEM_HEAD_DIM))).astype(BF16)
    heads = [slice(h * MEM_HEAD_DIM, (h + 1) * MEM_HEAD_DIM) for h in range(MEM_HEADS)]
    scores = [_dot_nt(q[:, sl], kv_ref[:, sl]) for sl in heads]
    outs = []
    for sl, s in zip(heads, scores):
        e = jnp.exp(s - jnp.max(s, axis=-1, keepdims=True))
        pr = e * (1.0 / jnp.sum(e, axis=-1, keepdims=True))
        outs.append(_dot(pr.astype(BF16), kv_ref[:, D_MODEL + sl.start:D_MODEL + sl.stop]).astype(BF16))
    o = jnp.concatenate(outs, axis=1)
    x2 = x + _dot(o, wo_ref[...])
    o_ref[...] = x2
    _route(x2, gf_ref, rw_ref, rb_ref, h_out, meta_out, cnt_out)


def mem_cross_attention(y, wy, x, gain, wq, kv, wo, seq_len, ffn_gain, w_grp, b_grp, w_exp, b_exp):
    n, d = x.shape
    tm = MOE_TILE
    m = kv.shape[0] // (n // seq_len)
    per_seq = seq_len // tm
    wt = jnp.zeros((LANES, d), F32).at[0:MOE_GROUPS].set(w_grp.T).at[8:8 + MOE_EXPERTS].set(w_exp.T)
    bt = jnp.zeros((LANES, 1), F32).at[0:MOE_GROUPS, 0].set(b_grp).at[8:8 + MOE_EXPERTS, 0].set(b_exp)
    const = lambda shape: pl.BlockSpec(shape, lambda i: (0, 0))
    return pl.pallas_call(
        _mem_xattn_kernel,
        grid=(n // tm,),
        in_specs=[pl.BlockSpec((tm, d), lambda i: (i, 0)),
                  const((d, d)),
                  pl.BlockSpec((tm, d), lambda i: (i, 0)),
                  const((1, d)),
                  const((d, d)),
                  pl.BlockSpec((m, 2 * d), lambda i: (i // per_seq, 0)),
                  const((d, d)),
                  const((1, d)),
                  const((LANES, d)),
                  const((LANES, 1))],
        out_specs=[pl.BlockSpec((tm, d), lambda i: (i, 0)),
                   pl.BlockSpec((tm, d), lambda i: (i, 0)),
                   pl.BlockSpec((8, tm), lambda i: (0, i)),
                   pl.BlockSpec((MOE_EXPERTS, LANES), lambda i: (i, 0))],
        out_shape=[jax.ShapeDtypeStruct((n, d), F32),
                   jax.ShapeDtypeStruct((n, d), BF16),
                   jax.ShapeDtypeStruct((8, n), F32),
                   jax.ShapeDtypeStruct((n // tm * MOE_EXPERTS, LANES), F32)],
        compiler_params=_params("parallel"),
        name="mem_cross_attention",
    )(y, wy.astype(BF16), x, gain.reshape(1, d), wq.astype(BF16), kv, wo.astype(BF16),
      ffn_gain.reshape(1, d), wt, bt)


def _route(x, g_ref, w_ref, b_ref, h_out, meta_out, cnt_out):
    tm = x.shape[0]
    h2 = _rms(x, g_ref[...])
    hi = h2.astype(BF16)
    h_out[...] = hi
    lo = (h2 - hi.astype(F32)).astype(BF16)
    w_hi, w_lo = _split(w_ref[...])
    lg = _dot_nt(w_hi, hi) + _dot_nt(w_hi, lo) + _dot_nt(w_lo, hi) + b_ref[...]
    row = _iota((8, tm), 0).astype(F32)

    def first_argmax(val, vmax):
        return jnp.min(jnp.where(val == vmax, row, 8.0), axis=0, keepdims=True)

    gl = jnp.where(row < MOE_GROUPS, lg[0:8], -jnp.inf)
    gmax = jnp.max(gl, axis=0, keepdims=True)
    p_g = 1.0 / jnp.sum(jnp.exp(gl - gmax), axis=0, keepdims=True)
    gidx = first_argmax(gl, gmax)
    el = jnp.zeros((8, tm), F32)
    for g in range(MOE_GROUPS):
        el = el + jnp.where(gidx == g, lg[8 + 8 * g:16 + 8 * g], 0.0)
    ee = jnp.exp(el - jnp.max(el, axis=0, keepdims=True))
    pe = ee / jnp.sum(ee, axis=0, keepdims=True)
    p1 = jnp.max(pe, axis=0, keepdims=True)
    i1 = first_argmax(pe, p1)
    pe2 = jnp.where(row == i1, -1.0, pe)
    p2 = jnp.max(pe2, axis=0, keepdims=True)
    i2 = first_argmax(pe2, p2)
    e1 = gidx * MOE_EPG + i1
    e2 = gidx * MOE_EPG + i2
    gate1 = p_g * p1 / (p1 + p2)
    gate2 = p_g * p2 / (p1 + p2)

    erow = _iota((MOE_EXPERTS, tm), 0).astype(F32)
    oh1 = erow == e1
    oh2 = erow == e2
    oh = jnp.where(oh1 | oh2, 1.0, 0.0)
    before = jnp.where(_iota((tm, tm), 0) < _iota((tm, tm), 1), 1.0, 0.0).astype(BF16)
    cnt_before = _dot(oh.astype(BF16), before)
    lr1 = jnp.sum(jnp.where(oh1, cnt_before, 0.0), axis=0, keepdims=True)
    lr2 = jnp.sum(jnp.where(oh2, cnt_before, 0.0), axis=0, keepdims=True)
    zero = jnp.zeros((1, tm), F32)
    meta_out[...] = jnp.concatenate([e1, e2, gate1, gate2, lr1, lr2, zero, zero], axis=0)
    cnt_out[...] = jnp.broadcast_to(jnp.sum(oh, axis=1, keepdims=True), (MOE_EXPERTS, LANES))


def _chunk_copies(tab_ref, t, make_copy, act):
    base = t * RUN_TAB
    for ci, ck in enumerate(RUN_CHUNKS):
        first = base + len(RUN_CHUNKS) + ci * 2 * MOE_EXPERTS

        def body(i, _, first=first, ck=ck):
            src = tab_ref[first + 2 * i]
            dst = tab_ref[first + 2 * i + 1]
            act(make_copy(pl.multiple_of(src, MOE_ALIGN), pl.multiple_of(dst, MOE_ALIGN), ck))
            return 0

        lax.fori_loop(0, tab_ref[base + ci], body, 0)


def _run_copies(tab_ref, first, count, chunks, make_copy, act):
    def body(e, _):
        base = first + e * 3
        length = tab_ref[base]
        src = tab_ref[base + 1]
        dst = tab_ref[base + 2]
        for ck in chunks:
            @pl.when((length & ck) != 0)
            def _():
                off = length & (-2 * ck)
                act(make_copy(pl.multiple_of(src + off, MOE_ALIGN), pl.multiple_of(dst + off, MOE_ALIGN), ck))
        return 0
    lax.fori_loop(0, count, body, 0)


def _start(cp):
    cp.start()


def _wait(cp):
    cp.wait()


def _dispatch_kernel(tab_ref, h_ref, pos_ref, rows_hbm, sorted_ref, zero_ref, sem, zsem):
    t = pl.program_id(0)
    nt = pl.num_programs(0)
    slot = t % 2
    r = _iota((MOE_SORTED, MOE_TILE), 0)
    perm = jnp.where((r == pos_ref[0:1, :]) | (r == pos_ref[1:2, :]), 1.0, 0.0).astype(BF16)
    sorted_ref[slot] = _dot(perm, h_ref[...]).astype(BF16)

    def run_copy(s):
        def make(src, dst, ck):
            return pltpu.make_async_copy(sorted_ref.at[s, pl.ds(src, ck)], rows_hbm.at[pl.ds(dst, ck)], sem.at[s])
        return make

    def zero_copy(src, dst, ck):
        del src
        return pltpu.make_async_copy(zero_ref.at[pl.ds(0, ck)], rows_hbm.at[pl.ds(dst, ck)], zsem)

    _chunk_copies(tab_ref, t, run_copy(slot), _start)

    @pl.when(t > 0)
    def _():
        _chunk_copies(tab_ref, t - 1, run_copy(1 - slot), _wait)

    @pl.when(t == nt - 1)
    def _():
        zero_ref[...] = jnp.zeros_like(zero_ref)
        pad = nt * RUN_TAB
        _run_copies(tab_ref, pad, MOE_EXPERTS, PAD_CHUNKS, zero_copy, _start)
        spare = pad + MOE_EXPERTS * 3
        n_spare = tab_ref[spare]

        def spare_copy(i):
            dst = pl.multiple_of(tab_ref[spare + 2] + i * PAD_ROWS, PAD_ROWS)
            return zero_copy(0, dst, PAD_ROWS)

        lax.fori_loop(0, n_spare, lambda i, c: (_start(spare_copy(i)), c)[1], 0)
        _chunk_copies(tab_ref, t, run_copy(slot), _wait)
        _run_copies(tab_ref, pad, MOE_EXPERTS, PAD_CHUNKS, zero_copy, _wait)
        lax.fori_loop(0, n_spare, lambda i, c: (_wait(spare_copy(i)), c)[1], 0)


def moe_dispatch(tab, h2, pos, n_rows):
    n, d = h2.shape
    tm = MOE_TILE
    return pl.pallas_call(
        _dispatch_kernel,
        grid_spec=pltpu.PrefetchScalarGridSpec(
            num_scalar_prefetch=1,
            grid=(n // tm,),
            in_specs=[pl.BlockSpec((tm, d), lambda i, tab: (i, 0)),
                      pl.BlockSpec((8, tm), lambda i, tab: (0, i))],
            out_specs=pl.BlockSpec(memory_space=pl.ANY),
            scratch_shapes=[pltpu.VMEM((2, MOE_SORTED, d), BF16), pltpu.VMEM((PAD_ROWS, d), BF16),
                            pltpu.SemaphoreType.DMA((2,)), pltpu.SemaphoreType.DMA],
        ),
        out_shape=jax.ShapeDtypeStruct((n_rows, d), BF16),
        compiler_params=_params("arbitrary"),
        name="moe_dispatch",
    )(tab, h2, pos)


def _expert_kernel(be_ref, nu_ref, x_ref, w1_ref, w3_ref, w2_ref, y_ref, w1b, w3b, w2b):
    b = pl.program_id(0)
    used = b < nu_ref[0]
    new_expert = (b == 0) | (be_ref[b] != be_ref[jnp.maximum(b - 1, 0)])

    @pl.when(used & new_expert)
    def _():
        w1b[...] = w1_ref[0, 0].astype(BF16)
        w3b[...] = w3_ref[0, 0].astype(BF16)
        w2b[...] = w2_ref[0, 0].astype(BF16)

    @pl.when(used)
    def _():
        xb = x_ref[...]
        a = _dot(xb, w1b[...])
        c = _dot(xb, w3b[...])
        hid = (a * jax.nn.sigmoid(a) * c).astype(BF16)
        y_ref[...] = _dot(hid, w2b[...]).astype(y_ref.dtype)

    @pl.when(jnp.logical_not(used))
    def _():
        y_ref[...] = jnp.zeros_like(y_ref)


def moe_experts(block_expert, n_used, rows, layer, w1, w3, w2):
    n_rows = rows.shape[0]
    nb = n_rows // MOE_ROWS
    d, ff = w1.shape[2], w1.shape[3]

    def xmap(b, be, nu):
        return (jnp.minimum(b, nu[0] - 1), 0)

    def wmap(b, be, nu):
        return (layer, be[jnp.minimum(b, nu[0] - 1)], 0, 0)

    return pl.pallas_call(
        _expert_kernel,
        grid_spec=pltpu.PrefetchScalarGridSpec(
            num_scalar_prefetch=2,
            grid=(nb,),
            in_specs=[pl.BlockSpec((MOE_ROWS, d), xmap),
                      pl.BlockSpec((1, 1, d, ff), wmap),
                      pl.BlockSpec((1, 1, d, ff), wmap),
                      pl.BlockSpec((1, 1, ff, d), wmap)],
            out_specs=pl.BlockSpec((MOE_ROWS, d), lambda b, be, nu: (b, 0)),
            scratch_shapes=[pltpu.VMEM((d, ff), BF16), pltpu.VMEM((d, ff), BF16), pltpu.VMEM((ff, d), BF16)],
        ),
        out_shape=jax.ShapeDtypeStruct((n_rows, d), BF16),
        compiler_params=_params("arbitrary"),
        name="moe_experts",
    )(block_expert, n_used, rows, w1, w3, w2)


def _combine_kernel(final_norm, tab_ref, y_hbm, pos_ref, gate_ref, x_ref, g_ref, o_ref, ys_ref, sem):
    t = pl.program_id(0)
    nt = pl.num_programs(0)
    slot = t % 2

    def run_copy(s):
        def make(src, dst, ck):
            return pltpu.make_async_copy(y_hbm.at[pl.ds(dst, ck)], ys_ref.at[s, pl.ds(src, ck)], sem.at[s])
        return make

    @pl.when(t == 0)
    def _():
        ys_ref[...] = jnp.zeros_like(ys_ref)
        _chunk_copies(tab_ref, 0, run_copy(0), _start)

    @pl.when(t + 1 < nt)
    def _():
        _chunk_copies(tab_ref, t + 1, run_copy(1 - slot), _start)

    r = _iota((MOE_SORTED, MOE_TILE), 0)
    hit1 = r == pos_ref[0:1, :]
    hit2 = r == pos_ref[1:2, :]
    wgt = jnp.where(hit1, gate_ref[2:3, :], jnp.where(hit2, gate_ref[3:4, :], 0.0)).astype(BF16)
    _chunk_copies(tab_ref, t, run_copy(slot), _wait)
    out = x_ref[...] + _dot_tn(wgt, ys_ref[slot])
    if final_norm:
        out = _rms(out, g_ref[...])
    o_ref[...] = out


def moe_combine(tab, y, pos, meta, x, final_gain):
    n, d = x.shape
    tm = MOE_TILE
    final_norm = final_gain is not None
    gain = (final_gain if final_norm else jnp.ones((d,), F32)).reshape(1, d)
    return pl.pallas_call(
        functools.partial(_combine_kernel, final_norm),
        grid_spec=pltpu.PrefetchScalarGridSpec(
            num_scalar_prefetch=1,
            grid=(n // tm,),
            in_specs=[pl.BlockSpec(memory_space=pl.ANY),
                      pl.BlockSpec((8, tm), lambda i, tab: (0, i)),
                      pl.BlockSpec((8, tm), lambda i, tab: (0, i)),
                      pl.BlockSpec((tm, d), lambda i, tab: (i, 0)),
                      pl.BlockSpec((1, d), lambda i, tab: (0, 0))],
            out_specs=pl.BlockSpec((tm, d), lambda i, tab: (i, 0)),
            scratch_shapes=[pltpu.VMEM((2, MOE_SORTED, d), BF16), pltpu.SemaphoreType.DMA((2,))],
        ),
        out_shape=jax.ShapeDtypeStruct((n, d), F32),
        compiler_params=_params("arbitrary"),
        name="moe_combine",
    )(tab, y, pos, meta, x, gain)


def hierarchical_moe(x, h2, meta, cnt, layer, w1, w3, w2, final_gain):
    n, d = x.shape
    tm = MOE_TILE
    nt = n // tm

    cnt = cnt.reshape(nt, MOE_EXPERTS, LANES)[:, :, 0].astype(I32)
    run = (cnt + MOE_ALIGN - 1) // MOE_ALIGN * MOE_ALIGN
    src = jnp.cumsum(run, axis=1) - run
    before = jnp.cumsum(run, axis=0) - run
    total = jnp.sum(run, axis=0)
    padded = (total + MOE_ROWS - 1) // MOE_ROWS * MOE_ROWS
    pad_end = jnp.cumsum(padded)
    dst = (pad_end - padded)[None, :] + before
    max_rows = 2 * n + nt * MOE_EXPERTS * (MOE_ALIGN - 1) + MOE_EXPERTS * (MOE_ROWS - 1)
    nb = -(-max_rows // MOE_ROWS)
    sizes = jnp.array(RUN_CHUNKS, I32)[None, :, None]
    has = (run[:, None, :] & sizes) != 0
    above = run[:, None, :] & (-2 * sizes)
    place = jnp.where(has, jnp.cumsum(has.astype(I32), axis=2) - 1, -1)
    front = place[..., None] == jnp.arange(MOE_EXPERTS, dtype=I32)
    c_src = jnp.sum(jnp.where(front, (src[:, None, :] + above)[..., None], 0), axis=2)
    c_dst = jnp.sum(jnp.where(front, (dst[:, None, :] + above)[..., None], 0), axis=2)
    chunks = jnp.stack([c_src, c_dst], axis=-1).reshape(nt, -1)
    runs = jnp.concatenate([jnp.sum(has.astype(I32), axis=2), chunks], axis=1).reshape(-1)
    zero = jnp.zeros((MOE_EXPERTS,), I32)
    pads = jnp.stack([padded - total, zero, pad_end - padded + total], axis=-1)
    spare = jnp.stack([(nb * MOE_ROWS - pad_end[-1]) // PAD_ROWS, zero[0], pad_end[-1]])
    tab = jnp.concatenate([runs, pads.reshape(-1), spare]).astype(I32)
    n_used = (pad_end[-1] // MOE_ROWS).astype(I32).reshape(1)
    block_start = jnp.arange(nb, dtype=I32) * MOE_ROWS
    block_expert = jnp.minimum(
        jnp.sum((pad_end[None, :] <= block_start[:, None]).astype(I32), axis=1), MOE_EXPERTS - 1).astype(I32)
    e = meta[0:2].astype(I32).reshape(2, nt, tm, 1)
    hit = e == jnp.arange(MOE_EXPERTS, dtype=I32)
    pos = jnp.sum(jnp.where(hit, src[None, :, None, :], 0), axis=-1).reshape(2, n) + meta[4:6].astype(I32)
    pos8 = jnp.zeros((8, n), I32).at[0:2].set(pos)

    rows = moe_dispatch(tab, h2, pos8, nb * MOE_ROWS)
    y = moe_experts(block_expert, n_used, rows, layer, w1, w3, w2)
    return moe_combine(tab, y, pos8, meta, x, final_gain)


def kernel(x, mem, positions, ln_mix, ln_mem, ln_memkv, ln_ffn, rw_mu, rw_w0, rw_w1, rw_w2, rw_a0, rw_a1, rw_a2, rw_g1, rw_g2, rw_kk, rw_ka, rw_rk, rw_wrkv, rw_lnx_g, rw_lnx_b, rw_wo, mb_wqkv, mb_wo, mx_wq, mx_wkv, mx_wo, moe_wg, moe_bg, moe_we, moe_be, moe_w1, moe_w3, moe_w2, ln_f):
    B, T, C = x.shape
    n = B * T
    depth = ln_mix.shape[0]
    xf = x.reshape(n, C)
    memf = mem.reshape(-1, C)
    for i in range(depth):
        j = i // 2
        if i % 2 == 0:
            r, k, v, kk, akk, lw, g = rwkv_mix(xf, T, ln_mix[i], rw_mu[j], rw_w0[j], rw_w1[j], rw_w2[j],
                                               rw_a0[j], rw_a1[j], rw_a2[j], rw_g1[j], rw_g2[j],
                                               rw_kk[j], rw_ka[j], rw_wrkv[j])
            y = rwkv_recurrence(r, k, v, kk, akk, lw, g, rw_lnx_g[j], rw_lnx_b[j], rw_rk[j], B, T)
            w_mix_o = rw_wo[j]
        else:
            qkv = qkv_rope(xf, ln_mix[i], positions, mb_wqkv[j])
            y = moba_attention(qkv, B, T)
            w_mix_o = mb_wo[j]
        kv = norm_linear(memf, ln_memkv[i], mx_wkv[i].astype(BF16))
        xf, h2, meta, cnt = mem_cross_attention(y, w_mix_o, xf, ln_mem[i], mx_wq[i], kv, mx_wo[i], T,
                                                ln_ffn[i], moe_wg[i], moe_bg[i], moe_we[i], moe_be[i])
        xf = hierarchical_moe(xf, h2, meta, cnt, i, moe_w1, moe_w3, moe_w2, ln_f if i == depth - 1 else None)
    return xf.reshape(B, T, C)
```

```python
import functools
import math

import jax
import jax.numpy as jnp
from jax import lax
from jax.experimental import pallas as pl
from jax.experimental.pallas import tpu as pltpu

F32 = jnp.float32
BF16 = jnp.bfloat16
I32 = jnp.int32

D_MODEL = 1024
HEAD_DIM = 64
PAIR = 2 * HEAD_DIM
N_PAIRS = D_MODEL // PAIR
RWKV_GN_EPS = 64e-5
RWKV_CHUNK = 64
RWKV_CHUNKS_PER_STEP = 8
MOBA_BLOCK = 256
MOBA_TOPK = 3
VT_ROWS = HEAD_DIM + 16
ROPE_THETA = 10000.0
MEM_HEADS = 4
MEM_HEAD_DIM = D_MODEL // MEM_HEADS
MOE_GROUPS = 4
MOE_EPG = 8
MOE_EXPERTS = MOE_GROUPS * MOE_EPG
RMS_EPS = 1e-6
NEG_INF = -1e30

LANES = 128
ROW_ALIGN = 8
MOE_TILE = 512
MOE_ROWS = 1024
MOE_ALIGN = 16
MOE_SORTED = -(-(2 * MOE_TILE + MOE_EXPERTS * (MOE_ALIGN - 1)) // MOE_ALIGN) * MOE_ALIGN
RUN_CHUNKS = tuple(MOE_ALIGN << s for s in range((MOE_TILE // MOE_ALIGN).bit_length() - 1, -1, -1))
RUN_TAB = len(RUN_CHUNKS) * (1 + 2 * MOE_EXPERTS)
PAD_ROWS = MOE_ROWS // 2
PAD_CHUNKS = tuple(MOE_ALIGN << s for s in range((PAD_ROWS // MOE_ALIGN).bit_length() - 1, -1, -1))
VMEM_LIMIT = 56 * 2 ** 20


def _params(*sem):
    return pltpu.CompilerParams(dimension_semantics=sem, vmem_limit_bytes=VMEM_LIMIT)


def _iota(shape, dim):
    return lax.broadcasted_iota(I32, shape, dim)


def _dot(a, b):
    return jnp.dot(a, b, preferred_element_type=F32)


def _dot_nt(a, b):
    return lax.dot_general(a, b, (((1,), (1,)), ((), ())), preferred_element_type=F32)


def _dot_tn(a, b):
    return lax.dot_general(a, b, (((0,), (0,)), ((), ())), preferred_element_type=F32)


def _split(x):
    hi = x.astype(BF16)
    lo = (x - hi.astype(F32)).astype(BF16)
    return hi, lo


def _rms(x, g):
    return x * lax.rsqrt(jnp.mean(x * x, axis=-1, keepdims=True) + RMS_EPS) * g


def _norm_linear_kernel(x_ref, g_ref, w_ref, o_ref):
    xn = _rms(x_ref[...], g_ref[...]).astype(BF16)
    o_ref[...] = _dot(xn, w_ref[...]).astype(o_ref.dtype)


def norm_linear(x, g, w, tm=256):
    n, k = x.shape
    dout = w.shape[1]
    return pl.pallas_call(
        _norm_linear_kernel,
        grid=(n // tm,),
        in_specs=[pl.BlockSpec((tm, k), lambda i: (i, 0)),
                  pl.BlockSpec((1, k), lambda i: (0, 0)),
                  pl.BlockSpec((k, dout), lambda i: (0, 0))],
        out_specs=pl.BlockSpec((tm, dout), lambda i: (i, 0)),
        out_shape=jax.ShapeDtypeStruct((n, dout), BF16),
        compiler_params=_params("parallel"),
        name="norm_linear",
    )(x, g.reshape(1, k), w)


def _head_sums(x_sq):
    rows = x_sq.shape[0]
    r = _iota((PAIR, PAIR), 0) // HEAD_DIM
    c = _iota((PAIR, PAIR), 1) // HEAD_DIM
    bd = jnp.where(r == c, 1.0, 0.0).astype(BF16)
    hi, lo = _split(x_sq)
    stacked = jnp.concatenate([part[:, p * PAIR:(p + 1) * PAIR] for part in (hi, lo) for p in range(N_PAIRS)], axis=0)
    sums = _dot(stacked, bd)
    sums = sums[:N_PAIRS * rows] + sums[N_PAIRS * rows:]
    return jnp.concatenate([sums[p * rows:(p + 1) * rows] for p in range(N_PAIRS)], axis=1)


def _rwkv_mix_kernel(seq_len, x_ref, xp_ref, g_ref, mu_ref, vec_ref, wr_ref, wk_ref, wv_ref,
                     l1_ref, w2_ref, a2_ref, g2_ref,
                     r_out, k_out, v_out, kk_out, akk_out, lw_out, g_out):
    i = pl.program_id(0)
    tm = x_ref.shape[0]
    gain = g_ref[...]
    h = _rms(x_ref[...], gain)
    hp = _rms(xp_ref[...], gain)[ROW_ALIGN - 1:ROW_ALIGN, :]
    hp = jnp.where((i * tm) % seq_len == 0, 0.0, hp)
    hs = pltpu.roll(h, 1, 0)
    hs = jnp.where(_iota((tm, 1), 0) == 0, hp, hs)
    hb = h.astype(BF16)
    dxb = (hs - h).astype(BF16)
    mub = mu_ref[...].astype(BF16)

    def mix(s):
        return hb + dxb * mub[s:s + 1, :]

    vec = vec_ref[...]
    w0, a0, k_k, k_a = vec[0:1], vec[1:2], vec[2:3], vec[3:4]
    l1 = l1_ref[...]
    tw = _dot(mix(1), l1[:, 0:64])
    ta = _dot(mix(4), l1[:, 64:128])
    tg = _dot(mix(5), l1[:, 128:256])
    r = _dot(mix(0), wr_ref[...])
    k = _dot(mix(2), wk_ref[...])
    v = _dot(mix(3), wv_ref[...])
    lw = -math.exp(-0.5) * jax.nn.sigmoid(w0 + _dot(jnp.tanh(tw).astype(BF16), w2_ref[...]))
    a = jax.nn.sigmoid(a0 + _dot(ta.astype(BF16), a2_ref[...]))
    g = _dot(jax.nn.sigmoid(tg).astype(BF16), g2_ref[...])

    kk = k * k_k
    kkn = kk * lax.rsqrt(jnp.maximum(_head_sums(kk * kk), 1e-24))
    kk_out[...] = kkn.astype(kk_out.dtype)
    akk_out[...] = (kkn * a).astype(akk_out.dtype)
    r_out[...] = r.astype(r_out.dtype)
    k_out[...] = (k * (1.0 + (a - 1.0) * k_a)).astype(k_out.dtype)
    v_out[...] = v.astype(v_out.dtype)
    lw_out[...] = lw
    g_out[...] = g.astype(g_out.dtype)


def rwkv_mix(x, seq_len, gain, mu, w0, w1, w2, a0, a1, a2, g1, g2, k_k, k_a, w_rkv, tm=512):
    n, d = x.shape
    mu8 = jnp.zeros((8, d), F32).at[:6].set(mu)
    vec = jnp.zeros((8, d), F32).at[0].set(w0).at[1].set(a0).at[2].set(k_k).at[3].set(k_a)
    l1 = jnp.concatenate([w1, a1, g1], axis=1).astype(BF16)
    wb = w_rkv.astype(BF16)
    row = pl.BlockSpec((tm, d), lambda i: (i, 0))
    full = lambda a: pl.BlockSpec(a.shape, lambda i: (0,) * a.ndim)
    args = (x, x, gain.reshape(1, d), mu8, vec, wb[0], wb[1], wb[2], l1,
            w2.astype(BF16), a2.astype(BF16), g2.astype(BF16))
    in_specs = [row, pl.BlockSpec((ROW_ALIGN, d), lambda i: (jnp.maximum(i * (tm // ROW_ALIGN) - 1, 0), 0))]
    in_specs += [full(a) for a in args[2:]]
    outs = [jax.ShapeDtypeStruct((n, d), BF16)] * 5 + [jax.ShapeDtypeStruct((n, d), F32),
                                                       jax.ShapeDtypeStruct((n, d), BF16)]
    return pl.pallas_call(
        functools.partial(_rwkv_mix_kernel, seq_len),
        grid=(n // tm,),
        in_specs=in_specs,
        out_specs=[row] * 7,
        out_shape=outs,
        compiler_params=_params("parallel"),
        name="rwkv_mix",
    )(*args)


def _rwkv_rec_kernel(r_ref, k_ref, v_ref, kk_ref, akk_ref, lw_ref, g_ref, vec_ref, y_ref, s_ref):
    c = pl.program_id(1)
    L = RWKV_CHUNK

    @pl.when(c == 0)
    def _():
        s_ref[...] = jnp.zeros_like(s_ref)

    lane = _iota((1, PAIR), 1)
    m0 = lane < HEAD_DIM
    ri = _iota((2 * L, 2 * L), 0)
    ci = _iota((2 * L, 2 * L), 1)
    same = (ri // L) == (ci // L)
    strict = same & (ci < ri)
    incl = same & (ci <= ri)
    eye = jnp.where(ri == ci, 1.0, 0.0)
    tri = jnp.where(_iota((L, L), 1) <= _iota((L, L), 0), 1.0, 0.0).astype(BF16)
    rb = _iota((PAIR, PAIR), 0) // HEAD_DIM
    cb = _iota((PAIR, PAIR), 1) // HEAD_DIM
    bd = rb == cb

    def stack(x):
        return jnp.concatenate([jnp.where(m0, x, 0.0), jnp.where(m0, 0.0, x)], axis=0)

    def fold(x):
        return x[:L] + x[L:]

    def head_mean(x):
        s0 = jnp.sum(jnp.where(m0, x, 0.0), axis=-1, keepdims=True)
        s1 = jnp.sum(jnp.where(m0, 0.0, x), axis=-1, keepdims=True)
        return jnp.where(m0, s0, s1) * (1.0 / HEAD_DIM)

    vec = vec_ref[...]
    n_sub = r_ref.shape[0] // L
    pairs = range(n_sub * N_PAIRS)
    rws = [slice((q // N_PAIRS) * L, (q // N_PAIRS + 1) * L) for q in pairs]
    sls = [slice((q % N_PAIRS) * PAIR, (q % N_PAIRS + 1) * PAIR) for q in pairs]
    lw = [lw_ref[rws[p], sls[p]] for p in pairs]
    cum = []
    for p in pairs:
        lw_hi, lw_lo = _split(lw[p])
        cum.append(_dot(tri, lw_hi) + _dot(tri, lw_lo))
    dec_all, x_kap, x_r, a_t, k_t, sc = [], [], [], [], [], []
    for p in pairs:
        rw, sl = rws[p], sls[p]
        dec = jnp.exp(cum[p])
        inv = jnp.exp(-cum[p])
        dec_prev = jnp.exp(cum[p] - lw[p])
        dec_all.append(dec[L - 1:L, :])
        x_kap.append(stack(kk_ref[rw, sl].astype(F32) * dec_prev).astype(BF16))
        x_r.append(stack(r_ref[rw, sl].astype(F32) * dec))
        a_t.append((akk_ref[rw, sl].astype(F32) * inv).astype(BF16))
        k_t.append((k_ref[rw, sl].astype(F32) * inv).astype(BF16))
        xs = jnp.concatenate([x_kap[p], x_r[p].astype(BF16)], axis=0)
        ys = jnp.concatenate([a_t[p], k_t[p]], axis=0)
        sc.append(_dot_nt(xs, ys))

    def both_heads(first, second, keep):
        upper_left = keep & (ri < L)
        lower_right = keep & (ri >= L)
        return jnp.where(upper_left, first, jnp.where(lower_right, second, 0.0))

    mp, m_kk = [], []
    for p in pairs:
        top = sc[p][:2 * L]
        top_r = pltpu.roll(top, L, 1)
        mp.append(-both_heads(top, top_r, strict))
        m_kk.append(both_heads(top_r, top, strict).astype(BF16))
    s_ra, s_rk = [], []
    for p in pairs:
        bot = sc[p][2 * L:]
        bot_r = pltpu.roll(bot, L, 1)
        s_ra.append(both_heads(bot, bot_r, incl).astype(BF16))
        s_rk.append(both_heads(bot_r, bot, incl).astype(BF16))
    v_st = [stack(v_ref[rws[p], sls[p]].astype(F32)).astype(BF16) for p in pairs]
    bo = [_dot(jnp.concatenate([m_kk[p], s_rk[p]], axis=0), v_st[p]) for p in pairs]
    b1 = [bo[p][:2 * L].astype(BF16) for p in pairs]
    o1 = [bo[p][2 * L:] for p in pairs]

    t_inv = [eye + mp[p] for p in pairs]
    mpb = [mp[p].astype(BF16) for p in pairs]
    mpb = [_dot(mpb[p], mpb[p]).astype(BF16) for p in pairs]
    steps = int(math.log2(L)) - 1
    for j in range(steps):
        if j + 1 < steps:
            both = [_dot(mpb[p], jnp.concatenate([mpb[p], t_inv[p].astype(BF16)], axis=1)) for p in pairs]
            mpb = [both[p][:, :2 * L].astype(BF16) for p in pairs]
            t_inv = [t_inv[p] + both[p][:, 2 * L:] for p in pairs]
        else:
            t_inv = [t_inv[p] + _dot(mpb[p], t_inv[p].astype(BF16)) for p in pairs]

    wu = [_dot(t_inv[p].astype(BF16), jnp.concatenate([x_kap[p], b1[p]], axis=1)).astype(BF16)
          for p in pairs]
    corr = [_dot(s_ra[p], wu[p]) for p in pairs]
    g_m, h0t, r_hat, o0 = [], [], [], []
    for p in pairs:
        r_hat.append(fold(x_r[p] - corr[p][:, :PAIR]).astype(BF16))
        o0.append(fold(o1[p] - corr[p][:, PAIR:]))
        w_f = fold(wu[p][:, :PAIR])
        u0_f = fold(wu[p][:, PAIR:])
        g_m.append(jnp.where(bd, _dot_tn(a_t[p], w_f), 0.0).astype(BF16))
        vu = jnp.concatenate([v_ref[rws[p], sls[p]], -u0_f], axis=0)
        h0t.append(jnp.where(bd, _dot_tn(vu, jnp.concatenate([k_t[p], a_t[p]], axis=0)), 0.0))
    o = []
    for p in pairs:
        hp = p % N_PAIRS
        s = s_ref[hp]
        sb = s.astype(BF16)
        o.append(_dot_nt(r_hat[p], sb) + o0[p])
        s_ref[hp] = (s - _dot_nt(sb, g_m[p]) + h0t[p]) * dec_all[p]
    for p in pairs:
        rw, sl = rws[p], sls[p]
        mean = head_mean(o[p])
        cen = o[p] - mean
        var = head_mean(cen * cen)
        gn = cen * lax.rsqrt(var + RWKV_GN_EPS) * vec[0:1, sl] + vec[1:2, sl]
        r = r_ref[rw, sl].astype(F32)
        k = k_ref[rw, sl].astype(F32)
        bonus = head_mean(r * k * vec[2:3, sl]) * HEAD_DIM * v_ref[rw, sl].astype(F32)
        y_ref[rw, sl] = ((gn + bonus) * g_ref[rw, sl].astype(F32)).astype(y_ref.dtype)


def rwkv_recurrence(r, k, v, kk, akk, lw, g, lnx_g, lnx_b, r_k, batch, seq_len):
    n, d = r.shape
    rows = RWKV_CHUNK * RWKV_CHUNKS_PER_STEP
    nc = seq_len // rows
    vec = jnp.zeros((8, d), F32).at[0].set(lnx_g).at[1].set(lnx_b).at[2].set(r_k.reshape(d))
    blk = pl.BlockSpec((rows, d), lambda b, c: (b * nc + c, 0))
    return pl.pallas_call(
        _rwkv_rec_kernel,
        grid=(batch, nc),
        in_specs=[blk] * 7 + [pl.BlockSpec((8, d), lambda b, c: (0, 0))],
        out_specs=blk,
        out_shape=jax.ShapeDtypeStruct((n, d), BF16),
        scratch_shapes=[pltpu.VMEM((N_PAIRS, PAIR, PAIR), F32)],
        compiler_params=_params("parallel", "arbitrary"),
        name="rwkv_recurrence",
    )(r, k, v, kk, akk, lw, g, vec)


def _qkv_rope_kernel(x_ref, g_ref, pos_ref, inv_ref, w_ref, o_ref):
    d = x_ref.shape[1]
    xn = _rms(x_ref[...], g_ref[...]).astype(BF16)
    y = [_dot(xn, w_ref[:, part * d:(part + 1) * d]) for part in range(3)]
    ang = pos_ref[...].astype(F32) * inv_ref[...]
    first = (_iota((1, PAIR), 1) % HEAD_DIM) < HEAD_DIM // 2
    sn = jnp.sin(ang)
    cos = jnp.cos(ang)
    sin_lo = jnp.where(first, -sn, 0.0)
    sin_hi = jnp.where(first, 0.0, sn)
    for part in range(2):
        for p in range(N_PAIRS):
            t = y[part][:, p * PAIR:(p + 1) * PAIR]
            rot = (t * cos + pltpu.roll(t, PAIR - HEAD_DIM // 2, 1) * sin_lo
                   + pltpu.roll(t, HEAD_DIM // 2, 1) * sin_hi)
            o_ref[:, part * d + p * PAIR:part * d + (p + 1) * PAIR] = rot.astype(o_ref.dtype)
    o_ref[:, 2 * d:] = y[2].astype(o_ref.dtype)


def qkv_rope(x, gain, positions, w_qkv, tm=1024):
    n, d = x.shape
    half = HEAD_DIM // 2
    inv = ROPE_THETA ** (-jnp.arange(half, dtype=F32) * 2.0 / HEAD_DIM)
    inv128 = jnp.tile(inv, PAIR // half).reshape(1, PAIR)
    col_scale = jnp.where(jnp.arange(3 * d) < d, math.log2(math.e) / math.sqrt(HEAD_DIM), 1.0).astype(F32)
    return pl.pallas_call(
        _qkv_rope_kernel,
        grid=(n // tm,),
        in_specs=[pl.BlockSpec((tm, d), lambda i: (i, 0)),
                  pl.BlockSpec((1, d), lambda i: (0, 0)),
                  pl.BlockSpec((tm, 1), lambda i: (i, 0)),
                  pl.BlockSpec((1, PAIR), lambda i: (0, 0)),
                  pl.BlockSpec((d, 3 * d), lambda i: (0, 0))],
        out_specs=pl.BlockSpec((tm, 3 * d), lambda i: (i, 0)),
        out_shape=jax.ShapeDtypeStruct((n, 3 * d), BF16),
        compiler_params=_params("parallel"),
        name="qkv_rope",
    )(x, gain.reshape(1, d), positions.reshape(n, 1), inv128, (w_qkv * col_scale).astype(BF16))


def _moba_kernel(q_ref, k_ref, v_ref, o_ref, vt_ref, ka_ref, qa_ref):
    nblk = k_ref.shape[0] // MOBA_BLOCK
    bq = MOBA_BLOCK
    km_rows = 16
    lane = _iota((1, PAIR), 1)
    own = (lane < HEAD_DIM, lane >= HEAD_DIM)
    spare = (HEAD_DIM, 0)

    means = []
    ones_row = jnp.where(_iota((VT_ROWS - HEAD_DIM, bq), 0) == 0, 1.0, 0.0).astype(BF16)
    for n in range(nblk):
        cols = slice(n * bq, (n + 1) * bq)
        kb = k_ref[cols, :]
        means.append(jnp.mean(kb.astype(F32), axis=0, keepdims=True))
        vt = v_ref[cols, :].astype(F32).T.astype(BF16)
        for h in range(2):
            vt_ref[h * VT_ROWS:h * VT_ROWS + HEAD_DIM, cols] = vt[h * HEAD_DIM:(h + 1) * HEAD_DIM]
            vt_ref[h * VT_ROWS + HEAD_DIM:(h + 1) * VT_ROWS, cols] = ones_row
            marker = jnp.where(lane == spare[h] + n, 1.0, 0.0).astype(BF16)
            ka_ref[h, cols, :] = jnp.where(own[h], kb, marker)
    km_hi, km_lo = _split(jnp.concatenate(means + [jnp.zeros((km_rows - nblk, PAIR), F32)], axis=0))

    blk = _iota((km_rows, bq), 0)
    causal = _iota((bq, bq), 0) <= _iota((bq, bq), 1)
    eye = jnp.where(_iota((bq, bq), 0) == _iota((bq, bq), 1), 1.0, 0.0).astype(BF16)

    q_all = q_ref[...]
    gates = []
    for h in range(2):
        qm = jnp.where(own[h], q_all, jnp.zeros_like(q_all))
        gates.append(_dot_nt(km_hi, qm) + _dot_nt(km_lo, qm))
    pad = jnp.zeros((HEAD_DIM - km_rows, bq), F32)
    for c in range(nblk):
        past = blk < c
        drop = []
        for h in range(2):
            gate = jnp.where(past, gates[h][:, c * bq:(c + 1) * bq], NEG_INF)
            rank = jnp.zeros(gate.shape, F32)
            for m in range(c):
                row = gate[m:m + 1, :]
                tie = jnp.where(m < blk, 1.0, 0.0)
                rank = rank + jnp.where(row > gate, 1.0, jnp.where(row == gate, tie, 0.0))
            drop.append(jnp.where(past & (rank >= MOBA_TOPK), 1.0, 0.0))
        rows = jnp.concatenate([drop[1], pad, drop[0], pad], axis=0).astype(BF16)
        bias = jnp.where(_dot_nt(eye, rows) > 0.5, NEG_INF, 0.0).astype(BF16)
        q = q_ref[c * bq:(c + 1) * bq, :]
        for h in range(2):
            qa_ref[h, c * bq:(c + 1) * bq, :] = jnp.where(own[h], q, bias)

    def score(c, h):
        return _dot_nt(ka_ref[h, 0:(c + 1) * bq, :], qa_ref[h, c * bq:(c + 1) * bq, :])

    def attend(c, h, s):
        parts = [s[n * bq:(n + 1) * bq] for n in range(c)]
        parts.append(jnp.where(causal, s[c * bq:], NEG_INF))
        top = parts[0].max(axis=0, keepdims=True)
        for part in parts[1:]:
            top = jnp.maximum(top, part.max(axis=0, keepdims=True))
        probs = jnp.concatenate([jnp.exp2((part - top).astype(BF16)) for part in parts], axis=0)
        acc = _dot(vt_ref[h * VT_ROWS:(h + 1) * VT_ROWS, 0:(c + 1) * bq], probs)
        return acc[:HEAD_DIM] * (1.0 / acc[HEAD_DIM:HEAD_DIM + 1])

    units = [(c, h) for c in range(nblk) for h in range(2)]
    ahead = 2
    pending = [score(*u) for u in units[:ahead]]
    outs = {}
    for i, (c, h) in enumerate(units):
        s = pending.pop(0)
        if i + ahead < len(units):
            pending.append(score(*units[i + ahead]))
        outs[h] = attend(c, h, s)
        if h == 1:
            o_ref[c * bq:(c + 1) * bq, :] = jnp.concatenate([outs[0], outs[1]], axis=0).T.astype(o_ref.dtype)


def moba_attention(qkv, batch, seq_len):
    n = qkv.shape[0]
    return pl.pallas_call(
        _moba_kernel,
        grid=(batch, N_PAIRS),
        in_specs=[pl.BlockSpec((seq_len, PAIR), lambda b, p: (b, p)),
                  pl.BlockSpec((seq_len, PAIR), lambda b, p: (b, N_PAIRS + p)),
                  pl.BlockSpec((seq_len, PAIR), lambda b, p: (b, 2 * N_PAIRS + p))],
        out_specs=pl.BlockSpec((seq_len, PAIR), lambda b, p: (b, p)),
        out_shape=jax.ShapeDtypeStruct((n, D_MODEL), BF16),
        scratch_shapes=[pltpu.VMEM((2 * VT_ROWS, seq_len), BF16),
                        pltpu.VMEM((2, seq_len, PAIR), BF16),
                        pltpu.VMEM((2, seq_len, PAIR), BF16)],
        compiler_params=_params("parallel", "parallel"),
        name="moba_attention",
    )(qkv, qkv, qkv)


def _mem_xattn_kernel(y_ref, wy_ref, x_ref, g_ref, wq_ref, kv_ref, wo_ref, gf_ref, rw_ref, rb_ref,
                      o_ref, h_out, meta_out, cnt_out):
    x = x_ref[...] + _dot(y_ref[...], wy_ref[...])
    xn = _rms(x, g_ref[...]).astype(BF16)
    q = (_dot(xn, wq_ref[...]) * (1.0 / math.sqrt(MEM_HEAD_DIM))).astype(BF16)
    heads = [slice(h * MEM_HEAD_DIM, (h + 1) * MEM_HEAD_DIM) for h in range(MEM_HEADS)]
    scores = [_dot_nt(q[:, sl], kv_ref[:, sl]) for sl in heads]
    outs = []
    for sl, s in zip(heads, scores):
        e = jnp.exp(s - jnp.max(s, axis=-1, keepdims=True))
        pr = e * (1.0 / jnp.sum(e, axis=-1, keepdims=True))
        outs.append(_dot(pr.astype(BF16), kv_ref[:, D_MODEL + sl.start:D_MODEL + sl.stop]).astype(BF16))
    o = jnp.concatenate(outs, axis=1)
    x2 = x + _dot(o, wo_ref[...])
    o_ref[...] = x2
    _route(x2, gf_ref, rw_ref, rb_ref, h_out, meta_out, cnt_out)


def mem_cross_attention(y, wy, x, gain, wq, kv, wo, seq_len, ffn_gain, w_grp, b_grp, w_exp, b_exp):
    n, d = x.shape
    tm = MOE_TILE
    m = kv.shape[0] // (n // seq_len)
    per_seq = seq_len // tm
    wt = jnp.zeros((LANES, d), F32).at[0:MOE_GROUPS].set(w_grp.T).at[8:8 + MOE_EXPERTS].set(w_exp.T)
    bt = jnp.zeros((LANES, 1), F32).at[0:MOE_GROUPS, 0].set(b_grp).at[8:8 + MOE_EXPERTS, 0].set(b_exp)
    const = lambda shape: pl.BlockSpec(shape, lambda i: (0, 0))
    return pl.pallas_call(
        _mem_xattn_kernel,
        grid=(n // tm,),
        in_specs=[pl.BlockSpec((tm, d), lambda i: (i, 0)),
                  const((d, d)),
                  pl.BlockSpec((tm, d), lambda i: (i, 0)),
                  const((1, d)),
                  const((d, d)),
                  pl.BlockSpec((m, 2 * d), lambda i: (i // per_seq, 0)),
                  const((d, d)),
                  const((1, d)),
                  const((LANES, d)),
                  const((LANES, 1))],
        out_specs=[pl.BlockSpec((tm, d), lambda i: (i, 0)),
                   pl.BlockSpec((tm, d), lambda i: (i, 0)),
                   pl.BlockSpec((8, tm), lambda i: (0, i)),
                   pl.BlockSpec((MOE_EXPERTS, LANES), lambda i: (i, 0))],
        out_shape=[jax.ShapeDtypeStruct((n, d), F32),
                   jax.ShapeDtypeStruct((n, d), BF16),
                   jax.ShapeDtypeStruct((8, n), F32),
                   jax.ShapeDtypeStruct((n // tm * MOE_EXPERTS, LANES), F32)],
        compiler_params=_params("parallel"),
        name="mem_cross_attention",
    )(y, wy.astype(BF16), x, gain.reshape(1, d), wq.astype(BF16), kv, wo.astype(BF16),
      ffn_gain.reshape(1, d), wt, bt)


def _route(x, g_ref, w_ref, b_ref, h_out, meta_out, cnt_out):
    tm = x.shape[0]
    h2 = _rms(x, g_ref[...])
    hi = h2.astype(BF16)
    h_out[...] = hi
    lo = (h2 - hi.astype(F32)).astype(BF16)
    w_hi, w_lo = _split(w_ref[...])
    lg = _dot_nt(w_hi, hi) + _dot_nt(w_hi, lo) + _dot_nt(w_lo, hi) + b_ref[...]
    row = _iota((8, tm), 0).astype(F32)

    def first_argmax(val, vmax):
        return jnp.min(jnp.where(val == vmax, row, 8.0), axis=0, keepdims=True)

    gl = jnp.where(row < MOE_GROUPS, lg[0:8], -jnp.inf)
    gmax = jnp.max(gl, axis=0, keepdims=True)
    p_g = 1.0 / jnp.sum(jnp.exp(gl - gmax), axis=0, keepdims=True)
    gidx = first_argmax(gl, gmax)
    el = jnp.zeros((8, tm), F32)
    for g in range(MOE_GROUPS):
        el = el + jnp.where(gidx == g, lg[8 + 8 * g:16 + 8 * g], 0.0)
    ee = jnp.exp(el - jnp.max(el, axis=0, keepdims=True))
    pe = ee / jnp.sum(ee, axis=0, keepdims=True)
    p1 = jnp.max(pe, axis=0, keepdims=True)
    i1 = first_argmax(pe, p1)
    pe2 = jnp.where(row == i1, -1.0, pe)
    p2 = jnp.max(pe2, axis=0, keepdims=True)
    i2 = first_argmax(pe2, p2)
    e1 = gidx * MOE_EPG + i1
    e2 = gidx * MOE_EPG + i2
    gate1 = p_g * p1 / (p1 + p2)
    gate2 = p_g * p2 / (p1 + p2)

    erow = _iota((MOE_EXPERTS, tm), 0).astype(F32)
    oh1 = erow == e1
    oh2 = erow == e2
    oh = jnp.where(oh1 | oh2, 1.0, 0.0)
    before = jnp.where(_iota((tm, tm), 0) < _iota((tm, tm), 1), 1.0, 0.0).astype(BF16)
    cnt_before = _dot(oh.astype(BF16), before)
    lr1 = jnp.sum(jnp.where(oh1, cnt_before, 0.0), axis=0, keepdims=True)
    lr2 = jnp.sum(jnp.where(oh2, cnt_before, 0.0), axis=0, keepdims=True)
    zero = jnp.zeros((1, tm), F32)
    meta_out[...] = jnp.concatenate([e1, e2, gate1, gate2, lr1, lr2, zero, zero], axis=0)
    cnt_out[...] = jnp.broadcast_to(jnp.sum(oh, axis=1, keepdims=True), (MOE_EXPERTS, LANES))


def _chunk_copies(tab_ref, t, make_copy, act):
    base = t * RUN_TAB
    for ci, ck in enumerate(RUN_CHUNKS):
        first = base + len(RUN_CHUNKS) + ci * 2 * MOE_EXPERTS

        def body(i, _, first=first, ck=ck):
            src = tab_ref[first + 2 * i]
            dst = tab_ref[first + 2 * i + 1]
            act(make_copy(pl.multiple_of(src, MOE_ALIGN), pl.multiple_of(dst, MOE_ALIGN), ck))
            return 0

        lax.fori_loop(0, tab_ref[base + ci], body, 0)


def _run_copies(tab_ref, first, count, chunks, make_copy, act):
    def body(e, _):
        base = first + e * 3
        length = tab_ref[base]
        src = tab_ref[base + 1]
        dst = tab_ref[base + 2]
        for ck in chunks:
            @pl.when((length & ck) != 0)
            def _():
                off = length & (-2 * ck)
                act(make_copy(pl.multiple_of(src + off, MOE_ALIGN), pl.multiple_of(dst + off, MOE_ALIGN), ck))
        return 0
    lax.fori_loop(0, count, body, 0)


def _start(cp):
    cp.start()


def _wait(cp):
    cp.wait()


def _dispatch_kernel(tab_ref, h_ref, pos_ref, rows_hbm, sorted_ref, zero_ref, sem, zsem):
    t = pl.program_id(0)
    nt = pl.num_programs(0)
    slot = t % 2
    r = _iota((MOE_SORTED, MOE_TILE), 0)
    perm = jnp.where((r == pos_ref[0:1, :]) | (r == pos_ref[1:2, :]), 1.0, 0.0).astype(BF16)
    sorted_ref[slot] = _dot(perm, h_ref[...]).astype(BF16)

    def run_copy(s):
        def make(src, dst, ck):
            return pltpu.make_async_copy(sorted_ref.at[s, pl.ds(src, ck)], rows_hbm.at[pl.ds(dst, ck)], sem.at[s])
        return make

    def zero_copy(src, dst, ck):
        del src
        return pltpu.make_async_copy(zero_ref.at[pl.ds(0, ck)], rows_hbm.at[pl.ds(dst, ck)], zsem)

    _chunk_copies(tab_ref, t, run_copy(slot), _start)

    @pl.when(t > 0)
    def _():
        _chunk_copies(tab_ref, t - 1, run_copy(1 - slot), _wait)

    @pl.when(t == nt - 1)
    def _():
        zero_ref[...] = jnp.zeros_like(zero_ref)
        pad = nt * RUN_TAB
        _run_copies(tab_ref, pad, MOE_EXPERTS, PAD_CHUNKS, zero_copy, _start)
        spare = pad + MOE_EXPERTS * 3
        n_spare = tab_ref[spare]

        def spare_copy(i):
            dst = pl.multiple_of(tab_ref[spare + 2] + i * PAD_ROWS, PAD_ROWS)
            return zero_copy(0, dst, PAD_ROWS)

        lax.fori_loop(0, n_spare, lambda i, c: (_start(spare_copy(i)), c)[1], 0)
        _chunk_copies(tab_ref, t, run_copy(slot), _wait)
        _run_copies(tab_ref, pad, MOE_EXPERTS, PAD_CHUNKS, zero_copy, _wait)
        lax.fori_loop(0, n_spare, lambda i, c: (_wait(spare_copy(i)), c)[1], 0)


def moe_dispatch(tab, h2, pos, n_rows):
    n, d = h2.shape
    tm = MOE_TILE
    return pl.pallas_call(
        _dispatch_kernel,
        grid_spec=pltpu.PrefetchScalarGridSpec(
            num_scalar_prefetch=1,
            grid=(n // tm,),
            in_specs=[pl.BlockSpec((tm, d), lambda i, tab: (i, 0)),
                      pl.BlockSpec((8, tm), lambda i, tab: (0, i))],
            out_specs=pl.BlockSpec(memory_space=pl.ANY),
            scratch_shapes=[pltpu.VMEM((2, MOE_SORTED, d), BF16), pltpu.VMEM((PAD_ROWS, d), BF16),
                            pltpu.SemaphoreType.DMA((2,)), pltpu.SemaphoreType.DMA],
        ),
        out_shape=jax.ShapeDtypeStruct((n_rows, d), BF16),
        compiler_params=_params("arbitrary"),
        name="moe_dispatch",
    )(tab, h2, pos)


def _expert_kernel(be_ref, nu_ref, x_ref, w1_ref, w3_ref, w2_ref, y_ref, w1b, w3b, w2b):
    b = pl.program_id(0)
    used = b < nu_ref[0]
    new_expert = (b == 0) | (be_ref[b] != be_ref[jnp.maximum(b - 1, 0)])

    @pl.when(used & new_expert)
    def _():
        w1b[...] = w1_ref[0, 0].astype(BF16)
        w3b[...] = w3_ref[0, 0].astype(BF16)
        w2b[...] = w2_ref[0, 0].astype(BF16)

    @pl.when(used)
    def _():
        xb = x_ref[...]
        a = _dot(xb, w1b[...])
        c = _dot(xb, w3b[...])
        hid = (a * jax.nn.sigmoid(a) * c).astype(BF16)
        y_ref[...] = _dot(hid, w2b[...]).astype(y_ref.dtype)

    @pl.when(jnp.logical_not(used))
    def _():
        y_ref[...] = jnp.zeros_like(y_ref)


def moe_experts(block_expert, n_used, rows, layer, w1, w3, w2):
    n_rows = rows.shape[0]
    nb = n_rows // MOE_ROWS
    d, ff = w1.shape[2], w1.shape[3]

    def xmap(b, be, nu):
        return (jnp.minimum(b, nu[0] - 1), 0)

    def wmap(b, be, nu):
        return (layer, be[jnp.minimum(b, nu[0] - 1)], 0, 0)

    return pl.pallas_call(
        _expert_kernel,
        grid_spec=pltpu.PrefetchScalarGridSpec(
            num_scalar_prefetch=2,
            grid=(nb,),
            in_specs=[pl.BlockSpec((MOE_ROWS, d), xmap),
                      pl.BlockSpec((1, 1, d, ff), wmap),
                      pl.BlockSpec((1, 1, d, ff), wmap),
                      pl.BlockSpec((1, 1, ff, d), wmap)],
            out_specs=pl.BlockSpec((MOE_ROWS, d), lambda b, be, nu: (b, 0)),
            scratch_shapes=[pltpu.VMEM((d, ff), BF16), pltpu.VMEM((d, ff), BF16), pltpu.VMEM((ff, d), BF16)],
        ),
        out_shape=jax.ShapeDtypeStruct((n_rows, d), BF16),
        compiler_params=_params("arbitrary"),
        name="moe_experts",
    )(block_expert, n_used, rows, w1, w3, w2)


def _combine_kernel(final_norm, tab_ref, y_hbm, pos_ref, gate_ref, x_ref, g_ref, o_ref, ys_ref, sem):
    t = pl.program_id(0)
    nt = pl.num_programs(0)
    slot = t % 2

    def run_copy(s):
        def make(src, dst, ck):
            return pltpu.make_async_copy(y_hbm.at[pl.ds(dst, ck)], ys_ref.at[s, pl.ds(src, ck)], sem.at[s])
        return make

    @pl.when(t == 0)
    def _():
        ys_ref[...] = jnp.zeros_like(ys_ref)
        _chunk_copies(tab_ref, 0, run_copy(0), _start)

    @pl.when(t + 1 < nt)
    def _():
        _chunk_copies(tab_ref, t + 1, run_copy(1 - slot), _start)

    r = _iota((MOE_SORTED, MOE_TILE), 0)
    hit1 = r == pos_ref[0:1, :]
    hit2 = r == pos_ref[1:2, :]
    wgt = jnp.where(hit1, gate_ref[2:3, :], jnp.where(hit2, gate_ref[3:4, :], 0.0)).astype(BF16)
    _chunk_copies(tab_ref, t, run_copy(slot), _wait)
    out = x_ref[...] + _dot_tn(wgt, ys_ref[slot])
    if final_norm:
        out = _rms(out, g_ref[...])
    o_ref[...] = out


def moe_combine(tab, y, pos, meta, x, final_gain):
    n, d = x.shape
    tm = MOE_TILE
    final_norm = final_gain is not None
    gain = (final_gain if final_norm else jnp.ones((d,), F32)).reshape(1, d)
    return pl.pallas_call(
        functools.partial(_combine_kernel, final_norm),
        grid_spec=pltpu.PrefetchScalarGridSpec(
            num_scalar_prefetch=1,
            grid=(n // tm,),
            in_specs=[pl.BlockSpec(memory_space=pl.ANY),
                      pl.BlockSpec((8, tm), lambda i, tab: (0, i)),
                      pl.BlockSpec((8, tm), lambda i, tab: (0, i)),
                      pl.BlockSpec((tm, d), lambda i, tab: (i, 0)),
                      pl.BlockSpec((1, d), lambda i, tab: (0, 0))],
            out_specs=pl.BlockSpec((tm, d), lambda i, tab: (i, 0)),
            scratch_shapes=[pltpu.VMEM((2, MOE_SORTED, d), BF16), pltpu.SemaphoreType.DMA((2,))],
        ),
        out_shape=jax.ShapeDtypeStruct((n, d), F32),
        compiler_params=_params("arbitrary"),
        name="moe_combine",
    )(tab, y, pos, meta, x, gain)


def hierarchical_moe(x, h2, meta, cnt, layer, w1, w3, w2, final_gain):
    n, d = x.shape
    tm = MOE_TILE
    nt = n // tm

    cnt = cnt.reshape(nt, MOE_EXPERTS, LANES)[:, :, 0].astype(I32)
    run = (cnt + MOE_ALIGN - 1) // MOE_ALIGN * MOE_ALIGN
    src = jnp.cumsum(run, axis=1) - run
    before = jnp.cumsum(run, axis=0) - run
    total = jnp.sum(run, axis=0)
    padded = (total + MOE_ROWS - 1) // MOE_ROWS * MOE_ROWS
    pad_end = jnp.cumsum(padded)
    dst = (pad_end - padded)[None, :] + before
    max_rows = 2 * n + nt * MOE_EXPERTS * (MOE_ALIGN - 1) + MOE_EXPERTS * (MOE_ROWS - 1)
    nb = -(-max_rows // MOE_ROWS)
    sizes = jnp.array(RUN_CHUNKS, I32)[None, :, None]
    has = (run[:, None, :] & sizes) != 0
    above = run[:, None, :] & (-2 * sizes)
    place = jnp.where(has, jnp.cumsum(has.astype(I32), axis=2) - 1, -1)
    front = place[..., None] == jnp.arange(MOE_EXPERTS, dtype=I32)
    c_src = jnp.sum(jnp.where(front, (src[:, None, :] + above)[..., None], 0), axis=2)
    c_dst = jnp.sum(jnp.where(front, (dst[:, None, :] + above)[..., None], 0), axis=2)
    chunks = jnp.stack([c_src, c_dst], axis=-1).reshape(nt, -1)
    runs = jnp.concatenate([jnp.sum(has.astype(I32), axis=2), chunks], axis=1).reshape(-1)
    zero = jnp.zeros((MOE_EXPERTS,), I32)
    pads = jnp.stack([padded - total, zero, pad_end - padded + total], axis=-1)
    spare = jnp.stack([(nb * MOE_ROWS - pad_end[-1]) // PAD_ROWS, zero[0], pad_end[-1]])
    tab = jnp.concatenate([runs, pads.reshape(-1), spare]).astype(I32)
    n_used = (pad_end[-1] // MOE_ROWS).astype(I32).reshape(1)
    block_start = jnp.arange(nb, dtype=I32) * MOE_ROWS
    block_expert = jnp.minimum(
        jnp.sum((pad_end[None, :] <= block_start[:, None]).astype(I32), axis=1), MOE_EXPERTS - 1).astype(I32)
    e = meta[0:2].astype(I32).reshape(2, nt, tm, 1)
    hit = e == jnp.arange(MOE_EXPERTS, dtype=I32)
    pos = jnp.sum(jnp.where(hit, src[None, :, None, :], 0), axis=-1).reshape(2, n) + meta[4:6].astype(I32)
    pos8 = jnp.zeros((8, n), I32).at[0:2].set(pos)

    rows = moe_dispatch(tab, h2, pos8, nb * MOE_ROWS)
    y = moe_experts(block_expert, n_used, rows, layer, w1, w3, w2)
    return moe_combine(tab, y, pos8, meta, x, final_gain)


def kernel(x, mem, positions, ln_mix, ln_mem, ln_memkv, ln_ffn, rw_mu, rw_w0, rw_w1, rw_w2, rw_a0, rw_a1, rw_a2, rw_g1, rw_g2, rw_kk, rw_ka, rw_rk, rw_wrkv, rw_lnx_g, rw_lnx_b, rw_wo, mb_wqkv, mb_wo, mx_wq, mx_wkv, mx_wo, moe_wg, moe_bg, moe_we, moe_be, moe_w1, moe_w3, moe_w2, ln_f):
    B, T, C = x.shape
    n = B * T
    depth = ln_mix.shape[0]
    assert C == D_MODEL and mem.shape[0] == B and mem.shape[1] % 16 == 0
    assert T % 1024 == 0 and T % (RWKV_CHUNK * RWKV_CHUNKS_PER_STEP) == 0 and T // MOBA_BLOCK <= 16
    xf = x.reshape(n, C)
    memf = mem.reshape(-1, C)
    for i in range(depth):
        j = i // 2
        if i % 2 == 0:
            r, k, v, kk, akk, lw, g = rwkv_mix(xf, T, ln_mix[i], rw_mu[j], rw_w0[j], rw_w1[j], rw_w2[j],
                                               rw_a0[j], rw_a1[j], rw_a2[j], rw_g1[j], rw_g2[j],
                                               rw_kk[j], rw_ka[j], rw_wrkv[j])
            y = rwkv_recurrence(r, k, v, kk, akk, lw, g, rw_lnx_g[j], rw_lnx_b[j], rw_rk[j], B, T)
            w_mix_o = rw_wo[j]
        else:
            qkv = qkv_rope(xf, ln_mix[i], positions, mb_wqkv[j])
            y = moba_attention(qkv, B, T)
            w_mix_o = mb_wo[j]
        kv = norm_linear(memf, ln_memkv[i], mx_wkv[i].astype(BF16))
        xf, h2, meta, cnt = mem_cross_attention(y, w_mix_o, xf, ln_mem[i], mx_wq[i], kv, mx_wo[i], T,
                                                ln_ffn[i], moe_wg[i], moe_bg[i], moe_we[i], moe_be[i])
        xf = hierarchical_moe(xf, h2, meta, cnt, i, moe_w1, moe_w3, moe_w2, ln_f if i == depth - 1 else None)
    return xf.reshape(B, T, C)
```

```python
import functools
import math

import jax
import jax.numpy as jnp
from jax import lax
from jax.experimental import pallas as pl
from jax.experimental.pallas import tpu as pltpu

F32 = jnp.float32
BF16 = jnp.bfloat16
I32 = jnp.int32

D_MODEL = 1024
HEAD_DIM = 64
PAIR = 2 * HEAD_DIM
N_PAIRS = D_MODEL // PAIR
RWKV_GN_EPS = 64e-5
RWKV_CHUNK = 64
RWKV_CHUNKS_PER_STEP = 8
MOBA_BLOCK = 256
MOBA_TOPK = 3
VT_ROWS = HEAD_DIM + 16
ROPE_THETA = 10000.0
MEM_HEADS = 4
MEM_HEAD_DIM = D_MODEL // MEM_HEADS
MOE_GROUPS = 4
MOE_EPG = 8
MOE_EXPERTS = MOE_GROUPS * MOE_EPG
RMS_EPS = 1e-6
NEG_INF = -1e30

LANES = 128
ROW_ALIGN = 8
MOE_TILE = 512
MOE_ROWS = 1024
MOE_ALIGN = 16
MOE_SORTED = -(-(2 * MOE_TILE + MOE_EXPERTS * (MOE_ALIGN - 1)) // MOE_ALIGN) * MOE_ALIGN
RUN_CHUNKS = tuple(MOE_ALIGN << s for s in range((MOE_TILE // MOE_ALIGN).bit_length() - 1, -1, -1))
RUN_TAB = len(RUN_CHUNKS) * (1 + 2 * MOE_EXPERTS)
PAD_ROWS = MOE_ROWS // 2
PAD_CHUNKS = tuple(MOE_ALIGN << s for s in range((PAD_ROWS // MOE_ALIGN).bit_length() - 1, -1, -1))
VMEM_LIMIT = 56 * 2 ** 20


def _params(*sem):
    return pltpu.CompilerParams(dimension_semantics=sem, vmem_limit_bytes=VMEM_LIMIT)


def _iota(shape, dim):
    return lax.broadcasted_iota(I32, shape, dim)


def _dot(a, b):
    return jnp.dot(a, b, preferred_element_type=F32)


def _dot_nt(a, b):
    return lax.dot_general(a, b, (((1,), (1,)), ((), ())), preferred_element_type=F32)


def _dot_tn(a, b):
    return lax.dot_general(a, b, (((0,), (0,)), ((), ())), preferred_element_type=F32)


def _split(x):
    hi = x.astype(BF16)
    lo = (x - hi.astype(F32)).astype(BF16)
    return hi, lo


def _rms(x, g):
    return x * lax.rsqrt(jnp.mean(x * x, axis=-1, keepdims=True) + RMS_EPS) * g


def _norm_linear_kernel(x_ref, g_ref, w_ref, o_ref):
    xn = _rms(x_ref[...], g_ref[...]).astype(BF16)
    o_ref[...] = _dot(xn, w_ref[...]).astype(o_ref.dtype)


def norm_linear(x, g, w, tm=256):
    n, k = x.shape
    dout = w.shape[1]
    return pl.pallas_call(
        _norm_linear_kernel,
        grid=(n // tm,),
        in_specs=[pl.BlockSpec((tm, k), lambda i: (i, 0)),
                  pl.BlockSpec((1, k), lambda i: (0, 0)),
                  pl.BlockSpec((k, dout), lambda i: (0, 0))],
        out_specs=pl.BlockSpec((tm, dout), lambda i: (i, 0)),
        out_shape=jax.ShapeDtypeStruct((n, dout), BF16),
        compiler_params=_params("parallel"),
        name="norm_linear",
    )(x, g.reshape(1, k), w)


def _head_sums(x_sq):
    rows = x_sq.shape[0]
    r = _iota((PAIR, PAIR), 0) // HEAD_DIM
    c = _iota((PAIR, PAIR), 1) // HEAD_DIM
    bd = jnp.where(r == c, 1.0, 0.0).astype(BF16)
    hi, lo = _split(x_sq)
    stacked = jnp.concatenate([part[:, p * PAIR:(p + 1) * PAIR] for part in (hi, lo) for p in range(N_PAIRS)], axis=0)
    sums = _dot(stacked, bd)
    sums = sums[:N_PAIRS * rows] + sums[N_PAIRS * rows:]
    return jnp.concatenate([sums[p * rows:(p + 1) * rows] for p in range(N_PAIRS)], axis=1)


def _rwkv_mix_kernel(seq_len, x_ref, xp_ref, g_ref, mu_ref, vec_ref, wr_ref, wk_ref, wv_ref,
                     l1_ref, w2_ref, a2_ref, g2_ref,
                     r_out, k_out, v_out, kk_out, akk_out, lw_out, g_out):
    i = pl.program_id(0)
    tm = x_ref.shape[0]
    gain = g_ref[...]
    h = _rms(x_ref[...], gain)
    hp = _rms(xp_ref[...], gain)[ROW_ALIGN - 1:ROW_ALIGN, :]
    hp = jnp.where((i * tm) % seq_len == 0, 0.0, hp)
    hs = pltpu.roll(h, 1, 0)
    hs = jnp.where(_iota((tm, 1), 0) == 0, hp, hs)
    hb = h.astype(BF16)
    dxb = (hs - h).astype(BF16)
    mub = mu_ref[...].astype(BF16)

    def mix(s):
        return hb + dxb * mub[s:s + 1, :]

    vec = vec_ref[...]
    w0, a0, k_k, k_a = vec[0:1], vec[1:2], vec[2:3], vec[3:4]
    l1 = l1_ref[...]
    tw = _dot(mix(1), l1[:, 0:64])
    ta = _dot(mix(4), l1[:, 64:128])
    tg = _dot(mix(5), l1[:, 128:256])
    r = _dot(mix(0), wr_ref[...])
    k = _dot(mix(2), wk_ref[...])
    v = _dot(mix(3), wv_ref[...])
    lw = -math.exp(-0.5) * jax.nn.sigmoid(w0 + _dot(jnp.tanh(tw).astype(BF16), w2_ref[...]))
    a = jax.nn.sigmoid(a0 + _dot(ta.astype(BF16), a2_ref[...]))
    g = _dot(jax.nn.sigmoid(tg).astype(BF16), g2_ref[...])

    kk = k * k_k
    kkn = kk * lax.rsqrt(jnp.maximum(_head_sums(kk * kk), 1e-24))
    kk_out[...] = kkn.astype(kk_out.dtype)
    akk_out[...] = (kkn * a).astype(akk_out.dtype)
    r_out[...] = r.astype(r_out.dtype)
    k_out[...] = (k * (1.0 + (a - 1.0) * k_a)).astype(k_out.dtype)
    v_out[...] = v.astype(v_out.dtype)
    lw_out[...] = lw
    g_out[...] = g.astype(g_out.dtype)


def rwkv_mix(x, seq_len, gain, mu, w0, w1, w2, a0, a1, a2, g1, g2, k_k, k_a, w_rkv, tm=512):
    n, d = x.shape
    mu8 = jnp.zeros((8, d), F32).at[:6].set(mu)
    vec = jnp.zeros((8, d), F32).at[0].set(w0).at[1].set(a0).at[2].set(k_k).at[3].set(k_a)
    l1 = jnp.concatenate([w1, a1, g1], axis=1).astype(BF16)
    wb = w_rkv.astype(BF16)
    row = pl.BlockSpec((tm, d), lambda i: (i, 0))
    full = lambda a: pl.BlockSpec(a.shape, lambda i: (0,) * a.ndim)
    args = (x, x, gain.reshape(1, d), mu8, vec, wb[0], wb[1], wb[2], l1,
            w2.astype(BF16), a2.astype(BF16), g2.astype(BF16))
    in_specs = [row, pl.BlockSpec((ROW_ALIGN, d), lambda i: (jnp.maximum(i * (tm // ROW_ALIGN) - 1, 0), 0))]
    in_specs += [full(a) for a in args[2:]]
    outs = [jax.ShapeDtypeStruct((n, d), BF16)] * 5 + [jax.ShapeDtypeStruct((n, d), F32),
                                                       jax.ShapeDtypeStruct((n, d), BF16)]
    return pl.pallas_call(
        functools.partial(_rwkv_mix_kernel, seq_len),
        grid=(n // tm,),
        in_specs=in_specs,
        out_specs=[row] * 7,
        out_shape=outs,
        compiler_params=_params("parallel"),
        name="rwkv_mix",
    )(*args)


def _rwkv_rec_kernel(r_ref, k_ref, v_ref, kk_ref, akk_ref, lw_ref, g_ref, vec_ref, y_ref, s_ref):
    c = pl.program_id(1)
    L = RWKV_CHUNK

    @pl.when(c == 0)
    def _():
        s_ref[...] = jnp.zeros_like(s_ref)

    lane = _iota((1, PAIR), 1)
    m0 = lane < HEAD_DIM
    ri = _iota((2 * L, 2 * L), 0)
    ci = _iota((2 * L, 2 * L), 1)
    same = (ri // L) == (ci // L)
    strict = same & (ci < ri)
    incl = same & (ci <= ri)
    eye = jnp.where(ri == ci, 1.0, 0.0)
    tri = jnp.where(_iota((L, L), 1) <= _iota((L, L), 0), 1.0, 0.0).astype(BF16)
    rb = _iota((PAIR, PAIR), 0) // HEAD_DIM
    cb = _iota((PAIR, PAIR), 1) // HEAD_DIM
    bd = rb == cb

    def stack(x):
        return jnp.concatenate([jnp.where(m0, x, 0.0), jnp.where(m0, 0.0, x)], axis=0)

    def fold(x):
        return x[:L] + x[L:]

    def head_mean(x):
        s0 = jnp.sum(jnp.where(m0, x, 0.0), axis=-1, keepdims=True)
        s1 = jnp.sum(jnp.where(m0, 0.0, x), axis=-1, keepdims=True)
        return jnp.where(m0, s0, s1) * (1.0 / HEAD_DIM)

    vec = vec_ref[...]
    n_sub = r_ref.shape[0] // L
    pairs = range(n_sub * N_PAIRS)
    rws = [slice((q // N_PAIRS) * L, (q // N_PAIRS + 1) * L) for q in pairs]
    sls = [slice((q % N_PAIRS) * PAIR, (q % N_PAIRS + 1) * PAIR) for q in pairs]
    lw = [lw_ref[rws[p], sls[p]] for p in pairs]
    cum = []
    for p in pairs:
        lw_hi, lw_lo = _split(lw[p])
        cum.append(_dot(tri, lw_hi) + _dot(tri, lw_lo))
    dec_all, x_kap, x_r, a_t, k_t, sc = [], [], [], [], [], []
    for p in pairs:
        rw, sl = rws[p], sls[p]
        dec = jnp.exp(cum[p])
        inv = jnp.exp(-cum[p])
        dec_prev = jnp.exp(cum[p] - lw[p])
        dec_all.append(dec[L - 1:L, :])
        x_kap.append(stack(kk_ref[rw, sl].astype(F32) * dec_prev).astype(BF16))
        x_r.append(stack(r_ref[rw, sl].astype(F32) * dec))
        a_t.append((akk_ref[rw, sl].astype(F32) * inv).astype(BF16))
        k_t.append((k_ref[rw, sl].astype(F32) * inv).astype(BF16))
        xs = jnp.concatenate([x_kap[p], x_r[p].astype(BF16)], axis=0)
        ys = jnp.concatenate([a_t[p], k_t[p]], axis=0)
        sc.append(_dot_nt(xs, ys))

    def both_heads(first, second, keep):
        upper_left = keep & (ri < L)
        lower_right = keep & (ri >= L)
        return jnp.where(upper_left, first, jnp.where(lower_right, second, 0.0))

    mp, m_kk = [], []
    for p in pairs:
        top = sc[p][:2 * L]
        top_r = pltpu.roll(top, L, 1)
        mp.append(-both_heads(top, top_r, strict))
        m_kk.append(both_heads(top_r, top, strict).astype(BF16))
    s_ra, s_rk = [], []
    for p in pairs:
        bot = sc[p][2 * L:]
        bot_r = pltpu.roll(bot, L, 1)
        s_ra.append(both_heads(bot, bot_r, incl).astype(BF16))
        s_rk.append(both_heads(bot_r, bot, incl).astype(BF16))
    v_st = [stack(v_ref[rws[p], sls[p]].astype(F32)).astype(BF16) for p in pairs]
    bo = [_dot(jnp.concatenate([m_kk[p], s_rk[p]], axis=0), v_st[p]) for p in pairs]
    b1 = [bo[p][:2 * L].astype(BF16) for p in pairs]
    o1 = [bo[p][2 * L:] for p in pairs]

    t_inv = [eye + mp[p] for p in pairs]
    mpb = [mp[p].astype(BF16) for p in pairs]
    mpb = [_dot(mpb[p], mpb[p]).astype(BF16) for p in pairs]
    steps = int(math.log2(L)) - 1
    for j in range(steps):
        if j + 1 < steps:
            both = [_dot(mpb[p], jnp.concatenate([mpb[p], t_inv[p].astype(BF16)], axis=1)) for p in pairs]
            mpb = [both[p][:, :2 * L].astype(BF16) for p in pairs]
            t_inv = [t_inv[p] + both[p][:, 2 * L:] for p in pairs]
        else:
            t_inv = [t_inv[p] + _dot(mpb[p], t_inv[p].astype(BF16)) for p in pairs]

    wu = [_dot(t_inv[p].astype(BF16), jnp.concatenate([x_kap[p], b1[p]], axis=1)).astype(BF16)
          for p in pairs]
    corr = [_dot(s_ra[p], wu[p]) for p in pairs]
    g_m, h0t, r_hat, o0 = [], [], [], []
    for p in pairs:
        r_hat.append(fold(x_r[p] - corr[p][:, :PAIR]).astype(BF16))
        o0.append(fold(o1[p] - corr[p][:, PAIR:]))
        w_f = fold(wu[p][:, :PAIR])
        u0_f = fold(wu[p][:, PAIR:])
        g_m.append(jnp.where(bd, _dot_tn(a_t[p], w_f), 0.0).astype(BF16))
        vu = jnp.concatenate([v_ref[rws[p], sls[p]], -u0_f], axis=0)
        h0t.append(jnp.where(bd, _dot_tn(vu, jnp.concatenate([k_t[p], a_t[p]], axis=0)), 0.0))
    o = []
    for p in pairs:
        hp = p % N_PAIRS
        s = s_ref[hp]
        sb = s.astype(BF16)
        o.append(_dot_nt(r_hat[p], sb) + o0[p])
        s_ref[hp] = (s - _dot_nt(sb, g_m[p]) + h0t[p]) * dec_all[p]
    for p in pairs:
        rw, sl = rws[p], sls[p]
        mean = head_mean(o[p])
        cen = o[p] - mean
        var = head_mean(cen * cen)
        gn = cen * lax.rsqrt(var + RWKV_GN_EPS) * vec[0:1, sl] + vec[1:2, sl]
        r = r_ref[rw, sl].astype(F32)
        k = k_ref[rw, sl].astype(F32)
        bonus = head_mean(r * k * vec[2:3, sl]) * HEAD_DIM * v_ref[rw, sl].astype(F32)
        y_ref[rw, sl] = ((gn + bonus) * g_ref[rw, sl].astype(F32)).astype(y_ref.dtype)


def rwkv_recurrence(r, k, v, kk, akk, lw, g, lnx_g, lnx_b, r_k, batch, seq_len):
    n, d = r.shape
    rows = RWKV_CHUNK * RWKV_CHUNKS_PER_STEP
    nc = seq_len // rows
    vec = jnp.zeros((8, d), F32).at[0].set(lnx_g).at[1].set(lnx_b).at[2].set(r_k.reshape(d))
    blk = pl.BlockSpec((rows, d), lambda b, c: (b * nc + c, 0))
    return pl.pallas_call(
        _rwkv_rec_kernel,
        grid=(batch, nc),
        in_specs=[blk] * 7 + [pl.BlockSpec((8, d), lambda b, c: (0, 0))],
        out_specs=blk,
        out_shape=jax.ShapeDtypeStruct((n, d), BF16),
        scratch_shapes=[pltpu.VMEM((N_PAIRS, PAIR, PAIR), F32)],
        compiler_params=_params("parallel", "arbitrary"),
        name="rwkv_recurrence",
    )(r, k, v, kk, akk, lw, g, vec)


def _qkv_rope_kernel(x_ref, g_ref, pos_ref, inv_ref, w_ref, o_ref):
    d = x_ref.shape[1]
    xn = _rms(x_ref[...], g_ref[...]).astype(BF16)
    y = [_dot(xn, w_ref[:, part * d:(part + 1) * d]) for part in range(3)]
    ang = pos_ref[...].astype(F32) * inv_ref[...]
    first = (_iota((1, PAIR), 1) % HEAD_DIM) < HEAD_DIM // 2
    sn = jnp.sin(ang)
    cos = jnp.cos(ang)
    sin_lo = jnp.where(first, -sn, 0.0)
    sin_hi = jnp.where(first, 0.0, sn)
    for part in range(2):
        for p in range(N_PAIRS):
            t = y[part][:, p * PAIR:(p + 1) * PAIR]
            rot = (t * cos + pltpu.roll(t, PAIR - HEAD_DIM // 2, 1) * sin_lo
                   + pltpu.roll(t, HEAD_DIM // 2, 1) * sin_hi)
            o_ref[:, part * d + p * PAIR:part * d + (p + 1) * PAIR] = rot.astype(o_ref.dtype)
    o_ref[:, 2 * d:] = y[2].astype(o_ref.dtype)


def qkv_rope(x, gain, positions, w_qkv, tm=1024):
    n, d = x.shape
    half = HEAD_DIM // 2
    inv = ROPE_THETA ** (-jnp.arange(half, dtype=F32) * 2.0 / HEAD_DIM)
    inv128 = jnp.tile(inv, PAIR // half).reshape(1, PAIR)
    col_scale = jnp.where(jnp.arange(3 * d) < d, math.log2(math.e) / math.sqrt(HEAD_DIM), 1.0).astype(F32)
    return pl.pallas_call(
        _qkv_rope_kernel,
        grid=(n // tm,),
        in_specs=[pl.BlockSpec((tm, d), lambda i: (i, 0)),
                  pl.BlockSpec((1, d), lambda i: (0, 0)),
                  pl.BlockSpec((tm, 1), lambda i: (i, 0)),
                  pl.BlockSpec((1, PAIR), lambda i: (0, 0)),
                  pl.BlockSpec((d, 3 * d), lambda i: (0, 0))],
        out_specs=pl.BlockSpec((tm, 3 * d), lambda i: (i, 0)),
        out_shape=jax.ShapeDtypeStruct((n, 3 * d), BF16),
        compiler_params=_params("parallel"),
        name="qkv_rope",
    )(x, gain.reshape(1, d), positions.reshape(n, 1), inv128, (w_qkv * col_scale).astype(BF16))


def _moba_kernel(q_ref, k_ref, v_ref, o_ref, vt_ref, ka_ref, qa_ref):
    nblk = k_ref.shape[0] // MOBA_BLOCK
    bq = MOBA_BLOCK
    km_rows = 16
    lane = _iota((1, PAIR), 1)
    own = (lane < HEAD_DIM, lane >= HEAD_DIM)
    spare = (HEAD_DIM, 0)

    means = []
    ones_row = jnp.where(_iota((VT_ROWS - HEAD_DIM, bq), 0) == 0, 1.0, 0.0).astype(BF16)
    for n in range(nblk):
        cols = slice(n * bq, (n + 1) * bq)
        kb = k_ref[cols, :]
        means.append(jnp.mean(kb.astype(F32), axis=0, keepdims=True))
        vt = v_ref[cols, :].astype(F32).T.astype(BF16)
        for h in range(2):
            vt_ref[h * VT_ROWS:h * VT_ROWS + HEAD_DIM, cols] = vt[h * HEAD_DIM:(h + 1) * HEAD_DIM]
            vt_ref[h * VT_ROWS + HEAD_DIM:(h + 1) * VT_ROWS, cols] = ones_row
            marker = jnp.where(lane == spare[h] + n, 1.0, 0.0).astype(BF16)
            ka_ref[h, cols, :] = jnp.where(own[h], kb, marker)
    km_hi, km_lo = _split(jnp.concatenate(means + [jnp.zeros((km_rows - nblk, PAIR), F32)], axis=0))

    blk = _iota((km_rows, bq), 0)
    causal = _iota((bq, bq), 0) <= _iota((bq, bq), 1)
    eye = jnp.where(_iota((bq, bq), 0) == _iota((bq, bq), 1), 1.0, 0.0).astype(BF16)

    q_all = q_ref[...]
    gates = []
    for h in range(2):
        qm = jnp.where(own[h], q_all, jnp.zeros_like(q_all))
        gates.append(_dot_nt(km_hi, qm) + _dot_nt(km_lo, qm))
    pad = jnp.zeros((HEAD_DIM - km_rows, bq), F32)
    for c in range(nblk):
        past = blk < c
        drop = []
        for h in range(2):
            gate = jnp.where(past, gates[h][:, c * bq:(c + 1) * bq], NEG_INF)
            rank = jnp.zeros(gate.shape, F32)
            for m in range(c):
                row = gate[m:m + 1, :]
                tie = jnp.where(m < blk, 1.0, 0.0)
                rank = rank + jnp.where(row > gate, 1.0, jnp.where(row == gate, tie, 0.0))
            drop.append(jnp.where(past & (rank >= MOBA_TOPK), 1.0, 0.0))
        rows = jnp.concatenate([drop[1], pad, drop[0], pad], axis=0).astype(BF16)
        bias = jnp.where(_dot_nt(eye, rows) > 0.5, NEG_INF, 0.0).astype(BF16)
        q = q_ref[c * bq:(c + 1) * bq, :]
        for h in range(2):
            qa_ref[h, c * bq:(c + 1) * bq, :] = jnp.where(own[h], q, bias)

    def score(c, h):
        return _dot_nt(ka_ref[h, 0:(c + 1) * bq, :], qa_ref[h, c * bq:(c + 1) * bq, :])

    def attend(c, h, s):
        parts = [s[n * bq:(n + 1) * bq] for n in range(c)]
        parts.append(jnp.where(causal, s[c * bq:], NEG_INF))
        top = parts[0].max(axis=0, keepdims=True)
        for part in parts[1:]:
            top = jnp.maximum(top, part.max(axis=0, keepdims=True))
        probs = jnp.concatenate([jnp.exp2((part - top).astype(BF16)) for part in parts], axis=0)
        acc = _dot(vt_ref[h * VT_ROWS:(h + 1) * VT_ROWS, 0:(c + 1) * bq], probs)
        return acc[:HEAD_DIM] * (1.0 / acc[HEAD_DIM:HEAD_DIM + 1])

    units = [(c, h) for c in range(nblk) for h in range(2)]
    ahead = 2
    pending = [score(*u) for u in units[:ahead]]
    outs = {}
    for i, (c, h) in enumerate(units):
        s = pending.pop(0)
        if i + ahead < len(units):
            pending.append(score(*units[i + ahead]))
        outs[h] = attend(c, h, s)
        if h == 1:
            o_ref[c * bq:(c + 1) * bq, :] = jnp.concatenate([outs[0], outs[1]], axis=0).T.astype(o_ref.dtype)


def moba_attention(qkv, batch, seq_len):
    n = qkv.shape[0]
    return pl.pallas_call(
        _moba_kernel,
        grid=(batch, N_PAIRS),
        in_specs=[pl.BlockSpec((seq_len, PAIR), lambda b, p: (b, p)),
                  pl.BlockSpec((seq_len, PAIR), lambda b, p: (b, N_PAIRS + p)),
                  pl.BlockSpec((seq_len, PAIR), lambda b, p: (b, 2 * N_PAIRS + p))],
        out_specs=pl.BlockSpec((seq_len, PAIR), lambda b, p: (b, p)),
        out_shape=jax.ShapeDtypeStruct((n, D_MODEL), BF16),
        scratch_shapes=[pltpu.VMEM((2 * VT_ROWS, seq_len), BF16),
                        pltpu.VMEM((2, seq_len, PAIR), BF16),
                        pltpu.VMEM((2, seq_len, PAIR), BF16)],
        compiler_params=_params("parallel", "parallel"),
        name="moba_attention",
    )(qkv, qkv, qkv)


def _mem_xattn_kernel(y_ref, wy_ref, x_ref, g_ref, wq_ref, kv_ref, wo_ref, gf_ref, rw_ref, rb_ref,
                      o_ref, h_out, meta_out, cnt_out):
    x = x_ref[...] + _dot(y_ref[...], wy_ref[...])
    xn = _rms(x, g_ref[...]).astype(BF16)
    q = (_dot(xn, wq_ref[...]) * (1.0 / math.sqrt(MEM_HEAD_DIM))).astype(BF16)
    heads = [slice(h * MEM_HEAD_DIM, (h + 1) * MEM_HEAD_DIM) for h in range(MEM_HEADS)]
    scores = [_dot_nt(q[:, sl], kv_ref[:, sl]) for sl in heads]
    outs = []
    for sl, s in zip(heads, scores):
        e = jnp.exp(s - jnp.max(s, axis=-1, keepdims=True))
        pr = e * (1.0 / jnp.sum(e, axis=-1, keepdims=True))
        outs.append(_dot(pr.astype(BF16), kv_ref[:, D_MODEL + sl.start:D_MODEL + sl.stop]).astype(BF16))
    o = jnp.concatenate(outs, axis=1)
    x2 = x + _dot(o, wo_ref[...])
    o_ref[...] = x2
    _route(x2, gf_ref, rw_ref, rb_ref, h_out, meta_out, cnt_out)


def mem_cross_attention(y, wy, x, gain, wq, kv, wo, seq_len, ffn_gain, w_grp, b_grp, w_exp, b_exp):
    n, d = x.shape
    tm = MOE_TILE
    m = kv.shape[0] // (n // seq_len)
    per_seq = seq_len // tm
    wt = jnp.zeros((LANES, d), F32).at[0:MOE_GROUPS].set(w_grp.T).at[8:8 + MOE_EXPERTS].set(w_exp.T)
    bt = jnp.zeros((LANES, 1), F32).at[0:MOE_GROUPS, 0].set(b_grp).at[8:8 + MOE_EXPERTS, 0].set(b_exp)
    const = lambda shape: pl.BlockSpec(shape, lambda i: (0, 0))
    return pl.pallas_call(
        _mem_xattn_kernel,
        grid=(n // tm,),
        in_specs=[pl.BlockSpec((tm, d), lambda i: (i, 0)),
                  const((d, d)),
                  pl.BlockSpec((tm, d), lambda i: (i, 0)),
                  const((1, d)),
                  const((d, d)),
                  pl.BlockSpec((m, 2 * d), lambda i: (i // per_seq, 0)),
                  const((d, d)),
                  const((1, d)),
                  const((LANES, d)),
                  const((LANES, 1))],
        out_specs=[pl.BlockSpec((tm, d), lambda i: (i, 0)),
                   pl.BlockSpec((tm, d), lambda i: (i, 0)),
                   pl.BlockSpec((8, tm), lambda i: (0, i)),
                   pl.BlockSpec((MOE_EXPERTS, LANES), lambda i: (i, 0))],
        out_shape=[jax.ShapeDtypeStruct((n, d), F32),
                   jax.ShapeDtypeStruct((n, d), BF16),
                   jax.ShapeDtypeStruct((8, n), F32),
                   jax.ShapeDtypeStruct((n // tm * MOE_EXPERTS, LANES), F32)],
        compiler_params=_params("parallel"),
        name="mem_cross_attention",
    )(y, wy.astype(BF16), x, gain.reshape(1, d), wq.astype(BF16), kv, wo.astype(BF16),
      ffn_gain.reshape(1, d), wt, bt)


def _route(x, g_ref, w_ref, b_ref, h_out, meta_out, cnt_out):
    tm = x.shape[0]
    h2 = _rms(x, g_ref[...])
    hi = h2.astype(BF16)
    h_out[...] = hi
    lo = (h2 - hi.astype(F32)).astype(BF16)
    w_hi, w_lo = _split(w_ref[...])
    lg = _dot_nt(w_hi, hi) + _dot_nt(w_hi, lo) + _dot_nt(w_lo, hi) + b_ref[...]
    row = _iota((8, tm), 0).astype(F32)

    def first_argmax(val, vmax):
        return jnp.min(jnp.where(val == vmax, row, 8.0), axis=0, keepdims=True)

    gl = jnp.where(row < MOE_GROUPS, lg[0:8], -jnp.inf)
    gmax = jnp.max(gl, axis=0, keepdims=True)
    p_g = 1.0 / jnp.sum(jnp.exp(gl - gmax), axis=0, keepdims=True)
    gidx = first_argmax(gl, gmax)
    el = jnp.zeros((8, tm), F32)
    for g in range(MOE_GROUPS):
        el = el + jnp.where(gidx == g, lg[8 + 8 * g:16 + 8 * g], 0.0)
    ee = jnp.exp(el - jnp.max(el, axis=0, keepdims=True))
    pe = ee / jnp.sum(ee, axis=0, keepdims=True)
    p1 = jnp.max(pe, axis=0, keepdims=True)
    i1 = first_argmax(pe, p1)
    pe2 = jnp.where(row == i1, -1.0, pe)
    p2 = jnp.max(pe2, axis=0, keepdims=True)
    i2 = first_argmax(pe2, p2)
    e1 = gidx * MOE_EPG + i1
    e2 = gidx * MOE_EPG + i2
    gate1 = p_g * p1 / (p1 + p2)
    gate2 = p_g * p2 / (p1 + p2)

    erow = _iota((MOE_EXPERTS, tm), 0).astype(F32)
    oh1 = erow == e1
    oh2 = erow == e2
    oh = jnp.where(oh1 | oh2, 1.0, 0.0)
    before = jnp.where(_iota((tm, tm), 0) < _iota((tm, tm), 1), 1.0, 0.0).astype(BF16)
    cnt_before = _dot(oh.astype(BF16), before)
    lr1 = jnp.sum(jnp.where(oh1, cnt_before, 0.0), axis=0, keepdims=True)
    lr2 = jnp.sum(jnp.where(oh2, cnt_before, 0.0), axis=0, keepdims=True)
    zero = jnp.zeros((1, tm), F32)
    meta_out[...] = jnp.concatenate([e1, e2, gate1, gate2, lr1, lr2, zero, zero], axis=0)
    cnt_out[...] = jnp.broadcast_to(jnp.sum(oh, axis=1, keepdims=True), (MOE_EXPERTS, LANES))


def _chunk_copies(tab_ref, t, make_copy, act):
    base = t * RUN_TAB
    for ci, ck in enumerate(RUN_CHUNKS):
        first = base + len(RUN_CHUNKS) + ci * 2 * MOE_EXPERTS

        def body(i, _, first=first, ck=ck):
            src = tab_ref[first + 2 * i]
            dst = tab_ref[first + 2 * i + 1]
            act(make_copy(pl.multiple_of(src, MOE_ALIGN), pl.multiple_of(dst, MOE_ALIGN), ck), ci % 2)
            return 0

        lax.fori_loop(0, tab_ref[base + ci], body, 0)


def _run_copies(tab_ref, first, count, chunks, make_copy, act):
    def body(e, _):
        base = first + e * 3
        length = tab_ref[base]
        src = tab_ref[base + 1]
        dst = tab_ref[base + 2]
        for ck in chunks:
            @pl.when((length & ck) != 0)
            def _():
                off = length & (-2 * ck)
                act(make_copy(pl.multiple_of(src + off, MOE_ALIGN), pl.multiple_of(dst + off, MOE_ALIGN), ck))
        return 0
    lax.fori_loop(0, count, body, 0)


def _start(cp, priority=0):
    cp.start(priority=priority)


def _wait(cp, priority=0):
    del priority
    cp.wait()


def _dispatch_kernel(tab_ref, h_ref, pos_ref, rows_hbm, sorted_ref, zero_ref, sem, zsem):
    t = pl.program_id(0)
    nt = pl.num_programs(0)
    slot = t % 2
    r = _iota((MOE_SORTED, MOE_TILE), 0)
    perm = jnp.where((r == pos_ref[0:1, :]) | (r == pos_ref[1:2, :]), 1.0, 0.0).astype(BF16)
    sorted_ref[slot] = _dot(perm, h_ref[...]).astype(BF16)

    def run_copy(s):
        def make(src, dst, ck):
            return pltpu.make_async_copy(sorted_ref.at[s, pl.ds(src, ck)], rows_hbm.at[pl.ds(dst, ck)], sem.at[s])
        return make

    def zero_copy(src, dst, ck):
        del src
        return pltpu.make_async_copy(zero_ref.at[pl.ds(0, ck)], rows_hbm.at[pl.ds(dst, ck)], zsem)

    _chunk_copies(tab_ref, t, run_copy(slot), _start)

    @pl.when(t > 0)
    def _():
        _chunk_copies(tab_ref, t - 1, run_copy(1 - slot), _wait)

    @pl.when(t == nt - 1)
    def _():
        zero_ref[...] = jnp.zeros_like(zero_ref)
        pad = nt * RUN_TAB
        _run_copies(tab_ref, pad, MOE_EXPERTS, PAD_CHUNKS, zero_copy, _start)
        spare = pad + MOE_EXPERTS * 3
        n_spare = tab_ref[spare]

        def spare_copy(i):
            dst = pl.multiple_of(tab_ref[spare + 2] + i * PAD_ROWS, PAD_ROWS)
            return zero_copy(0, dst, PAD_ROWS)

        lax.fori_loop(0, n_spare, lambda i, c: (_start(spare_copy(i)), c)[1], 0)
        _chunk_copies(tab_ref, t, run_copy(slot), _wait)
        _run_copies(tab_ref, pad, MOE_EXPERTS, PAD_CHUNKS, zero_copy, _wait)
        lax.fori_loop(0, n_spare, lambda i, c: (_wait(spare_copy(i)), c)[1], 0)


def moe_dispatch(tab, h2, pos, n_rows):
    n, d = h2.shape
    tm = MOE_TILE
    return pl.pallas_call(
        _dispatch_kernel,
        grid_spec=pltpu.PrefetchScalarGridSpec(
            num_scalar_prefetch=1,
            grid=(n // tm,),
            in_specs=[pl.BlockSpec((tm, d), lambda i, tab: (i, 0)),
                      pl.BlockSpec((8, tm), lambda i, tab: (0, i))],
            out_specs=pl.BlockSpec(memory_space=pl.ANY),
            scratch_shapes=[pltpu.VMEM((2, MOE_SORTED, d), BF16), pltpu.VMEM((PAD_ROWS, d), BF16),
                            pltpu.SemaphoreType.DMA((2,)), pltpu.SemaphoreType.DMA],
        ),
        out_shape=jax.ShapeDtypeStruct((n_rows, d), BF16),
        compiler_params=_params("arbitrary"),
        name="moe_dispatch",
    )(tab, h2, pos)


def _expert_kernel(be_ref, nu_ref, x_ref, w1_ref, w3_ref, w2_ref, y_ref, w1b, w3b, w2b):
    b = pl.program_id(0)
    used = b < nu_ref[0]
    new_expert = (b == 0) | (be_ref[b] != be_ref[jnp.maximum(b - 1, 0)])

    @pl.when(used & new_expert)
    def _():
        w1b[...] = w1_ref[0, 0].astype(BF16)
        w3b[...] = w3_ref[0, 0].astype(BF16)
        w2b[...] = w2_ref[0, 0].astype(BF16)

    @pl.when(used)
    def _():
        xb = x_ref[...]
        a = _dot(xb, w1b[...])
        c = _dot(xb, w3b[...])
        hid = (a * jax.nn.sigmoid(a) * c).astype(BF16)
        y_ref[...] = _dot(hid, w2b[...]).astype(y_ref.dtype)

    @pl.when(jnp.logical_not(used))
    def _():
        y_ref[...] = jnp.zeros_like(y_ref)


def moe_experts(block_expert, n_used, rows, layer, w1, w3, w2):
    n_rows = rows.shape[0]
    nb = n_rows // MOE_ROWS
    d, ff = w1.shape[2], w1.shape[3]

    def xmap(b, be, nu):
        return (jnp.minimum(b, nu[0] - 1), 0)

    def wmap(b, be, nu):
        return (layer, be[jnp.minimum(b, nu[0] - 1)], 0, 0)

    return pl.pallas_call(
        _expert_kernel,
        grid_spec=pltpu.PrefetchScalarGridSpec(
            num_scalar_prefetch=2,
            grid=(nb,),
            in_specs=[pl.BlockSpec((MOE_ROWS, d), xmap),
                      pl.BlockSpec((1, 1, d, ff), wmap),
                      pl.BlockSpec((1, 1, d, ff), wmap),
                      pl.BlockSpec((1, 1, ff, d), wmap)],
            out_specs=pl.BlockSpec((MOE_ROWS, d), lambda b, be, nu: (b, 0)),
            scratch_shapes=[pltpu.VMEM((d, ff), BF16), pltpu.VMEM((d, ff), BF16), pltpu.VMEM((ff, d), BF16)],
        ),
        out_shape=jax.ShapeDtypeStruct((n_rows, d), BF16),
        compiler_params=_params("arbitrary"),
        name="moe_experts",
    )(block_expert, n_used, rows, w1, w3, w2)


def _combine_kernel(final_norm, tab_ref, y_hbm, pos_ref, gate_ref, x_ref, g_ref, o_ref, ys_ref, sem):
    t = pl.program_id(0)
    nt = pl.num_programs(0)
    slot = t % 2

    def run_copy(s):
        def make(src, dst, ck):
            return pltpu.make_async_copy(y_hbm.at[pl.ds(dst, ck)], ys_ref.at[s, pl.ds(src, ck)], sem.at[s])
        return make

    @pl.when(t == 0)
    def _():
        ys_ref[...] = jnp.zeros_like(ys_ref)
        _chunk_copies(tab_ref, 0, run_copy(0), _start)

    @pl.when(t + 1 < nt)
    def _():
        _chunk_copies(tab_ref, t + 1, run_copy(1 - slot), _start)

    r = _iota((MOE_SORTED, MOE_TILE), 0)
    hit1 = r == pos_ref[0:1, :]
    hit2 = r == pos_ref[1:2, :]
    wgt = jnp.where(hit1, gate_ref[2:3, :], jnp.where(hit2, gate_ref[3:4, :], 0.0)).astype(BF16)
    _chunk_copies(tab_ref, t, run_copy(slot), _wait)
    out = x_ref[...] + _dot_tn(wgt, ys_ref[slot])
    if final_norm:
        out = _rms(out, g_ref[...])
    o_ref[...] = out


def moe_combine(tab, y, pos, meta, x, final_gain):
    n, d = x.shape
    tm = MOE_TILE
    final_norm = final_gain is not None
    gain = (final_gain if final_norm else jnp.ones((d,), F32)).reshape(1, d)
    return pl.pallas_call(
        functools.partial(_combine_kernel, final_norm),
        grid_spec=pltpu.PrefetchScalarGridSpec(
            num_scalar_prefetch=1,
            grid=(n // tm,),
            in_specs=[pl.BlockSpec(memory_space=pl.ANY),
                      pl.BlockSpec((8, tm), lambda i, tab: (0, i)),
                      pl.BlockSpec((8, tm), lambda i, tab: (0, i)),
                      pl.BlockSpec((tm, d), lambda i, tab: (i, 0)),
                      pl.BlockSpec((1, d), lambda i, tab: (0, 0))],
            out_specs=pl.BlockSpec((tm, d), lambda i, tab: (i, 0)),
            scratch_shapes=[pltpu.VMEM((2, MOE_SORTED, d), BF16), pltpu.SemaphoreType.DMA((2,))],
        ),
        out_shape=jax.ShapeDtypeStruct((n, d), F32),
        compiler_params=_params("arbitrary"),
        name="moe_combine",
    )(tab, y, pos, meta, x, gain)


def hierarchical_moe(x, h2, meta, cnt, layer, w1, w3, w2, final_gain):
    n, d = x.shape
    tm = MOE_TILE
    nt = n // tm

    cnt = cnt.reshape(nt, MOE_EXPERTS, LANES)[:, :, 0].astype(I32)
    run = (cnt + MOE_ALIGN - 1) // MOE_ALIGN * MOE_ALIGN
    src = jnp.cumsum(run, axis=1) - run
    before = jnp.cumsum(run, axis=0) - run
    total = jnp.sum(run, axis=0)
    padded = (total + MOE_ROWS - 1) // MOE_ROWS * MOE_ROWS
    pad_end = jnp.cumsum(padded)
    dst = (pad_end - padded)[None, :] + before
    max_rows = 2 * n + nt * MOE_EXPERTS * (MOE_ALIGN - 1) + MOE_EXPERTS * (MOE_ROWS - 1)
    nb = -(-max_rows // MOE_ROWS)
    sizes = jnp.array(RUN_CHUNKS, I32)[None, :, None]
    has = (run[:, None, :] & sizes) != 0
    above = run[:, None, :] & (-2 * sizes)
    place = jnp.where(has, jnp.cumsum(has.astype(I32), axis=2) - 1, -1)
    front = place[..., None] == jnp.arange(MOE_EXPERTS, dtype=I32)
    c_src = jnp.sum(jnp.where(front, (src[:, None, :] + above)[..., None], 0), axis=2)
    c_dst = jnp.sum(jnp.where(front, (dst[:, None, :] + above)[..., None], 0), axis=2)
    chunks = jnp.stack([c_src, c_dst], axis=-1).reshape(nt, -1)
    runs = jnp.concatenate([jnp.sum(has.astype(I32), axis=2), chunks], axis=1).reshape(-1)
    zero = jnp.zeros((MOE_EXPERTS,), I32)
    pads = jnp.stack([padded - total, zero, pad_end - padded + total], axis=-1)
    spare = jnp.stack([(nb * MOE_ROWS - pad_end[-1]) // PAD_ROWS, zero[0], pad_end[-1]])
    tab = jnp.concatenate([runs, pads.reshape(-1), spare]).astype(I32)
    n_used = (pad_end[-1] // MOE_ROWS).astype(I32).reshape(1)
    block_start = jnp.arange(nb, dtype=I32) * MOE_ROWS
    block_expert = jnp.minimum(
        jnp.sum((pad_end[None, :] <= block_start[:, None]).astype(I32), axis=1), MOE_EXPERTS - 1).astype(I32)
    e = meta[0:2].astype(I32).reshape(2, nt, tm, 1)
    hit = e == jnp.arange(MOE_EXPERTS, dtype=I32)
    pos = jnp.sum(jnp.where(hit, src[None, :, None, :], 0), axis=-1).reshape(2, n) + meta[4:6].astype(I32)
    pos8 = jnp.zeros((8, n), I32).at[0:2].set(pos)

    rows = moe_dispatch(tab, h2, pos8, nb * MOE_ROWS)
    y = moe_experts(block_expert, n_used, rows, layer, w1, w3, w2)
    return moe_combine(tab, y, pos8, meta, x, final_gain)


def kernel(x, mem, positions, ln_mix, ln_mem, ln_memkv, ln_ffn, rw_mu, rw_w0, rw_w1, rw_w2, rw_a0, rw_a1, rw_a2, rw_g1, rw_g2, rw_kk, rw_ka, rw_rk, rw_wrkv, rw_lnx_g, rw_lnx_b, rw_wo, mb_wqkv, mb_wo, mx_wq, mx_wkv, mx_wo, moe_wg, moe_bg, moe_we, moe_be, moe_w1, moe_w3, moe_w2, ln_f):
    B, T, C = x.shape
    n = B * T
    depth = ln_mix.shape[0]
    assert C == D_MODEL and mem.shape[0] == B and mem.shape[1] % 16 == 0
    assert T % 1024 == 0 and T % (RWKV_CHUNK * RWKV_CHUNKS_PER_STEP) == 0 and T // MOBA_BLOCK <= 16
    xf = x.reshape(n, C)
    memf = mem.reshape(-1, C)
    for i in range(depth):
        j = i // 2
        if i % 2 == 0:
            r, k, v, kk, akk, lw, g = rwkv_mix(xf, T, ln_mix[i], rw_mu[j], rw_w0[j], rw_w1[j], rw_w2[j],
                                               rw_a0[j], rw_a1[j], rw_a2[j], rw_g1[j], rw_g2[j],
                                               rw_kk[j], rw_ka[j], rw_wrkv[j])
            y = rwkv_recurrence(r, k, v, kk, akk, lw, g, rw_lnx_g[j], rw_lnx_b[j], rw_rk[j], B, T)
            w_mix_o = rw_wo[j]
        else:
            qkv = qkv_rope(xf, ln_mix[i], positions, mb_wqkv[j])
            y = moba_attention(qkv, B, T)
            w_mix_o = mb_wo[j]
        kv = norm_linear(memf, ln_memkv[i], mx_wkv[i].astype(BF16))
        xf, h2, meta, cnt = mem_cross_attention(y, w_mix_o, xf, ln_mem[i], mx_wq[i], kv, mx_wo[i], T,
                                                ln_ffn[i], moe_wg[i], moe_bg[i], moe_we[i], moe_be[i])
        xf = hierarchical_moe(xf, h2, meta, cnt, i, moe_w1, moe_w3, moe_w2, ln_f if i == depth - 1 else None)
    return xf.reshape(B, T, C)
```
